```python
import jax, jax.numpy as jnp
from jax import lax
import numpy as np

D_MODEL = 2048
BATCH = 8
SEQ = 4096
DEPTH = 2

CHUNK = 64
N_HEADS = 16
HEAD_DIM = D_MODEL // N_HEADS
D_FF = 4 * D_MODEL
Q_BLOCK = 128
N_PREV_CHUNKS = 8
REL_CLIP = 256
N_REL = REL_CLIP + CHUNK
N_A = DEPTH // 2
N_B = DEPTH - N_A
EPS = 1e-6
FGATE_BIAS = 3.0

kernel_name = "fox_yoco_chunked_relbias_hybrid"


def rms_norm(x, g):
    xf = x.astype(jnp.float32)
    y = xf * lax.rsqrt(jnp.mean(xf * xf, axis=-1, keepdims=True) + EPS)
    return (y * g.astype(jnp.float32)).astype(x.dtype)


def sq_relu_mlp(h, g, w1, w2):
    a = jax.nn.relu(rms_norm(h, g) @ w1)
    return (a * a) @ w2


def forgetting_attention(q, k, v, logf):
    S = q.shape[1]
    scale = HEAD_DIM ** -0.5
    c = jnp.transpose(jnp.cumsum(logf, axis=1), (0, 2, 1))
    outs = []
    for i in range(S // Q_BLOCK):
        q0, q1 = i * Q_BLOCK, (i + 1) * Q_BLOCK
        qb = q[:, q0:q1]
        kb, vb = k[:, :q1], v[:, :q1]
        s = jnp.einsum('bqhd,bkhd->bhqk', qb, kb).astype(jnp.float32) * scale
        s = s + c[:, :, q0:q1, None] - c[:, :, None, :q1]
        causal = (q0 + jnp.arange(Q_BLOCK))[:, None] >= jnp.arange(q1)[None, :]
        s = jnp.where(causal[None, None], s, -jnp.inf)
        p = jax.nn.softmax(s, axis=-1).astype(vb.dtype)
        outs.append(jnp.einsum('bhqk,bkhd->bqhd', p, vb))
    return jnp.concatenate(outs, axis=1)


def chunked_relbias_attention(q, k, v, rel_table):
    B, S, H, Dh = q.shape
    n_chunks = S // CHUNK
    pad = N_PREV_CHUNKS * CHUNK
    band = pad + CHUNK
    scale = HEAD_DIM ** -0.5
    kp = jnp.pad(k, ((0, 0), (pad, 0), (0, 0), (0, 0)))
    vp = jnp.pad(v, ((0, 0), (pad, 0), (0, 0), (0, 0)))
    qi = jnp.arange(CHUNK)[:, None]
    km = jnp.arange(band)[None, :]
    dist = pad + qi - km
    idx = jnp.clip(dist, -(CHUNK - 1), REL_CLIP) + (CHUNK - 1)
    bias = rel_table[:, idx].astype(jnp.float32)
    qc = q.reshape(B, n_chunks, CHUNK, H, Dh)

    def one_chunk(ci):
        qb = lax.dynamic_index_in_dim(qc, ci, axis=1, keepdims=False)
        kb = lax.dynamic_slice_in_dim(kp, ci * CHUNK, band, axis=1)
        vb = lax.dynamic_slice_in_dim(vp, ci * CHUNK, band, axis=1)
        s = jnp.einsum('bqhd,bkhd->bhqk', qb, kb).astype(jnp.float32) * scale + bias[None]
        valid = km >= pad - ci * CHUNK
        s = jnp.where(valid[None, None], s, -jnp.inf)
        p = jax.nn.softmax(s, axis=-1).astype(vb.dtype)
        return jnp.einsum('bhqk,bkhd->bqhd', p, vb)

    out = lax.map(one_chunk, jnp.arange(n_chunks))
    return jnp.transpose(out, (1, 0, 2, 3, 4)).reshape(B, S, H, Dh)


def _fwd_setup_inputs(seed: int = 0) -> dict:
    key = jax.random.key(seed)
    ks = jax.random.split(key, 20)
    D, H, Dh = D_MODEL, N_HEADS, HEAD_DIM
    nrm = jax.random.normal
    f32 = jnp.float32
    return {
        "x": nrm(ks[0], (BATCH, SEQ, D), f32),
        "a_norm_g": 1.0 + 0.02 * nrm(ks[1], (N_A, D), f32),
        "a_w_in": nrm(ks[2], (N_A, D, 3 * D + H), f32) * D ** -0.5,
        "a_b_f": FGATE_BIAS + 0.5 * nrm(ks[3], (N_A, H), f32),
        "a_q_g": 1.0 + 0.02 * nrm(ks[4], (N_A, Dh), f32),
        "a_k_g": 1.0 + 0.02 * nrm(ks[5], (N_A, Dh), f32),
        "a_w_out": nrm(ks[6], (N_A, D, D), f32) * D ** -0.5,
        "mlp_norm_g": 1.0 + 0.02 * nrm(ks[7], (DEPTH, D), f32),
        "mlp_w1": nrm(ks[8], (DEPTH, D, D_FF), f32) * D ** -0.5,
        "mlp_w2": nrm(ks[9], (DEPTH, D_FF, D), f32) * D_FF ** -0.5,
        "kv_norm_g": 1.0 + 0.02 * nrm(ks[10], (D,), f32),
        "kv_w": nrm(ks[11], (D, 2 * D), f32) * D ** -0.5,
        "kv_k_g": 1.0 + 0.02 * nrm(ks[12], (Dh,), f32),
        "b_norm_g": 1.0 + 0.02 * nrm(ks[13], (N_B, D), f32),
        "b_w_q": nrm(ks[14], (N_B, D, D), f32) * D ** -0.5,
        "b_q_g": 1.0 + 0.02 * nrm(ks[15], (N_B, Dh), f32),
        "b_rel": 0.5 * nrm(ks[16], (N_B, H, N_REL), f32),
        "b_w_out": nrm(ks[17], (N_B, D, D), f32) * D ** -0.5,
    }


def _fwd_reference(x, a_norm_g, a_w_in, a_b_f, a_q_g, a_k_g, a_w_out,
              mlp_norm_g, mlp_w1, mlp_w2,
              kv_norm_g, kv_w, kv_k_g,
              b_norm_g, b_w_q, b_q_g, b_rel, b_w_out):
    B, S, D = x.shape
    H, Dh = N_HEADS, HEAD_DIM
    h = x
    layer = 0
    for l in range(N_A):
        u = rms_norm(h, a_norm_g[l])
        proj = u @ a_w_in[l]
        q, k, v, fz = jnp.split(proj, [D, 2 * D, 3 * D], axis=-1)
        q = rms_norm(q.reshape(B, S, H, Dh), a_q_g[l])
        k = rms_norm(k.reshape(B, S, H, Dh), a_k_g[l])
        v = v.reshape(B, S, H, Dh)
        logf = jax.nn.log_sigmoid(fz.astype(jnp.float32) + a_b_f[l].astype(jnp.float32))
        o = forgetting_attention(q, k, v, logf)
        h = h + o.reshape(B, S, D) @ a_w_out[l]
        h = h + sq_relu_mlp(h, mlp_norm_g[layer], mlp_w1[layer], mlp_w2[layer])
        layer += 1
    kv = rms_norm(h, kv_norm_g) @ kv_w
    k_sh, v_sh = jnp.split(kv, [D], axis=-1)
    k_sh = rms_norm(k_sh.reshape(B, S, H, Dh), kv_k_g)
    v_sh = v_sh.reshape(B, S, H, Dh)
    for l in range(N_B):
        u = rms_norm(h, b_norm_g[l])
        q = rms_norm((u @ b_w_q[l]).reshape(B, S, H, Dh), b_q_g[l])
        o = chunked_relbias_attention(q, k_sh, v_sh, b_rel[l])
        h = h + o.reshape(B, S, D) @ b_w_out[l]
        h = h + sq_relu_mlp(h, mlp_norm_g[layer], mlp_w1[layer], mlp_w2[layer])
        layer += 1
    return h


import jax as _jax
import jax.numpy as _jnp

TWIN_FORMAT = 'train_step'
FWD_PARAMS = ['x', 'a_norm_g', 'a_w_in', 'a_b_f', 'a_q_g', 'a_k_g', 'a_w_out', 'mlp_norm_g', 'mlp_w1', 'mlp_w2', 'kv_norm_g', 'kv_w', 'kv_k_g', 'b_norm_g', 'b_w_q', 'b_q_g', 'b_rel', 'b_w_out']
TWIN_WEIGHTS = ['a_norm_g', 'a_w_in', 'a_b_f', 'a_q_g', 'a_k_g', 'a_w_out', 'mlp_norm_g', 'mlp_w1', 'mlp_w2', 'kv_norm_g', 'kv_w', 'kv_k_g', 'b_norm_g', 'b_w_q', 'b_q_g', 'b_rel', 'b_w_out']
TWIN_DIFF_INPUT = 'x'
TWIN_INPUTS = ['x', 'a_norm_g', 'a_w_in', 'a_b_f', 'a_q_g', 'a_k_g', 'a_w_out', 'mlp_norm_g', 'mlp_w1', 'mlp_w2', 'kv_norm_g', 'kv_w', 'kv_k_g', 'b_norm_g', 'b_w_q', 'b_q_g', 'b_rel', 'b_w_out', 'loss_target', 'm_a_norm_g', 'm_a_w_in', 'm_a_b_f', 'm_a_q_g', 'm_a_k_g', 'm_a_w_out', 'm_mlp_norm_g', 'm_mlp_w1', 'm_mlp_w2', 'm_kv_norm_g', 'm_kv_w', 'm_kv_k_g', 'm_b_norm_g', 'm_b_w_q', 'm_b_q_g', 'm_b_rel', 'm_b_w_out', 'v_a_norm_g', 'v_a_w_in', 'v_a_b_f', 'v_a_q_g', 'v_a_k_g', 'v_a_w_out', 'v_mlp_norm_g', 'v_mlp_w1', 'v_mlp_w2', 'v_kv_norm_g', 'v_kv_w', 'v_kv_k_g', 'v_b_norm_g', 'v_b_w_q', 'v_b_q_g', 'v_b_rel', 'v_b_w_out']
TWIN_OUTPUTS = ['loss', 'grad_x', 'grad_a_norm_g', 'grad_a_w_in', 'grad_a_b_f', 'grad_a_q_g', 'grad_a_k_g', 'grad_a_w_out', 'grad_mlp_norm_g', 'grad_mlp_w1', 'grad_mlp_w2', 'grad_kv_norm_g', 'grad_kv_w', 'grad_kv_k_g', 'grad_b_norm_g', 'grad_b_w_q', 'grad_b_q_g', 'grad_b_rel', 'grad_b_w_out', 'delta_a_norm_g', 'delta_a_w_in', 'delta_a_b_f', 'delta_a_q_g', 'delta_a_k_g', 'delta_a_w_out', 'delta_mlp_norm_g', 'delta_mlp_w1', 'delta_mlp_w2', 'delta_kv_norm_g', 'delta_kv_w', 'delta_kv_k_g', 'delta_b_norm_g', 'delta_b_w_q', 'delta_b_q_g', 'delta_b_rel', 'delta_b_w_out', 'new_m_a_norm_g', 'new_m_a_w_in', 'new_m_a_b_f', 'new_m_a_q_g', 'new_m_a_k_g', 'new_m_a_w_out', 'new_m_mlp_norm_g', 'new_m_mlp_w1', 'new_m_mlp_w2', 'new_m_kv_norm_g', 'new_m_kv_w', 'new_m_kv_k_g', 'new_m_b_norm_g', 'new_m_b_w_q', 'new_m_b_q_g', 'new_m_b_rel', 'new_m_b_w_out', 'new_v_a_norm_g', 'new_v_a_w_in', 'new_v_a_b_f', 'new_v_a_q_g', 'new_v_a_k_g', 'new_v_a_w_out', 'new_v_mlp_norm_g', 'new_v_mlp_w1', 'new_v_mlp_w2', 'new_v_kv_norm_g', 'new_v_kv_w', 'new_v_kv_k_g', 'new_v_b_norm_g', 'new_v_b_w_q', 'new_v_b_q_g', 'new_v_b_rel', 'new_v_b_w_out']
TWIN_LEAF_KINDS = {'loss': 'loss', 'grad_x': 'grad_x', 'grad_a_norm_g': 'grad_w', 'grad_a_w_in': 'grad_w', 'grad_a_b_f': 'grad_w', 'grad_a_q_g': 'grad_w', 'grad_a_k_g': 'grad_w', 'grad_a_w_out': 'grad_w', 'grad_mlp_norm_g': 'grad_w', 'grad_mlp_w1': 'grad_w', 'grad_mlp_w2': 'grad_w', 'grad_kv_norm_g': 'grad_w', 'grad_kv_w': 'grad_w', 'grad_kv_k_g': 'grad_w', 'grad_b_norm_g': 'grad_w', 'grad_b_w_q': 'grad_w', 'grad_b_q_g': 'grad_w', 'grad_b_rel': 'grad_w', 'grad_b_w_out': 'grad_w', 'delta_a_norm_g': 'delta_w', 'delta_a_w_in': 'delta_w', 'delta_a_b_f': 'delta_w', 'delta_a_q_g': 'delta_w', 'delta_a_k_g': 'delta_w', 'delta_a_w_out': 'delta_w', 'delta_mlp_norm_g': 'delta_w', 'delta_mlp_w1': 'delta_w', 'delta_mlp_w2': 'delta_w', 'delta_kv_norm_g': 'delta_w', 'delta_kv_w': 'delta_w', 'delta_kv_k_g': 'delta_w', 'delta_b_norm_g': 'delta_w', 'delta_b_w_q': 'delta_w', 'delta_b_q_g': 'delta_w', 'delta_b_rel': 'delta_w', 'delta_b_w_out': 'delta_w', 'new_m_a_norm_g': 'new_m', 'new_m_a_w_in': 'new_m', 'new_m_a_b_f': 'new_m', 'new_m_a_q_g': 'new_m', 'new_m_a_k_g': 'new_m', 'new_m_a_w_out': 'new_m', 'new_m_mlp_norm_g': 'new_m', 'new_m_mlp_w1': 'new_m', 'new_m_mlp_w2': 'new_m', 'new_m_kv_norm_g': 'new_m', 'new_m_kv_w': 'new_m', 'new_m_kv_k_g': 'new_m', 'new_m_b_norm_g': 'new_m', 'new_m_b_w_q': 'new_m', 'new_m_b_q_g': 'new_m', 'new_m_b_rel': 'new_m', 'new_m_b_w_out': 'new_m', 'new_v_a_norm_g': 'new_v', 'new_v_a_w_in': 'new_v', 'new_v_a_b_f': 'new_v', 'new_v_a_q_g': 'new_v', 'new_v_a_k_g': 'new_v', 'new_v_a_w_out': 'new_v', 'new_v_mlp_norm_g': 'new_v', 'new_v_mlp_w1': 'new_v', 'new_v_mlp_w2': 'new_v', 'new_v_kv_norm_g': 'new_v', 'new_v_kv_w': 'new_v', 'new_v_kv_k_g': 'new_v', 'new_v_b_norm_g': 'new_v', 'new_v_b_w_q': 'new_v', 'new_v_b_q_g': 'new_v', 'new_v_b_rel': 'new_v', 'new_v_b_w_out': 'new_v'}


def _forward(args):
    return _fwd_reference(*[args[k] for k in FWD_PARAMS])


def _output_shape():
    out = _jax.eval_shape(lambda: _forward(_fwd_setup_inputs(0)))
    return out.shape, out.dtype

N_MICROBATCH = 1
ADAM_LR = 0.001
ADAM_B1 = 0.9
ADAM_B2 = 0.999
ADAM_EPS = 1e-08
ADAM_WD = 0.01
ADAM_STEP = 10
PER_EXAMPLE_BATCH_AXIS = {'x': 0, 'loss_target': 0}
SHARED_INPUTS = []
_WEIGHT_DTYPES = {'a_norm_g': _jnp.float32, 'a_w_in': _jnp.float32, 'a_b_f': _jnp.float32, 'a_q_g': _jnp.float32, 'a_k_g': _jnp.float32, 'a_w_out': _jnp.float32, 'mlp_norm_g': _jnp.float32, 'mlp_w1': _jnp.float32, 'mlp_w2': _jnp.float32, 'kv_norm_g': _jnp.float32, 'kv_w': _jnp.float32, 'kv_k_g': _jnp.float32, 'b_norm_g': _jnp.float32, 'b_w_q': _jnp.float32, 'b_q_g': _jnp.float32, 'b_rel': _jnp.float32, 'b_w_out': _jnp.float32}
MOMENT_SCALE = {'a_norm_g': 1.254060e+00, 'a_w_in': 2.052390e-01, 'a_b_f': 6.009367e+01, 'a_q_g': 1.045574e+01, 'a_k_g': 1.047730e+01, 'a_w_out': 2.764554e-01, 'mlp_norm_g': 4.939323e+01, 'mlp_w1': 2.349407e+00, 'mlp_w2': 9.557679e+00, 'kv_norm_g': 5.891463e+00, 'kv_w': 4.048843e+00, 'kv_k_g': 7.403170e-01, 'b_norm_g': 7.678453e-02, 'b_w_q': 7.976642e-02, 'b_q_g': 7.441455e-01, 'b_rel': 3.407618e-02, 'b_w_out': 5.607592e+00}


def _to_microbatches(a, axis):
    t = _jnp.moveaxis(a, axis, 0)
    t = t.reshape((N_MICROBATCH, t.shape[0] // N_MICROBATCH) + t.shape[1:])
    return _jnp.moveaxis(t, 1, axis + 1)


def setup_inputs(seed: int = 0) -> dict:
    inp = _fwd_setup_inputs(seed)
    key = _jax.random.fold_in(_jax.random.key(seed), 7919)
    shape, _ = _output_shape()
    out = dict(inp)
    out["loss_target"] = _jax.random.normal(_jax.random.fold_in(key, 0), shape, _jnp.float32)
    for i, name in enumerate(TWIN_WEIGHTS):
        w = inp[name].astype(_jnp.float32)
        if MOMENT_SCALE is None:
            s = _jnp.sqrt(_jnp.mean(_jnp.square(w)) + 1e-30)
        else:
            s = MOMENT_SCALE[name]
        km, kv = _jax.random.split(_jax.random.fold_in(key, i + 1))
        out[name] = w
        out["m_" + name] = s * _jax.random.normal(km, w.shape, _jnp.float32)
        out["v_" + name] = (s * s) * _jax.random.uniform(kv, w.shape, _jnp.float32, 0.5, 1.5)
    if N_MICROBATCH > 1:
        for name, axis in PER_EXAMPLE_BATCH_AXIS.items():
            out[name] = _to_microbatches(out[name], axis)
    return {'x': out['x'], 'a_norm_g': out['a_norm_g'], 'a_w_in': out['a_w_in'], 'a_b_f': out['a_b_f'], 'a_q_g': out['a_q_g'], 'a_k_g': out['a_k_g'], 'a_w_out': out['a_w_out'], 'mlp_norm_g': out['mlp_norm_g'], 'mlp_w1': out['mlp_w1'], 'mlp_w2': out['mlp_w2'], 'kv_norm_g': out['kv_norm_g'], 'kv_w': out['kv_w'], 'kv_k_g': out['kv_k_g'], 'b_norm_g': out['b_norm_g'], 'b_w_q': out['b_w_q'], 'b_q_g': out['b_q_g'], 'b_rel': out['b_rel'], 'b_w_out': out['b_w_out'], 'loss_target': out['loss_target'], 'm_a_norm_g': out['m_a_norm_g'], 'm_a_w_in': out['m_a_w_in'], 'm_a_b_f': out['m_a_b_f'], 'm_a_q_g': out['m_a_q_g'], 'm_a_k_g': out['m_a_k_g'], 'm_a_w_out': out['m_a_w_out'], 'm_mlp_norm_g': out['m_mlp_norm_g'], 'm_mlp_w1': out['m_mlp_w1'], 'm_mlp_w2': out['m_mlp_w2'], 'm_kv_norm_g': out['m_kv_norm_g'], 'm_kv_w': out['m_kv_w'], 'm_kv_k_g': out['m_kv_k_g'], 'm_b_norm_g': out['m_b_norm_g'], 'm_b_w_q': out['m_b_w_q'], 'm_b_q_g': out['m_b_q_g'], 'm_b_rel': out['m_b_rel'], 'm_b_w_out': out['m_b_w_out'], 'v_a_norm_g': out['v_a_norm_g'], 'v_a_w_in': out['v_a_w_in'], 'v_a_b_f': out['v_a_b_f'], 'v_a_q_g': out['v_a_q_g'], 'v_a_k_g': out['v_a_k_g'], 'v_a_w_out': out['v_a_w_out'], 'v_mlp_norm_g': out['v_mlp_norm_g'], 'v_mlp_w1': out['v_mlp_w1'], 'v_mlp_w2': out['v_mlp_w2'], 'v_kv_norm_g': out['v_kv_norm_g'], 'v_kv_w': out['v_kv_w'], 'v_kv_k_g': out['v_kv_k_g'], 'v_b_norm_g': out['v_b_norm_g'], 'v_b_w_q': out['v_b_w_q'], 'v_b_q_g': out['v_b_q_g'], 'v_b_rel': out['v_b_rel'], 'v_b_w_out': out['v_b_w_out']}


def _loss(weights, diff, rest, loss_target):
    with _jax.named_scope("forward"):
        args = {**rest, TWIN_DIFF_INPUT: diff, **{k: w.astype(_WEIGHT_DTYPES[k]) for k, w in weights.items()}}
        y = _forward(args)
    with _jax.named_scope("loss_head"):
        err = _jnp.square(y.astype(_jnp.float32) - loss_target)
        return 0.5 * _jnp.sum(_jnp.mean(err, axis=-1)) if err.ndim else 0.5 * err


def _adamw(w, g, m, v):
    m = ADAM_B1 * m + (1.0 - ADAM_B1) * g
    v = ADAM_B2 * v + (1.0 - ADAM_B2) * _jnp.square(g)
    m_hat = m / (1.0 - ADAM_B1 ** ADAM_STEP)
    v_hat = v / (1.0 - ADAM_B2 ** ADAM_STEP)
    delta = -ADAM_LR * (m_hat / (_jnp.sqrt(v_hat) + ADAM_EPS) + ADAM_WD * w)
    return delta, m, v


def reference(x, a_norm_g, a_w_in, a_b_f, a_q_g, a_k_g, a_w_out, mlp_norm_g, mlp_w1, mlp_w2, kv_norm_g, kv_w, kv_k_g, b_norm_g, b_w_q, b_q_g, b_rel, b_w_out, loss_target, m_a_norm_g, m_a_w_in, m_a_b_f, m_a_q_g, m_a_k_g, m_a_w_out, m_mlp_norm_g, m_mlp_w1, m_mlp_w2, m_kv_norm_g, m_kv_w, m_kv_k_g, m_b_norm_g, m_b_w_q, m_b_q_g, m_b_rel, m_b_w_out, v_a_norm_g, v_a_w_in, v_a_b_f, v_a_q_g, v_a_k_g, v_a_w_out, v_mlp_norm_g, v_mlp_w1, v_mlp_w2, v_kv_norm_g, v_kv_w, v_kv_k_g, v_b_norm_g, v_b_w_q, v_b_q_g, v_b_rel, v_b_w_out):
    given = dict(x=x, a_norm_g=a_norm_g, a_w_in=a_w_in, a_b_f=a_b_f, a_q_g=a_q_g, a_k_g=a_k_g, a_w_out=a_w_out, mlp_norm_g=mlp_norm_g, mlp_w1=mlp_w1, mlp_w2=mlp_w2, kv_norm_g=kv_norm_g, kv_w=kv_w, kv_k_g=kv_k_g, b_norm_g=b_norm_g, b_w_q=b_w_q, b_q_g=b_q_g, b_rel=b_rel, b_w_out=b_w_out, loss_target=loss_target, m_a_norm_g=m_a_norm_g, m_a_w_in=m_a_w_in, m_a_b_f=m_a_b_f, m_a_q_g=m_a_q_g, m_a_k_g=m_a_k_g, m_a_w_out=m_a_w_out, m_mlp_norm_g=m_mlp_norm_g, m_mlp_w1=m_mlp_w1, m_mlp_w2=m_mlp_w2, m_kv_norm_g=m_kv_norm_g, m_kv_w=m_kv_w, m_kv_k_g=m_kv_k_g, m_b_norm_g=m_b_norm_g, m_b_w_q=m_b_w_q, m_b_q_g=m_b_q_g, m_b_rel=m_b_rel, m_b_w_out=m_b_w_out, v_a_norm_g=v_a_norm_g, v_a_w_in=v_a_w_in, v_a_b_f=v_a_b_f, v_a_q_g=v_a_q_g, v_a_k_g=v_a_k_g, v_a_w_out=v_a_w_out, v_mlp_norm_g=v_mlp_norm_g, v_mlp_w1=v_mlp_w1, v_mlp_w2=v_mlp_w2, v_kv_norm_g=v_kv_norm_g, v_kv_w=v_kv_w, v_kv_k_g=v_kv_k_g, v_b_norm_g=v_b_norm_g, v_b_w_q=v_b_w_q, v_b_q_g=v_b_q_g, v_b_rel=v_b_rel, v_b_w_out=v_b_w_out)
    weights = {n: given[n] for n in TWIN_WEIGHTS}
    shared = {n: given[n] for n in SHARED_INPUTS}
    per_example = {n: given[n] for n in ['x']}
    grad_fn = _jax.value_and_grad(_loss, argnums=(0, 1))

    def one_microbatch(ex, loss_target):
        ex = dict(ex)
        diff = ex.pop(TWIN_DIFF_INPUT)
        return grad_fn(weights, diff, {**shared, **ex}, loss_target)

    if N_MICROBATCH == 1:
        loss, (grad_w, grad_x) = one_microbatch(per_example, given["loss_target"])
    else:
        def body(carry, xs):
            loss_sum, grad_sum = carry
            l_k, (gw_k, gx_k) = one_microbatch(xs[0], xs[1])
            with _jax.named_scope("update"):
                return (loss_sum + l_k, _jax.tree.map(_jnp.add, grad_sum, gw_k)), gx_k

        init = (_jnp.zeros((), _jnp.float32), _jax.tree.map(_jnp.zeros_like, weights))
        (loss, grad_w), grad_x = _jax.lax.scan(body, init, (per_example, given["loss_target"]))
    with _jax.named_scope("update"):
        delta_w, new_m, new_v = {}, {}, {}
        for n in TWIN_WEIGHTS:
            delta_w[n], new_m[n], new_v[n] = _adamw(weights[n], grad_w[n], given["m_" + n], given["v_" + n])
    return (loss, grad_x, *[grad_w[n] for n in TWIN_WEIGHTS], *[delta_w[n] for n in TWIN_WEIGHTS],
            *[new_m[n] for n in TWIN_WEIGHTS], *[new_v[n] for n in TWIN_WEIGHTS])
```

```python
import functools

import numpy as np
import jax
import jax.numpy as jnp
from jax import lax
from jax.experimental import pallas as pl
from jax.experimental.pallas import tpu as pltpu

F32 = jnp.float32
BF16 = jnp.bfloat16
HIGHEST = lax.Precision.HIGHEST
MESH = pl.DeviceIdType.MESH
ANY = pl.BlockSpec(memory_space=pl.ANY)

N_DEV = 8
N_CHIP = 4
LANES = 128
HEAD_DIM = 128
CHUNK = 64
N_PREV_CHUNKS = 8
REL_CLIP = 256
N_REL = REL_CLIP + CHUNK
N_REL_PAD = 384
WIN = N_PREV_CHUNKS * CHUNK
ROLL_W = 4 * WIN
EPS = 1e-6
NEG = -1e30
SCALE = HEAD_DIM ** -0.5
ADAM_LR, ADAM_B1, ADAM_B2, ADAM_EPS, ADAM_WD, ADAM_STEP = 0.001, 0.9, 0.999, 1e-08, 0.01, 10
V7X_VMEM_BYTES = 64 * 1024 * 1024
VMEM_LIMIT = V7X_VMEM_BYTES * 3 // 4

_NN = (((1,), (0,)), ((), ()))
_NT = (((1,), (1,)), ((), ()))
_TN = (((0,), (0,)), ((), ()))


def _params(*sem):
    return pltpu.CompilerParams(dimension_semantics=sem or None, vmem_limit_bytes=VMEM_LIMIT)


def _tile(n, pref):
    t = n
    while t > pref and t % 2 == 0:
        t //= 2
    return t


def _ep_store(acc, ex, outs):
    outs[0][...] = acc.astype(outs[0].dtype)


def _ep_add(acc, ex, outs):
    outs[0][...] = (acc + ex[0][...]).astype(outs[0].dtype)


def _ep_relu_square(acc, ex, outs):
    a = jnp.maximum(acc, 0.0)
    outs[0][...] = a.astype(outs[0].dtype)
    outs[1][...] = (a * a).astype(outs[1].dtype)


def _ep_times_2a(acc, ex, outs):
    outs[0][...] = (acc * (2.0 * ex[0][...].astype(F32))).astype(outs[0].dtype)


def _mm(name, kind, a, b, M, N, K, *, b_groups=0, out_groups=0, out_dtypes=(F32,), extras=(),
        epilogue=_ep_store, tm=512, tn=1024, tk=2048):
    tm, tn, tk = _tile(M, tm), _tile(N, tn), _tile(K, tk)
    if b_groups:
        cg = b.shape[-1]
        if kind == "nn":
            tn = min(tn, cg)
        else:
            tk = min(tk, cg)
    if out_groups:
        tn = min(tn, N // out_groups)
    grid = (M // tm, N // tn, K // tk)
    nk = grid[2]
    if kind == "nn":
        dims = _NN
        a_spec = pl.BlockSpec((tm, tk), lambda i, j, k: (i, k))
        if b_groups:
            npg = cg // tn
            b_spec = pl.BlockSpec((None, tk, tn), lambda i, j, k: (j // npg, k, j % npg))
        else:
            b_spec = pl.BlockSpec((tk, tn), lambda i, j, k: (k, j))
    elif kind == "nt":
        dims = _NT
        a_spec = pl.BlockSpec((tm, tk), lambda i, j, k: (i, k))
        if b_groups:
            npg = cg // tk
            b_spec = pl.BlockSpec((None, tn, tk), lambda i, j, k: (k // npg, j, k % npg))
        else:
            b_spec = pl.BlockSpec((tn, tk), lambda i, j, k: (j, k))
    else:
        dims = _TN
        a_spec = pl.BlockSpec((tk, tm), lambda i, j, k: (k, i))
        b_spec = pl.BlockSpec((tk, tn), lambda i, j, k: (k, j))
    tile_spec = pl.BlockSpec((tm, tn), lambda i, j, k: (i, j))
    if out_groups:
        ng = (N // out_groups) // tn
        out_spec = pl.BlockSpec((None, tm, tn), lambda i, j, k: (j // ng, i, j % ng))
        out_shape = [jax.ShapeDtypeStruct((out_groups, M, N // out_groups), d) for d in out_dtypes]
    else:
        out_spec = tile_spec
        out_shape = [jax.ShapeDtypeStruct((M, N), d) for d in out_dtypes]
    n_ex, n_out = len(extras), len(out_dtypes)

    def body(a_ref, b_ref, *rest):
        ex, outs, acc = rest[:n_ex], rest[n_ex:n_ex + n_out], rest[-1]
        k = pl.program_id(2)

        @pl.when(k == 0)
        def _():
            acc[...] = jnp.zeros_like(acc)

        acc[...] += lax.dot_general(a_ref[...].astype(BF16), b_ref[...].astype(BF16), dims,
                                    preferred_element_type=F32)

        @pl.when(k == nk - 1)
        def _():
            epilogue(acc[...], ex, outs)

    res = pl.pallas_call(
        body, name=name, grid=grid,
        in_specs=[a_spec, b_spec] + [tile_spec] * n_ex,
        out_specs=[out_spec] * n_out, out_shape=out_shape,
        scratch_shapes=[pltpu.VMEM((tm, tn), F32)],
        compiler_params=_params("parallel", "parallel", "arbitrary"),
    )(a, b, *extras)
    return res[0] if n_out == 1 else res


def _rmsnorm_fwd(name, x, gains):
    T, D = x.shape
    bt = _tile(T, 256)
    n = len(gains)

    def body(x_ref, *rest):
        xv = x_ref[...]
        y = xv * lax.rsqrt(jnp.mean(xv * xv, axis=-1, keepdims=True) + EPS)
        for g_ref, o_ref in zip(rest[:n], rest[n:]):
            o_ref[...] = (y * g_ref[...]).astype(BF16)

    row = pl.BlockSpec((bt, D), lambda i: (i, 0))
    gsp = pl.BlockSpec((1, D), lambda i: (0, 0))
    return pl.pallas_call(
        body, name=name, grid=(T // bt,), in_specs=[row] + [gsp] * n, out_specs=[row] * n,
        out_shape=[jax.ShapeDtypeStruct((T, D), BF16)] * n, compiler_params=_params("parallel"),
    )(x, *gains)


def _rmsnorm_bwd(name, x, g, dn, res):
    T, D = x.shape
    bt = _tile(T, 256)

    def body(x_ref, g_ref, dn_ref, res_ref, dx_ref, dg_ref):
        @pl.when(pl.program_id(0) == 0)
        def _():
            dg_ref[...] = jnp.zeros_like(dg_ref)

        xv, dnv = x_ref[...], dn_ref[...]
        r = lax.rsqrt(jnp.mean(xv * xv, axis=-1, keepdims=True) + EPS)
        xhat = xv * r
        dyg = dnv * g_ref[...]
        dx = r * (dyg - xhat * jnp.mean(dyg * xhat, axis=-1, keepdims=True))
        dx_ref[...] = res_ref[...] + dx
        dg_ref[...] += jnp.sum(dnv * xhat, axis=0, keepdims=True)

    row = pl.BlockSpec((bt, D), lambda i: (i, 0))
    gsp = pl.BlockSpec((1, D), lambda i: (0, 0))
    return pl.pallas_call(
        body, name=name, grid=(T // bt,), in_specs=[row, gsp, row, row], out_specs=[row, gsp],
        out_shape=[jax.ShapeDtypeStruct((T, D), F32), jax.ShapeDtypeStruct((1, D), F32)],
        compiler_params=_params("arbitrary"),
    )(x, g, dn, res)


def _head_post(name, x, D, items):
    T = x.shape[0]
    H = D // HEAD_DIM
    bt = _tile(T, 256)
    gains = [g for _, g in items if g is not None]
    n_it, n_g = len(items), len(gains)

    def body(*refs):
        x_refs, g_refs, o_refs = refs[:n_it], refs[n_it:n_it + n_g], refs[n_it + n_g:]
        gi = 0
        for it, (_, g) in enumerate(items):
            if g is None:
                o_refs[it][...] = x_refs[it][...].astype(BF16)
                continue
            gv = g_refs[gi][...]
            gi += 1
            for h in range(H):
                sl = slice(h * HEAD_DIM, (h + 1) * HEAD_DIM)
                xs = x_refs[it][:, sl]
                r = lax.rsqrt(jnp.mean(xs * xs, axis=-1, keepdims=True) + EPS)
                o_refs[it][:, sl] = (xs * r * gv).astype(BF16)

    in_specs = [pl.BlockSpec((bt, D), functools.partial(lambda cb, i: (i, cb), cb)) for cb, _ in items]
    in_specs += [pl.BlockSpec((1, HEAD_DIM), lambda i: (0, 0))] * n_g
    row = pl.BlockSpec((bt, D), lambda i: (i, 0))
    res = pl.pallas_call(
        body, name=name, grid=(T // bt,), in_specs=in_specs, out_specs=[row] * n_it,
        out_shape=[jax.ShapeDtypeStruct((T, D), BF16)] * n_it, compiler_params=_params("parallel"),
    )(*([x] * n_it), *gains)
    return res


def _head_post_bwd(name, x, D, items):
    T = x.shape[0]
    H = D // HEAD_DIM
    bt = _tile(T, 128)
    gains = [g for _, g, _ in items if g is not None]
    n_it, n_g = len(items), len(gains)

    def body(*refs):
        x_refs, dy_refs = refs[:n_it], refs[n_it:2 * n_it]
        g_refs = refs[2 * n_it:2 * n_it + n_g]
        dx_ref = refs[2 * n_it + n_g]
        dg_refs = refs[2 * n_it + n_g + 1:]

        @pl.when(pl.program_id(0) == 0)
        def _():
            for r in dg_refs:
                r[...] = jnp.zeros_like(r)

        gi = 0
        for it, (_, g, _) in enumerate(items):
            if g is None:
                dx_ref[:, it * D:(it + 1) * D] = dy_refs[it][...].astype(BF16)
                continue
            gv = g_refs[gi][...]
            dg = jnp.zeros((1, HEAD_DIM), F32)
            for h in range(H):
                sl = slice(h * HEAD_DIM, (h + 1) * HEAD_DIM)
                xs, dy = x_refs[it][:, sl], dy_refs[it][:, sl]
                r = lax.rsqrt(jnp.mean(xs * xs, axis=-1, keepdims=True) + EPS)
                xhat = xs * r
                dyg = dy * gv
                dx = r * (dyg - xhat * jnp.mean(dyg * xhat, axis=-1, keepdims=True))
                dx_ref[:, it * D + h * HEAD_DIM:it * D + (h + 1) * HEAD_DIM] = dx.astype(BF16)
                dg = dg + jnp.sum(dy * xhat, axis=0, keepdims=True)
            dg_refs[gi][...] += dg
            gi += 1

    in_specs = [pl.BlockSpec((bt, D), functools.partial(lambda cb, i: (i, cb), cb)) for cb, _, _ in items]
    in_specs += [pl.BlockSpec((bt, D), lambda i: (i, 0))] * n_it
    gsp = pl.BlockSpec((1, HEAD_DIM), lambda i: (0, 0))
    in_specs += [gsp] * n_g
    res = pl.pallas_call(
        body, name=name, grid=(T // bt,), in_specs=in_specs,
        out_specs=[pl.BlockSpec((bt, n_it * D), lambda i: (i, 0))] + [gsp] * n_g,
        out_shape=[jax.ShapeDtypeStruct((T, n_it * D), BF16)] + [jax.ShapeDtypeStruct((1, HEAD_DIM), F32)] * n_g,
        compiler_params=_params("arbitrary"),
    )(*([x] * n_it), *[dy for _, _, dy in items], *gains)
    return res[0], list(res[1:])


def _gates_fwd(fz, bf, H):
    T = fz.shape[0]
    bt = _tile(T, 512)

    def body(fz_ref, bf_ref, c_ref, carry):
        @pl.when(pl.program_id(0) == 0)
        def _():
            carry[...] = jnp.zeros_like(carry)

        z = fz_ref[...] + bf_ref[...]
        logf = jnp.minimum(z, 0.0) - jnp.log(1.0 + jnp.exp(-jnp.abs(z)))
        lane = lax.broadcasted_iota(jnp.int32, (bt, LANES), 1)
        logf = jnp.where(lane < H, logf, 0.0)
        tri = (lax.broadcasted_iota(jnp.int32, (bt, bt), 0) >= lax.broadcasted_iota(jnp.int32, (bt, bt), 1))
        c = jnp.dot(tri.astype(F32), logf, precision=HIGHEST, preferred_element_type=F32) + carry[...]
        c_ref[...] = c
        carry[...] = c[bt - 1:bt, :]

    row = pl.BlockSpec((bt, LANES), lambda i: (i, 0))
    return pl.pallas_call(
        body, name="fox_gates_fwd", grid=(T // bt,),
        in_specs=[row, pl.BlockSpec((1, LANES), lambda i: (0, 0))], out_specs=row,
        out_shape=jax.ShapeDtypeStruct((T, LANES), F32), scratch_shapes=[pltpu.VMEM((1, LANES), F32)],
        compiler_params=_params("arbitrary"),
    )(fz, bf)


def _gates_bwd(dcq, dck, fz, bf, H):
    T = fz.shape[0]
    bt = _tile(T, 512)
    nb = T // bt

    def body(dcq_ref, dck_ref, fz_ref, bf_ref, dfz_ref, dbf_ref, carry):
        @pl.when(pl.program_id(0) == 0)
        def _():
            carry[...] = jnp.zeros_like(carry)
            dbf_ref[...] = jnp.zeros_like(dbf_ref)

        tri = (lax.broadcasted_iota(jnp.int32, (bt, bt), 0) <= lax.broadcasted_iota(jnp.int32, (bt, bt), 1))
        dc = dcq_ref[...] + dck_ref[...]
        dlogf = jnp.dot(tri.astype(F32), dc, precision=HIGHEST, preferred_element_type=F32) + carry[...]
        carry[...] = dlogf[0:1, :]
        z = fz_ref[...] + bf_ref[...]
        lane = lax.broadcasted_iota(jnp.int32, (bt, LANES), 1)
        dfz = jnp.where(lane < H, dlogf / (1.0 + jnp.exp(z)), 0.0)
        dfz_ref[...] = dfz
        dbf_ref[...] += jnp.sum(dfz, axis=0, keepdims=True)

    row = pl.BlockSpec((bt, LANES), lambda i: (nb - 1 - i, 0))
    one = pl.BlockSpec((1, LANES), lambda i: (0, 0))
    return pl.pallas_call(
        body, name="fox_gates_bwd", grid=(nb,), in_specs=[row, row, row, one], out_specs=[row, one],
        out_shape=[jax.ShapeDtypeStruct((T, LANES), F32), jax.ShapeDtypeStruct((1, LANES), F32)],
        scratch_shapes=[pltpu.VMEM((1, LANES), F32)], compiler_params=_params("arbitrary"),
    )(dcq, dck, fz, bf)


def _lane_pick(ref_value, h):
    lane = lax.broadcasted_iota(jnp.int32, (1, LANES), 1)
    return jnp.sum(jnp.where(lane == h, ref_value, 0.0), axis=1, keepdims=True)


def _lane_put(old, h, col):
    lane = lax.broadcasted_iota(jnp.int32, (1, LANES), 1)
    return jnp.where(lane == h, col, old)


def _fox_fwd(q, k, v, ccol, crow, blk):
    T, D = q.shape
    H = D // HEAD_DIM
    nb = T // blk

    def body(q_ref, k_ref, v_ref, ccol_ref, crow_ref, o_ref, lse_ref, m_s, l_s, acc_s):
        qi, h, kj = pl.program_id(0), pl.program_id(1), pl.program_id(2)

        @pl.when(kj == 0)
        def _():
            m_s[...] = jnp.full_like(m_s, NEG)
            l_s[...] = jnp.zeros_like(l_s)
            acc_s[...] = jnp.zeros_like(acc_s)

        @pl.when((kj == 0) & (h == 0))
        def _():
            lse_ref[...] = jnp.zeros_like(lse_ref)

        @pl.when(kj <= qi)
        def _():
            cq = _lane_pick(ccol_ref[...], h)
            ck = crow_ref[pl.ds(h, 1), :]
            s = lax.dot_general(q_ref[...], k_ref[...], _NT, preferred_element_type=F32) * SCALE + (cq - ck)
            row = qi * blk + lax.broadcasted_iota(jnp.int32, (blk, blk), 0)
            col = kj * blk + lax.broadcasted_iota(jnp.int32, (blk, blk), 1)
            s = jnp.where(row >= col, s, NEG)
            m_old = m_s[...]
            m_new = jnp.maximum(m_old, jnp.max(s, axis=1, keepdims=True))
            alpha = jnp.exp(m_old - m_new)
            p = jnp.exp(s - m_new)
            l_s[...] = alpha * l_s[...] + jnp.sum(p, axis=1, keepdims=True)
            acc_s[...] = alpha * acc_s[...] + jnp.dot(p.astype(BF16), v_ref[...], preferred_element_type=F32)
            m_s[...] = m_new

        @pl.when(kj == qi)
        def _():
            o_ref[...] = (acc_s[...] / l_s[...]).astype(BF16)
            lse_ref[...] = _lane_put(lse_ref[...], h, m_s[...] + jnp.log(l_s[...]))

    qsp = pl.BlockSpec((blk, HEAD_DIM), lambda qi, h, kj: (qi, h))
    ksp = pl.BlockSpec((blk, HEAD_DIM), lambda qi, h, kj: (jnp.minimum(kj, qi), h))
    stat = pl.BlockSpec((blk, LANES), lambda qi, h, kj: (qi, 0))
    return pl.pallas_call(
        body, name="fox_fwd", grid=(nb, H, nb),
        in_specs=[qsp, ksp, ksp, stat, pl.BlockSpec((H, blk), lambda qi, h, kj: (0, jnp.minimum(kj, qi)))],
        out_specs=[qsp, stat],
        out_shape=[jax.ShapeDtypeStruct((T, D), BF16), jax.ShapeDtypeStruct((T, LANES), F32)],
        scratch_shapes=[pltpu.VMEM((blk, 1), F32), pltpu.VMEM((blk, 1), F32), pltpu.VMEM((blk, HEAD_DIM), F32)],
        compiler_params=_params("arbitrary", "arbitrary", "arbitrary"),
    )(q, k, v, ccol, crow)


def _head_stat(name, a, b):
    T, D = a.shape
    H = D // HEAD_DIM
    bt = _tile(T, 256)

    def body(a_ref, b_ref, o_ref):
        out = jnp.zeros((bt, LANES), F32)
        for h in range(H):
            sl = slice(h * HEAD_DIM, (h + 1) * HEAD_DIM)
            prod = a_ref[:, sl].astype(F32) * b_ref[:, sl].astype(F32)
            out = _lane_put(out, h, jnp.sum(prod, axis=1, keepdims=True))
        o_ref[...] = out

    row = pl.BlockSpec((bt, D), lambda i: (i, 0))
    return pl.pallas_call(
        body, name=name, grid=(T // bt,), in_specs=[row, row],
        out_specs=pl.BlockSpec((bt, LANES), lambda i: (i, 0)),
        out_shape=jax.ShapeDtypeStruct((T, LANES), F32), compiler_params=_params("parallel"),
    )(a, b)


def _fox_bwd_dq(q, k, v, do, ccol, crow, lse, delta, blk):
    T, D = q.shape
    H = D // HEAD_DIM
    nb = T // blk

    def body(q_ref, k_ref, v_ref, do_ref, ccol_ref, crow_ref, lse_ref, delta_ref, dq_ref, dc_ref, acc_s, dc_s):
        qi, h, kj = pl.program_id(0), pl.program_id(1), pl.program_id(2)

        @pl.when(kj == 0)
        def _():
            acc_s[...] = jnp.zeros_like(acc_s)
            dc_s[...] = jnp.zeros_like(dc_s)

        @pl.when((kj == 0) & (h == 0))
        def _():
            dc_ref[...] = jnp.zeros_like(dc_ref)

        @pl.when(kj <= qi)
        def _():
            cq = _lane_pick(ccol_ref[...], h)
            ck = crow_ref[pl.ds(h, 1), :]
            s = lax.dot_general(q_ref[...], k_ref[...], _NT, preferred_element_type=F32) * SCALE + (cq - ck)
            row = qi * blk + lax.broadcasted_iota(jnp.int32, (blk, blk), 0)
            col = kj * blk + lax.broadcasted_iota(jnp.int32, (blk, blk), 1)
            s = jnp.where(row >= col, s, NEG)
            p = jnp.exp(s - _lane_pick(lse_ref[...], h))
            dp = lax.dot_general(do_ref[...], v_ref[...], _NT, preferred_element_type=F32)
            ds = p * (dp - _lane_pick(delta_ref[...], h))
            acc_s[...] += jnp.dot(ds.astype(BF16), k_ref[...], preferred_element_type=F32)
            dc_s[...] += jnp.sum(ds, axis=1, keepdims=True)

        @pl.when(kj == qi)
        def _():
            dq_ref[...] = acc_s[...] * SCALE
            dc_ref[...] = _lane_put(dc_ref[...], h, dc_s[...])

    qsp = pl.BlockSpec((blk, HEAD_DIM), lambda qi, h, kj: (qi, h))
    ksp = pl.BlockSpec((blk, HEAD_DIM), lambda qi, h, kj: (jnp.minimum(kj, qi), h))
    stat = pl.BlockSpec((blk, LANES), lambda qi, h, kj: (qi, 0))
    return pl.pallas_call(
        body, name="fox_bwd_dq", grid=(nb, H, nb),
        in_specs=[qsp, ksp, ksp, qsp, stat, pl.BlockSpec((H, blk), lambda qi, h, kj: (0, jnp.minimum(kj, qi))),
                  stat, stat],
        out_specs=[qsp, stat],
        out_shape=[jax.ShapeDtypeStruct((T, D), F32), jax.ShapeDtypeStruct((T, LANES), F32)],
        scratch_shapes=[pltpu.VMEM((blk, HEAD_DIM), F32), pltpu.VMEM((blk, 1), F32)],
        compiler_params=_params("arbitrary", "arbitrary", "arbitrary"),
    )(q, k, v, do, ccol, crow, lse, delta)


def _fox_bwd_dkv(q, k, v, do, ccol, crow, lse_row, delta_row, blk):
    T, D = q.shape
    H = D // HEAD_DIM
    nb = T // blk

    def body(q_ref, k_ref, v_ref, do_ref, ccol_ref, crow_ref, lse_ref, delta_ref, dk_ref, dv_ref, dc_ref,
             dk_s, dv_s, dc_s):
        kj, h, qi = pl.program_id(0), pl.program_id(1), pl.program_id(2)

        @pl.when(qi == 0)
        def _():
            dk_s[...] = jnp.zeros_like(dk_s)
            dv_s[...] = jnp.zeros_like(dv_s)
            dc_s[...] = jnp.zeros_like(dc_s)

        @pl.when((qi == 0) & (h == 0))
        def _():
            dc_ref[...] = jnp.zeros_like(dc_ref)

        @pl.when(qi >= kj)
        def _():
            ck = _lane_pick(ccol_ref[...], h)
            cq = crow_ref[pl.ds(h, 1), :]
            st = lax.dot_general(k_ref[...], q_ref[...], _NT, preferred_element_type=F32) * SCALE + (cq - ck)
            key = kj * blk + lax.broadcasted_iota(jnp.int32, (blk, blk), 0)
            qry = qi * blk + lax.broadcasted_iota(jnp.int32, (blk, blk), 1)
            st = jnp.where(qry >= key, st, NEG)
            pt = jnp.exp(st - lse_ref[pl.ds(h, 1), :])
            dpt = lax.dot_general(v_ref[...], do_ref[...], _NT, preferred_element_type=F32)
            dst = pt * (dpt - delta_ref[pl.ds(h, 1), :])
            dv_s[...] += jnp.dot(pt.astype(BF16), do_ref[...], preferred_element_type=F32)
            dk_s[...] += jnp.dot(dst.astype(BF16), q_ref[...], preferred_element_type=F32)
            dc_s[...] -= jnp.sum(dst, axis=1, keepdims=True)

        @pl.when(qi == nb - 1)
        def _():
            dk_ref[...] = dk_s[...] * SCALE
            dv_ref[...] = dv_s[...]
            dc_ref[...] = _lane_put(dc_ref[...], h, dc_s[...])

    ksp = pl.BlockSpec((blk, HEAD_DIM), lambda kj, h, qi: (kj, h))
    qsp = pl.BlockSpec((blk, HEAD_DIM), lambda kj, h, qi: (jnp.maximum(qi, kj), h))
    kstat = pl.BlockSpec((blk, LANES), lambda kj, h, qi: (kj, 0))
    qrow = pl.BlockSpec((H, blk), lambda kj, h, qi: (0, jnp.maximum(qi, kj)))
    return pl.pallas_call(
        body, name="fox_bwd_dkv", grid=(nb, H, nb),
        in_specs=[qsp, ksp, ksp, qsp, kstat, qrow, qrow, qrow],
        out_specs=[ksp, ksp, kstat],
        out_shape=[jax.ShapeDtypeStruct((T, D), F32), jax.ShapeDtypeStruct((T, D), F32),
                   jax.ShapeDtypeStruct((T, LANES), F32)],
        scratch_shapes=[pltpu.VMEM((blk, HEAD_DIM), F32), pltpu.VMEM((blk, HEAD_DIM), F32), pltpu.VMEM((blk, 1), F32)],
        compiler_params=_params("arbitrary", "arbitrary", "arbitrary"),
    )(q, k, v, do, ccol, crow, lse_row, delta_row)


def _rel_index():
    j = np.arange(ROLL_W)
    u = np.where(j < 2 * WIN, j, j - ROLL_W)
    return np.clip(WIN - u, -(CHUNK - 1), REL_CLIP) + (CHUNK - 1)


def _band_bias(g_row, bias_s):
    base = jnp.broadcast_to(g_row, (CHUNK, ROLL_W))
    rowb = lax.broadcasted_iota(jnp.int32, (CHUNK, ROLL_W), 0)
    for bit in range(CHUNK.bit_length() - 1):
        base = jnp.where((rowb >> bit) & 1 == 1, pltpu.roll(base, 1 << bit, 1), base)
    kc = lax.broadcasted_iota(jnp.int32, (CHUNK, 2 * WIN), 1) // CHUNK
    for a in range(N_PREV_CHUNKS):
        part = base if a == 0 else pltpu.roll(base, CHUNK * a, 1)
        valid = (kc >= a) & (kc <= a + N_PREV_CHUNKS)
        bias_s[a * CHUNK:(a + 1) * CHUNK, :] = jnp.where(valid, part[:, :2 * WIN], NEG)


def _chunk_probs(q, kp, kc, bias_s, qi):
    sp = lax.dot_general(q, kp, _NT, preferred_element_type=F32) * SCALE + bias_s[:, :WIN]
    sc = lax.dot_general(q, kc, _NT, preferred_element_type=F32) * SCALE + bias_s[:, WIN:]
    sp = jnp.where(qi > 0, sp, NEG)
    m = jnp.maximum(jnp.max(sp, axis=1, keepdims=True), jnp.max(sc, axis=1, keepdims=True))
    pp, pc = jnp.exp(sp - m), jnp.exp(sc - m)
    inv = 1.0 / (jnp.sum(pp, axis=1, keepdims=True) + jnp.sum(pc, axis=1, keepdims=True))
    return pp * inv, pc * inv


def _chunk_fwd(q, k, v, g_rows):
    T, D = q.shape
    H = D // HEAD_DIM
    nq = T // WIN

    def body(q_ref, kp_ref, kc_ref, vp_ref, vc_ref, g_ref, o_ref, bias_s):
        qi = pl.program_id(1)

        @pl.when(qi == 0)
        def _():
            _band_bias(g_ref[...], bias_s)

        pp, pc = _chunk_probs(q_ref[...], kp_ref[...], kc_ref[...], bias_s, qi)
        o = jnp.dot(pp.astype(BF16), vp_ref[...], preferred_element_type=F32)
        o += jnp.dot(pc.astype(BF16), vc_ref[...], preferred_element_type=F32)
        o_ref[...] = o.astype(BF16)

    cur = pl.BlockSpec((WIN, HEAD_DIM), lambda h, qi: (qi, h))
    prev = pl.BlockSpec((WIN, HEAD_DIM), lambda h, qi: (jnp.maximum(qi - 1, 0), h))
    return pl.pallas_call(
        body, name="chunk_fwd", grid=(H, nq),
        in_specs=[cur, prev, cur, prev, cur, pl.BlockSpec((None, 1, ROLL_W), lambda h, qi: (h, 0, 0))],
        out_specs=cur, out_shape=jax.ShapeDtypeStruct((T, D), BF16),
        scratch_shapes=[pltpu.VMEM((WIN, 2 * WIN), F32)],
        compiler_params=_params("arbitrary", "arbitrary"),
    )(q, k, k, v, v, g_rows)


def _chunk_bwd(q, k, v, g_rows, do, o):
    T, D = q.shape
    H = D // HEAD_DIM
    nq = T // WIN

    def body(q_ref, kp_ref, kc_ref, vp_ref, vc_ref, g_ref, do_ref, o_ref,
             dq_ref, dk_ref, dv_ref, ds_ref, bias_s, ck_s, cv_s):
        qi = pl.program_id(1)

        @pl.when(qi == 0)
        def _():
            _band_bias(g_ref[...], bias_s)
            ds_ref[...] = jnp.zeros_like(ds_ref)
            ck_s[...] = jnp.zeros_like(ck_s)
            cv_s[...] = jnp.zeros_like(cv_s)

        @pl.when(qi < nq)
        def _():
            qv, dov = q_ref[...], do_ref[...]
            pp, pc = _chunk_probs(qv, kp_ref[...], kc_ref[...], bias_s, qi)
            delta = jnp.sum(dov.astype(F32) * o_ref[...].astype(F32), axis=1, keepdims=True)
            dsp = pp * (lax.dot_general(dov, vp_ref[...], _NT, preferred_element_type=F32) - delta)
            dsc = pc * (lax.dot_general(dov, vc_ref[...], _NT, preferred_element_type=F32) - delta)
            dsp16, dsc16 = dsp.astype(BF16), dsc.astype(BF16)
            dq = jnp.dot(dsp16, kp_ref[...], preferred_element_type=F32)
            dq += jnp.dot(dsc16, kc_ref[...], preferred_element_type=F32)
            dq_ref[...] = dq * SCALE
            ds_ref[:, :WIN] += dsp
            ds_ref[:, WIN:] += dsc
            dk_ref[...] = ck_s[...] + lax.dot_general(dsp16, qv, _TN, preferred_element_type=F32) * SCALE
            dv_ref[...] = cv_s[...] + lax.dot_general(pp.astype(BF16), dov, _TN, preferred_element_type=F32)
            ck_s[...] = lax.dot_general(dsc16, qv, _TN, preferred_element_type=F32) * SCALE
            cv_s[...] = lax.dot_general(pc.astype(BF16), dov, _TN, preferred_element_type=F32)

        @pl.when(qi == nq)
        def _():
            dk_ref[...] = ck_s[...]
            dv_ref[...] = cv_s[...]

    cur = pl.BlockSpec((WIN, HEAD_DIM), lambda h, qi: (jnp.minimum(qi, nq - 1), h))
    prev = pl.BlockSpec((WIN, HEAD_DIM), lambda h, qi: (jnp.clip(qi - 1, 0, nq - 2), h))
    late = pl.BlockSpec((WIN, HEAD_DIM), lambda h, qi: (jnp.maximum(qi - 1, 0), h))
    return pl.pallas_call(
        body, name="chunk_bwd", grid=(H, nq + 1),
        in_specs=[cur, prev, cur, prev, cur, pl.BlockSpec((None, 1, ROLL_W), lambda h, qi: (h, 0, 0)), cur, cur],
        out_specs=[cur, late, late, pl.BlockSpec((None, WIN, 2 * WIN), lambda h, qi: (h, 0, 0))],
        out_shape=[jax.ShapeDtypeStruct((T, D), F32)] * 3 + [jax.ShapeDtypeStruct((H, WIN, 2 * WIN), F32)],
        scratch_shapes=[pltpu.VMEM((WIN, 2 * WIN), F32), pltpu.VMEM((WIN, HEAD_DIM), F32),
                        pltpu.VMEM((WIN, HEAD_DIM), F32)],
        compiler_params=_params("arbitrary", "arbitrary"),
    )(q, k, k, v, v, g_rows, do, o)


def _rel_grad(ds_sum, onehot):
    H = ds_sum.shape[0]

    def body(ds_ref, oh_ref, out_ref):
        y = jnp.zeros((CHUNK, ROLL_W), F32)
        pad = jnp.zeros((CHUNK, ROLL_W - 2 * WIN), F32)
        for a in range(N_PREV_CHUNKS):
            part = jnp.concatenate([ds_ref[a * CHUNK:(a + 1) * CHUNK, :], pad], axis=1)
            y = y + (part if a == 0 else pltpu.roll(part, ROLL_W - CHUNK * a, 1))
        rowb = lax.broadcasted_iota(jnp.int32, (CHUNK, ROLL_W), 0)
        for bit in range(CHUNK.bit_length() - 1):
            y = jnp.where((rowb >> bit) & 1 == 1, pltpu.roll(y, ROLL_W - (1 << bit), 1), y)
        diag = jnp.broadcast_to(jnp.sum(y, axis=0, keepdims=True), (8, ROLL_W))
        out_ref[...] = jnp.dot(diag, oh_ref[...], precision=HIGHEST, preferred_element_type=F32)

    return pl.pallas_call(
        body, name="rel_grad", grid=(H,),
        in_specs=[pl.BlockSpec((None, WIN, 2 * WIN), lambda h: (h, 0, 0)),
                  pl.BlockSpec((ROLL_W, N_REL_PAD), lambda h: (0, 0))],
        out_specs=pl.BlockSpec((None, 8, N_REL_PAD), lambda h: (h, 0, 0)),
        out_shape=jax.ShapeDtypeStruct((H, 8, N_REL_PAD), F32), compiler_params=_params("parallel"),
    )(ds_sum, onehot)


def _loss_head(y, target):
    T, D = y.shape
    bt = _tile(T, 256)

    def body(y_ref, t_ref, dy_ref, loss_ref):
        @pl.when(pl.program_id(0) == 0)
        def _():
            loss_ref[...] = jnp.zeros_like(loss_ref)

        err = y_ref[...] - t_ref[...]
        dy_ref[...] = err * (1.0 / D)
        loss_ref[...] += 0.5 * jnp.sum(jnp.mean(err * err, axis=-1, keepdims=True))

    row = pl.BlockSpec((bt, D), lambda i: (i, 0))
    one = pl.BlockSpec((8, LANES), lambda i: (0, 0))
    return pl.pallas_call(
        body, name="loss_head", grid=(T // bt,), in_specs=[row, row], out_specs=[row, one],
        out_shape=[jax.ShapeDtypeStruct((T, D), F32), jax.ShapeDtypeStruct((8, LANES), F32)],
        compiler_params=_params("arbitrary"),
    )(y, target)


def _adamw(name, w, m, v, parts):
    R, C = w.shape
    P = parts.shape[0]
    br = _tile(R, max(8, (128 * 1024) // C)) if R % 8 == 0 else R
    c1 = 1.0 - ADAM_B1 ** ADAM_STEP
    c2 = 1.0 - ADAM_B2 ** ADAM_STEP

    def body(w_ref, m_ref, v_ref, p_ref, g_ref, d_ref, nm_ref, nv_ref):
        g = p_ref[0].astype(F32)
        for i in range(1, P):
            g = g + p_ref[i].astype(F32)
        nm = ADAM_B1 * m_ref[...] + (1.0 - ADAM_B1) * g
        nv = ADAM_B2 * v_ref[...] + (1.0 - ADAM_B2) * (g * g)
        g_ref[...] = g
        nm_ref[...] = nm
        nv_ref[...] = nv
        d_ref[...] = -ADAM_LR * ((nm / c1) / (jnp.sqrt(nv / c2) + ADAM_EPS) + ADAM_WD * w_ref[...])

    row = pl.BlockSpec((br, C), lambda i: (i, 0))
    return pl.pallas_call(
        body, name=name, grid=(R // br,),
        in_specs=[row, row, row, pl.BlockSpec((P, br, C), lambda i: (0, i, 0))], out_specs=[row] * 4,
        out_shape=[jax.ShapeDtypeStruct((R, C), F32)] * 4, compiler_params=_params("parallel"),
    )(w, m, v, parts)


def _sum_parts(name, parts, out_dtype):
    P, R, C = parts.shape
    br = _tile(R, max(8, (256 * 1024) // C)) if R % 8 == 0 else R

    def body(p_ref, o_ref):
        g = p_ref[0].astype(F32)
        for i in range(1, P):
            g = g + p_ref[i].astype(F32)
        o_ref[...] = g.astype(out_dtype)

    return pl.pallas_call(
        body, name=name, grid=(R // br,),
        in_specs=[pl.BlockSpec((P, br, C), lambda i: (0, i, 0))], out_specs=pl.BlockSpec((br, C), lambda i: (i, 0)),
        out_shape=jax.ShapeDtypeStruct((R, C), out_dtype), compiler_params=_params("parallel"),
    )(parts)


def _pair_sum(name, g, recv, core):
    _, R, C = g.shape
    br = _tile(R, max(8, (256 * 1024) // C))

    def body(core_ref, g_ref, r_ref, o_ref):
        o_ref[...] = (g_ref[...] + r_ref[...]).astype(BF16)

    return pl.pallas_call(
        body, name=name,
        grid_spec=pltpu.PrefetchScalarGridSpec(
            num_scalar_prefetch=1, grid=(N_CHIP, R // br),
            in_specs=[pl.BlockSpec((None, br, C), lambda q, i, core_ref: (2 * q + core_ref[0], i, 0)),
                      pl.BlockSpec((None, br, C), lambda q, i, core_ref: (q, i, 0))],
            out_specs=pl.BlockSpec((None, br, C), lambda q, i, core_ref: (q, i, 0))),
        out_shape=jax.ShapeDtypeStruct((N_CHIP, R, C), BF16), compiler_params=_params("parallel", "parallel"),
    )(core, g, recv)


def _position():
    x, y, c = lax.axis_index("x"), lax.axis_index("y"), lax.axis_index("c")
    return x, y, c, [(1 - x, y), (x, 1 - y), (1 - x, 1 - y)]


def _all_gather(name, xs):
    n = len(xs)

    def body(*refs):
        x_refs, out_refs = refs[:n], refs[n:2 * n]
        send_sems, recv_sems, local_sems = refs[2 * n:]
        x, y, c, chips = _position()
        me, sibling = (x, y, c), (x, y, 1 - c)

        def copy(a, k, block, to, src=None):
            px, py, pc = block
            slot = out_refs[a].at[4 * px + 2 * py + pc]
            return pltpu.make_async_remote_copy(
                src_ref=slot if src is None else src, dst_ref=slot,
                send_sem=send_sems.at[a, k], recv_sem=recv_sems.at[a, k], device_id=to, device_id_type=MESH)

        mine = [pltpu.make_async_copy(x_refs[a], out_refs[a].at[4 * x + 2 * y + c], local_sems.at[a]) for a in range(n)]
        for cp in mine:
            cp.start()
        first = []
        for a in range(n):
            first.append(copy(a, 0, me, sibling, src=x_refs[a]))
            first += [copy(a, 1 + j, me, (*chip, c), src=x_refs[a]) for j, chip in enumerate(chips)]
        for cp in first:
            cp.start()
        passed = []
        for j, chip in enumerate(chips):
            for a in range(n):
                copy(a, 1 + j, (*chip, c), me).wait_recv()
                fwd = copy(a, 4 + j, (*chip, c), sibling)
                fwd.start()
                passed.append(fwd)
        for a in range(n):
            copy(a, 0, sibling, me).wait_recv()
            for j, chip in enumerate(chips):
                copy(a, 4 + j, (*chip, 1 - c), me).wait_recv()
        for cp in first + passed:
            cp.wait_send()
        for cp in mine:
            cp.wait()

    return pl.pallas_call(
        body, name=name, in_specs=[ANY] * n, out_specs=[ANY] * n,
        out_shape=[jax.ShapeDtypeStruct((N_DEV,) + x.shape, x.dtype) for x in xs],
        scratch_shapes=[pltpu.SemaphoreType.DMA((n, 7)), pltpu.SemaphoreType.DMA((n, 7)),
                        pltpu.SemaphoreType.DMA((n,))],
    )(*xs)


def _swap_cores(name, gs):
    n = len(gs)

    def body(*refs):
        g_refs, out_refs = refs[:n], refs[n:2 * n]
        send_sems, recv_sems = refs[2 * n:]
        x, y, c, _ = _position()
        copies = [pltpu.make_async_remote_copy(
            src_ref=g_refs[a].at[2 * q + (1 - c)], dst_ref=out_refs[a].at[q],
            send_sem=send_sems.at[a, q], recv_sem=recv_sems.at[a, q], device_id=(x, y, 1 - c), device_id_type=MESH)
            for a in range(n) for q in range(N_CHIP)]
        for cp in copies:
            cp.start()
        for cp in copies:
            cp.wait_recv()
        for cp in copies:
            cp.wait_send()

    return pl.pallas_call(
        body, name=name, in_specs=[ANY] * n, out_specs=[ANY] * n,
        out_shape=[jax.ShapeDtypeStruct((N_CHIP,) + g.shape[1:], g.dtype) for g in gs],
        scratch_shapes=[pltpu.SemaphoreType.DMA((n, N_CHIP)), pltpu.SemaphoreType.DMA((n, N_CHIP))],
    )(*gs)


def _swap_chips(name, ps):
    n = len(ps)

    def body(*refs):
        p_refs, out_refs = refs[:n], refs[n:2 * n]
        send_sems, recv_sems, local_sems = refs[2 * n:]
        x, y, c, chips = _position()
        mine = [pltpu.make_async_copy(p_refs[a].at[2 * x + y], out_refs[a].at[2 * x + y], local_sems.at[a])
                for a in range(n)]
        for cp in mine:
            cp.start()
        copies = [pltpu.make_async_remote_copy(
            src_ref=p_refs[a].at[2 * cx + cy], dst_ref=out_refs[a].at[2 * x + y],
            send_sem=send_sems.at[a, j], recv_sem=recv_sems.at[a, j], device_id=(cx, cy, c), device_id_type=MESH)
            for a in range(n) for j, (cx, cy) in enumerate(chips)]
        for cp in copies:
            cp.start()
        for a in range(n):
            for j, (cx, cy) in enumerate(chips):
                pltpu.make_async_remote_copy(
                    src_ref=p_refs[a].at[2 * x + y], dst_ref=out_refs[a].at[2 * cx + cy],
                    send_sem=send_sems.at[a, j], recv_sem=recv_sems.at[a, j], device_id=(cx, cy, c),
                    device_id_type=MESH).wait_recv()
        for cp in copies:
            cp.wait_send()
        for cp in mine:
            cp.wait()

    return pl.pallas_call(
        body, name=name, in_specs=[ANY] * n, out_specs=[ANY] * n,
        out_shape=[jax.ShapeDtypeStruct(p.shape, p.dtype) for p in ps],
        scratch_shapes=[pltpu.SemaphoreType.DMA((n, 3)), pltpu.SemaphoreType.DMA((n, 3)),
                        pltpu.SemaphoreType.DMA((n,))],
    )(*ps)


def _mlp_fwd(tag, h, g, w1g, w2):
    T, D = h.shape
    F = w2.shape[0]
    (n,) = _rmsnorm_fwd(f"mlp{tag}_norm", h, [g])
    a, hid = _mm(f"mlp{tag}_up", "nn", n, w1g, T, F, D, b_groups=N_DEV, out_dtypes=(BF16, BF16),
                 epilogue=_ep_relu_square)
    out = _mm(f"mlp{tag}_down", "nn", hid, w2, T, D, F, extras=(h,), epilogue=_ep_add)
    return out, (n, a, hid)


def _mlp_bwd(tag, h, g, w1g, w2, saved, dy):
    T, D = h.shape
    F = w2.shape[0]
    n, a, hid = saved
    dw2 = _mm(f"mlp{tag}_dw2", "tn", hid, dy, F, D, T)
    dpre = _mm(f"mlp{tag}_dpre", "nt", dy, w2, T, F, D, out_dtypes=(BF16,), extras=(a,), epilogue=_ep_times_2a)
    dw1 = _mm(f"mlp{tag}_dw1", "tn", n, dpre, D, F, T, out_groups=N_DEV)
    dn = _mm(f"mlp{tag}_dn", "nt", dpre, w1g, T, D, F, b_groups=N_DEV)
    dh, dg = _rmsnorm_bwd(f"mlp{tag}_norm_bwd", h, g, dn, dy)
    return dh, dg, dw1, dw2


def _local_step(x, target, W, fox_blk):
    T, D = x.shape
    H = D // HEAD_DIM
    G = {}

    (ua,) = _rmsnorm_fwd("a_norm", x, [W["a_norm_g"]])
    qkv = _mm("a_qkv", "nn", ua, W["a_w_qkv"], T, 3 * D, D)
    fz = _mm("a_fz", "nn", ua, W["a_w_f"], T, LANES, D)
    qa, ka, va = _head_post("a_heads", qkv, D, [(0, W["a_q_g"]), (1, W["a_k_g"]), (2, None)])
    ccol = _gates_fwd(fz, W["a_b_f"], H)
    crow = ccol[:, :H].T
    oa, lse = _fox_fwd(qa, ka, va, ccol, crow, fox_blk)
    h1 = _mm("a_out", "nn", oa, W["a_w_out"], T, D, D, extras=(x,), epilogue=_ep_add)
    h2, mlp0 = _mlp_fwd("0", h1, W["mlp_norm_g0"], W["mlp_w1_0"], W["mlp_w2_0"])
    nkv, ub = _rmsnorm_fwd("kv_b_norm", h2, [W["kv_norm_g"], W["b_norm_g"]])
    kv = _mm("kv_proj", "nn", nkv, W["kv_w"], T, 2 * D, D, b_groups=N_DEV)
    ks, vs = _head_post("kv_heads", kv, D, [(0, W["kv_k_g"]), (1, None)])
    qb_pre = _mm("b_q", "nn", ub, W["b_w_q"], T, D, D)
    (qb,) = _head_post("b_heads", qb_pre, D, [(0, W["b_q_g"])])
    g_rows = W["b_rel"][:, _rel_index()].reshape(H, 1, ROLL_W)
    ob = _chunk_fwd(qb, ks, vs, g_rows)
    h3 = _mm("b_out", "nn", ob, W["b_w_out"], T, D, D, extras=(h2,), epilogue=_ep_add)
    h4, mlp1 = _mlp_fwd("1", h3, W["mlp_norm_g1"], W["mlp_w1_1"], W["mlp_w2_1"])
    dy, loss = _loss_head(h4, target)

    dh3, G["mlp_norm_g1"], G["mlp_w1_1"], G["mlp_w2_1"] = _mlp_bwd(
        "1", h3, W["mlp_norm_g1"], W["mlp_w1_1"], W["mlp_w2_1"], mlp1, dy)
    dob = _mm("b_out_dx", "nt", dh3, W["b_w_out"], T, D, D, out_dtypes=(BF16,))
    G["b_w_out"] = _mm("b_out_dw", "tn", ob, dh3, D, D, T)
    dqb, dks, dvs, ds_sum = _chunk_bwd(qb, ks, vs, g_rows, dob, ob)
    onehot = (jnp.asarray(_rel_index())[:, None] == jnp.arange(N_REL_PAD)[None, :]).astype(F32)
    G["b_rel"] = _rel_grad(ds_sum, onehot)[:, 0, :]
    dqb_pre, (G["b_q_g"],) = _head_post_bwd("b_heads_bwd", qb_pre, D, [(0, W["b_q_g"], dqb)])
    G["b_w_q"] = _mm("b_q_dw", "tn", ub, dqb_pre, D, D, T)
    dub = _mm("b_q_dx", "nt", dqb_pre, W["b_w_q"], T, D, D)
    dkv, (G["kv_k_g"],) = _head_post_bwd("kv_heads_bwd", kv, D, [(0, W["kv_k_g"], dks), (1, None, dvs)])
    G["kv_w"] = _mm("kv_dw", "tn", nkv, dkv, D, 2 * D, T, out_groups=N_DEV)
    dnkv = _mm("kv_dx", "nt", dkv, W["kv_w"], T, D, 2 * D, b_groups=N_DEV)
    dh2, G["b_norm_g"] = _rmsnorm_bwd("b_norm_bwd", h2, W["b_norm_g"], dub, dh3)
    dh2, G["kv_norm_g"] = _rmsnorm_bwd("kv_norm_bwd", h2, W["kv_norm_g"], dnkv, dh2)
    dh1, G["mlp_norm_g0"], G["mlp_w1_0"], G["mlp_w2_0"] = _mlp_bwd(
        "0", h1, W["mlp_norm_g0"], W["mlp_w1_0"], W["mlp_w2_0"], mlp0, dh2)
    doa = _mm("a_out_dx", "nt", dh1, W["a_w_out"], T, D, D, out_dtypes=(BF16,))
    G["a_w_out"] = _mm("a_out_dw", "tn", oa, dh1, D, D, T)
    delta = _head_stat("fox_delta", doa, oa)
    dqa, dcq = _fox_bwd_dq(qa, ka, va, doa, ccol, crow, lse, delta, fox_blk)
    dka, dva, dck = _fox_bwd_dkv(qa, ka, va, doa, ccol, crow, lse[:, :H].T, delta[:, :H].T, fox_blk)
    dfz, G["a_b_f"] = _gates_bwd(dcq, dck, fz, W["a_b_f"], H)
    dqkv, (G["a_q_g"], G["a_k_g"]) = _head_post_bwd(
        "a_heads_bwd", qkv, D, [(0, W["a_q_g"], dqa), (1, W["a_k_g"], dka), (2, None, dva)])
    G["a_w_qkv"] = _mm("a_qkv_dw", "tn", ua, dqkv, D, 3 * D, T)
    G["a_w_f"] = _mm("a_fz_dw", "tn", ua, dfz, D, LANES, T)
    dua = _mm("a_qkv_dx", "nt", dqkv, W["a_w_qkv"], T, D, 3 * D)
    dua = _mm("a_fz_dx", "nt", dfz, W["a_w_f"], T, D, LANES, extras=(dua,), epilogue=_ep_add)
    dx, G["a_norm_g"] = _rmsnorm_bwd("a_norm_bwd", x, W["a_norm_g"], dua, dh1)
    return loss, dx, G


def _pad_lanes(a):
    return jnp.pad(a, ((0, 0), (0, LANES - a.shape[1])))


def kernel(x, a_norm_g, a_w_in, a_b_f, a_q_g, a_k_g, a_w_out, mlp_norm_g, mlp_w1, mlp_w2, kv_norm_g, kv_w, kv_k_g, b_norm_g, b_w_q, b_q_g, b_rel, b_w_out, loss_target, m_a_norm_g, m_a_w_in, m_a_b_f, m_a_q_g, m_a_k_g, m_a_w_out, m_mlp_norm_g, m_mlp_w1, m_mlp_w2, m_kv_norm_g, m_kv_w, m_kv_k_g, m_b_norm_g, m_b_w_q, m_b_q_g, m_b_rel, m_b_w_out, v_a_norm_g, v_a_w_in, v_a_b_f, v_a_q_g, v_a_k_g, v_a_w_out, v_mlp_norm_g, v_mlp_w1, v_mlp_w2, v_kv_norm_g, v_kv_w, v_kv_k_g, v_b_norm_g, v_b_w_q, v_b_q_g, v_b_rel, v_b_w_out):
    T, D = x.shape[1], x.shape[2]
    H = D // HEAD_DIM
    F = mlp_w2.shape[1] * N_DEV
    me = 4 * lax.axis_index("x") + 2 * lax.axis_index("y") + lax.axis_index("c")
    core = lax.axis_index("c").astype(jnp.int32).reshape(1)
    n_in = a_w_in.shape[2]
    n_rel = b_rel.shape[2]

    small_in = jnp.concatenate([a_norm_g.reshape(1, -1), b_rel.reshape(1, -1)], axis=1)
    local = [a_w_in[0], a_w_out[0], mlp_w1[0], mlp_w1[1], mlp_w2[0], mlp_w2[1], kv_w, b_w_q[0], b_w_out[0]]
    gathered = _all_gather("gather_weights", [w.astype(BF16) for w in local] + [small_in])
    g_in, g_aout, g_w1_0, g_w1_1, g_w2_0, g_w2_1, g_kv, g_bq, g_bout, g_small = gathered
    w_in = jnp.transpose(g_in, (1, 0, 2)).reshape(D, N_DEV * n_in)
    rel_full = jnp.transpose(g_small[:, 0, D // N_DEV:].reshape(N_DEV, H, n_rel), (1, 0, 2)).reshape(H, N_DEV * n_rel)
    W = {
        "a_norm_g": g_small[:, 0, :D // N_DEV].reshape(1, D),
        "a_w_qkv": w_in[:, :3 * D], "a_w_f": _pad_lanes(w_in[:, 3 * D:]),
        "a_b_f": _pad_lanes(a_b_f), "a_q_g": a_q_g, "a_k_g": a_k_g,
        "a_w_out": g_aout.reshape(D, D),
        "mlp_norm_g0": mlp_norm_g[0:1], "mlp_norm_g1": mlp_norm_g[1:2],
        "mlp_w1_0": g_w1_0, "mlp_w1_1": g_w1_1, "mlp_w2_0": g_w2_0.reshape(F, D), "mlp_w2_1": g_w2_1.reshape(F, D),
        "kv_norm_g": kv_norm_g.reshape(1, D), "kv_w": g_kv, "kv_k_g": kv_k_g.reshape(1, HEAD_DIM),
        "b_norm_g": b_norm_g, "b_w_q": g_bq.reshape(D, D), "b_q_g": b_q_g, "b_rel": rel_full,
        "b_w_out": g_bout.reshape(D, D),
    }

    loss_tile, dx, G = _local_step(x[0], loss_target[0], W, min(T, 512))
    loss = lax.psum(loss_tile[0, 0], ("x", "y", "c"))

    dw_in = jnp.concatenate([G["a_w_qkv"], G["a_w_f"][:, :H]], axis=1)
    dw_in = jnp.transpose(dw_in.reshape(D, N_DEV, n_in), (1, 0, 2))
    big = [dw_in, G["a_w_out"].reshape(N_DEV, D // N_DEV, D), G["mlp_w1_0"], G["mlp_w1_1"],
           G["mlp_w2_0"].reshape(N_DEV, F // N_DEV, D), G["mlp_w2_1"].reshape(N_DEV, F // N_DEV, D),
           G["kv_w"], G["b_w_q"].reshape(N_DEV, D // N_DEV, D), G["b_w_out"].reshape(N_DEV, D // N_DEV, D)]
    names = ["a_w_in", "a_w_out", "mlp_w1_0", "mlp_w1_1", "mlp_w2_0", "mlp_w2_1", "kv_w", "b_w_q", "b_w_out"]
    from_sibling = _swap_cores("reduce_cores", big)
    pair = [_pair_sum(f"pair_sum_{nm}", g, r, core) for nm, g, r in zip(names, big, from_sibling)]
    from_chips = _swap_chips("reduce_chips", pair)
    moments = {"a_w_in": (m_a_w_in[0], v_a_w_in[0]), "a_w_out": (m_a_w_out[0], v_a_w_out[0]),
               "mlp_w1_0": (m_mlp_w1[0], v_mlp_w1[0]), "mlp_w1_1": (m_mlp_w1[1], v_mlp_w1[1]),
               "mlp_w2_0": (m_mlp_w2[0], v_mlp_w2[0]), "mlp_w2_1": (m_mlp_w2[1], v_mlp_w2[1]),
               "kv_w": (m_kv_w, v_kv_w), "b_w_q": (m_b_w_q[0], v_b_w_q[0]), "b_w_out": (m_b_w_out[0], v_b_w_out[0])}
    upd = {nm: _adamw(f"adamw_{nm}", w, *moments[nm], parts) for nm, w, parts in zip(names, local, from_chips)}

    rows = lambda a: a.reshape(-1, LANES)
    packed = [rows(G["a_norm_g"]), G["a_b_f"], G["a_q_g"], G["a_k_g"], rows(G["mlp_norm_g0"]), rows(G["mlp_norm_g1"]),
              rows(G["kv_norm_g"]), G["kv_k_g"], rows(G["b_norm_g"]), G["b_q_g"], rows(G["b_rel"])]
    sizes = [p.shape[0] for p in packed]
    (all_small,) = _all_gather("gather_small_grads", [jnp.concatenate(packed, axis=0)])
    total = _sum_parts("sum_small_grads", all_small, F32)
    offs = np.concatenate([[0], np.cumsum(sizes)])
    s = [total[offs[i]:offs[i + 1]] for i in range(len(sizes))]
    g_a_norm = lax.dynamic_slice(s[0].reshape(1, D), (0, me * (D // N_DEV)), (1, D // N_DEV))
    g_rel = lax.dynamic_slice(s[10].reshape(H, N_REL_PAD), (0, me * n_rel), (H, n_rel))
    small = {
        "a_norm_g": (a_norm_g, m_a_norm_g, v_a_norm_g, g_a_norm),
        "a_b_f": (a_b_f, m_a_b_f, v_a_b_f, s[1][:, :H]),
        "a_q_g": (a_q_g, m_a_q_g, v_a_q_g, s[2]),
        "a_k_g": (a_k_g, m_a_k_g, v_a_k_g, s[3]),
        "mlp_norm_g": (mlp_norm_g, m_mlp_norm_g, v_mlp_norm_g,
                       jnp.concatenate([s[4].reshape(1, D), s[5].reshape(1, D)], axis=0)),
        "kv_norm_g": (kv_norm_g.reshape(1, D), m_kv_norm_g.reshape(1, D), v_kv_norm_g.reshape(1, D), s[6].reshape(1, D)),
        "kv_k_g": (kv_k_g.reshape(1, HEAD_DIM), m_kv_k_g.reshape(1, HEAD_DIM), v_kv_k_g.reshape(1, HEAD_DIM), s[7]),
        "b_norm_g": (b_norm_g, m_b_norm_g, v_b_norm_g, s[8].reshape(1, D)),
        "b_q_g": (b_q_g, m_b_q_g, v_b_q_g, s[9]),
        "b_rel": (b_rel[0], m_b_rel[0], v_b_rel[0], g_rel),
    }
    for nm, (w, m, v, g) in small.items():
        upd[nm] = _adamw(f"adamw_{nm}", w, m, v, g[None])

    def stacked(nm, shape):
        return [jnp.stack([a, b]).reshape(shape) for a, b in zip(upd[nm + "_0"], upd[nm + "_1"])]

    out = {
        "a_norm_g": upd["a_norm_g"], "a_w_in": [a[None] for a in upd["a_w_in"]], "a_b_f": upd["a_b_f"],
        "a_q_g": upd["a_q_g"], "a_k_g": upd["a_k_g"], "a_w_out": [a[None] for a in upd["a_w_out"]],
        "mlp_norm_g": upd["mlp_norm_g"], "mlp_w1": stacked("mlp_w1", mlp_w1.shape),
        "mlp_w2": stacked("mlp_w2", mlp_w2.shape),
        "kv_norm_g": [a.reshape(D) for a in upd["kv_norm_g"]], "kv_w": upd["kv_w"],
        "kv_k_g": [a.reshape(HEAD_DIM) for a in upd["kv_k_g"]], "b_norm_g": upd["b_norm_g"],
        "b_w_q": [a[None] for a in upd["b_w_q"]], "b_q_g": upd["b_q_g"], "b_rel": [a[None] for a in upd["b_rel"]],
        "b_w_out": [a[None] for a in upd["b_w_out"]],
    }
    order = ["a_norm_g", "a_w_in", "a_b_f", "a_q_g", "a_k_g", "a_w_out", "mlp_norm_g", "mlp_w1", "mlp_w2",
             "kv_norm_g", "kv_w", "kv_k_g", "b_norm_g", "b_w_q", "b_q_g", "b_rel", "b_w_out"]
    return (loss, dx[None], *[out[n][0] for n in order], *[out[n][1] for n in order],
            *[out[n][2] for n in order], *[out[n][3] for n in order])
```

```python
import functools

import numpy as np
import jax
import jax.numpy as jnp
from jax import lax
from jax.experimental import pallas as pl
from jax.experimental.pallas import tpu as pltpu

F32 = jnp.float32
BF16 = jnp.bfloat16
HIGHEST = lax.Precision.HIGHEST
MESH = pl.DeviceIdType.MESH
ANY = pl.BlockSpec(memory_space=pl.ANY)

N_DEV = 8
N_CHIP = 4
LANES = 128
HEAD_DIM = 128
CHUNK = 64
N_PREV_CHUNKS = 8
REL_CLIP = 256
N_REL = REL_CLIP + CHUNK
N_REL_PAD = 384
WIN = N_PREV_CHUNKS * CHUNK
ROLL_W = 4 * WIN
EPS = 1e-6
NEG = -1e30
SCALE = HEAD_DIM ** -0.5
ADAM_LR, ADAM_B1, ADAM_B2, ADAM_EPS, ADAM_WD, ADAM_STEP = 0.001, 0.9, 0.999, 1e-08, 0.01, 10
V7X_VMEM_BYTES = 64 * 1024 * 1024
VMEM_LIMIT = V7X_VMEM_BYTES * 3 // 4

_NN = (((1,), (0,)), ((), ()))
_NT = (((1,), (1,)), ((), ()))
_TN = (((0,), (0,)), ((), ()))


def _params(*sem):
    return pltpu.CompilerParams(dimension_semantics=sem or None, vmem_limit_bytes=VMEM_LIMIT)


def _tile(n, pref):
    t = n
    while t > pref and t % 2 == 0:
        t //= 2
    return t


def _ep_store(acc, ex, outs):
    outs[0][...] = acc.astype(outs[0].dtype)


def _ep_add(acc, ex, outs):
    outs[0][...] = (acc + ex[0][...]).astype(outs[0].dtype)


def _ep_relu_square(acc, ex, outs):
    a = jnp.maximum(acc, 0.0)
    outs[0][...] = a.astype(outs[0].dtype)
    outs[1][...] = (a * a).astype(outs[1].dtype)


def _ep_times_2a(acc, ex, outs):
    outs[0][...] = (acc * (2.0 * ex[0][...].astype(F32))).astype(outs[0].dtype)


def _mm(name, kind, a, b, M, N, K, *, b_groups=0, out_groups=0, out_dtypes=(F32,), extras=(),
        epilogue=_ep_store, tm=512, tn=1024, tk=2048):
    tm, tn, tk = _tile(M, tm), _tile(N, tn), _tile(K, tk)
    if b_groups:
        cg = b.shape[-1]
        if kind == "nn":
            tn = min(tn, cg)
        else:
            tk = min(tk, cg)
    if out_groups:
        tn = min(tn, N // out_groups)
    grid = (M // tm, N // tn, K // tk)
    nk = grid[2]
    if kind == "nn":
        dims = _NN
        a_spec = pl.BlockSpec((tm, tk), lambda i, j, k: (i, k))
        if b_groups:
            npg = cg // tn
            b_spec = pl.BlockSpec((None, tk, tn), lambda i, j, k: (j // npg, k, j % npg))
        else:
            b_spec = pl.BlockSpec((tk, tn), lambda i, j, k: (k, j))
    elif kind == "nt":
        dims = _NT
        a_spec = pl.BlockSpec((tm, tk), lambda i, j, k: (i, k))
        if b_groups:
            npg = cg // tk
            b_spec = pl.BlockSpec((None, tn, tk), lambda i, j, k: (k // npg, j, k % npg))
        else:
            b_spec = pl.BlockSpec((tn, tk), lambda i, j, k: (j, k))
    else:
        dims = _TN
        a_spec = pl.BlockSpec((tk, tm), lambda i, j, k: (k, i))
        b_spec = pl.BlockSpec((tk, tn), lambda i, j, k: (k, j))
    tile_spec = pl.BlockSpec((tm, tn), lambda i, j, k: (i, j))
    if out_groups:
        ng = (N // out_groups) // tn
        out_spec = pl.BlockSpec((None, tm, tn), lambda i, j, k: (j // ng, i, j % ng))
        out_shape = [jax.ShapeDtypeStruct((out_groups, M, N // out_groups), d) for d in out_dtypes]
    else:
        out_spec = tile_spec
        out_shape = [jax.ShapeDtypeStruct((M, N), d) for d in out_dtypes]
    n_ex, n_out = len(extras), len(out_dtypes)

    def body(a_ref, b_ref, *rest):
        ex, outs, acc = rest[:n_ex], rest[n_ex:n_ex + n_out], rest[-1]
        k = pl.program_id(2)

        @pl.when(k == 0)
        def _():
            acc[...] = jnp.zeros_like(acc)

        acc[...] += lax.dot_general(a_ref[...].astype(BF16), b_ref[...].astype(BF16), dims,
                                    preferred_element_type=F32)

        @pl.when(k == nk - 1)
        def _():
            epilogue(acc[...], ex, outs)

    res = pl.pallas_call(
        body, name=name, grid=grid,
        in_specs=[a_spec, b_spec] + [tile_spec] * n_ex,
        out_specs=[out_spec] * n_out, out_shape=out_shape,
        scratch_shapes=[pltpu.VMEM((tm, tn), F32)],
        compiler_params=_params("parallel", "parallel", "arbitrary"),
    )(a, b, *extras)
    return res[0] if n_out == 1 else res


def _rmsnorm_fwd(name, x, gains):
    T, D = x.shape
    bt = _tile(T, 256)
    n = len(gains)

    def body(x_ref, *rest):
        xv = x_ref[...]
        y = xv * lax.rsqrt(jnp.mean(xv * xv, axis=-1, keepdims=True) + EPS)
        for g_ref, o_ref in zip(rest[:n], rest[n:]):
            o_ref[...] = (y * g_ref[...]).astype(BF16)

    row = pl.BlockSpec((bt, D), lambda i: (i, 0))
    gsp = pl.BlockSpec((1, D), lambda i: (0, 0))
    return pl.pallas_call(
        body, name=name, grid=(T // bt,), in_specs=[row] + [gsp] * n, out_specs=[row] * n,
        out_shape=[jax.ShapeDtypeStruct((T, D), BF16)] * n, compiler_params=_params("parallel"),
    )(x, *gains)


def _rmsnorm_bwd(name, x, g, dn, res):
    T, D = x.shape
    bt = _tile(T, 256)

    def body(x_ref, g_ref, dn_ref, res_ref, dx_ref, dg_ref):
        @pl.when(pl.program_id(0) == 0)
        def _():
            dg_ref[...] = jnp.zeros_like(dg_ref)

        xv, dnv = x_ref[...], dn_ref[...]
        r = lax.rsqrt(jnp.mean(xv * xv, axis=-1, keepdims=True) + EPS)
        xhat = xv * r
        dyg = dnv * g_ref[...]
        dx = r * (dyg - xhat * jnp.mean(dyg * xhat, axis=-1, keepdims=True))
        dx_ref[...] = res_ref[...] + dx
        dg_ref[...] += jnp.sum(dnv * xhat, axis=0, keepdims=True)

    row = pl.BlockSpec((bt, D), lambda i: (i, 0))
    gsp = pl.BlockSpec((1, D), lambda i: (0, 0))
    return pl.pallas_call(
        body, name=name, grid=(T // bt,), in_specs=[row, gsp, row, row], out_specs=[row, gsp],
        out_shape=[jax.ShapeDtypeStruct((T, D), F32), jax.ShapeDtypeStruct((1, D), F32)],
        compiler_params=_params("arbitrary"),
    )(x, g, dn, res)


def _head_post(name, x, D, items):
    T = x.shape[0]
    H = D // HEAD_DIM
    bt = _tile(T, 256)
    gains = [g for _, g in items if g is not None]
    n_it, n_g = len(items), len(gains)

    def body(*refs):
        x_refs, g_refs, o_refs = refs[:n_it], refs[n_it:n_it + n_g], refs[n_it + n_g:]
        gi = 0
        for it, (_, g) in enumerate(items):
            if g is None:
                o_refs[it][...] = x_refs[it][...].astype(BF16)
                continue
            gv = g_refs[gi][...]
            gi += 1
            for h in range(H):
                sl = slice(h * HEAD_DIM, (h + 1) * HEAD_DIM)
                xs = x_refs[it][:, sl]
                r = lax.rsqrt(jnp.mean(xs * xs, axis=-1, keepdims=True) + EPS)
                o_refs[it][:, sl] = (xs * r * gv).astype(BF16)

    in_specs = [pl.BlockSpec((bt, D), functools.partial(lambda cb, i: (i, cb), cb)) for cb, _ in items]
    in_specs += [pl.BlockSpec((1, HEAD_DIM), lambda i: (0, 0))] * n_g
    row = pl.BlockSpec((bt, D), lambda i: (i, 0))
    res = pl.pallas_call(
        body, name=name, grid=(T // bt,), in_specs=in_specs, out_specs=[row] * n_it,
        out_shape=[jax.ShapeDtypeStruct((T, D), BF16)] * n_it, compiler_params=_params("parallel"),
    )(*([x] * n_it), *gains)
    return res


def _head_post_bwd(name, x, D, items):
    T = x.shape[0]
    H = D // HEAD_DIM
    bt = _tile(T, 128)
    gains = [g for _, g, _ in items if g is not None]
    n_it, n_g = len(items), len(gains)

    def body(*refs):
        x_refs, dy_refs = refs[:n_it], refs[n_it:2 * n_it]
        g_refs = refs[2 * n_it:2 * n_it + n_g]
        dx_ref = refs[2 * n_it + n_g]
        dg_refs = refs[2 * n_it + n_g + 1:]

        @pl.when(pl.program_id(0) == 0)
        def _():
            for r in dg_refs:
                r[...] = jnp.zeros_like(r)

        gi = 0
        for it, (_, g, _) in enumerate(items):
            if g is None:
                dx_ref[:, it * D:(it + 1) * D] = dy_refs[it][...].astype(BF16)
                continue
            gv = g_refs[gi][...]
            dg = jnp.zeros((1, HEAD_DIM), F32)
            for h in range(H):
                sl = slice(h * HEAD_DIM, (h + 1) * HEAD_DIM)
                xs, dy = x_refs[it][:, sl], dy_refs[it][:, sl]
                r = lax.rsqrt(jnp.mean(xs * xs, axis=-1, keepdims=True) + EPS)
                xhat = xs * r
                dyg = dy * gv
                dx = r * (dyg - xhat * jnp.mean(dyg * xhat, axis=-1, keepdims=True))
                dx_ref[:, it * D + h * HEAD_DIM:it * D + (h + 1) * HEAD_DIM] = dx.astype(BF16)
                dg = dg + jnp.sum(dy * xhat, axis=0, keepdims=True)
            dg_refs[gi][...] += dg
            gi += 1

    in_specs = [pl.BlockSpec((bt, D), functools.partial(lambda cb, i: (i, cb), cb)) for cb, _, _ in items]
    in_specs += [pl.BlockSpec((bt, D), lambda i: (i, 0))] * n_it
    gsp = pl.BlockSpec((1, HEAD_DIM), lambda i: (0, 0))
    in_specs += [gsp] * n_g
    res = pl.pallas_call(
        body, name=name, grid=(T // bt,), in_specs=in_specs,
        out_specs=[pl.BlockSpec((bt, n_it * D), lambda i: (i, 0))] + [gsp] * n_g,
        out_shape=[jax.ShapeDtypeStruct((T, n_it * D), BF16)] + [jax.ShapeDtypeStruct((1, HEAD_DIM), F32)] * n_g,
        compiler_params=_params("arbitrary"),
    )(*([x] * n_it), *[dy for _, _, dy in items], *gains)
    return res[0], list(res[1:])


def _gates_fwd(fz, bf, H):
    T = fz.shape[0]
    bt = _tile(T, 512)

    def body(fz_ref, bf_ref, c_ref, carry):
        @pl.when(pl.program_id(0) == 0)
        def _():
            carry[...] = jnp.zeros_like(carry)

        z = fz_ref[...] + bf_ref[...]
        logf = jnp.minimum(z, 0.0) - jnp.log(1.0 + jnp.exp(-jnp.abs(z)))
        lane = lax.broadcasted_iota(jnp.int32, (bt, LANES), 1)
        logf = jnp.where(lane < H, logf, 0.0)
        tri = (lax.broadcasted_iota(jnp.int32, (bt, bt), 0) >= lax.broadcasted_iota(jnp.int32, (bt, bt), 1))
        c = jnp.dot(tri.astype(F32), logf, precision=HIGHEST, preferred_element_type=F32) + carry[...]
        c_ref[...] = c
        carry[...] = c[bt - 1:bt, :]

    row = pl.BlockSpec((bt, LANES), lambda i: (i, 0))
    return pl.pallas_call(
        body, name="fox_gates_fwd", grid=(T // bt,),
        in_specs=[row, pl.BlockSpec((1, LANES), lambda i: (0, 0))], out_specs=row,
        out_shape=jax.ShapeDtypeStruct((T, LANES), F32), scratch_shapes=[pltpu.VMEM((1, LANES), F32)],
        compiler_params=_params("arbitrary"),
    )(fz, bf)


def _gates_bwd(dcq, dck, fz, bf, H):
    T = fz.shape[0]
    bt = _tile(T, 512)
    nb = T // bt

    def body(dcq_ref, dck_ref, fz_ref, bf_ref, dfz_ref, dbf_ref, carry):
        @pl.when(pl.program_id(0) == 0)
        def _():
            carry[...] = jnp.zeros_like(carry)
            dbf_ref[...] = jnp.zeros_like(dbf_ref)

        tri = (lax.broadcasted_iota(jnp.int32, (bt, bt), 0) <= lax.broadcasted_iota(jnp.int32, (bt, bt), 1))
        dc = dcq_ref[...] + dck_ref[...]
        dlogf = jnp.dot(tri.astype(F32), dc, precision=HIGHEST, preferred_element_type=F32) + carry[...]
        carry[...] = dlogf[0:1, :]
        z = fz_ref[...] + bf_ref[...]
        lane = lax.broadcasted_iota(jnp.int32, (bt, LANES), 1)
        dfz = jnp.where(lane < H, dlogf / (1.0 + jnp.exp(z)), 0.0)
        dfz_ref[...] = dfz
        dbf_ref[...] += jnp.sum(dfz, axis=0, keepdims=True)

    row = pl.BlockSpec((bt, LANES), lambda i: (nb - 1 - i, 0))
    one = pl.BlockSpec((1, LANES), lambda i: (0, 0))
    return pl.pallas_call(
        body, name="fox_gates_bwd", grid=(nb,), in_specs=[row, row, row, one], out_specs=[row, one],
        out_shape=[jax.ShapeDtypeStruct((T, LANES), F32), jax.ShapeDtypeStruct((1, LANES), F32)],
        scratch_shapes=[pltpu.VMEM((1, LANES), F32)], compiler_params=_params("arbitrary"),
    )(dcq, dck, fz, bf)


def _lane_pick(ref_value, h):
    lane = lax.broadcasted_iota(jnp.int32, (1, LANES), 1)
    return jnp.sum(jnp.where(lane == h, ref_value, 0.0), axis=1, keepdims=True)


def _lane_put(old, h, col):
    lane = lax.broadcasted_iota(jnp.int32, (1, LANES), 1)
    return jnp.where(lane == h, col, old)


FOX_HEADS = 4


def _causal(rows, cols, row0=0, transposed=False):
    row = row0 + lax.broadcasted_iota(jnp.int32, (rows, cols), 0)
    col = lax.broadcasted_iota(jnp.int32, (rows, cols), 1)
    return col >= row if transposed else row >= col


def _fox_fwd(q, k, v, ccol, crow, blk):
    T, D = q.shape
    H = D // HEAD_DIM
    nb = T // blk

    def body(q_ref, k_ref, v_ref, ccol_ref, crow_ref, o_ref, lse_ref, m_s, l_s, acc_s):
        qi, hg, kj = pl.program_id(0), pl.program_id(1), pl.program_id(2)

        @pl.when(kj == 0)
        def _():
            m_s[...] = jnp.full_like(m_s, NEG)
            l_s[...] = jnp.zeros_like(l_s)
            acc_s[...] = jnp.zeros_like(acc_s)

        @pl.when((kj == 0) & (hg == 0))
        def _():
            lse_ref[...] = jnp.zeros_like(lse_ref)

        def step(diagonal):
            heads = range(FOX_HEADS)
            sls = [slice(hh * HEAD_DIM, (hh + 1) * HEAD_DIM) for hh in heads]
            m_old = [m_s[hh] for hh in heads]
            l_old = [l_s[hh] for hh in heads]
            acc_old = [acc_s[:, sl] for sl in sls]
            s = []
            for hh in heads:
                h = hg * FOX_HEADS + hh
                bias = _lane_pick(ccol_ref[...], h) - crow_ref[pl.ds(h, 1), :]
                sc = lax.dot_general(q_ref[:, sls[hh]], k_ref[:, sls[hh]], _NT, preferred_element_type=F32) * SCALE + bias
                s.append(jnp.where(_causal(blk, blk), sc, NEG) if diagonal else sc)
            m_new = [jnp.maximum(m_old[hh], jnp.max(s[hh], axis=1, keepdims=True)) for hh in heads]
            p = [jnp.exp(s[hh] - m_new[hh]) for hh in heads]
            alpha = [jnp.exp(m_old[hh] - m_new[hh]) for hh in heads]
            pv = [jnp.dot(p[hh].astype(BF16), v_ref[:, sls[hh]], preferred_element_type=F32) for hh in heads]
            for hh in heads:
                l_s[hh] = alpha[hh] * l_old[hh] + jnp.sum(p[hh], axis=1, keepdims=True)
                acc_s[:, sls[hh]] = alpha[hh] * acc_old[hh] + pv[hh]
                m_s[hh] = m_new[hh]

        @pl.when(kj < qi)
        def _():
            step(False)

        @pl.when(kj == qi)
        def _():
            step(True)
            for hh in range(FOX_HEADS):
                sl = slice(hh * HEAD_DIM, (hh + 1) * HEAD_DIM)
                o_ref[:, sl] = (acc_s[:, sl] / l_s[hh]).astype(BF16)
                lse_ref[...] = _lane_put(lse_ref[...], hg * FOX_HEADS + hh, m_s[hh] + jnp.log(l_s[hh]))

    wide = FOX_HEADS * HEAD_DIM
    qsp = pl.BlockSpec((blk, wide), lambda qi, h, kj: (qi, h))
    ksp = pl.BlockSpec((blk, wide), lambda qi, h, kj: (jnp.minimum(kj, qi), h))
    stat = pl.BlockSpec((blk, LANES), lambda qi, h, kj: (qi, 0))
    return pl.pallas_call(
        body, name="fox_fwd", grid=(nb, H // FOX_HEADS, nb),
        in_specs=[qsp, ksp, ksp, stat, pl.BlockSpec((H, blk), lambda qi, h, kj: (0, jnp.minimum(kj, qi)))],
        out_specs=[qsp, stat],
        out_shape=[jax.ShapeDtypeStruct((T, D), BF16), jax.ShapeDtypeStruct((T, LANES), F32)],
        scratch_shapes=[pltpu.VMEM((FOX_HEADS, blk, 1), F32), pltpu.VMEM((FOX_HEADS, blk, 1), F32),
                        pltpu.VMEM((blk, wide), F32)],
        compiler_params=_params("arbitrary", "arbitrary", "arbitrary"),
    )(q, k, v, ccol, crow)


def _head_stat(name, a, b):
    T, D = a.shape
    H = D // HEAD_DIM
    bt = _tile(T, 256)

    def body(a_ref, b_ref, o_ref):
        out = jnp.zeros((bt, LANES), F32)
        for h in range(H):
            sl = slice(h * HEAD_DIM, (h + 1) * HEAD_DIM)
            prod = a_ref[:, sl].astype(F32) * b_ref[:, sl].astype(F32)
            out = _lane_put(out, h, jnp.sum(prod, axis=1, keepdims=True))
        o_ref[...] = out

    row = pl.BlockSpec((bt, D), lambda i: (i, 0))
    return pl.pallas_call(
        body, name=name, grid=(T // bt,), in_specs=[row, row],
        out_specs=pl.BlockSpec((bt, LANES), lambda i: (i, 0)),
        out_shape=jax.ShapeDtypeStruct((T, LANES), F32), compiler_params=_params("parallel"),
    )(a, b)


def _fox_bwd_dq(q, k, v, do, ccol, crow, lse, delta, blk):
    T, D = q.shape
    H = D // HEAD_DIM
    nb = T // blk

    def body(q_ref, k_ref, v_ref, do_ref, ccol_ref, crow_ref, lse_ref, delta_ref, dq_ref, dc_ref, acc_s, dc_s):
        qi, hg, kj = pl.program_id(0), pl.program_id(1), pl.program_id(2)

        @pl.when(kj == 0)
        def _():
            acc_s[...] = jnp.zeros_like(acc_s)
            dc_s[...] = jnp.zeros_like(dc_s)

        @pl.when((kj == 0) & (hg == 0))
        def _():
            dc_ref[...] = jnp.zeros_like(dc_ref)

        def step(diagonal):
            heads = range(FOX_HEADS)
            sls = [slice(hh * HEAD_DIM, (hh + 1) * HEAD_DIM) for hh in heads]
            p, dp = [], []
            for hh in heads:
                h = hg * FOX_HEADS + hh
                bias = _lane_pick(ccol_ref[...], h) - crow_ref[pl.ds(h, 1), :]
                sc = lax.dot_general(q_ref[:, sls[hh]], k_ref[:, sls[hh]], _NT, preferred_element_type=F32) * SCALE + bias
                if diagonal:
                    sc = jnp.where(_causal(blk, blk), sc, NEG)
                p.append(jnp.exp(sc - _lane_pick(lse_ref[...], h)))
                dp.append(lax.dot_general(do_ref[:, sls[hh]], v_ref[:, sls[hh]], _NT, preferred_element_type=F32)
                          - _lane_pick(delta_ref[...], h))
            ds = [p[hh] * dp[hh] for hh in heads]
            for hh in heads:
                acc_s[:, sls[hh]] += jnp.dot(ds[hh].astype(BF16), k_ref[:, sls[hh]], preferred_element_type=F32)
                dc_s[hh] += jnp.sum(ds[hh], axis=1, keepdims=True)

        @pl.when(kj < qi)
        def _():
            step(False)

        @pl.when(kj == qi)
        def _():
            step(True)
            dq_ref[...] = acc_s[...] * SCALE
            for hh in range(FOX_HEADS):
                dc_ref[...] = _lane_put(dc_ref[...], hg * FOX_HEADS + hh, dc_s[hh])

    wide = FOX_HEADS * HEAD_DIM
    qsp = pl.BlockSpec((blk, wide), lambda qi, h, kj: (qi, h))
    ksp = pl.BlockSpec((blk, wide), lambda qi, h, kj: (jnp.minimum(kj, qi), h))
    stat = pl.BlockSpec((blk, LANES), lambda qi, h, kj: (qi, 0))
    return pl.pallas_call(
        body, name="fox_bwd_dq", grid=(nb, H // FOX_HEADS, nb),
        in_specs=[qsp, ksp, ksp, qsp, stat, pl.BlockSpec((H, blk), lambda qi, h, kj: (0, jnp.minimum(kj, qi))),
                  stat, stat],
        out_specs=[qsp, stat],
        out_shape=[jax.ShapeDtypeStruct((T, D), F32), jax.ShapeDtypeStruct((T, LANES), F32)],
        scratch_shapes=[pltpu.VMEM((blk, wide), F32), pltpu.VMEM((FOX_HEADS, blk, 1), F32)],
        compiler_params=_params("arbitrary", "arbitrary", "arbitrary"),
    )(q, k, v, do, ccol, crow, lse, delta)


def _fox_bwd_dkv(q, k, v, do, ccol, crow, lse_row, delta_row, blk):
    T, D = q.shape
    H = D // HEAD_DIM
    nb = T // blk

    def body(q_ref, k_ref, v_ref, do_ref, ccol_ref, crow_ref, lse_ref, delta_ref, dk_ref, dv_ref, dc_ref,
             dk_s, dv_s, dc_s):
        kj, hg, qi = pl.program_id(0), pl.program_id(1), pl.program_id(2)

        @pl.when(qi == 0)
        def _():
            dk_s[...] = jnp.zeros_like(dk_s)
            dv_s[...] = jnp.zeros_like(dv_s)
            dc_s[...] = jnp.zeros_like(dc_s)

        @pl.when((qi == 0) & (hg == 0))
        def _():
            dc_ref[...] = jnp.zeros_like(dc_ref)

        def step(diagonal):
            heads = range(FOX_HEADS)
            sls = [slice(hh * HEAD_DIM, (hh + 1) * HEAD_DIM) for hh in heads]
            pt, dpt = [], []
            for hh in heads:
                h = hg * FOX_HEADS + hh
                bias = crow_ref[pl.ds(h, 1), :] - _lane_pick(ccol_ref[...], h)
                st = lax.dot_general(k_ref[:, sls[hh]], q_ref[:, sls[hh]], _NT, preferred_element_type=F32) * SCALE + bias
                if diagonal:
                    st = jnp.where(_causal(blk, blk, transposed=True), st, NEG)
                pt.append(jnp.exp(st - lse_ref[pl.ds(h, 1), :]))
                dpt.append(lax.dot_general(v_ref[:, sls[hh]], do_ref[:, sls[hh]], _NT, preferred_element_type=F32)
                           - delta_ref[pl.ds(h, 1), :])
            dst = [pt[hh] * dpt[hh] for hh in heads]
            for hh in heads:
                dv_s[:, sls[hh]] += jnp.dot(pt[hh].astype(BF16), do_ref[:, sls[hh]], preferred_element_type=F32)
                dk_s[:, sls[hh]] += jnp.dot(dst[hh].astype(BF16), q_ref[:, sls[hh]], preferred_element_type=F32)
                dc_s[hh] -= jnp.sum(dst[hh], axis=1, keepdims=True)

        @pl.when(qi > kj)
        def _():
            step(False)

        @pl.when(qi == kj)
        def _():
            step(True)

        @pl.when(qi == nb - 1)
        def _():
            dk_ref[...] = dk_s[...] * SCALE
            dv_ref[...] = dv_s[...]
            for hh in range(FOX_HEADS):
                dc_ref[...] = _lane_put(dc_ref[...], hg * FOX_HEADS + hh, dc_s[hh])

    wide = FOX_HEADS * HEAD_DIM
    ksp = pl.BlockSpec((blk, wide), lambda kj, h, qi: (kj, h))
    qsp = pl.BlockSpec((blk, wide), lambda kj, h, qi: (jnp.maximum(qi, kj), h))
    kstat = pl.BlockSpec((blk, LANES), lambda kj, h, qi: (kj, 0))
    qrow = pl.BlockSpec((H, blk), lambda kj, h, qi: (0, jnp.maximum(qi, kj)))
    return pl.pallas_call(
        body, name="fox_bwd_dkv", grid=(nb, H // FOX_HEADS, nb),
        in_specs=[qsp, ksp, ksp, qsp, kstat, qrow, qrow, qrow],
        out_specs=[ksp, ksp, kstat],
        out_shape=[jax.ShapeDtypeStruct((T, D), F32), jax.ShapeDtypeStruct((T, D), F32),
                   jax.ShapeDtypeStruct((T, LANES), F32)],
        scratch_shapes=[pltpu.VMEM((blk, wide), F32), pltpu.VMEM((blk, wide), F32),
                        pltpu.VMEM((FOX_HEADS, blk, 1), F32)],
        compiler_params=_params("arbitrary", "arbitrary", "arbitrary"),
    )(q, k, v, do, ccol, crow, lse_row, delta_row)


def _rel_index():
    j = np.arange(ROLL_W)
    u = np.where(j < 2 * WIN, j, j - ROLL_W)
    return np.clip(WIN - u, -(CHUNK - 1), REL_CLIP) + (CHUNK - 1)


def _band_bias(g_row, bias_s):
    base = jnp.broadcast_to(g_row, (CHUNK, ROLL_W))
    rowb = lax.broadcasted_iota(jnp.int32, (CHUNK, ROLL_W), 0)
    for bit in range(CHUNK.bit_length() - 1):
        base = jnp.where((rowb >> bit) & 1 == 1, pltpu.roll(base, 1 << bit, 1), base)
    kc = lax.broadcasted_iota(jnp.int32, (CHUNK, 2 * WIN), 1) // CHUNK
    for a in range(N_PREV_CHUNKS):
        part = base if a == 0 else pltpu.roll(base, CHUNK * a, 1)
        valid = (kc >= a) & (kc <= a + N_PREV_CHUNKS)
        bias_s[a * CHUNK:(a + 1) * CHUNK, :] = jnp.where(valid, part[:, :2 * WIN], NEG)


def _chunk_probs(q, kp, kc, bias_s, qi):
    sp = lax.dot_general(q, kp, _NT, preferred_element_type=F32) * SCALE + bias_s[:, :WIN]
    sc = lax.dot_general(q, kc, _NT, preferred_element_type=F32) * SCALE + bias_s[:, WIN:]
    sp = jnp.where(qi > 0, sp, NEG)
    m = jnp.maximum(jnp.max(sp, axis=1, keepdims=True), jnp.max(sc, axis=1, keepdims=True))
    pp, pc = jnp.exp(sp - m), jnp.exp(sc - m)
    inv = 1.0 / (jnp.sum(pp, axis=1, keepdims=True) + jnp.sum(pc, axis=1, keepdims=True))
    return pp * inv, pc * inv


def _chunk_fwd(q, k, v, g_rows):
    T, D = q.shape
    H = D // HEAD_DIM
    nq = T // WIN

    def body(q_ref, kp_ref, kc_ref, vp_ref, vc_ref, g_ref, o_ref, bias_s):
        qi = pl.program_id(1)

        @pl.when(qi == 0)
        def _():
            _band_bias(g_ref[...], bias_s)

        pp, pc = _chunk_probs(q_ref[...], kp_ref[...], kc_ref[...], bias_s, qi)
        o = jnp.dot(pp.astype(BF16), vp_ref[...], preferred_element_type=F32)
        o += jnp.dot(pc.astype(BF16), vc_ref[...], preferred_element_type=F32)
        o_ref[...] = o.astype(BF16)

    cur = pl.BlockSpec((WIN, HEAD_DIM), lambda h, qi: (qi, h))
    prev = pl.BlockSpec((WIN, HEAD_DIM), lambda h, qi: (jnp.maximum(qi - 1, 0), h))
    return pl.pallas_call(
        body, name="chunk_fwd", grid=(H, nq),
        in_specs=[cur, prev, cur, prev, cur, pl.BlockSpec((None, 1, ROLL_W), lambda h, qi: (h, 0, 0))],
        out_specs=cur, out_shape=jax.ShapeDtypeStruct((T, D), BF16),
        scratch_shapes=[pltpu.VMEM((WIN, 2 * WIN), F32)],
        compiler_params=_params("arbitrary", "arbitrary"),
    )(q, k, k, v, v, g_rows)


def _chunk_bwd(q, k, v, g_rows, do, o):
    T, D = q.shape
    H = D // HEAD_DIM
    nq = T // WIN

    def body(q_ref, kp_ref, kc_ref, vp_ref, vc_ref, g_ref, do_ref, o_ref,
             dq_ref, dk_ref, dv_ref, ds_ref, bias_s, ck_s, cv_s):
        qi = pl.program_id(1)

        @pl.when(qi == 0)
        def _():
            _band_bias(g_ref[...], bias_s)
            ds_ref[...] = jnp.zeros_like(ds_ref)
            ck_s[...] = jnp.zeros_like(ck_s)
            cv_s[...] = jnp.zeros_like(cv_s)

        @pl.when(qi < nq)
        def _():
            qv, dov = q_ref[...], do_ref[...]
            pp, pc = _chunk_probs(qv, kp_ref[...], kc_ref[...], bias_s, qi)
            delta = jnp.sum(dov.astype(F32) * o_ref[...].astype(F32), axis=1, keepdims=True)
            dsp = pp * (lax.dot_general(dov, vp_ref[...], _NT, preferred_element_type=F32) - delta)
            dsc = pc * (lax.dot_general(dov, vc_ref[...], _NT, preferred_element_type=F32) - delta)
            dsp16, dsc16 = dsp.astype(BF16), dsc.astype(BF16)
            dq = jnp.dot(dsp16, kp_ref[...], preferred_element_type=F32)
            dq += jnp.dot(dsc16, kc_ref[...], preferred_element_type=F32)
            dq_ref[...] = dq * SCALE
            ds_ref[:, :WIN] += dsp
            ds_ref[:, WIN:] += dsc
            dk_ref[...] = ck_s[...] + lax.dot_general(dsp16, qv, _TN, preferred_element_type=F32) * SCALE
            dv_ref[...] = cv_s[...] + lax.dot_general(pp.astype(BF16), dov, _TN, preferred_element_type=F32)
            ck_s[...] = lax.dot_general(dsc16, qv, _TN, preferred_element_type=F32) * SCALE
            cv_s[...] = lax.dot_general(pc.astype(BF16), dov, _TN, preferred_element_type=F32)

        @pl.when(qi == nq)
        def _():
            dk_ref[...] = ck_s[...]
            dv_ref[...] = cv_s[...]

    cur = pl.BlockSpec((WIN, HEAD_DIM), lambda h, qi: (jnp.minimum(qi, nq - 1), h))
    prev = pl.BlockSpec((WIN, HEAD_DIM), lambda h, qi: (jnp.clip(qi - 1, 0, nq - 2), h))
    late = pl.BlockSpec((WIN, HEAD_DIM), lambda h, qi: (jnp.maximum(qi - 1, 0), h))
    return pl.pallas_call(
        body, name="chunk_bwd", grid=(H, nq + 1),
        in_specs=[cur, prev, cur, prev, cur, pl.BlockSpec((None, 1, ROLL_W), lambda h, qi: (h, 0, 0)), cur, cur],
        out_specs=[cur, late, late, pl.BlockSpec((None, WIN, 2 * WIN), lambda h, qi: (h, 0, 0))],
        out_shape=[jax.ShapeDtypeStruct((T, D), F32)] * 3 + [jax.ShapeDtypeStruct((H, WIN, 2 * WIN), F32)],
        scratch_shapes=[pltpu.VMEM((WIN, 2 * WIN), F32), pltpu.VMEM((WIN, HEAD_DIM), F32),
                        pltpu.VMEM((WIN, HEAD_DIM), F32)],
        compiler_params=_params("arbitrary", "arbitrary"),
    )(q, k, k, v, v, g_rows, do, o)


def _rel_grad(ds_sum, onehot):
    H = ds_sum.shape[0]

    def body(ds_ref, oh_ref, out_ref):
        y = jnp.zeros((CHUNK, ROLL_W), F32)
        pad = jnp.zeros((CHUNK, ROLL_W - 2 * WIN), F32)
        for a in range(N_PREV_CHUNKS):
            part = jnp.concatenate([ds_ref[a * CHUNK:(a + 1) * CHUNK, :], pad], axis=1)
            y = y + (part if a == 0 else pltpu.roll(part, ROLL_W - CHUNK * a, 1))
        rowb = lax.broadcasted_iota(jnp.int32, (CHUNK, ROLL_W), 0)
        for bit in range(CHUNK.bit_length() - 1):
            y = jnp.where((rowb >> bit) & 1 == 1, pltpu.roll(y, ROLL_W - (1 << bit), 1), y)
        diag = jnp.broadcast_to(jnp.sum(y, axis=0, keepdims=True), (8, ROLL_W))
        out_ref[...] = jnp.dot(diag, oh_ref[...], precision=HIGHEST, preferred_element_type=F32)

    return pl.pallas_call(
        body, name="rel_grad", grid=(H,),
        in_specs=[pl.BlockSpec((None, WIN, 2 * WIN), lambda h: (h, 0, 0)),
                  pl.BlockSpec((ROLL_W, N_REL_PAD), lambda h: (0, 0))],
        out_specs=pl.BlockSpec((None, 8, N_REL_PAD), lambda h: (h, 0, 0)),
        out_shape=jax.ShapeDtypeStruct((H, 8, N_REL_PAD), F32), compiler_params=_params("parallel"),
    )(ds_sum, onehot)


def _loss_head(y, target):
    T, D = y.shape
    bt = _tile(T, 256)

    def body(y_ref, t_ref, dy_ref, loss_ref):
        @pl.when(pl.program_id(0) == 0)
        def _():
            loss_ref[...] = jnp.zeros_like(loss_ref)

        err = y_ref[...] - t_ref[...]
        dy_ref[...] = err * (1.0 / D)
        loss_ref[...] += 0.5 * jnp.sum(jnp.mean(err * err, axis=-1, keepdims=True))

    row = pl.BlockSpec((bt, D), lambda i: (i, 0))
    one = pl.BlockSpec((8, LANES), lambda i: (0, 0))
    return pl.pallas_call(
        body, name="loss_head", grid=(T // bt,), in_specs=[row, row], out_specs=[row, one],
        out_shape=[jax.ShapeDtypeStruct((T, D), F32), jax.ShapeDtypeStruct((8, LANES), F32)],
        compiler_params=_params("arbitrary"),
    )(y, target)


def _adamw(name, w, m, v, parts, row0=0, into=None):
    rows_all, C = w.shape
    P, R, _ = parts.shape
    c1 = 1.0 - ADAM_B1 ** ADAM_STEP
    c2 = 1.0 - ADAM_B2 ** ADAM_STEP

    def body(w_ref, m_ref, v_ref, p_ref, *rest):
        g_ref, d_ref, nm_ref, nv_ref = rest[-4:]
        g = p_ref[0].astype(F32)
        for i in range(1, P):
            g = g + p_ref[i].astype(F32)
        nm = ADAM_B1 * m_ref[...] + (1.0 - ADAM_B1) * g
        nv = ADAM_B2 * v_ref[...] + (1.0 - ADAM_B2) * (g * g)
        g_ref[...] = g
        nm_ref[...] = nm
        nv_ref[...] = nv
        d_ref[...] = -ADAM_LR * ((nm / c1) / (jnp.sqrt(nv / c2) + ADAM_EPS) + ADAM_WD * w_ref[...])

    if R % 8 == 0 and R >= 8:
        br = _tile(R, max(8, (128 * 1024) // C))
        first = row0 // br
        grid = (R // br,)
        wsp = pl.BlockSpec((br, C), lambda i: (first + i, 0))
        psp = pl.BlockSpec((P, br, C), lambda i: (0, i, 0))
    else:
        bc = _tile(C, 512) if C % LANES == 0 else C
        grid = (C // bc,)
        wsp = pl.BlockSpec((R, bc), lambda i: (0, i))
        psp = pl.BlockSpec((P, R, bc), lambda i: (0, 0, i))
    extra = [] if into is None else list(into)
    return pl.pallas_call(
        body, name=name, grid=grid,
        in_specs=[wsp, wsp, wsp, psp] + [ANY] * len(extra), out_specs=[wsp] * 4,
        out_shape=[jax.ShapeDtypeStruct((rows_all, C), F32)] * 4,
        input_output_aliases={4 + j: j for j in range(len(extra))}, compiler_params=_params("parallel"),
    )(w, m, v, parts, *extra)


def _sum_parts(name, parts, out_dtype):
    P, R, C = parts.shape
    br = _tile(R, max(8, (256 * 1024) // C)) if R % 8 == 0 else R

    def body(p_ref, o_ref):
        g = p_ref[0].astype(F32)
        for i in range(1, P):
            g = g + p_ref[i].astype(F32)
        o_ref[...] = g.astype(out_dtype)

    return pl.pallas_call(
        body, name=name, grid=(R // br,),
        in_specs=[pl.BlockSpec((P, br, C), lambda i: (0, i, 0))], out_specs=pl.BlockSpec((br, C), lambda i: (i, 0)),
        out_shape=jax.ShapeDtypeStruct((R, C), out_dtype), compiler_params=_params("parallel"),
    )(parts)


def _pair_sum(name, g, recv, core):
    _, R, C = g.shape

    def body(core_ref, g_ref, r_ref, o_ref):
        o_ref[...] = (g_ref[...] + r_ref[...]).astype(BF16)

    if R % 16 == 0:
        br = _tile(R, max(16, (256 * 1024) // C))
        blk, n = (None, br, C), R // br
        mine = lambda q, i, core_ref: (2 * q + core_ref[0], i, 0)
        same = lambda q, i, core_ref: (q, i, 0)
    else:
        bc = _tile(C, 512)
        blk, n = (None, R, bc), C // bc
        mine = lambda q, i, core_ref: (2 * q + core_ref[0], 0, i)
        same = lambda q, i, core_ref: (q, 0, i)
    return pl.pallas_call(
        body, name=name,
        grid_spec=pltpu.PrefetchScalarGridSpec(
            num_scalar_prefetch=1, grid=(N_CHIP, n),
            in_specs=[pl.BlockSpec(blk, mine), pl.BlockSpec(blk, same)], out_specs=pl.BlockSpec(blk, same)),
        out_shape=jax.ShapeDtypeStruct((N_CHIP, R, C), BF16), compiler_params=_params("parallel", "parallel"),
    )(core, g, recv)


def _position():
    x, y, c = lax.axis_index("x"), lax.axis_index("y"), lax.axis_index("c")
    return x, y, c, [(1 - x, y), (x, 1 - y), (1 - x, 1 - y)]


def _all_gather(name, xs):
    n = len(xs)

    def body(*refs):
        x_refs, out_refs = refs[:n], refs[n:2 * n]
        send_sems, recv_sems, local_sems = refs[2 * n:]
        x, y, c, chips = _position()
        me, sibling = (x, y, c), (x, y, 1 - c)

        def copy(a, k, block, to, src=None):
            px, py, pc = block
            slot = out_refs[a].at[4 * px + 2 * py + pc]
            return pltpu.make_async_remote_copy(
                src_ref=slot if src is None else src, dst_ref=slot,
                send_sem=send_sems.at[a, k], recv_sem=recv_sems.at[a, k], device_id=to, device_id_type=MESH)

        mine = [pltpu.make_async_copy(x_refs[a], out_refs[a].at[4 * x + 2 * y + c], local_sems.at[a]) for a in range(n)]
        for cp in mine:
            cp.start()
        first = []
        for a in range(n):
            first.append(copy(a, 0, me, sibling, src=x_refs[a]))
            first += [copy(a, 1 + j, me, (*chip, c), src=x_refs[a]) for j, chip in enumerate(chips)]
        for cp in first:
            cp.start()
        passed = []
        for j, chip in enumerate(chips):
            for a in range(n):
                copy(a, 1 + j, (*chip, c), me).wait_recv()
                fwd = copy(a, 4 + j, (*chip, c), sibling)
                fwd.start()
                passed.append(fwd)
        for a in range(n):
            copy(a, 0, sibling, me).wait_recv()
            for j, chip in enumerate(chips):
                copy(a, 4 + j, (*chip, 1 - c), me).wait_recv()
        for cp in first + passed:
            cp.wait_send()
        for cp in mine:
            cp.wait()

    return pl.pallas_call(
        body, name=name, in_specs=[ANY] * n, out_specs=[ANY] * n,
        out_shape=[jax.ShapeDtypeStruct((N_DEV,) + x.shape, x.dtype) for x in xs],
        scratch_shapes=[pltpu.SemaphoreType.DMA((n, 7)), pltpu.SemaphoreType.DMA((n, 7)),
                        pltpu.SemaphoreType.DMA((n,))],
    )(*xs)


def _swap_cores(name, gs):
    n = len(gs)

    def body(*refs):
        g_refs, out_refs = refs[:n], refs[n:2 * n]
        send_sems, recv_sems = refs[2 * n:]
        x, y, c, _ = _position()
        copies = [pltpu.make_async_remote_copy(
            src_ref=g_refs[a].at[2 * q + (1 - c)], dst_ref=out_refs[a].at[q],
            send_sem=send_sems.at[a, q], recv_sem=recv_sems.at[a, q], device_id=(x, y, 1 - c), device_id_type=MESH)
            for a in range(n) for q in range(N_CHIP)]
        for cp in copies:
            cp.start()
        for cp in copies:
            cp.wait_recv()
        for cp in copies:
            cp.wait_send()

    return pl.pallas_call(
        body, name=name, in_specs=[ANY] * n, out_specs=[ANY] * n,
        out_shape=[jax.ShapeDtypeStruct((N_CHIP,) + g.shape[1:], g.dtype) for g in gs],
        scratch_shapes=[pltpu.SemaphoreType.DMA((n, N_CHIP)), pltpu.SemaphoreType.DMA((n, N_CHIP))],
    )(*gs)


def _swap_chips(name, ps):
    n = len(ps)

    def body(*refs):
        p_refs, out_refs = refs[:n], refs[n:2 * n]
        send_sems, recv_sems, local_sems = refs[2 * n:]
        x, y, c, chips = _position()
        mine = [pltpu.make_async_copy(p_refs[a].at[2 * x + y], out_refs[a].at[2 * x + y], local_sems.at[a])
                for a in range(n)]
        for cp in mine:
            cp.start()
        copies = [pltpu.make_async_remote_copy(
            src_ref=p_refs[a].at[2 * cx + cy], dst_ref=out_refs[a].at[2 * x + y],
            send_sem=send_sems.at[a, j], recv_sem=recv_sems.at[a, j], device_id=(cx, cy, c), device_id_type=MESH)
            for a in range(n) for j, (cx, cy) in enumerate(chips)]
        for cp in copies:
            cp.start()
        for a in range(n):
            for j, (cx, cy) in enumerate(chips):
                pltpu.make_async_remote_copy(
                    src_ref=p_refs[a].at[2 * x + y], dst_ref=out_refs[a].at[2 * cx + cy],
                    send_sem=send_sems.at[a, j], recv_sem=recv_sems.at[a, j], device_id=(cx, cy, c),
                    device_id_type=MESH).wait_recv()
        for cp in copies:
            cp.wait_send()
        for cp in mine:
            cp.wait()

    return pl.pallas_call(
        body, name=name, in_specs=[ANY] * n, out_specs=[ANY] * n,
        out_shape=[jax.ShapeDtypeStruct(p.shape, p.dtype) for p in ps],
        scratch_shapes=[pltpu.SemaphoreType.DMA((n, 3)), pltpu.SemaphoreType.DMA((n, 3)),
                        pltpu.SemaphoreType.DMA((n,))],
    )(*ps)


def _mlp_fwd(tag, h, g, w1g, w2):
    T, D = h.shape
    F = w2.shape[0]
    (n,) = _rmsnorm_fwd(f"mlp{tag}_norm", h, [g])
    a, hid = _mm(f"mlp{tag}_up", "nn", n, w1g, T, F, D, b_groups=N_DEV, out_dtypes=(BF16, BF16),
                 epilogue=_ep_relu_square)
    out = _mm(f"mlp{tag}_down", "nn", hid, w2, T, D, F, extras=(h,), epilogue=_ep_add)
    return out, (n, a, hid)


def _mlp_bwd(tag, h, g, w1g, w2, saved, dy):
    T, D = h.shape
    F = w2.shape[0]
    n, a, hid = saved
    dw2 = _mm(f"mlp{tag}_dw2", "tn", hid, dy, F, D, T)
    dpre = _mm(f"mlp{tag}_dpre", "nt", dy, w2, T, F, D, out_dtypes=(BF16,), extras=(a,), epilogue=_ep_times_2a)
    dw1 = _mm(f"mlp{tag}_dw1", "tn", n, dpre, D, F, T, out_groups=N_DEV)
    dn = _mm(f"mlp{tag}_dn", "nt", dpre, w1g, T, D, F, b_groups=N_DEV)
    dh, dg = _rmsnorm_bwd(f"mlp{tag}_norm_bwd", h, g, dn, dy)
    return dh, dg, dw1, dw2


def _local_step(x, target, W, fox_blk):
    T, D = x.shape
    H = D // HEAD_DIM
    G = {}

    (ua,) = _rmsnorm_fwd("a_norm", x, [W["a_norm_g"]])
    qkv = _mm("a_qkv", "nt", ua, W["a_w_in_t"], T, 3 * D, D)
    fz = _mm("a_fz", "nt", ua, W["a_w_f_t"], T, LANES, D)
    qa, ka, va = _head_post("a_heads", qkv, D, [(0, W["a_q_g"]), (1, W["a_k_g"]), (2, None)])
    ccol = _gates_fwd(fz, W["a_b_f"], H)
    crow = ccol[:, :H].T
    oa, lse = _fox_fwd(qa, ka, va, ccol, crow, fox_blk)
    h1 = _mm("a_out", "nn", oa, W["a_w_out"], T, D, D, extras=(x,), epilogue=_ep_add)
    h2, mlp0 = _mlp_fwd("0", h1, W["mlp_norm_g0"], W["mlp_w1_0"], W["mlp_w2_0"])
    nkv, ub = _rmsnorm_fwd("kv_b_norm", h2, [W["kv_norm_g"], W["b_norm_g"]])
    kv = _mm("kv_proj", "nn", nkv, W["kv_w"], T, 2 * D, D, b_groups=N_DEV)
    ks, vs = _head_post("kv_heads", kv, D, [(0, W["kv_k_g"]), (1, None)])
    qb_pre = _mm("b_q", "nn", ub, W["b_w_q"], T, D, D)
    (qb,) = _head_post("b_heads", qb_pre, D, [(0, W["b_q_g"])])
    g_rows = W["b_rel"][:, _rel_index()].reshape(H, 1, ROLL_W)
    ob = _chunk_fwd(qb, ks, vs, g_rows)
    h3 = _mm("b_out", "nn", ob, W["b_w_out"], T, D, D, extras=(h2,), epilogue=_ep_add)
    h4, mlp1 = _mlp_fwd("1", h3, W["mlp_norm_g1"], W["mlp_w1_1"], W["mlp_w2_1"])
    dy, loss = _loss_head(h4, target)

    dh3, G["mlp_norm_g1"], G["mlp_w1_1"], G["mlp_w2_1"] = _mlp_bwd(
        "1", h3, W["mlp_norm_g1"], W["mlp_w1_1"], W["mlp_w2_1"], mlp1, dy)
    dob = _mm("b_out_dx", "nt", dh3, W["b_w_out"], T, D, D, out_dtypes=(BF16,))
    G["b_w_out"] = _mm("b_out_dw", "tn", ob, dh3, D, D, T)
    dqb, dks, dvs, ds_sum = _chunk_bwd(qb, ks, vs, g_rows, dob, ob)
    onehot = (jnp.asarray(_rel_index())[:, None] == jnp.arange(N_REL_PAD)[None, :]).astype(F32)
    G["b_rel"] = _rel_grad(ds_sum, onehot)[:, 0, :]
    dqb_pre, (G["b_q_g"],) = _head_post_bwd("b_heads_bwd", qb_pre, D, [(0, W["b_q_g"], dqb)])
    G["b_w_q"] = _mm("b_q_dw", "tn", ub, dqb_pre, D, D, T)
    dub = _mm("b_q_dx", "nt", dqb_pre, W["b_w_q"], T, D, D)
    dkv, (G["kv_k_g"],) = _head_post_bwd("kv_heads_bwd", kv, D, [(0, W["kv_k_g"], dks), (1, None, dvs)])
    G["kv_w"] = _mm("kv_dw", "tn", nkv, dkv, D, 2 * D, T, out_groups=N_DEV)
    dnkv = _mm("kv_dx", "nt", dkv, W["kv_w"], T, D, 2 * D, b_groups=N_DEV)
    dh2, G["b_norm_g"] = _rmsnorm_bwd("b_norm_bwd", h2, W["b_norm_g"], dub, dh3)
    dh2, G["kv_norm_g"] = _rmsnorm_bwd("kv_norm_bwd", h2, W["kv_norm_g"], dnkv, dh2)
    dh1, G["mlp_norm_g0"], G["mlp_w1_0"], G["mlp_w2_0"] = _mlp_bwd(
        "0", h1, W["mlp_norm_g0"], W["mlp_w1_0"], W["mlp_w2_0"], mlp0, dh2)
    doa = _mm("a_out_dx", "nt", dh1, W["a_w_out"], T, D, D, out_dtypes=(BF16,))
    G["a_w_out"] = _mm("a_out_dw", "tn", oa, dh1, D, D, T)
    delta = _head_stat("fox_delta", doa, oa)
    dqa, dcq = _fox_bwd_dq(qa, ka, va, doa, ccol, crow, lse, delta, fox_blk)
    dka, dva, dck = _fox_bwd_dkv(qa, ka, va, doa, ccol, crow, lse[:, :H].T, delta[:, :H].T, fox_blk)
    dfz, G["a_b_f"] = _gates_bwd(dcq, dck, fz, W["a_b_f"], H)
    dqkv, (G["a_q_g"], G["a_k_g"]) = _head_post_bwd(
        "a_heads_bwd", qkv, D, [(0, W["a_q_g"], dqa), (1, W["a_k_g"], dka), (2, None, dva)])
    G["a_w_qkv_t"] = _mm("a_qkv_dw", "tn", dqkv, ua, 3 * D, D, T)
    G["a_w_f_t"] = _mm("a_fz_dw", "tn", dfz, ua, LANES, D, T)
    dua = _mm("a_qkv_dx", "nn", dqkv, W["a_w_in_t"], T, D, 3 * D)
    dua = _mm("a_fz_dx", "nn", dfz, W["a_w_f_t"], T, D, LANES, extras=(dua,), epilogue=_ep_add)
    dx, G["a_norm_g"] = _rmsnorm_bwd("a_norm_bwd", x, W["a_norm_g"], dua, dh1)
    return loss, dx, G


def _pad_lanes(a):
    return jnp.pad(a, ((0, 0), (0, LANES - a.shape[1])))


def kernel(x, a_norm_g, a_w_in, a_b_f, a_q_g, a_k_g, a_w_out, mlp_norm_g, mlp_w1, mlp_w2, kv_norm_g, kv_w, kv_k_g, b_norm_g, b_w_q, b_q_g, b_rel, b_w_out, loss_target, m_a_norm_g, m_a_w_in, m_a_b_f, m_a_q_g, m_a_k_g, m_a_w_out, m_mlp_norm_g, m_mlp_w1, m_mlp_w2, m_kv_norm_g, m_kv_w, m_kv_k_g, m_b_norm_g, m_b_w_q, m_b_q_g, m_b_rel, m_b_w_out, v_a_norm_g, v_a_w_in, v_a_b_f, v_a_q_g, v_a_k_g, v_a_w_out, v_mlp_norm_g, v_mlp_w1, v_mlp_w2, v_kv_norm_g, v_kv_w, v_kv_k_g, v_b_norm_g, v_b_w_q, v_b_q_g, v_b_rel, v_b_w_out):
    T, D = x.shape[1], x.shape[2]
    H = D // HEAD_DIM
    F = mlp_w2.shape[1] * N_DEV
    me = 4 * lax.axis_index("x") + 2 * lax.axis_index("y") + lax.axis_index("c")
    core = lax.axis_index("c").astype(jnp.int32).reshape(1)
    n_in = a_w_in.shape[2]
    n_rel = b_rel.shape[2]

    small_in = jnp.concatenate([a_norm_g.reshape(1, -1), b_rel.reshape(1, -1)], axis=1)
    local = [jnp.swapaxes(a_w_in[0], 0, 1), a_w_out[0], mlp_w1[0], mlp_w1[1], mlp_w2[0], mlp_w2[1], kv_w,
             b_w_q[0], b_w_out[0]]
    gathered = _all_gather("gather_weights", [w.astype(BF16) for w in local] + [small_in])
    g_in, g_aout, g_w1_0, g_w1_1, g_w2_0, g_w2_1, g_kv, g_bq, g_bout, g_small = gathered
    w_in_t = g_in.reshape(N_DEV * n_in, D)
    rel_full = jnp.transpose(g_small[:, 0, D // N_DEV:].reshape(N_DEV, H, n_rel), (1, 0, 2)).reshape(H, N_DEV * n_rel)
    W = {
        "a_norm_g": g_small[:, 0, :D // N_DEV].reshape(1, D),
        "a_w_in_t": w_in_t, "a_w_f_t": jnp.pad(w_in_t[3 * D:], ((0, LANES - H), (0, 0))),
        "a_b_f": _pad_lanes(a_b_f), "a_q_g": a_q_g, "a_k_g": a_k_g,
        "a_w_out": g_aout.reshape(D, D),
        "mlp_norm_g0": mlp_norm_g[0:1], "mlp_norm_g1": mlp_norm_g[1:2],
        "mlp_w1_0": g_w1_0, "mlp_w1_1": g_w1_1, "mlp_w2_0": g_w2_0.reshape(F, D), "mlp_w2_1": g_w2_1.reshape(F, D),
        "kv_norm_g": kv_norm_g.reshape(1, D), "kv_w": g_kv, "kv_k_g": kv_k_g.reshape(1, HEAD_DIM),
        "b_norm_g": b_norm_g, "b_w_q": g_bq.reshape(D, D), "b_q_g": b_q_g, "b_rel": rel_full,
        "b_w_out": g_bout.reshape(D, D),
    }

    loss_tile, dx, G = _local_step(x[0], loss_target[0], W, min(T, 512))

    dw_in_t = jnp.concatenate([G["a_w_qkv_t"], G["a_w_f_t"][:H]], axis=0).reshape(N_DEV, n_in, D)
    big = [dw_in_t, G["a_w_out"].reshape(N_DEV, D // N_DEV, D), G["mlp_w1_0"], G["mlp_w1_1"],
           G["mlp_w2_0"].reshape(N_DEV, F // N_DEV, D), G["mlp_w2_1"].reshape(N_DEV, F // N_DEV, D),
           G["kv_w"], G["b_w_q"].reshape(N_DEV, D // N_DEV, D), G["b_w_out"].reshape(N_DEV, D // N_DEV, D)]
    names = ["a_w_in", "a_w_out", "mlp_w1_0", "mlp_w1_1", "mlp_w2_0", "mlp_w2_1", "kv_w", "b_w_q", "b_w_out"]
    from_sibling = _swap_cores("reduce_cores", big)
    pair = [_pair_sum(f"pair_sum_{nm}", g, r, core) for nm, g, r in zip(names, big, from_sibling)]
    parts = dict(zip(names, _swap_chips("reduce_chips", pair)))

    flat = lambda a: a.reshape(-1, a.shape[-1])
    upd = {}
    in_t = [jnp.swapaxes(a[0], 0, 1) for a in (a_w_in, m_a_w_in, v_a_w_in)]
    upd["a_w_in"] = [jnp.swapaxes(o, 0, 1)[None] for o in _adamw("adamw_a_w_in", *in_t, parts["a_w_in"])]
    for nm, w, m, v in [("a_w_out", a_w_out, m_a_w_out, v_a_w_out), ("kv_w", kv_w, m_kv_w, v_kv_w),
                        ("b_w_q", b_w_q, m_b_w_q, v_b_w_q), ("b_w_out", b_w_out, m_b_w_out, v_b_w_out)]:
        upd[nm] = [o.reshape(w.shape) for o in _adamw(f"adamw_{nm}", flat(w), flat(m), flat(v), parts[nm])]
    for nm, w, m, v in [("mlp_w1", mlp_w1, m_mlp_w1, v_mlp_w1), ("mlp_w2", mlp_w2, m_mlp_w2, v_mlp_w2)]:
        layer0 = _adamw(f"adamw_{nm}_0", flat(w), flat(m), flat(v), parts[nm + "_0"])
        both = _adamw(f"adamw_{nm}_1", flat(w), flat(m), flat(v), parts[nm + "_1"], row0=w.shape[1], into=layer0)
        upd[nm] = [o.reshape(w.shape) for o in both]

    def rows8(vecs):
        r = lax.broadcasted_iota(jnp.int32, (8, LANES), 0)
        tile = jnp.zeros((8, LANES), F32)
        for i, vec in enumerate(vecs):
            tile = jnp.where(r == i, vec, tile)
        return tile

    rows = lambda a: a.reshape(-1, LANES)
    packed = [rows(G["a_norm_g"]), rows(G["mlp_norm_g0"]), rows(G["mlp_norm_g1"]), rows(G["kv_norm_g"]),
              rows(G["b_norm_g"]), rows(G["b_rel"]),
              rows8([G["a_b_f"], G["a_q_g"], G["a_k_g"], G["kv_k_g"], G["b_q_g"], loss_tile[0:1]])]
    sizes = [p.shape[0] for p in packed]
    (all_small,) = _all_gather("gather_small_grads", [jnp.concatenate(packed, axis=0)])
    total = _sum_parts("sum_small_grads", all_small, F32)
    offs = np.concatenate([[0], np.cumsum(sizes)])
    s = [total[offs[i]:offs[i + 1]] for i in range(len(sizes))]
    loss = s[6][5, 0]
    g_a_norm = lax.dynamic_slice(s[0].reshape(1, D), (0, me * (D // N_DEV)), (1, D // N_DEV))
    g_rel = lax.dynamic_slice(s[5].reshape(H, N_REL_PAD), (0, me * n_rel), (H, n_rel))
    small = {
        "a_norm_g": (a_norm_g, m_a_norm_g, v_a_norm_g, g_a_norm),
        "a_b_f": (a_b_f, m_a_b_f, v_a_b_f, s[6][0:1, :H]),
        "a_q_g": (a_q_g, m_a_q_g, v_a_q_g, s[6][1:2]),
        "a_k_g": (a_k_g, m_a_k_g, v_a_k_g, s[6][2:3]),
        "mlp_norm_g": (mlp_norm_g, m_mlp_norm_g, v_mlp_norm_g,
                       jnp.concatenate([s[1].reshape(1, D), s[2].reshape(1, D)], axis=0)),
        "kv_norm_g": (kv_norm_g.reshape(1, D), m_kv_norm_g.reshape(1, D), v_kv_norm_g.reshape(1, D), s[3].reshape(1, D)),
        "kv_k_g": (kv_k_g.reshape(1, HEAD_DIM), m_kv_k_g.reshape(1, HEAD_DIM), v_kv_k_g.reshape(1, HEAD_DIM),
                   s[6][3:4]),
        "b_norm_g": (b_norm_g, m_b_norm_g, v_b_norm_g, s[4].reshape(1, D)),
        "b_q_g": (b_q_g, m_b_q_g, v_b_q_g, s[6][4:5]),
        "b_rel": (b_rel[0], m_b_rel[0], v_b_rel[0], g_rel),
    }
    for nm, (w, m, v, g) in small.items():
        upd[nm] = _adamw(f"adamw_{nm}", w, m, v, g[None])
    upd["kv_norm_g"] = [a.reshape(D) for a in upd["kv_norm_g"]]
    upd["kv_k_g"] = [a.reshape(HEAD_DIM) for a in upd["kv_k_g"]]
    upd["b_rel"] = [a[None] for a in upd["b_rel"]]

    order = ["a_norm_g", "a_w_in", "a_b_f", "a_q_g", "a_k_g", "a_w_out", "mlp_norm_g", "mlp_w1", "mlp_w2",
             "kv_norm_g", "kv_w", "kv_k_g", "b_norm_g", "b_w_q", "b_q_g", "b_rel", "b_w_out"]
    return (loss, dx[None], *[upd[n][0] for n in order], *[upd[n][1] for n in order],
            *[upd[n][2] for n in order], *[upd[n][3] for n in order])
```

```python
import functools

import numpy as np
import jax
import jax.numpy as jnp
from jax import lax
from jax.experimental import pallas as pl
from jax.experimental.pallas import tpu as pltpu

F32 = jnp.float32
BF16 = jnp.bfloat16
HIGHEST = lax.Precision.HIGHEST
MESH = pl.DeviceIdType.MESH
ANY = pl.BlockSpec(memory_space=pl.ANY)

N_DEV = 8
N_CHIP = 4
LANES = 128
HEAD_DIM = 128
CHUNK = 64
N_PREV_CHUNKS = 8
REL_CLIP = 256
N_REL = REL_CLIP + CHUNK
N_REL_PAD = 384
WIN = N_PREV_CHUNKS * CHUNK
ROLL_W = 4 * WIN
EPS = 1e-6
NEG = -1e30
SCALE = HEAD_DIM ** -0.5
ADAM_LR, ADAM_B1, ADAM_B2, ADAM_EPS, ADAM_WD, ADAM_STEP = 0.001, 0.9, 0.999, 1e-08, 0.01, 10
V7X_VMEM_BYTES = 64 * 1024 * 1024
VMEM_LIMIT = V7X_VMEM_BYTES * 3 // 4

_NN = (((1,), (0,)), ((), ()))
_NT = (((1,), (1,)), ((), ()))
_TN = (((0,), (0,)), ((), ()))


def _params(*sem):
    return pltpu.CompilerParams(dimension_semantics=sem or None, vmem_limit_bytes=VMEM_LIMIT)


def _tile(n, pref):
    t = n
    while t > pref and t % 2 == 0:
        t //= 2
    return t


def _ep_store(acc, ex, outs):
    outs[0][...] = acc.astype(outs[0].dtype)


def _ep_add(acc, ex, outs):
    outs[0][...] = (acc + ex[0][...]).astype(outs[0].dtype)


def _ep_relu_square(acc, ex, outs):
    a = jnp.maximum(acc, 0.0)
    outs[0][...] = a.astype(outs[0].dtype)
    outs[1][...] = (a * a).astype(outs[1].dtype)


def _ep_times_2a(acc, ex, outs):
    outs[0][...] = (acc * (2.0 * ex[0][...].astype(F32))).astype(outs[0].dtype)


def _mm(name, kind, a, b, M, N, K, *, b_groups=0, out_groups=0, out_dtypes=(F32,), extras=(),
        epilogue=_ep_store, tm=512, tn=1024, tk=2048):
    tm, tn, tk = _tile(M, tm), _tile(N, tn), _tile(K, tk)
    if b_groups:
        cg = b.shape[-1]
        if kind == "nn":
            tn = min(tn, cg)
        else:
            tk = min(tk, cg)
    if out_groups:
        tn = min(tn, N // out_groups)
    grid = (M // tm, N // tn, K // tk)
    nk = grid[2]
    if kind == "nn":
        dims = _NN
        a_spec = pl.BlockSpec((tm, tk), lambda i, j, k: (i, k))
        if b_groups:
            npg = cg // tn
            b_spec = pl.BlockSpec((None, tk, tn), lambda i, j, k: (j // npg, k, j % npg))
        else:
            b_spec = pl.BlockSpec((tk, tn), lambda i, j, k: (k, j))
    elif kind == "nt":
        dims = _NT
        a_spec = pl.BlockSpec((tm, tk), lambda i, j, k: (i, k))
        if b_groups:
            npg = cg // tk
            b_spec = pl.BlockSpec((None, tn, tk), lambda i, j, k: (k // npg, j, k % npg))
        else:
            b_spec = pl.BlockSpec((tn, tk), lambda i, j, k: (j, k))
    else:
        dims = _TN
        a_spec = pl.BlockSpec((tk, tm), lambda i, j, k: (k, i))
        b_spec = pl.BlockSpec((tk, tn), lambda i, j, k: (k, j))
    tile_spec = pl.BlockSpec((tm, tn), lambda i, j, k: (i, j))
    if out_groups:
        ng = (N // out_groups) // tn
        out_spec = pl.BlockSpec((None, tm, tn), lambda i, j, k: (j // ng, i, j % ng))
        out_shape = [jax.ShapeDtypeStruct((out_groups, M, N // out_groups), d) for d in out_dtypes]
    else:
        out_spec = tile_spec
        out_shape = [jax.ShapeDtypeStruct((M, N), d) for d in out_dtypes]
    n_ex, n_out = len(extras), len(out_dtypes)

    def body(a_ref, b_ref, *rest):
        ex, outs, acc = rest[:n_ex], rest[n_ex:n_ex + n_out], rest[-1]
        k = pl.program_id(2)

        @pl.when(k == 0)
        def _():
            acc[...] = jnp.zeros_like(acc)

        acc[...] += lax.dot_general(a_ref[...].astype(BF16), b_ref[...].astype(BF16), dims,
                                    preferred_element_type=F32)

        @pl.when(k == nk - 1)
        def _():
            epilogue(acc[...], ex, outs)

    res = pl.pallas_call(
        body, name=name, grid=grid,
        in_specs=[a_spec, b_spec] + [tile_spec] * n_ex,
        out_specs=[out_spec] * n_out, out_shape=out_shape,
        scratch_shapes=[pltpu.VMEM((tm, tn), F32)],
        compiler_params=_params("parallel", "parallel", "arbitrary"),
    )(a, b, *extras)
    return res[0] if n_out == 1 else res


def _rmsnorm_fwd(name, x, gains):
    T, D = x.shape
    bt = _tile(T, 256)
    n = len(gains)

    def body(x_ref, *rest):
        xv = x_ref[...]
        y = xv * lax.rsqrt(jnp.mean(xv * xv, axis=-1, keepdims=True) + EPS)
        for g_ref, o_ref in zip(rest[:n], rest[n:]):
            o_ref[...] = (y * g_ref[...]).astype(BF16)

    row = pl.BlockSpec((bt, D), lambda i: (i, 0))
    gsp = pl.BlockSpec((1, D), lambda i: (0, 0))
    return pl.pallas_call(
        body, name=name, grid=(T // bt,), in_specs=[row] + [gsp] * n, out_specs=[row] * n,
        out_shape=[jax.ShapeDtypeStruct((T, D), BF16)] * n, compiler_params=_params("parallel"),
    )(x, *gains)


def _rmsnorm_bwd(name, x, g, dn, res):
    T, D = x.shape
    bt = _tile(T, 256)

    def body(x_ref, g_ref, dn_ref, res_ref, dx_ref, dg_ref):
        @pl.when(pl.program_id(0) == 0)
        def _():
            dg_ref[...] = jnp.zeros_like(dg_ref)

        xv, dnv = x_ref[...], dn_ref[...]
        r = lax.rsqrt(jnp.mean(xv * xv, axis=-1, keepdims=True) + EPS)
        xhat = xv * r
        dyg = dnv * g_ref[...]
        dx = r * (dyg - xhat * jnp.mean(dyg * xhat, axis=-1, keepdims=True))
        dx_ref[...] = res_ref[...] + dx
        dg_ref[...] += jnp.sum(dnv * xhat, axis=0, keepdims=True)

    row = pl.BlockSpec((bt, D), lambda i: (i, 0))
    gsp = pl.BlockSpec((1, D), lambda i: (0, 0))
    return pl.pallas_call(
        body, name=name, grid=(T // bt,), in_specs=[row, gsp, row, row], out_specs=[row, gsp],
        out_shape=[jax.ShapeDtypeStruct((T, D), F32), jax.ShapeDtypeStruct((1, D), F32)],
        compiler_params=_params("arbitrary"),
    )(x, g, dn, res)


def _head_post(name, x, D, items):
    T = x.shape[0]
    H = D // HEAD_DIM
    bt = _tile(T, 256)
    gains = [g for _, g in items if g is not None]
    n_it, n_g = len(items), len(gains)

    def body(*refs):
        x_refs, g_refs, o_refs = refs[:n_it], refs[n_it:n_it + n_g], refs[n_it + n_g:]
        gi = 0
        for it, (_, g) in enumerate(items):
            if g is None:
                o_refs[it][...] = x_refs[it][...].astype(BF16)
                continue
            gv = g_refs[gi][...]
            gi += 1
            for h in range(H):
                sl = slice(h * HEAD_DIM, (h + 1) * HEAD_DIM)
                xs = x_refs[it][:, sl]
                r = lax.rsqrt(jnp.mean(xs * xs, axis=-1, keepdims=True) + EPS)
                o_refs[it][:, sl] = (xs * r * gv).astype(BF16)

    in_specs = [pl.BlockSpec((bt, D), functools.partial(lambda cb, i: (i, cb), cb)) for cb, _ in items]
    in_specs += [pl.BlockSpec((1, HEAD_DIM), lambda i: (0, 0))] * n_g
    row = pl.BlockSpec((bt, D), lambda i: (i, 0))
    res = pl.pallas_call(
        body, name=name, grid=(T // bt,), in_specs=in_specs, out_specs=[row] * n_it,
        out_shape=[jax.ShapeDtypeStruct((T, D), BF16)] * n_it, compiler_params=_params("parallel"),
    )(*([x] * n_it), *gains)
    return res


def _head_post_bwd(name, x, D, items):
    T = x.shape[0]
    H = D // HEAD_DIM
    bt = _tile(T, 128)
    gains = [g for _, g, _ in items if g is not None]
    n_it, n_g = len(items), len(gains)

    def body(*refs):
        x_refs, dy_refs = refs[:n_it], refs[n_it:2 * n_it]
        g_refs = refs[2 * n_it:2 * n_it + n_g]
        dx_ref = refs[2 * n_it + n_g]
        dg_refs = refs[2 * n_it + n_g + 1:]

        @pl.when(pl.program_id(0) == 0)
        def _():
            for r in dg_refs:
                r[...] = jnp.zeros_like(r)

        gi = 0
        for it, (_, g, _) in enumerate(items):
            if g is None:
                dx_ref[:, it * D:(it + 1) * D] = dy_refs[it][...].astype(BF16)
                continue
            gv = g_refs[gi][...]
            dg = jnp.zeros((1, HEAD_DIM), F32)
            for h in range(H):
                sl = slice(h * HEAD_DIM, (h + 1) * HEAD_DIM)
                xs, dy = x_refs[it][:, sl], dy_refs[it][:, sl]
                r = lax.rsqrt(jnp.mean(xs * xs, axis=-1, keepdims=True) + EPS)
                xhat = xs * r
                dyg = dy * gv
                dx = r * (dyg - xhat * jnp.mean(dyg * xhat, axis=-1, keepdims=True))
                dx_ref[:, it * D + h * HEAD_DIM:it * D + (h + 1) * HEAD_DIM] = dx.astype(BF16)
                dg = dg + jnp.sum(dy * xhat, axis=0, keepdims=True)
            dg_refs[gi][...] += dg
            gi += 1

    in_specs = [pl.BlockSpec((bt, D), functools.partial(lambda cb, i: (i, cb), cb)) for cb, _, _ in items]
    in_specs += [pl.BlockSpec((bt, D), lambda i: (i, 0))] * n_it
    gsp = pl.BlockSpec((1, HEAD_DIM), lambda i: (0, 0))
    in_specs += [gsp] * n_g
    res = pl.pallas_call(
        body, name=name, grid=(T // bt,), in_specs=in_specs,
        out_specs=[pl.BlockSpec((bt, n_it * D), lambda i: (i, 0))] + [gsp] * n_g,
        out_shape=[jax.ShapeDtypeStruct((T, n_it * D), BF16)] + [jax.ShapeDtypeStruct((1, HEAD_DIM), F32)] * n_g,
        compiler_params=_params("arbitrary"),
    )(*([x] * n_it), *[dy for _, _, dy in items], *gains)
    return res[0], list(res[1:])


def _gates_fwd(fz, bf, H):
    T = fz.shape[0]
    bt = _tile(T, 512)

    def body(fz_ref, bf_ref, c_ref, carry):
        @pl.when(pl.program_id(0) == 0)
        def _():
            carry[...] = jnp.zeros_like(carry)

        z = fz_ref[...] + bf_ref[...]
        logf = jnp.minimum(z, 0.0) - jnp.log(1.0 + jnp.exp(-jnp.abs(z)))
        lane = lax.broadcasted_iota(jnp.int32, (bt, LANES), 1)
        logf = jnp.where(lane < H, logf, 0.0)
        tri = (lax.broadcasted_iota(jnp.int32, (bt, bt), 0) >= lax.broadcasted_iota(jnp.int32, (bt, bt), 1))
        c = jnp.dot(tri.astype(F32), logf, precision=HIGHEST, preferred_element_type=F32) + carry[...]
        c_ref[...] = c
        carry[...] = c[bt - 1:bt, :]

    row = pl.BlockSpec((bt, LANES), lambda i: (i, 0))
    return pl.pallas_call(
        body, name="fox_gates_fwd", grid=(T // bt,),
        in_specs=[row, pl.BlockSpec((1, LANES), lambda i: (0, 0))], out_specs=row,
        out_shape=jax.ShapeDtypeStruct((T, LANES), F32), scratch_shapes=[pltpu.VMEM((1, LANES), F32)],
        compiler_params=_params("arbitrary"),
    )(fz, bf)


def _gates_bwd(dcq, dck, fz, bf, H):
    T = fz.shape[0]
    bt = _tile(T, 512)
    nb = T // bt

    def body(dcq_ref, dck_ref, fz_ref, bf_ref, dfz_ref, dbf_ref, carry):
        @pl.when(pl.program_id(0) == 0)
        def _():
            carry[...] = jnp.zeros_like(carry)
            dbf_ref[...] = jnp.zeros_like(dbf_ref)

        tri = (lax.broadcasted_iota(jnp.int32, (bt, bt), 0) <= lax.broadcasted_iota(jnp.int32, (bt, bt), 1))
        dc = dcq_ref[...] + dck_ref[...]
        dlogf = jnp.dot(tri.astype(F32), dc, precision=HIGHEST, preferred_element_type=F32) + carry[...]
        carry[...] = dlogf[0:1, :]
        z = fz_ref[...] + bf_ref[...]
        lane = lax.broadcasted_iota(jnp.int32, (bt, LANES), 1)
        dfz = jnp.where(lane < H, dlogf / (1.0 + jnp.exp(z)), 0.0)
        dfz_ref[...] = dfz
        dbf_ref[...] += jnp.sum(dfz, axis=0, keepdims=True)

    row = pl.BlockSpec((bt, LANES), lambda i: (nb - 1 - i, 0))
    one = pl.BlockSpec((1, LANES), lambda i: (0, 0))
    return pl.pallas_call(
        body, name="fox_gates_bwd", grid=(nb,), in_specs=[row, row, row, one], out_specs=[row, one],
        out_shape=[jax.ShapeDtypeStruct((T, LANES), F32), jax.ShapeDtypeStruct((1, LANES), F32)],
        scratch_shapes=[pltpu.VMEM((1, LANES), F32)], compiler_params=_params("arbitrary"),
    )(dcq, dck, fz, bf)


def _lane_pick(ref_value, h):
    lane = lax.broadcasted_iota(jnp.int32, (1, LANES), 1)
    return jnp.sum(jnp.where(lane == h, ref_value, 0.0), axis=1, keepdims=True)


def _lane_put(old, h, col):
    lane = lax.broadcasted_iota(jnp.int32, (1, LANES), 1)
    return jnp.where(lane == h, col, old)


FOX_HEADS = 4


def _causal(rows, cols, row0=0, transposed=False):
    row = row0 + lax.broadcasted_iota(jnp.int32, (rows, cols), 0)
    col = lax.broadcasted_iota(jnp.int32, (rows, cols), 1)
    return col >= row if transposed else row >= col


def _fox_fwd(q, k, v, ccol, crow, blk):
    T, D = q.shape
    H = D // HEAD_DIM
    nb = T // blk

    def body(q_ref, k_ref, v_ref, ccol_ref, crow_ref, o_ref, lse_ref, m_s, l_s, acc_s):
        qi, hg, kj = pl.program_id(0), pl.program_id(1), pl.program_id(2)

        @pl.when(kj == 0)
        def _():
            m_s[...] = jnp.full_like(m_s, NEG)
            l_s[...] = jnp.zeros_like(l_s)
            acc_s[...] = jnp.zeros_like(acc_s)

        @pl.when((kj == 0) & (hg == 0))
        def _():
            lse_ref[...] = jnp.zeros_like(lse_ref)

        def step(diagonal):
            heads = range(FOX_HEADS)
            sls = [slice(hh * HEAD_DIM, (hh + 1) * HEAD_DIM) for hh in heads]
            m_old = [m_s[hh] for hh in heads]
            l_old = [l_s[hh] for hh in heads]
            acc_old = [acc_s[:, sl] for sl in sls]
            s = []
            for hh in heads:
                h = hg * FOX_HEADS + hh
                bias = _lane_pick(ccol_ref[...], h) - crow_ref[pl.ds(h, 1), :]
                sc = lax.dot_general(q_ref[:, sls[hh]], k_ref[:, sls[hh]], _NT, preferred_element_type=F32) * SCALE + bias
                s.append(jnp.where(_causal(blk, blk), sc, NEG) if diagonal else sc)
            m_new = [jnp.maximum(m_old[hh], jnp.max(s[hh], axis=1, keepdims=True)) for hh in heads]
            p = [jnp.exp(s[hh] - m_new[hh]) for hh in heads]
            alpha = [jnp.exp(m_old[hh] - m_new[hh]) for hh in heads]
            pv = [jnp.dot(p[hh].astype(BF16), v_ref[:, sls[hh]], preferred_element_type=F32) for hh in heads]
            for hh in heads:
                l_s[hh] = alpha[hh] * l_old[hh] + jnp.sum(p[hh], axis=1, keepdims=True)
                acc_s[:, sls[hh]] = alpha[hh] * acc_old[hh] + pv[hh]
                m_s[hh] = m_new[hh]

        @pl.when(kj < qi)
        def _():
            step(False)

        @pl.when(kj == qi)
        def _():
            step(True)
            for hh in range(FOX_HEADS):
                sl = slice(hh * HEAD_DIM, (hh + 1) * HEAD_DIM)
                o_ref[:, sl] = (acc_s[:, sl] / l_s[hh]).astype(BF16)
                lse_ref[...] = _lane_put(lse_ref[...], hg * FOX_HEADS + hh, m_s[hh] + jnp.log(l_s[hh]))

    wide = FOX_HEADS * HEAD_DIM
    qsp = pl.BlockSpec((blk, wide), lambda qi, h, kj: (qi, h))
    ksp = pl.BlockSpec((blk, wide), lambda qi, h, kj: (jnp.minimum(kj, qi), h))
    stat = pl.BlockSpec((blk, LANES), lambda qi, h, kj: (qi, 0))
    return pl.pallas_call(
        body, name="fox_fwd", grid=(nb, H // FOX_HEADS, nb),
        in_specs=[qsp, ksp, ksp, stat, pl.BlockSpec((H, blk), lambda qi, h, kj: (0, jnp.minimum(kj, qi)))],
        out_specs=[qsp, stat],
        out_shape=[jax.ShapeDtypeStruct((T, D), BF16), jax.ShapeDtypeStruct((T, LANES), F32)],
        scratch_shapes=[pltpu.VMEM((FOX_HEADS, blk, 1), F32), pltpu.VMEM((FOX_HEADS, blk, 1), F32),
                        pltpu.VMEM((blk, wide), F32)],
        compiler_params=_params("arbitrary", "arbitrary", "arbitrary"),
    )(q, k, v, ccol, crow)


def _head_stat(name, a, b):
    T, D = a.shape
    H = D // HEAD_DIM
    bt = _tile(T, 256)

    def body(a_ref, b_ref, o_ref):
        out = jnp.zeros((bt, LANES), F32)
        for h in range(H):
            sl = slice(h * HEAD_DIM, (h + 1) * HEAD_DIM)
            prod = a_ref[:, sl].astype(F32) * b_ref[:, sl].astype(F32)
            out = _lane_put(out, h, jnp.sum(prod, axis=1, keepdims=True))
        o_ref[...] = out

    row = pl.BlockSpec((bt, D), lambda i: (i, 0))
    return pl.pallas_call(
        body, name=name, grid=(T // bt,), in_specs=[row, row],
        out_specs=pl.BlockSpec((bt, LANES), lambda i: (i, 0)),
        out_shape=jax.ShapeDtypeStruct((T, LANES), F32), compiler_params=_params("parallel"),
    )(a, b)


def _fox_bwd_dq(q, k, v, do, ccol, crow, lse, delta, blk):
    T, D = q.shape
    H = D // HEAD_DIM
    nb = T // blk

    def body(q_ref, k_ref, v_ref, do_ref, ccol_ref, crow_ref, lse_ref, delta_ref, dq_ref, dc_ref, acc_s, dc_s):
        qi, hg, kj = pl.program_id(0), pl.program_id(1), pl.program_id(2)

        @pl.when(kj == 0)
        def _():
            acc_s[...] = jnp.zeros_like(acc_s)
            dc_s[...] = jnp.zeros_like(dc_s)

        @pl.when((kj == 0) & (hg == 0))
        def _():
            dc_ref[...] = jnp.zeros_like(dc_ref)

        def step(diagonal):
            heads = range(FOX_HEADS)
            sls = [slice(hh * HEAD_DIM, (hh + 1) * HEAD_DIM) for hh in heads]
            p, dp = [], []
            for hh in heads:
                h = hg * FOX_HEADS + hh
                bias = _lane_pick(ccol_ref[...], h) - crow_ref[pl.ds(h, 1), :]
                sc = lax.dot_general(q_ref[:, sls[hh]], k_ref[:, sls[hh]], _NT, preferred_element_type=F32) * SCALE + bias
                if diagonal:
                    sc = jnp.where(_causal(blk, blk), sc, NEG)
                p.append(jnp.exp(sc - _lane_pick(lse_ref[...], h)))
                dp.append(lax.dot_general(do_ref[:, sls[hh]], v_ref[:, sls[hh]], _NT, preferred_element_type=F32)
                          - _lane_pick(delta_ref[...], h))
            ds = [p[hh] * dp[hh] for hh in heads]
            for hh in heads:
                acc_s[:, sls[hh]] += jnp.dot(ds[hh].astype(BF16), k_ref[:, sls[hh]], preferred_element_type=F32)
                dc_s[hh] += jnp.sum(ds[hh], axis=1, keepdims=True)

        @pl.when(kj < qi)
        def _():
            step(False)

        @pl.when(kj == qi)
        def _():
            step(True)
            dq_ref[...] = acc_s[...] * SCALE
            for hh in range(FOX_HEADS):
                dc_ref[...] = _lane_put(dc_ref[...], hg * FOX_HEADS + hh, dc_s[hh])

    wide = FOX_HEADS * HEAD_DIM
    qsp = pl.BlockSpec((blk, wide), lambda qi, h, kj: (qi, h))
    ksp = pl.BlockSpec((blk, wide), lambda qi, h, kj: (jnp.minimum(kj, qi), h))
    stat = pl.BlockSpec((blk, LANES), lambda qi, h, kj: (qi, 0))
    return pl.pallas_call(
        body, name="fox_bwd_dq", grid=(nb, H // FOX_HEADS, nb),
        in_specs=[qsp, ksp, ksp, qsp, stat, pl.BlockSpec((H, blk), lambda qi, h, kj: (0, jnp.minimum(kj, qi))),
                  stat, stat],
        out_specs=[qsp, stat],
        out_shape=[jax.ShapeDtypeStruct((T, D), F32), jax.ShapeDtypeStruct((T, LANES), F32)],
        scratch_shapes=[pltpu.VMEM((blk, wide), F32), pltpu.VMEM((FOX_HEADS, blk, 1), F32)],
        compiler_params=_params("arbitrary", "arbitrary", "arbitrary"),
    )(q, k, v, do, ccol, crow, lse, delta)


def _fox_bwd_dkv(q, k, v, do, ccol, crow, lse_row, delta_row, blk):
    T, D = q.shape
    H = D // HEAD_DIM
    nb = T // blk

    def body(q_ref, k_ref, v_ref, do_ref, ccol_ref, crow_ref, lse_ref, delta_ref, dk_ref, dv_ref, dc_ref,
             dk_s, dv_s, dc_s):
        kj, hg, qi = pl.program_id(0), pl.program_id(1), pl.program_id(2)

        @pl.when(qi == 0)
        def _():
            dk_s[...] = jnp.zeros_like(dk_s)
            dv_s[...] = jnp.zeros_like(dv_s)
            dc_s[...] = jnp.zeros_like(dc_s)

        @pl.when((qi == 0) & (hg == 0))
        def _():
            dc_ref[...] = jnp.zeros_like(dc_ref)

        def step(diagonal):
            heads = range(FOX_HEADS)
            sls = [slice(hh * HEAD_DIM, (hh + 1) * HEAD_DIM) for hh in heads]
            pt, dpt = [], []
            for hh in heads:
                h = hg * FOX_HEADS + hh
                bias = crow_ref[pl.ds(h, 1), :] - _lane_pick(ccol_ref[...], h)
                st = lax.dot_general(k_ref[:, sls[hh]], q_ref[:, sls[hh]], _NT, preferred_element_type=F32) * SCALE + bias
                if diagonal:
                    st = jnp.where(_causal(blk, blk, transposed=True), st, NEG)
                pt.append(jnp.exp(st - lse_ref[pl.ds(h, 1), :]))
                dpt.append(lax.dot_general(v_ref[:, sls[hh]], do_ref[:, sls[hh]], _NT, preferred_element_type=F32)
                           - delta_ref[pl.ds(h, 1), :])
            dst = [pt[hh] * dpt[hh] for hh in heads]
            for hh in heads:
                dv_s[:, sls[hh]] += jnp.dot(pt[hh].astype(BF16), do_ref[:, sls[hh]], preferred_element_type=F32)
                dk_s[:, sls[hh]] += jnp.dot(dst[hh].astype(BF16), q_ref[:, sls[hh]], preferred_element_type=F32)
                dc_s[hh] -= jnp.sum(dst[hh], axis=1, keepdims=True)

        @pl.when(qi > kj)
        def _():
            step(False)

        @pl.when(qi == kj)
        def _():
            step(True)

        @pl.when(qi == nb - 1)
        def _():
            dk_ref[...] = dk_s[...] * SCALE
            dv_ref[...] = dv_s[...]
            for hh in range(FOX_HEADS):
                dc_ref[...] = _lane_put(dc_ref[...], hg * FOX_HEADS + hh, dc_s[hh])

    wide = FOX_HEADS * HEAD_DIM
    ksp = pl.BlockSpec((blk, wide), lambda kj, h, qi: (kj, h))
    qsp = pl.BlockSpec((blk, wide), lambda kj, h, qi: (jnp.maximum(qi, kj), h))
    kstat = pl.BlockSpec((blk, LANES), lambda kj, h, qi: (kj, 0))
    qrow = pl.BlockSpec((H, blk), lambda kj, h, qi: (0, jnp.maximum(qi, kj)))
    return pl.pallas_call(
        body, name="fox_bwd_dkv", grid=(nb, H // FOX_HEADS, nb),
        in_specs=[qsp, ksp, ksp, qsp, kstat, qrow, qrow, qrow],
        out_specs=[ksp, ksp, kstat],
        out_shape=[jax.ShapeDtypeStruct((T, D), F32), jax.ShapeDtypeStruct((T, D), F32),
                   jax.ShapeDtypeStruct((T, LANES), F32)],
        scratch_shapes=[pltpu.VMEM((blk, wide), F32), pltpu.VMEM((blk, wide), F32),
                        pltpu.VMEM((FOX_HEADS, blk, 1), F32)],
        compiler_params=_params("arbitrary", "arbitrary", "arbitrary"),
    )(q, k, v, do, ccol, crow, lse_row, delta_row)


def _rel_index():
    j = np.arange(ROLL_W)
    u = np.where(j < 2 * WIN, j, j - ROLL_W)
    return np.clip(WIN - u, -(CHUNK - 1), REL_CLIP) + (CHUNK - 1)


def _band_bias(g_row, bias_s):
    base = jnp.broadcast_to(g_row, (CHUNK, ROLL_W))
    rowb = lax.broadcasted_iota(jnp.int32, (CHUNK, ROLL_W), 0)
    for bit in range(CHUNK.bit_length() - 1):
        base = jnp.where((rowb >> bit) & 1 == 1, pltpu.roll(base, 1 << bit, 1), base)
    kc = lax.broadcasted_iota(jnp.int32, (CHUNK, 2 * WIN), 1) // CHUNK
    for a in range(N_PREV_CHUNKS):
        part = base if a == 0 else pltpu.roll(base, CHUNK * a, 1)
        valid = (kc >= a) & (kc <= a + N_PREV_CHUNKS)
        bias_s[a * CHUNK:(a + 1) * CHUNK, :] = jnp.where(valid, part[:, :2 * WIN], NEG)


def _chunk_probs(q, kp, kc, bias_s, qi):
    sp = lax.dot_general(q, kp, _NT, preferred_element_type=F32) * SCALE + bias_s[:, :WIN]
    sc = lax.dot_general(q, kc, _NT, preferred_element_type=F32) * SCALE + bias_s[:, WIN:]
    sp = jnp.where(qi > 0, sp, NEG)
    m = jnp.maximum(jnp.max(sp, axis=1, keepdims=True), jnp.max(sc, axis=1, keepdims=True))
    pp, pc = jnp.exp(sp - m), jnp.exp(sc - m)
    inv = 1.0 / (jnp.sum(pp, axis=1, keepdims=True) + jnp.sum(pc, axis=1, keepdims=True))
    return pp * inv, pc * inv


def _chunk_fwd(q, k, v, g_rows):
    T, D = q.shape
    H = D // HEAD_DIM
    nq = T // WIN

    def body(q_ref, kp_ref, kc_ref, vp_ref, vc_ref, g_ref, o_ref, bias_s):
        qi = pl.program_id(1)

        @pl.when(qi == 0)
        def _():
            _band_bias(g_ref[...], bias_s)

        pp, pc = _chunk_probs(q_ref[...], kp_ref[...], kc_ref[...], bias_s, qi)
        o = jnp.dot(pp.astype(BF16), vp_ref[...], preferred_element_type=F32)
        o += jnp.dot(pc.astype(BF16), vc_ref[...], preferred_element_type=F32)
        o_ref[...] = o.astype(BF16)

    cur = pl.BlockSpec((WIN, HEAD_DIM), lambda h, qi: (qi, h))
    prev = pl.BlockSpec((WIN, HEAD_DIM), lambda h, qi: (jnp.maximum(qi - 1, 0), h))
    return pl.pallas_call(
        body, name="chunk_fwd", grid=(H, nq),
        in_specs=[cur, prev, cur, prev, cur, pl.BlockSpec((None, 1, ROLL_W), lambda h, qi: (h, 0, 0))],
        out_specs=cur, out_shape=jax.ShapeDtypeStruct((T, D), BF16),
        scratch_shapes=[pltpu.VMEM((WIN, 2 * WIN), F32)],
        compiler_params=_params("arbitrary", "arbitrary"),
    )(q, k, k, v, v, g_rows)


def _chunk_bwd(q, k, v, g_rows, do, o):
    T, D = q.shape
    H = D // HEAD_DIM
    nq = T // WIN

    def body(q_ref, kp_ref, kc_ref, vp_ref, vc_ref, g_ref, do_ref, o_ref,
             dq_ref, dk_ref, dv_ref, ds_ref, bias_s, ck_s, cv_s):
        qi = pl.program_id(1)

        @pl.when(qi == 0)
        def _():
            _band_bias(g_ref[...], bias_s)
            ds_ref[...] = jnp.zeros_like(ds_ref)
            ck_s[...] = jnp.zeros_like(ck_s)
            cv_s[...] = jnp.zeros_like(cv_s)

        @pl.when(qi < nq)
        def _():
            qv, dov = q_ref[...], do_ref[...]
            pp, pc = _chunk_probs(qv, kp_ref[...], kc_ref[...], bias_s, qi)
            delta = jnp.sum(dov.astype(F32) * o_ref[...].astype(F32), axis=1, keepdims=True)
            dsp = pp * (lax.dot_general(dov, vp_ref[...], _NT, preferred_element_type=F32) - delta)
            dsc = pc * (lax.dot_general(dov, vc_ref[...], _NT, preferred_element_type=F32) - delta)
            dsp16, dsc16 = dsp.astype(BF16), dsc.astype(BF16)
            dq = jnp.dot(dsp16, kp_ref[...], preferred_element_type=F32)
            dq += jnp.dot(dsc16, kc_ref[...], preferred_element_type=F32)
            dq_ref[...] = dq * SCALE
            ds_ref[:, :WIN] += dsp
            ds_ref[:, WIN:] += dsc
            dk_ref[...] = ck_s[...] + lax.dot_general(dsp16, qv, _TN, preferred_element_type=F32) * SCALE
            dv_ref[...] = cv_s[...] + lax.dot_general(pp.astype(BF16), dov, _TN, preferred_element_type=F32)
            ck_s[...] = lax.dot_general(dsc16, qv, _TN, preferred_element_type=F32) * SCALE
            cv_s[...] = lax.dot_general(pc.astype(BF16), dov, _TN, preferred_element_type=F32)

        @pl.when(qi == nq)
        def _():
            dk_ref[...] = ck_s[...]
            dv_ref[...] = cv_s[...]

    cur = pl.BlockSpec((WIN, HEAD_DIM), lambda h, qi: (jnp.minimum(qi, nq - 1), h))
    prev = pl.BlockSpec((WIN, HEAD_DIM), lambda h, qi: (jnp.clip(qi - 1, 0, nq - 2), h))
    late = pl.BlockSpec((WIN, HEAD_DIM), lambda h, qi: (jnp.maximum(qi - 1, 0), h))
    return pl.pallas_call(
        body, name="chunk_bwd", grid=(H, nq + 1),
        in_specs=[cur, prev, cur, prev, cur, pl.BlockSpec((None, 1, ROLL_W), lambda h, qi: (h, 0, 0)), cur, cur],
        out_specs=[cur, late, late, pl.BlockSpec((None, WIN, 2 * WIN), lambda h, qi: (h, 0, 0))],
        out_shape=[jax.ShapeDtypeStruct((T, D), F32)] * 3 + [jax.ShapeDtypeStruct((H, WIN, 2 * WIN), F32)],
        scratch_shapes=[pltpu.VMEM((WIN, 2 * WIN), F32), pltpu.VMEM((WIN, HEAD_DIM), F32),
                        pltpu.VMEM((WIN, HEAD_DIM), F32)],
        compiler_params=_params("arbitrary", "arbitrary"),
    )(q, k, k, v, v, g_rows, do, o)


def _rel_grad(ds_sum, onehot):
    H = ds_sum.shape[0]

    def body(ds_ref, oh_ref, out_ref):
        y = jnp.zeros((CHUNK, ROLL_W), F32)
        pad = jnp.zeros((CHUNK, ROLL_W - 2 * WIN), F32)
        for a in range(N_PREV_CHUNKS):
            part = jnp.concatenate([ds_ref[a * CHUNK:(a + 1) * CHUNK, :], pad], axis=1)
            y = y + (part if a == 0 else pltpu.roll(part, ROLL_W - CHUNK * a, 1))
        rowb = lax.broadcasted_iota(jnp.int32, (CHUNK, ROLL_W), 0)
        for bit in range(CHUNK.bit_length() - 1):
            y = jnp.where((rowb >> bit) & 1 == 1, pltpu.roll(y, ROLL_W - (1 << bit), 1), y)
        diag = jnp.broadcast_to(jnp.sum(y, axis=0, keepdims=True), (8, ROLL_W))
        out_ref[...] = jnp.dot(diag, oh_ref[...], precision=HIGHEST, preferred_element_type=F32)

    return pl.pallas_call(
        body, name="rel_grad", grid=(H,),
        in_specs=[pl.BlockSpec((None, WIN, 2 * WIN), lambda h: (h, 0, 0)),
                  pl.BlockSpec((ROLL_W, N_REL_PAD), lambda h: (0, 0))],
        out_specs=pl.BlockSpec((None, 8, N_REL_PAD), lambda h: (h, 0, 0)),
        out_shape=jax.ShapeDtypeStruct((H, 8, N_REL_PAD), F32), compiler_params=_params("parallel"),
    )(ds_sum, onehot)


def _loss_head(y, target):
    T, D = y.shape
    bt = _tile(T, 256)

    def body(y_ref, t_ref, dy_ref, loss_ref):
        @pl.when(pl.program_id(0) == 0)
        def _():
            loss_ref[...] = jnp.zeros_like(loss_ref)

        err = y_ref[...] - t_ref[...]
        dy_ref[...] = err * (1.0 / D)
        loss_ref[...] += 0.5 * jnp.sum(jnp.mean(err * err, axis=-1, keepdims=True))

    row = pl.BlockSpec((bt, D), lambda i: (i, 0))
    one = pl.BlockSpec((8, LANES), lambda i: (0, 0))
    return pl.pallas_call(
        body, name="loss_head", grid=(T // bt,), in_specs=[row, row], out_specs=[row, one],
        out_shape=[jax.ShapeDtypeStruct((T, D), F32), jax.ShapeDtypeStruct((8, LANES), F32)],
        compiler_params=_params("arbitrary"),
    )(y, target)


def _adamw(name, w, m, v, parts, row0=0, into=None):
    rows_all, C = w.shape
    P, R, _ = parts.shape
    c1 = 1.0 - ADAM_B1 ** ADAM_STEP
    c2 = 1.0 - ADAM_B2 ** ADAM_STEP

    def body(w_ref, m_ref, v_ref, p_ref, *rest):
        g_ref, d_ref, nm_ref, nv_ref = rest[-4:]
        g = p_ref[0].astype(F32)
        for i in range(1, P):
            g = g + p_ref[i].astype(F32)
        nm = ADAM_B1 * m_ref[...] + (1.0 - ADAM_B1) * g
        nv = ADAM_B2 * v_ref[...] + (1.0 - ADAM_B2) * (g * g)
        g_ref[...] = g
        nm_ref[...] = nm
        nv_ref[...] = nv
        d_ref[...] = -ADAM_LR * ((nm / c1) / (jnp.sqrt(nv / c2) + ADAM_EPS) + ADAM_WD * w_ref[...])

    if R % 8 == 0 and R >= 8:
        br = _tile(R, max(8, (128 * 1024) // C))
        first = row0 // br
        grid = (R // br,)
        wsp = pl.BlockSpec((br, C), lambda i: (first + i, 0))
        psp = pl.BlockSpec((P, br, C), lambda i: (0, i, 0))
    else:
        bc = _tile(C, 512) if C % LANES == 0 else C
        grid = (C // bc,)
        wsp = pl.BlockSpec((R, bc), lambda i: (0, i))
        psp = pl.BlockSpec((P, R, bc), lambda i: (0, 0, i))
    extra = [] if into is None else list(into)
    return pl.pallas_call(
        body, name=name, grid=grid,
        in_specs=[wsp, wsp, wsp, psp] + [ANY] * len(extra), out_specs=[wsp] * 4,
        out_shape=[jax.ShapeDtypeStruct((rows_all, C), F32)] * 4,
        input_output_aliases={4 + j: j for j in range(len(extra))}, compiler_params=_params("parallel"),
    )(w, m, v, parts, *extra)


def _sum_parts(name, parts, out_dtype):
    P, R, C = parts.shape
    br = _tile(R, max(8, (256 * 1024) // C)) if R % 8 == 0 else R

    def body(p_ref, o_ref):
        g = p_ref[0].astype(F32)
        for i in range(1, P):
            g = g + p_ref[i].astype(F32)
        o_ref[...] = g.astype(out_dtype)

    return pl.pallas_call(
        body, name=name, grid=(R // br,),
        in_specs=[pl.BlockSpec((P, br, C), lambda i: (0, i, 0))], out_specs=pl.BlockSpec((br, C), lambda i: (i, 0)),
        out_shape=jax.ShapeDtypeStruct((R, C), out_dtype), compiler_params=_params("parallel"),
    )(parts)


def _pair_sum(name, g, recv, core):
    _, R, C = g.shape

    def body(core_ref, g_ref, r_ref, o_ref):
        o_ref[...] = (g_ref[...] + r_ref[...]).astype(BF16)

    if R % 16 == 0:
        br = _tile(R, max(16, (256 * 1024) // C))
        blk, n = (None, br, C), R // br
        mine = lambda q, i, core_ref: (2 * q + core_ref[0], i, 0)
        same = lambda q, i, core_ref: (q, i, 0)
    else:
        bc = _tile(C, 512)
        blk, n = (None, R, bc), C // bc
        mine = lambda q, i, core_ref: (2 * q + core_ref[0], 0, i)
        same = lambda q, i, core_ref: (q, 0, i)
    return pl.pallas_call(
        body, name=name,
        grid_spec=pltpu.PrefetchScalarGridSpec(
            num_scalar_prefetch=1, grid=(N_CHIP, n),
            in_specs=[pl.BlockSpec(blk, mine), pl.BlockSpec(blk, same)], out_specs=pl.BlockSpec(blk, same)),
        out_shape=jax.ShapeDtypeStruct((N_CHIP, R, C), BF16), compiler_params=_params("parallel", "parallel"),
    )(core, g, recv)


def _position():
    x, y, c = lax.axis_index("x"), lax.axis_index("y"), lax.axis_index("c")
    return x, y, c, [(1 - x, y), (x, 1 - y), (1 - x, 1 - y)]


def _all_gather(name, xs):
    n = len(xs)

    def body(*refs):
        x_refs, out_refs = refs[:n], refs[n:2 * n]
        send_sems, recv_sems, local_sems = refs[2 * n:]
        x, y, c, chips = _position()
        me, sibling = (x, y, c), (x, y, 1 - c)

        def copy(a, k, block, to, src=None):
            px, py, pc = block
            slot = out_refs[a].at[4 * px + 2 * py + pc]
            return pltpu.make_async_remote_copy(
                src_ref=slot if src is None else src, dst_ref=slot,
                send_sem=send_sems.at[a, k], recv_sem=recv_sems.at[a, k], device_id=to, device_id_type=MESH)

        mine = [pltpu.make_async_copy(x_refs[a], out_refs[a].at[4 * x + 2 * y + c], local_sems.at[a]) for a in range(n)]
        for cp in mine:
            cp.start()
        first = []
        for a in range(n):
            first.append(copy(a, 0, me, sibling, src=x_refs[a]))
            first += [copy(a, 1 + j, me, (*chip, c), src=x_refs[a]) for j, chip in enumerate(chips)]
        for cp in first:
            cp.start()
        passed = []
        for j, chip in enumerate(chips):
            for a in range(n):
                copy(a, 1 + j, (*chip, c), me).wait_recv()
                fwd = copy(a, 4 + j, (*chip, c), sibling)
                fwd.start()
                passed.append(fwd)
        for a in range(n):
            copy(a, 0, sibling, me).wait_recv()
            for j, chip in enumerate(chips):
                copy(a, 4 + j, (*chip, 1 - c), me).wait_recv()
        for cp in first + passed:
            cp.wait_send()
        for cp in mine:
            cp.wait()

    return pl.pallas_call(
        body, name=name, in_specs=[ANY] * n, out_specs=[ANY] * n,
        out_shape=[jax.ShapeDtypeStruct((N_DEV,) + x.shape, x.dtype) for x in xs],
        scratch_shapes=[pltpu.SemaphoreType.DMA((n, 7)), pltpu.SemaphoreType.DMA((n, 7)),
                        pltpu.SemaphoreType.DMA((n,))],
    )(*xs)


HBM = pl.BlockSpec(memory_space=pltpu.HBM)
SEM = pl.BlockSpec(memory_space=pltpu.SEMAPHORE)
EFFECT = pltpu.SideEffectType.DATAFLOW_SIDE_EFFECTING


def _peers(x, y, c):
    return [(x, y, 1 - c), (1 - x, y, c), (x, 1 - y, c), (1 - x, 1 - y, c),
            (1 - x, y, 1 - c), (x, 1 - y, 1 - c), (1 - x, 1 - y, 1 - c)]


def _gather_start(name, lands):
    n = len(lands)

    def body(*refs):
        land = refs[:n]
        send_sems, recv_sems = refs[n:2 * n], refs[2 * n:3 * n]
        x, y, c, _ = _position()
        mine = 4 * x + 2 * y + c
        for a in range(n):
            for k, peer in enumerate(_peers(x, y, c)):
                pltpu.make_async_remote_copy(
                    src_ref=land[a].at[mine], dst_ref=land[a].at[mine], send_sem=send_sems[a].at[k],
                    recv_sem=recv_sems[a].at[k], device_id=peer, device_id_type=MESH).start()

    res = pl.pallas_call(
        body, name=name, in_specs=[HBM] * n, out_specs=[SEM] * (2 * n) + [HBM] * n,
        out_shape=[pltpu.SemaphoreType.DMA((N_DEV - 1,))] * (2 * n) + [pltpu.HBM(l.shape, l.dtype) for l in lands],
        input_output_aliases={a: 2 * n + a for a in range(n)},
        compiler_params=pltpu.CompilerParams(has_side_effects=EFFECT),
    )(*[pltpu.with_memory_space_constraint(l, pltpu.HBM) for l in lands])
    return res[:n], res[n:2 * n], res[2 * n:]


def _gather_wait(name, lands, send_sems, recv_sems, after):
    n = len(lands)
    tail = [] if after is None else [after]

    def body(*refs):
        land = refs[:n]
        send_sems, recv_sems = refs[n:2 * n], refs[2 * n:3 * n]
        x, y, c, _ = _position()
        mine = 4 * x + 2 * y + c
        for a in range(n):
            for k, (px, py, pc) in enumerate(_peers(x, y, c)):
                cp = pltpu.make_async_remote_copy(
                    src_ref=land[a].at[mine], dst_ref=land[a].at[4 * px + 2 * py + pc], send_sem=send_sems[a].at[k],
                    recv_sem=recv_sems[a].at[k], device_id=(px, py, pc), device_id_type=MESH)
                cp.wait_send()
                cp.wait_recv()

    return pl.pallas_call(
        body, name=name, in_specs=[HBM] * n + [SEM] * (2 * n) + [ANY] * len(tail), out_specs=[HBM] * n,
        out_shape=[pltpu.HBM(l.shape, l.dtype) for l in lands], input_output_aliases={a: a for a in range(n)},
        compiler_params=pltpu.CompilerParams(has_side_effects=EFFECT),
    )(*lands, *send_sems, *recv_sems, *tail)


def _swap_cores(name, gs):
    n = len(gs)

    def body(*refs):
        g_refs, out_refs = refs[:n], refs[n:2 * n]
        send_sems, recv_sems = refs[2 * n:]
        x, y, c, _ = _position()
        copies = [pltpu.make_async_remote_copy(
            src_ref=g_refs[a].at[2 * q + (1 - c)], dst_ref=out_refs[a].at[q],
            send_sem=send_sems.at[a, q], recv_sem=recv_sems.at[a, q], device_id=(x, y, 1 - c), device_id_type=MESH)
            for a in range(n) for q in range(N_CHIP)]
        for cp in copies:
            cp.start()
        for cp in copies:
            cp.wait_recv()
        for cp in copies:
            cp.wait_send()

    return pl.pallas_call(
        body, name=name, in_specs=[ANY] * n, out_specs=[ANY] * n,
        out_shape=[jax.ShapeDtypeStruct((N_CHIP,) + g.shape[1:], g.dtype) for g in gs],
        scratch_shapes=[pltpu.SemaphoreType.DMA((n, N_CHIP)), pltpu.SemaphoreType.DMA((n, N_CHIP))],
    )(*gs)


def _swap_chips(name, ps):
    n = len(ps)

    def body(*refs):
        p_refs, out_refs = refs[:n], refs[n:2 * n]
        send_sems, recv_sems, local_sems = refs[2 * n:]
        x, y, c, chips = _position()
        mine = [pltpu.make_async_copy(p_refs[a].at[2 * x + y], out_refs[a].at[2 * x + y], local_sems.at[a])
                for a in range(n)]
        for cp in mine:
            cp.start()
        copies = [pltpu.make_async_remote_copy(
            src_ref=p_refs[a].at[2 * cx + cy], dst_ref=out_refs[a].at[2 * x + y],
            send_sem=send_sems.at[a, j], recv_sem=recv_sems.at[a, j], device_id=(cx, cy, c), device_id_type=MESH)
            for a in range(n) for j, (cx, cy) in enumerate(chips)]
        for cp in copies:
            cp.start()
        for a in range(n):
            for j, (cx, cy) in enumerate(chips):
                pltpu.make_async_remote_copy(
                    src_ref=p_refs[a].at[2 * x + y], dst_ref=out_refs[a].at[2 * cx + cy],
                    send_sem=send_sems.at[a, j], recv_sem=recv_sems.at[a, j], device_id=(cx, cy, c),
                    device_id_type=MESH).wait_recv()
        for cp in copies:
            cp.wait_send()
        for cp in mine:
            cp.wait()

    return pl.pallas_call(
        body, name=name, in_specs=[ANY] * n, out_specs=[ANY] * n,
        out_shape=[jax.ShapeDtypeStruct(p.shape, p.dtype) for p in ps],
        scratch_shapes=[pltpu.SemaphoreType.DMA((n, 3)), pltpu.SemaphoreType.DMA((n, 3)),
                        pltpu.SemaphoreType.DMA((n,))],
    )(*ps)


def _mlp_fwd(tag, h, g, w1g, w2):
    T, D = h.shape
    F = w2.shape[0]
    (n,) = _rmsnorm_fwd(f"mlp{tag}_norm", h, [g])
    a, hid = _mm(f"mlp{tag}_up", "nn", n, w1g, T, F, D, b_groups=N_DEV, out_dtypes=(BF16, BF16),
                 epilogue=_ep_relu_square)
    out = _mm(f"mlp{tag}_down", "nn", hid, w2, T, D, F, extras=(h,), epilogue=_ep_add)
    return out, (n, a, hid)


def _mlp_bwd(tag, h, g, w1g, w2, saved, dy):
    T, D = h.shape
    F = w2.shape[0]
    n, a, hid = saved
    dw2 = _mm(f"mlp{tag}_dw2", "tn", hid, dy, F, D, T)
    dpre = _mm(f"mlp{tag}_dpre", "nt", dy, w2, T, F, D, out_dtypes=(BF16,), extras=(a,), epilogue=_ep_times_2a)
    dw1 = _mm(f"mlp{tag}_dw1", "tn", n, dpre, D, F, T, out_groups=N_DEV)
    dn = _mm(f"mlp{tag}_dn", "nt", dpre, w1g, T, D, F, b_groups=N_DEV)
    dh, dg = _rmsnorm_bwd(f"mlp{tag}_norm_bwd", h, g, dn, dy)
    return dh, dg, dw1, dw2


def _local_step(x, target, W, fox_blk, arrive=lambda group, after: {}):
    T, D = x.shape
    H = D // HEAD_DIM
    G = {}
    W = dict(W)

    (ua,) = _rmsnorm_fwd("a_norm", x, [W["a_norm_g"]])
    qkv = _mm("a_qkv", "nt", ua, W["a_w_in_t"], T, 3 * D, D)
    fz = _mm("a_fz", "nt", ua, W["a_w_f_t"], T, LANES, D)
    qa, ka, va = _head_post("a_heads", qkv, D, [(0, W["a_q_g"]), (1, W["a_k_g"]), (2, None)])
    ccol = _gates_fwd(fz, W["a_b_f"], H)
    crow = ccol[:, :H].T
    oa, lse = _fox_fwd(qa, ka, va, ccol, crow, fox_blk)
    W.update(arrive("a_out", oa))
    h1 = _mm("a_out", "nn", oa, W["a_w_out"], T, D, D, extras=(x,), epilogue=_ep_add)
    W.update(arrive("middle", h1))
    h2, mlp0 = _mlp_fwd("0", h1, W["mlp_norm_g0"], W["mlp_w1_0"], W["mlp_w2_0"])
    nkv, ub = _rmsnorm_fwd("kv_b_norm", h2, [W["kv_norm_g"], W["b_norm_g"]])
    kv = _mm("kv_proj", "nn", nkv, W["kv_w"], T, 2 * D, D, b_groups=N_DEV)
    ks, vs = _head_post("kv_heads", kv, D, [(0, W["kv_k_g"]), (1, None)])
    qb_pre = _mm("b_q", "nn", ub, W["b_w_q"], T, D, D)
    (qb,) = _head_post("b_heads", qb_pre, D, [(0, W["b_q_g"])])
    g_rows = W["b_rel"][:, _rel_index()].reshape(H, 1, ROLL_W)
    ob = _chunk_fwd(qb, ks, vs, g_rows)
    W.update(arrive("late", ob))
    h3 = _mm("b_out", "nn", ob, W["b_w_out"], T, D, D, extras=(h2,), epilogue=_ep_add)
    h4, mlp1 = _mlp_fwd("1", h3, W["mlp_norm_g1"], W["mlp_w1_1"], W["mlp_w2_1"])
    dy, loss = _loss_head(h4, target)

    dh3, G["mlp_norm_g1"], G["mlp_w1_1"], G["mlp_w2_1"] = _mlp_bwd(
        "1", h3, W["mlp_norm_g1"], W["mlp_w1_1"], W["mlp_w2_1"], mlp1, dy)
    dob = _mm("b_out_dx", "nt", dh3, W["b_w_out"], T, D, D, out_dtypes=(BF16,))
    G["b_w_out"] = _mm("b_out_dw", "tn", ob, dh3, D, D, T)
    dqb, dks, dvs, ds_sum = _chunk_bwd(qb, ks, vs, g_rows, dob, ob)
    onehot = (jnp.asarray(_rel_index())[:, None] == jnp.arange(N_REL_PAD)[None, :]).astype(F32)
    G["b_rel"] = _rel_grad(ds_sum, onehot)[:, 0, :]
    dqb_pre, (G["b_q_g"],) = _head_post_bwd("b_heads_bwd", qb_pre, D, [(0, W["b_q_g"], dqb)])
    G["b_w_q"] = _mm("b_q_dw", "tn", ub, dqb_pre, D, D, T)
    dub = _mm("b_q_dx", "nt", dqb_pre, W["b_w_q"], T, D, D)
    dkv, (G["kv_k_g"],) = _head_post_bwd("kv_heads_bwd", kv, D, [(0, W["kv_k_g"], dks), (1, None, dvs)])
    G["kv_w"] = _mm("kv_dw", "tn", nkv, dkv, D, 2 * D, T, out_groups=N_DEV)
    dnkv = _mm("kv_dx", "nt", dkv, W["kv_w"], T, D, 2 * D, b_groups=N_DEV)
    dh2, G["b_norm_g"] = _rmsnorm_bwd("b_norm_bwd", h2, W["b_norm_g"], dub, dh3)
    dh2, G["kv_norm_g"] = _rmsnorm_bwd("kv_norm_bwd", h2, W["kv_norm_g"], dnkv, dh2)
    dh1, G["mlp_norm_g0"], G["mlp_w1_0"], G["mlp_w2_0"] = _mlp_bwd(
        "0", h1, W["mlp_norm_g0"], W["mlp_w1_0"], W["mlp_w2_0"], mlp0, dh2)
    doa = _mm("a_out_dx", "nt", dh1, W["a_w_out"], T, D, D, out_dtypes=(BF16,))
    G["a_w_out"] = _mm("a_out_dw", "tn", oa, dh1, D, D, T)
    delta = _head_stat("fox_delta", doa, oa)
    dqa, dcq = _fox_bwd_dq(qa, ka, va, doa, ccol, crow, lse, delta, fox_blk)
    dka, dva, dck = _fox_bwd_dkv(qa, ka, va, doa, ccol, crow, lse[:, :H].T, delta[:, :H].T, fox_blk)
    dfz, G["a_b_f"] = _gates_bwd(dcq, dck, fz, W["a_b_f"], H)
    dqkv, (G["a_q_g"], G["a_k_g"]) = _head_post_bwd(
        "a_heads_bwd", qkv, D, [(0, W["a_q_g"], dqa), (1, W["a_k_g"], dka), (2, None, dva)])
    G["a_w_qkv_t"] = _mm("a_qkv_dw", "tn", dqkv, ua, 3 * D, D, T)
    G["a_w_f_t"] = _mm("a_fz_dw", "tn", dfz, ua, LANES, D, T)
    dua = _mm("a_qkv_dx", "nn", dqkv, W["a_w_in_t"], T, D, 3 * D)
    dua = _mm("a_fz_dx", "nn", dfz, W["a_w_f_t"], T, D, LANES, extras=(dua,), epilogue=_ep_add)
    dx, G["a_norm_g"] = _rmsnorm_bwd("a_norm_bwd", x, W["a_norm_g"], dua, dh1)
    return loss, dx, G


def _pad_lanes(a):
    return jnp.pad(a, ((0, 0), (0, LANES - a.shape[1])))


def kernel(x, a_norm_g, a_w_in, a_b_f, a_q_g, a_k_g, a_w_out, mlp_norm_g, mlp_w1, mlp_w2, kv_norm_g, kv_w, kv_k_g, b_norm_g, b_w_q, b_q_g, b_rel, b_w_out, loss_target, m_a_norm_g, m_a_w_in, m_a_b_f, m_a_q_g, m_a_k_g, m_a_w_out, m_mlp_norm_g, m_mlp_w1, m_mlp_w2, m_kv_norm_g, m_kv_w, m_kv_k_g, m_b_norm_g, m_b_w_q, m_b_q_g, m_b_rel, m_b_w_out, v_a_norm_g, v_a_w_in, v_a_b_f, v_a_q_g, v_a_k_g, v_a_w_out, v_mlp_norm_g, v_mlp_w1, v_mlp_w2, v_kv_norm_g, v_kv_w, v_kv_k_g, v_b_norm_g, v_b_w_q, v_b_q_g, v_b_rel, v_b_w_out):
    T, D = x.shape[1], x.shape[2]
    H = D // HEAD_DIM
    F = mlp_w2.shape[1] * N_DEV
    me = 4 * lax.axis_index("x") + 2 * lax.axis_index("y") + lax.axis_index("c")
    core = lax.axis_index("c").astype(jnp.int32).reshape(1)
    n_in = a_w_in.shape[2]
    n_rel = b_rel.shape[2]

    small_in = jnp.concatenate([a_norm_g.reshape(1, -1), b_rel.reshape(1, -1)], axis=1)
    local = [jnp.swapaxes(a_w_in[0], 0, 1), a_w_out[0], mlp_w1[0], mlp_w2[0], kv_w, b_w_q[0], b_w_out[0],
             mlp_w1[1], mlp_w2[1]]
    sending = [small_in] + [w.astype(BF16) for w in local]
    lands = [lax.dynamic_update_slice(lax.empty((N_DEV,) + w.shape, w.dtype), w[None], (me,) + (0,) * w.ndim)
             for w in sending]
    send_sems, recv_sems, lands = _gather_start("gather_start", lands)
    groups = {"first": [0, 1], "a_out": [2], "middle": [3, 4, 5, 6], "late": [7, 8, 9]}

    def arrive(group, after):
        idx = groups[group]
        got = _gather_wait(f"gather_wait_{group}", [lands[i] for i in idx], [send_sems[i] for i in idx],
                           [recv_sems[i] for i in idx], after)
        if group == "first":
            g_small, w_in_t = got[0], got[1].reshape(N_DEV * n_in, D)
            rel = jnp.transpose(g_small[:, 0, D // N_DEV:].reshape(N_DEV, H, n_rel), (1, 0, 2))
            return {"a_norm_g": g_small[:, 0, :D // N_DEV].reshape(1, D), "b_rel": rel.reshape(H, N_DEV * n_rel),
                    "a_w_in_t": w_in_t, "a_w_f_t": jnp.pad(w_in_t[3 * D:], ((0, LANES - H), (0, 0)))}
        if group == "a_out":
            return {"a_w_out": got[0].reshape(D, D)}
        if group == "middle":
            return {"mlp_w1_0": got[0], "mlp_w2_0": got[1].reshape(F, D), "kv_w": got[2], "b_w_q": got[3].reshape(D, D)}
        return {"b_w_out": got[0].reshape(D, D), "mlp_w1_1": got[1], "mlp_w2_1": got[2].reshape(F, D)}

    W = {
        "a_b_f": _pad_lanes(a_b_f), "a_q_g": a_q_g, "a_k_g": a_k_g,
        "mlp_norm_g0": mlp_norm_g[0:1], "mlp_norm_g1": mlp_norm_g[1:2],
        "kv_norm_g": kv_norm_g.reshape(1, D), "kv_k_g": kv_k_g.reshape(1, HEAD_DIM),
        "b_norm_g": b_norm_g, "b_q_g": b_q_g, **arrive("first", None),
    }

    loss_tile, dx, G = _local_step(x[0], loss_target[0], W, min(T, 512), arrive)

    dw_in_t = jnp.concatenate([G["a_w_qkv_t"], G["a_w_f_t"][:H]], axis=0).reshape(N_DEV, n_in, D)
    big = [dw_in_t, G["a_w_out"].reshape(N_DEV, D // N_DEV, D), G["mlp_w1_0"], G["mlp_w1_1"],
           G["mlp_w2_0"].reshape(N_DEV, F // N_DEV, D), G["mlp_w2_1"].reshape(N_DEV, F // N_DEV, D),
           G["kv_w"], G["b_w_q"].reshape(N_DEV, D // N_DEV, D), G["b_w_out"].reshape(N_DEV, D // N_DEV, D)]
    names = ["a_w_in", "a_w_out", "mlp_w1_0", "mlp_w1_1", "mlp_w2_0", "mlp_w2_1", "kv_w", "b_w_q", "b_w_out"]
    from_sibling = _swap_cores("reduce_cores", big)
    pair = [_pair_sum(f"pair_sum_{nm}", g, r, core) for nm, g, r in zip(names, big, from_sibling)]
    parts = dict(zip(names, _swap_chips("reduce_chips", pair)))

    flat = lambda a: a.reshape(-1, a.shape[-1])
    upd = {}
    in_t = [jnp.swapaxes(a[0], 0, 1) for a in (a_w_in, m_a_w_in, v_a_w_in)]
    upd["a_w_in"] = [jnp.swapaxes(o, 0, 1)[None] for o in _adamw("adamw_a_w_in", *in_t, parts["a_w_in"])]
    for nm, w, m, v in [("a_w_out", a_w_out, m_a_w_out, v_a_w_out), ("kv_w", kv_w, m_kv_w, v_kv_w),
                        ("b_w_q", b_w_q, m_b_w_q, v_b_w_q), ("b_w_out", b_w_out, m_b_w_out, v_b_w_out)]:
        upd[nm] = [o.reshape(w.shape) for o in _adamw(f"adamw_{nm}", flat(w), flat(m), flat(v), parts[nm])]
    for nm, w, m, v in [("mlp_w1", mlp_w1, m_mlp_w1, v_mlp_w1), ("mlp_w2", mlp_w2, m_mlp_w2, v_mlp_w2)]:
        layer0 = _adamw(f"adamw_{nm}_0", flat(w), flat(m), flat(v), parts[nm + "_0"])
        both = _adamw(f"adamw_{nm}_1", flat(w), flat(m), flat(v), parts[nm + "_1"], row0=w.shape[1], into=layer0)
        upd[nm] = [o.reshape(w.shape) for o in both]

    def rows8(vecs):
        r = lax.broadcasted_iota(jnp.int32, (8, LANES), 0)
        tile = jnp.zeros((8, LANES), F32)
        for i, vec in enumerate(vecs):
            tile = jnp.where(r == i, vec, tile)
        return tile

    rows = lambda a: a.reshape(-1, LANES)
    packed = [rows(G["a_norm_g"]), rows(G["mlp_norm_g0"]), rows(G["mlp_norm_g1"]), rows(G["kv_norm_g"]),
              rows(G["b_norm_g"]), rows(G["b_rel"]),
              rows8([G["a_b_f"], G["a_q_g"], G["a_k_g"], G["kv_k_g"], G["b_q_g"], loss_tile[0:1]])]
    sizes = [p.shape[0] for p in packed]
    (all_small,) = _all_gather("gather_small_grads", [jnp.concatenate(packed, axis=0)])
    total = _sum_parts("sum_small_grads", all_small, F32)
    offs = np.concatenate([[0], np.cumsum(sizes)])
    s = [total[offs[i]:offs[i + 1]] for i in range(len(sizes))]
    loss = s[6][5, 0]
    g_a_norm = lax.dynamic_slice(s[0].reshape(1, D), (0, me * (D // N_DEV)), (1, D // N_DEV))
    g_rel = lax.dynamic_slice(s[5].reshape(H, N_REL_PAD), (0, me * n_rel), (H, n_rel))
    small = {
        "a_norm_g": (a_norm_g, m_a_norm_g, v_a_norm_g, g_a_norm),
        "a_b_f": (a_b_f, m_a_b_f, v_a_b_f, s[6][0:1, :H]),
        "a_q_g": (a_q_g, m_a_q_g, v_a_q_g, s[6][1:2]),
        "a_k_g": (a_k_g, m_a_k_g, v_a_k_g, s[6][2:3]),
        "mlp_norm_g": (mlp_norm_g, m_mlp_norm_g, v_mlp_norm_g,
                       jnp.concatenate([s[1].reshape(1, D), s[2].reshape(1, D)], axis=0)),
        "kv_norm_g": (kv_norm_g.reshape(1, D), m_kv_norm_g.reshape(1, D), v_kv_norm_g.reshape(1, D), s[3].reshape(1, D)),
        "kv_k_g": (kv_k_g.reshape(1, HEAD_DIM), m_kv_k_g.reshape(1, HEAD_DIM), v_kv_k_g.reshape(1, HEAD_DIM),
                   s[6][3:4]),
        "b_norm_g": (b_norm_g, m_b_norm_g, v_b_norm_g, s[4].reshape(1, D)),
        "b_q_g": (b_q_g, m_b_q_g, v_b_q_g, s[6][4:5]),
        "b_rel": (b_rel[0], m_b_rel[0], v_b_rel[0], g_rel),
    }
    for nm, (w, m, v, g) in small.items():
        upd[nm] = _adamw(f"adamw_{nm}", w, m, v, g[None])
    upd["kv_norm_g"] = [a.reshape(D) for a in upd["kv_norm_g"]]
    upd["kv_k_g"] = [a.reshape(HEAD_DIM) for a in upd["kv_k_g"]]
    upd["b_rel"] = [a[None] for a in upd["b_rel"]]

    order = ["a_norm_g", "a_w_in", "a_b_f", "a_q_g", "a_k_g", "a_w_out", "mlp_norm_g", "mlp_w1", "mlp_w2",
             "kv_norm_g", "kv_w", "kv_k_g", "b_norm_g", "b_w_q", "b_q_g", "b_rel", "b_w_out"]
    return (loss, dx[None], *[upd[n][0] for n in order], *[upd[n][1] for n in order],
            *[upd[n][2] for n in order], *[upd[n][3] for n in order])
```

```python
import functools

import numpy as np
import jax
import jax.numpy as jnp
from jax import lax
from jax.experimental import pallas as pl
from jax.experimental.pallas import tpu as pltpu

F32 = jnp.float32
BF16 = jnp.bfloat16
HIGHEST = lax.Precision.HIGHEST
MESH = pl.DeviceIdType.MESH
ANY = pl.BlockSpec(memory_space=pl.ANY)

N_DEV = 8
N_CHIP = 4
LANES = 128
HEAD_DIM = 128
CHUNK = 64
N_PREV_CHUNKS = 8
REL_CLIP = 256
N_REL = REL_CLIP + CHUNK
N_REL_PAD = 384
WIN = N_PREV_CHUNKS * CHUNK
ROLL_W = 4 * WIN
EPS = 1e-6
NEG = -1e30
SCALE = HEAD_DIM ** -0.5
ADAM_LR, ADAM_B1, ADAM_B2, ADAM_EPS, ADAM_WD, ADAM_STEP = 0.001, 0.9, 0.999, 1e-08, 0.01, 10
V7X_VMEM_BYTES = 64 * 1024 * 1024
VMEM_LIMIT = V7X_VMEM_BYTES * 3 // 4

_NN = (((1,), (0,)), ((), ()))
_NT = (((1,), (1,)), ((), ()))
_TN = (((0,), (0,)), ((), ()))


def _params(*sem):
    return pltpu.CompilerParams(dimension_semantics=sem or None, vmem_limit_bytes=VMEM_LIMIT)


def _tile(n, pref):
    t = n
    while t > pref and t % 2 == 0:
        t //= 2
    return t


def _ep_store(acc, ex, outs):
    outs[0][...] = acc.astype(outs[0].dtype)


def _ep_add(acc, ex, outs):
    outs[0][...] = (acc + ex[0][...]).astype(outs[0].dtype)


def _ep_relu_square(acc, ex, outs):
    a = jnp.maximum(acc, 0.0)
    outs[0][...] = a.astype(outs[0].dtype)
    outs[1][...] = (a * a).astype(outs[1].dtype)


def _ep_times_2a(acc, ex, outs):
    outs[0][...] = (acc * (2.0 * ex[0][...].astype(F32))).astype(outs[0].dtype)


def _mm(name, kind, a, b, M, N, K, *, b_groups=0, out_groups=0, out_dtypes=(F32,), extras=(),
        epilogue=_ep_store, tm=512, tn=1024, tk=2048, behind=()):
    tm, tn, tk = _tile(M, tm), _tile(N, tn), _tile(K, tk)
    if b_groups:
        cg = b.shape[-1]
        if kind == "nn":
            tn = min(tn, cg)
        else:
            tk = min(tk, cg)
    if out_groups:
        tn = min(tn, N // out_groups)
    grid = (M // tm, N // tn, K // tk)
    nk = grid[2]
    if kind == "nn":
        dims = _NN
        a_spec = pl.BlockSpec((tm, tk), lambda i, j, k: (i, k))
        if b_groups:
            npg = cg // tn
            b_spec = pl.BlockSpec((None, tk, tn), lambda i, j, k: (j // npg, k, j % npg))
        else:
            b_spec = pl.BlockSpec((tk, tn), lambda i, j, k: (k, j))
    elif kind == "nt":
        dims = _NT
        a_spec = pl.BlockSpec((tm, tk), lambda i, j, k: (i, k))
        if b_groups:
            npg = cg // tk
            b_spec = pl.BlockSpec((None, tn, tk), lambda i, j, k: (k // npg, j, k % npg))
        else:
            b_spec = pl.BlockSpec((tn, tk), lambda i, j, k: (j, k))
    else:
        dims = _TN
        a_spec = pl.BlockSpec((tk, tm), lambda i, j, k: (k, i))
        b_spec = pl.BlockSpec((tk, tn), lambda i, j, k: (k, j))
    tile_spec = pl.BlockSpec((tm, tn), lambda i, j, k: (i, j))
    if out_groups:
        ng = (N // out_groups) // tn
        out_spec = pl.BlockSpec((None, tm, tn), lambda i, j, k: (j // ng, i, j % ng))
        out_shape = [jax.ShapeDtypeStruct((out_groups, M, N // out_groups), d) for d in out_dtypes]
    else:
        out_spec = tile_spec
        out_shape = [jax.ShapeDtypeStruct((M, N), d) for d in out_dtypes]
    n_ex, n_out = len(extras), len(out_dtypes)

    def body(a_ref, b_ref, *rest):
        ex, outs, acc = rest[:n_ex], rest[-1 - n_out:-1], rest[-1]
        k = pl.program_id(2)

        @pl.when(k == 0)
        def _():
            acc[...] = jnp.zeros_like(acc)

        acc[...] += lax.dot_general(a_ref[...].astype(BF16), b_ref[...].astype(BF16), dims,
                                    preferred_element_type=F32)

        @pl.when(k == nk - 1)
        def _():
            epilogue(acc[...], ex, outs)

    res = pl.pallas_call(
        body, name=name, grid=grid,
        in_specs=[a_spec, b_spec] + [tile_spec] * n_ex + [ANY] * len(behind),
        out_specs=[out_spec] * n_out, out_shape=out_shape,
        scratch_shapes=[pltpu.VMEM((tm, tn), F32)],
        compiler_params=_params("parallel", "parallel", "arbitrary"),
    )(a, b, *extras, *behind)
    return res[0] if n_out == 1 else res


def _rmsnorm_fwd(name, x, gains):
    T, D = x.shape
    bt = _tile(T, 256)
    n = len(gains)

    def body(x_ref, *rest):
        xv = x_ref[...]
        y = xv * lax.rsqrt(jnp.mean(xv * xv, axis=-1, keepdims=True) + EPS)
        for g_ref, o_ref in zip(rest[:n], rest[n:]):
            o_ref[...] = (y * g_ref[...]).astype(BF16)

    row = pl.BlockSpec((bt, D), lambda i: (i, 0))
    gsp = pl.BlockSpec((1, D), lambda i: (0, 0))
    return pl.pallas_call(
        body, name=name, grid=(T // bt,), in_specs=[row] + [gsp] * n, out_specs=[row] * n,
        out_shape=[jax.ShapeDtypeStruct((T, D), BF16)] * n, compiler_params=_params("parallel"),
    )(x, *gains)


def _rmsnorm_bwd(name, x, g, dn, res):
    T, D = x.shape
    bt = _tile(T, 256)

    def body(x_ref, g_ref, dn_ref, res_ref, dx_ref, dg_ref):
        @pl.when(pl.program_id(0) == 0)
        def _():
            dg_ref[...] = jnp.zeros_like(dg_ref)

        xv, dnv = x_ref[...], dn_ref[...]
        r = lax.rsqrt(jnp.mean(xv * xv, axis=-1, keepdims=True) + EPS)
        xhat = xv * r
        dyg = dnv * g_ref[...]
        dx = r * (dyg - xhat * jnp.mean(dyg * xhat, axis=-1, keepdims=True))
        dx_ref[...] = res_ref[...] + dx
        dg_ref[...] += jnp.sum(dnv * xhat, axis=0, keepdims=True)

    row = pl.BlockSpec((bt, D), lambda i: (i, 0))
    gsp = pl.BlockSpec((1, D), lambda i: (0, 0))
    return pl.pallas_call(
        body, name=name, grid=(T // bt,), in_specs=[row, gsp, row, row], out_specs=[row, gsp],
        out_shape=[jax.ShapeDtypeStruct((T, D), F32), jax.ShapeDtypeStruct((1, D), F32)],
        compiler_params=_params("arbitrary"),
    )(x, g, dn, res)


def _head_post(name, x, D, items):
    T = x.shape[0]
    H = D // HEAD_DIM
    bt = _tile(T, 256)
    gains = [g for _, g in items if g is not None]
    n_it, n_g = len(items), len(gains)

    def body(*refs):
        x_refs, g_refs, o_refs = refs[:n_it], refs[n_it:n_it + n_g], refs[n_it + n_g:]
        gi = 0
        for it, (_, g) in enumerate(items):
            if g is None:
                o_refs[it][...] = x_refs[it][...].astype(BF16)
                continue
            gv = g_refs[gi][...]
            gi += 1
            for h in range(H):
                sl = slice(h * HEAD_DIM, (h + 1) * HEAD_DIM)
                xs = x_refs[it][:, sl]
                r = lax.rsqrt(jnp.mean(xs * xs, axis=-1, keepdims=True) + EPS)
                o_refs[it][:, sl] = (xs * r * gv).astype(BF16)

    in_specs = [pl.BlockSpec((bt, D), functools.partial(lambda cb, i: (i, cb), cb)) for cb, _ in items]
    in_specs += [pl.BlockSpec((1, HEAD_DIM), lambda i: (0, 0))] * n_g
    row = pl.BlockSpec((bt, D), lambda i: (i, 0))
    res = pl.pallas_call(
        body, name=name, grid=(T // bt,), in_specs=in_specs, out_specs=[row] * n_it,
        out_shape=[jax.ShapeDtypeStruct((T, D), BF16)] * n_it, compiler_params=_params("parallel"),
    )(*([x] * n_it), *gains)
    return res


def _head_post_bwd(name, x, D, items):
    T = x.shape[0]
    H = D // HEAD_DIM
    bt = _tile(T, 128)
    gains = [g for _, g, _ in items if g is not None]
    n_it, n_g = len(items), len(gains)

    def body(*refs):
        x_refs, dy_refs = refs[:n_it], refs[n_it:2 * n_it]
        g_refs = refs[2 * n_it:2 * n_it + n_g]
        dx_ref = refs[2 * n_it + n_g]
        dg_refs = refs[2 * n_it + n_g + 1:]

        @pl.when(pl.program_id(0) == 0)
        def _():
            for r in dg_refs:
                r[...] = jnp.zeros_like(r)

        gi = 0
        for it, (_, g, _) in enumerate(items):
            if g is None:
                dx_ref[:, it * D:(it + 1) * D] = dy_refs[it][...].astype(BF16)
                continue
            gv = g_refs[gi][...]
            dg = jnp.zeros((1, HEAD_DIM), F32)
            for h in range(H):
                sl = slice(h * HEAD_DIM, (h + 1) * HEAD_DIM)
                xs, dy = x_refs[it][:, sl], dy_refs[it][:, sl]
                r = lax.rsqrt(jnp.mean(xs * xs, axis=-1, keepdims=True) + EPS)
                xhat = xs * r
                dyg = dy * gv
                dx = r * (dyg - xhat * jnp.mean(dyg * xhat, axis=-1, keepdims=True))
                dx_ref[:, it * D + h * HEAD_DIM:it * D + (h + 1) * HEAD_DIM] = dx.astype(BF16)
                dg = dg + jnp.sum(dy * xhat, axis=0, keepdims=True)
            dg_refs[gi][...] += dg
            gi += 1

    in_specs = [pl.BlockSpec((bt, D), functools.partial(lambda cb, i: (i, cb), cb)) for cb, _, _ in items]
    in_specs += [pl.BlockSpec((bt, D), lambda i: (i, 0))] * n_it
    gsp = pl.BlockSpec((1, HEAD_DIM), lambda i: (0, 0))
    in_specs += [gsp] * n_g
    res = pl.pallas_call(
        body, name=name, grid=(T // bt,), in_specs=in_specs,
        out_specs=[pl.BlockSpec((bt, n_it * D), lambda i: (i, 0))] + [gsp] * n_g,
        out_shape=[jax.ShapeDtypeStruct((T, n_it * D), BF16)] + [jax.ShapeDtypeStruct((1, HEAD_DIM), F32)] * n_g,
        compiler_params=_params("arbitrary"),
    )(*([x] * n_it), *[dy for _, _, dy in items], *gains)
    return res[0], list(res[1:])


def _gates_fwd(fz, bf, H):
    T = fz.shape[0]
    bt = _tile(T, 512)

    def body(fz_ref, bf_ref, c_ref, carry):
        @pl.when(pl.program_id(0) == 0)
        def _():
            carry[...] = jnp.zeros_like(carry)

        z = fz_ref[...] + bf_ref[...]
        logf = jnp.minimum(z, 0.0) - jnp.log(1.0 + jnp.exp(-jnp.abs(z)))
        lane = lax.broadcasted_iota(jnp.int32, (bt, LANES), 1)
        logf = jnp.where(lane < H, logf, 0.0)
        tri = (lax.broadcasted_iota(jnp.int32, (bt, bt), 0) >= lax.broadcasted_iota(jnp.int32, (bt, bt), 1))
        c = jnp.dot(tri.astype(F32), logf, precision=HIGHEST, preferred_element_type=F32) + carry[...]
        c_ref[...] = c
        carry[...] = c[bt - 1:bt, :]

    row = pl.BlockSpec((bt, LANES), lambda i: (i, 0))
    return pl.pallas_call(
        body, name="fox_gates_fwd", grid=(T // bt,),
        in_specs=[row, pl.BlockSpec((1, LANES), lambda i: (0, 0))], out_specs=row,
        out_shape=jax.ShapeDtypeStruct((T, LANES), F32), scratch_shapes=[pltpu.VMEM((1, LANES), F32)],
        compiler_params=_params("arbitrary"),
    )(fz, bf)


def _gates_bwd(dcq, dck, fz, bf, H):
    T = fz.shape[0]
    bt = _tile(T, 512)
    nb = T // bt

    def body(dcq_ref, dck_ref, fz_ref, bf_ref, dfz_ref, dbf_ref, carry):
        @pl.when(pl.program_id(0) == 0)
        def _():
            carry[...] = jnp.zeros_like(carry)
            dbf_ref[...] = jnp.zeros_like(dbf_ref)

        tri = (lax.broadcasted_iota(jnp.int32, (bt, bt), 0) <= lax.broadcasted_iota(jnp.int32, (bt, bt), 1))
        dc = dcq_ref[...] + dck_ref[...]
        dlogf = jnp.dot(tri.astype(F32), dc, precision=HIGHEST, preferred_element_type=F32) + carry[...]
        carry[...] = dlogf[0:1, :]
        z = fz_ref[...] + bf_ref[...]
        lane = lax.broadcasted_iota(jnp.int32, (bt, LANES), 1)
        dfz = jnp.where(lane < H, dlogf / (1.0 + jnp.exp(z)), 0.0)
        dfz_ref[...] = dfz
        dbf_ref[...] += jnp.sum(dfz, axis=0, keepdims=True)

    row = pl.BlockSpec((bt, LANES), lambda i: (nb - 1 - i, 0))
    one = pl.BlockSpec((1, LANES), lambda i: (0, 0))
    return pl.pallas_call(
        body, name="fox_gates_bwd", grid=(nb,), in_specs=[row, row, row, one], out_specs=[row, one],
        out_shape=[jax.ShapeDtypeStruct((T, LANES), F32), jax.ShapeDtypeStruct((1, LANES), F32)],
        scratch_shapes=[pltpu.VMEM((1, LANES), F32)], compiler_params=_params("arbitrary"),
    )(dcq, dck, fz, bf)


def _lane_pick(ref_value, h):
    lane = lax.broadcasted_iota(jnp.int32, (1, LANES), 1)
    return jnp.sum(jnp.where(lane == h, ref_value, 0.0), axis=1, keepdims=True)


def _lane_put(old, h, col):
    lane = lax.broadcasted_iota(jnp.int32, (1, LANES), 1)
    return jnp.where(lane == h, col, old)


FOX_HEADS = 4


def _causal(rows, cols, row0=0, transposed=False):
    row = row0 + lax.broadcasted_iota(jnp.int32, (rows, cols), 0)
    col = lax.broadcasted_iota(jnp.int32, (rows, cols), 1)
    return col >= row if transposed else row >= col


def _fox_fwd(q, k, v, ccol, crow, blk):
    T, D = q.shape
    H = D // HEAD_DIM
    nb = T // blk

    def body(q_ref, k_ref, v_ref, ccol_ref, crow_ref, o_ref, lse_ref, m_s, l_s, acc_s):
        qi, hg, kj = pl.program_id(0), pl.program_id(1), pl.program_id(2)

        @pl.when(kj == 0)
        def _():
            m_s[...] = jnp.full_like(m_s, NEG)
            l_s[...] = jnp.zeros_like(l_s)
            acc_s[...] = jnp.zeros_like(acc_s)

        @pl.when((kj == 0) & (hg == 0))
        def _():
            lse_ref[...] = jnp.zeros_like(lse_ref)

        def step(diagonal):
            heads = range(FOX_HEADS)
            sls = [slice(hh * HEAD_DIM, (hh + 1) * HEAD_DIM) for hh in heads]
            m_old = [m_s[hh] for hh in heads]
            l_old = [l_s[hh] for hh in heads]
            acc_old = [acc_s[:, sl] for sl in sls]
            s = []
            for hh in heads:
                h = hg * FOX_HEADS + hh
                bias = _lane_pick(ccol_ref[...], h) - crow_ref[pl.ds(h, 1), :]
                sc = lax.dot_general(q_ref[:, sls[hh]], k_ref[:, sls[hh]], _NT, preferred_element_type=F32) * SCALE + bias
                s.append(jnp.where(_causal(blk, blk), sc, NEG) if diagonal else sc)
            m_new = [jnp.maximum(m_old[hh], jnp.max(s[hh], axis=1, keepdims=True)) for hh in heads]
            p = [jnp.exp(s[hh] - m_new[hh]) for hh in heads]
            alpha = [jnp.exp(m_old[hh] - m_new[hh]) for hh in heads]
            pv = [jnp.dot(p[hh].astype(BF16), v_ref[:, sls[hh]], preferred_element_type=F32) for hh in heads]
            for hh in heads:
                l_s[hh] = alpha[hh] * l_old[hh] + jnp.sum(p[hh], axis=1, keepdims=True)
                acc_s[:, sls[hh]] = alpha[hh] * acc_old[hh] + pv[hh]
                m_s[hh] = m_new[hh]

        @pl.when(kj < qi)
        def _():
            step(False)

        @pl.when(kj == qi)
        def _():
            step(True)
            for hh in range(FOX_HEADS):
                sl = slice(hh * HEAD_DIM, (hh + 1) * HEAD_DIM)
                o_ref[:, sl] = (acc_s[:, sl] / l_s[hh]).astype(BF16)
                lse_ref[...] = _lane_put(lse_ref[...], hg * FOX_HEADS + hh, m_s[hh] + jnp.log(l_s[hh]))

    wide = FOX_HEADS * HEAD_DIM
    qsp = pl.BlockSpec((blk, wide), lambda qi, h, kj: (qi, h))
    ksp = pl.BlockSpec((blk, wide), lambda qi, h, kj: (jnp.minimum(kj, qi), h))
    stat = pl.BlockSpec((blk, LANES), lambda qi, h, kj: (qi, 0))
    return pl.pallas_call(
        body, name="fox_fwd", grid=(nb, H // FOX_HEADS, nb),
        in_specs=[qsp, ksp, ksp, stat, pl.BlockSpec((H, blk), lambda qi, h, kj: (0, jnp.minimum(kj, qi)))],
        out_specs=[qsp, stat],
        out_shape=[jax.ShapeDtypeStruct((T, D), BF16), jax.ShapeDtypeStruct((T, LANES), F32)],
        scratch_shapes=[pltpu.VMEM((FOX_HEADS, blk, 1), F32), pltpu.VMEM((FOX_HEADS, blk, 1), F32),
                        pltpu.VMEM((blk, wide), F32)],
        compiler_params=_params("arbitrary", "arbitrary", "arbitrary"),
    )(q, k, v, ccol, crow)


def _head_stat(name, a, b, behind=()):
    T, D = a.shape
    H = D // HEAD_DIM
    bt = _tile(T, 256)

    def body(a_ref, b_ref, *rest):
        o_ref = rest[-1]
        out = jnp.zeros((bt, LANES), F32)
        for h in range(H):
            sl = slice(h * HEAD_DIM, (h + 1) * HEAD_DIM)
            prod = a_ref[:, sl].astype(F32) * b_ref[:, sl].astype(F32)
            out = _lane_put(out, h, jnp.sum(prod, axis=1, keepdims=True))
        o_ref[...] = out

    row = pl.BlockSpec((bt, D), lambda i: (i, 0))
    return pl.pallas_call(
        body, name=name, grid=(T // bt,), in_specs=[row, row] + [ANY] * len(behind),
        out_specs=pl.BlockSpec((bt, LANES), lambda i: (i, 0)),
        out_shape=jax.ShapeDtypeStruct((T, LANES), F32), compiler_params=_params("parallel"),
    )(a, b, *behind)


def _fox_bwd_dq(q, k, v, do, ccol, crow, lse, delta, blk):
    T, D = q.shape
    H = D // HEAD_DIM
    nb = T // blk

    def body(q_ref, k_ref, v_ref, do_ref, ccol_ref, crow_ref, lse_ref, delta_ref, dq_ref, dc_ref, acc_s, dc_s):
        qi, hg, kj = pl.program_id(0), pl.program_id(1), pl.program_id(2)

        @pl.when(kj == 0)
        def _():
            acc_s[...] = jnp.zeros_like(acc_s)
            dc_s[...] = jnp.zeros_like(dc_s)

        @pl.when((kj == 0) & (hg == 0))
        def _():
            dc_ref[...] = jnp.zeros_like(dc_ref)

        def step(diagonal):
            heads = range(FOX_HEADS)
            sls = [slice(hh * HEAD_DIM, (hh + 1) * HEAD_DIM) for hh in heads]
            p, dp = [], []
            for hh in heads:
                h = hg * FOX_HEADS + hh
                bias = _lane_pick(ccol_ref[...], h) - crow_ref[pl.ds(h, 1), :]
                sc = lax.dot_general(q_ref[:, sls[hh]], k_ref[:, sls[hh]], _NT, preferred_element_type=F32) * SCALE + bias
                if diagonal:
                    sc = jnp.where(_causal(blk, blk), sc, NEG)
                p.append(jnp.exp(sc - _lane_pick(lse_ref[...], h)))
                dp.append(lax.dot_general(do_ref[:, sls[hh]], v_ref[:, sls[hh]], _NT, preferred_element_type=F32)
                          - _lane_pick(delta_ref[...], h))
            ds = [p[hh] * dp[hh] for hh in heads]
            for hh in heads:
                acc_s[:, sls[hh]] += jnp.dot(ds[hh].astype(BF16), k_ref[:, sls[hh]], preferred_element_type=F32)
                dc_s[hh] += jnp.sum(ds[hh], axis=1, keepdims=True)

        @pl.when(kj < qi)
        def _():
            step(False)

        @pl.when(kj == qi)
        def _():
            step(True)
            dq_ref[...] = acc_s[...] * SCALE
            for hh in range(FOX_HEADS):
                dc_ref[...] = _lane_put(dc_ref[...], hg * FOX_HEADS + hh, dc_s[hh])

    wide = FOX_HEADS * HEAD_DIM
    qsp = pl.BlockSpec((blk, wide), lambda qi, h, kj: (qi, h))
    ksp = pl.BlockSpec((blk, wide), lambda qi, h, kj: (jnp.minimum(kj, qi), h))
    stat = pl.BlockSpec((blk, LANES), lambda qi, h, kj: (qi, 0))
    return pl.pallas_call(
        body, name="fox_bwd_dq", grid=(nb, H // FOX_HEADS, nb),
        in_specs=[qsp, ksp, ksp, qsp, stat, pl.BlockSpec((H, blk), lambda qi, h, kj: (0, jnp.minimum(kj, qi))),
                  stat, stat],
        out_specs=[qsp, stat],
        out_shape=[jax.ShapeDtypeStruct((T, D), F32), jax.ShapeDtypeStruct((T, LANES), F32)],
        scratch_shapes=[pltpu.VMEM((blk, wide), F32), pltpu.VMEM((FOX_HEADS, blk, 1), F32)],
        compiler_params=_params("arbitrary", "arbitrary", "arbitrary"),
    )(q, k, v, do, ccol, crow, lse, delta)


def _fox_bwd_dkv(q, k, v, do, ccol, crow, lse_row, delta_row, blk):
    T, D = q.shape
    H = D // HEAD_DIM
    nb = T // blk

    def body(q_ref, k_ref, v_ref, do_ref, ccol_ref, crow_ref, lse_ref, delta_ref, dk_ref, dv_ref, dc_ref,
             dk_s, dv_s, dc_s):
        kj, hg, qi = pl.program_id(0), pl.program_id(1), pl.program_id(2)

        @pl.when(qi == 0)
        def _():
            dk_s[...] = jnp.zeros_like(dk_s)
            dv_s[...] = jnp.zeros_like(dv_s)
            dc_s[...] = jnp.zeros_like(dc_s)

        @pl.when((qi == 0) & (hg == 0))
        def _():
            dc_ref[...] = jnp.zeros_like(dc_ref)

        def step(diagonal):
            heads = range(FOX_HEADS)
            sls = [slice(hh * HEAD_DIM, (hh + 1) * HEAD_DIM) for hh in heads]
            pt, dpt = [], []
            for hh in heads:
                h = hg * FOX_HEADS + hh
                bias = crow_ref[pl.ds(h, 1), :] - _lane_pick(ccol_ref[...], h)
                st = lax.dot_general(k_ref[:, sls[hh]], q_ref[:, sls[hh]], _NT, preferred_element_type=F32) * SCALE + bias
                if diagonal:
                    st = jnp.where(_causal(blk, blk, transposed=True), st, NEG)
                pt.append(jnp.exp(st - lse_ref[pl.ds(h, 1), :]))
                dpt.append(lax.dot_general(v_ref[:, sls[hh]], do_ref[:, sls[hh]], _NT, preferred_element_type=F32)
                           - delta_ref[pl.ds(h, 1), :])
            dst = [pt[hh] * dpt[hh] for hh in heads]
            for hh in heads:
                dv_s[:, sls[hh]] += jnp.dot(pt[hh].astype(BF16), do_ref[:, sls[hh]], preferred_element_type=F32)
                dk_s[:, sls[hh]] += jnp.dot(dst[hh].astype(BF16), q_ref[:, sls[hh]], preferred_element_type=F32)
                dc_s[hh] -= jnp.sum(dst[hh], axis=1, keepdims=True)

        @pl.when(qi > kj)
        def _():
            step(False)

        @pl.when(qi == kj)
        def _():
            step(True)

        @pl.when(qi == nb - 1)
        def _():
            dk_ref[...] = dk_s[...] * SCALE
            dv_ref[...] = dv_s[...]
            for hh in range(FOX_HEADS):
                dc_ref[...] = _lane_put(dc_ref[...], hg * FOX_HEADS + hh, dc_s[hh])

    wide = FOX_HEADS * HEAD_DIM
    ksp = pl.BlockSpec((blk, wide), lambda kj, h, qi: (kj, h))
    qsp = pl.BlockSpec((blk, wide), lambda kj, h, qi: (jnp.maximum(qi, kj), h))
    kstat = pl.BlockSpec((blk, LANES), lambda kj, h, qi: (kj, 0))
    qrow = pl.BlockSpec((H, blk), lambda kj, h, qi: (0, jnp.maximum(qi, kj)))
    return pl.pallas_call(
        body, name="fox_bwd_dkv", grid=(nb, H // FOX_HEADS, nb),
        in_specs=[qsp, ksp, ksp, qsp, kstat, qrow, qrow, qrow],
        out_specs=[ksp, ksp, kstat],
        out_shape=[jax.ShapeDtypeStruct((T, D), F32), jax.ShapeDtypeStruct((T, D), F32),
                   jax.ShapeDtypeStruct((T, LANES), F32)],
        scratch_shapes=[pltpu.VMEM((blk, wide), F32), pltpu.VMEM((blk, wide), F32),
                        pltpu.VMEM((FOX_HEADS, blk, 1), F32)],
        compiler_params=_params("arbitrary", "arbitrary", "arbitrary"),
    )(q, k, v, do, ccol, crow, lse_row, delta_row)


def _rel_index():
    j = np.arange(ROLL_W)
    u = np.where(j < 2 * WIN, j, j - ROLL_W)
    return np.clip(WIN - u, -(CHUNK - 1), REL_CLIP) + (CHUNK - 1)


def _band_bias(g_row, bias_s):
    base = jnp.broadcast_to(g_row, (CHUNK, ROLL_W))
    rowb = lax.broadcasted_iota(jnp.int32, (CHUNK, ROLL_W), 0)
    for bit in range(CHUNK.bit_length() - 1):
        base = jnp.where((rowb >> bit) & 1 == 1, pltpu.roll(base, 1 << bit, 1), base)
    kc = lax.broadcasted_iota(jnp.int32, (CHUNK, 2 * WIN), 1) // CHUNK
    for a in range(N_PREV_CHUNKS):
        part = base if a == 0 else pltpu.roll(base, CHUNK * a, 1)
        valid = (kc >= a) & (kc <= a + N_PREV_CHUNKS)
        bias_s[a * CHUNK:(a + 1) * CHUNK, :] = jnp.where(valid, part[:, :2 * WIN], NEG)


def _chunk_probs(q, kp, kc, bias_s, qi):
    sp = lax.dot_general(q, kp, _NT, preferred_element_type=F32) * SCALE + bias_s[:, :WIN]
    sc = lax.dot_general(q, kc, _NT, preferred_element_type=F32) * SCALE + bias_s[:, WIN:]
    sp = jnp.where(qi > 0, sp, NEG)
    m = jnp.maximum(jnp.max(sp, axis=1, keepdims=True), jnp.max(sc, axis=1, keepdims=True))
    pp, pc = jnp.exp(sp - m), jnp.exp(sc - m)
    inv = 1.0 / (jnp.sum(pp, axis=1, keepdims=True) + jnp.sum(pc, axis=1, keepdims=True))
    return pp * inv, pc * inv


def _chunk_fwd(q, k, v, g_rows):
    T, D = q.shape
    H = D // HEAD_DIM
    nq = T // WIN

    def body(q_ref, kp_ref, kc_ref, vp_ref, vc_ref, g_ref, o_ref, bias_s):
        qi = pl.program_id(1)

        @pl.when(qi == 0)
        def _():
            _band_bias(g_ref[...], bias_s)

        pp, pc = _chunk_probs(q_ref[...], kp_ref[...], kc_ref[...], bias_s, qi)
        o = jnp.dot(pp.astype(BF16), vp_ref[...], preferred_element_type=F32)
        o += jnp.dot(pc.astype(BF16), vc_ref[...], preferred_element_type=F32)
        o_ref[...] = o.astype(BF16)

    cur = pl.BlockSpec((WIN, HEAD_DIM), lambda h, qi: (qi, h))
    prev = pl.BlockSpec((WIN, HEAD_DIM), lambda h, qi: (jnp.maximum(qi - 1, 0), h))
    return pl.pallas_call(
        body, name="chunk_fwd", grid=(H, nq),
        in_specs=[cur, prev, cur, prev, cur, pl.BlockSpec((None, 1, ROLL_W), lambda h, qi: (h, 0, 0))],
        out_specs=cur, out_shape=jax.ShapeDtypeStruct((T, D), BF16),
        scratch_shapes=[pltpu.VMEM((WIN, 2 * WIN), F32)],
        compiler_params=_params("arbitrary", "arbitrary"),
    )(q, k, k, v, v, g_rows)


def _chunk_bwd(q, k, v, g_rows, do, o):
    T, D = q.shape
    H = D // HEAD_DIM
    nq = T // WIN

    def body(q_ref, kp_ref, kc_ref, vp_ref, vc_ref, g_ref, do_ref, o_ref,
             dq_ref, dk_ref, dv_ref, ds_ref, bias_s, ck_s, cv_s):
        qi = pl.program_id(1)

        @pl.when(qi == 0)
        def _():
            _band_bias(g_ref[...], bias_s)
            ds_ref[...] = jnp.zeros_like(ds_ref)
            ck_s[...] = jnp.zeros_like(ck_s)
            cv_s[...] = jnp.zeros_like(cv_s)

        @pl.when(qi < nq)
        def _():
            qv, dov = q_ref[...], do_ref[...]
            pp, pc = _chunk_probs(qv, kp_ref[...], kc_ref[...], bias_s, qi)
            delta = jnp.sum(dov.astype(F32) * o_ref[...].astype(F32), axis=1, keepdims=True)
            dsp = pp * (lax.dot_general(dov, vp_ref[...], _NT, preferred_element_type=F32) - delta)
            dsc = pc * (lax.dot_general(dov, vc_ref[...], _NT, preferred_element_type=F32) - delta)
            dsp16, dsc16 = dsp.astype(BF16), dsc.astype(BF16)
            dq = jnp.dot(dsp16, kp_ref[...], preferred_element_type=F32)
            dq += jnp.dot(dsc16, kc_ref[...], preferred_element_type=F32)
            dq_ref[...] = dq * SCALE
            ds_ref[:, :WIN] += dsp
            ds_ref[:, WIN:] += dsc
            dk_ref[...] = ck_s[...] + lax.dot_general(dsp16, qv, _TN, preferred_element_type=F32) * SCALE
            dv_ref[...] = cv_s[...] + lax.dot_general(pp.astype(BF16), dov, _TN, preferred_element_type=F32)
            ck_s[...] = lax.dot_general(dsc16, qv, _TN, preferred_element_type=F32) * SCALE
            cv_s[...] = lax.dot_general(pc.astype(BF16), dov, _TN, preferred_element_type=F32)

        @pl.when(qi == nq)
        def _():
            dk_ref[...] = ck_s[...]
            dv_ref[...] = cv_s[...]

    cur = pl.BlockSpec((WIN, HEAD_DIM), lambda h, qi: (jnp.minimum(qi, nq - 1), h))
    prev = pl.BlockSpec((WIN, HEAD_DIM), lambda h, qi: (jnp.clip(qi - 1, 0, nq - 2), h))
    late = pl.BlockSpec((WIN, HEAD_DIM), lambda h, qi: (jnp.maximum(qi - 1, 0), h))
    return pl.pallas_call(
        body, name="chunk_bwd", grid=(H, nq + 1),
        in_specs=[cur, prev, cur, prev, cur, pl.BlockSpec((None, 1, ROLL_W), lambda h, qi: (h, 0, 0)), cur, cur],
        out_specs=[cur, late, late, pl.BlockSpec((None, WIN, 2 * WIN), lambda h, qi: (h, 0, 0))],
        out_shape=[jax.ShapeDtypeStruct((T, D), F32)] * 3 + [jax.ShapeDtypeStruct((H, WIN, 2 * WIN), F32)],
        scratch_shapes=[pltpu.VMEM((WIN, 2 * WIN), F32), pltpu.VMEM((WIN, HEAD_DIM), F32),
                        pltpu.VMEM((WIN, HEAD_DIM), F32)],
        compiler_params=_params("arbitrary", "arbitrary"),
    )(q, k, k, v, v, g_rows, do, o)


def _rel_rows(rel, onehot_t):
    H = rel.shape[0]

    def body(rel_ref, oh_ref, out_ref):
        out_ref[...] = jnp.dot(rel_ref[...], oh_ref[...], precision=HIGHEST, preferred_element_type=F32)

    return pl.pallas_call(body, name="rel_rows", out_shape=jax.ShapeDtypeStruct((H, ROLL_W), F32))(rel, onehot_t)


def _rel_grad(ds_sum, onehot):
    H = ds_sum.shape[0]

    def body(ds_ref, oh_ref, out_ref):
        y = jnp.zeros((CHUNK, ROLL_W), F32)
        pad = jnp.zeros((CHUNK, ROLL_W - 2 * WIN), F32)
        for a in range(N_PREV_CHUNKS):
            part = jnp.concatenate([ds_ref[a * CHUNK:(a + 1) * CHUNK, :], pad], axis=1)
            y = y + (part if a == 0 else pltpu.roll(part, ROLL_W - CHUNK * a, 1))
        rowb = lax.broadcasted_iota(jnp.int32, (CHUNK, ROLL_W), 0)
        for bit in range(CHUNK.bit_length() - 1):
            y = jnp.where((rowb >> bit) & 1 == 1, pltpu.roll(y, ROLL_W - (1 << bit), 1), y)
        diag = jnp.broadcast_to(jnp.sum(y, axis=0, keepdims=True), (8, ROLL_W))
        out_ref[...] = jnp.dot(diag, oh_ref[...], precision=HIGHEST, preferred_element_type=F32)

    return pl.pallas_call(
        body, name="rel_grad", grid=(H,),
        in_specs=[pl.BlockSpec((None, WIN, 2 * WIN), lambda h: (h, 0, 0)),
                  pl.BlockSpec((ROLL_W, N_REL_PAD), lambda h: (0, 0))],
        out_specs=pl.BlockSpec((None, 8, N_REL_PAD), lambda h: (h, 0, 0)),
        out_shape=jax.ShapeDtypeStruct((H, 8, N_REL_PAD), F32), compiler_params=_params("parallel"),
    )(ds_sum, onehot)


def _loss_head(y, target):
    T, D = y.shape
    bt = _tile(T, 256)

    def body(y_ref, t_ref, dy_ref, loss_ref):
        @pl.when(pl.program_id(0) == 0)
        def _():
            loss_ref[...] = jnp.zeros_like(loss_ref)

        err = y_ref[...] - t_ref[...]
        dy_ref[...] = err * (1.0 / D)
        loss_ref[...] += 0.5 * jnp.sum(jnp.mean(err * err, axis=-1, keepdims=True))

    row = pl.BlockSpec((bt, D), lambda i: (i, 0))
    one = pl.BlockSpec((8, LANES), lambda i: (0, 0))
    return pl.pallas_call(
        body, name="loss_head", grid=(T // bt,), in_specs=[row, row], out_specs=[row, one],
        out_shape=[jax.ShapeDtypeStruct((T, D), F32), jax.ShapeDtypeStruct((8, LANES), F32)],
        compiler_params=_params("arbitrary"),
    )(y, target)


def _adam_step(g, w_ref, m_ref, v_ref, g_ref, d_ref, nm_ref, nv_ref):
    nm = ADAM_B1 * m_ref[...] + (1.0 - ADAM_B1) * g
    nv = ADAM_B2 * v_ref[...] + (1.0 - ADAM_B2) * (g * g)
    m_hat = nm / (1.0 - ADAM_B1 ** ADAM_STEP)
    v_hat = nv / (1.0 - ADAM_B2 ** ADAM_STEP)
    g_ref[...] = g
    nm_ref[...] = nm
    nv_ref[...] = nv
    d_ref[...] = -ADAM_LR * (m_hat / (jnp.sqrt(v_hat) + ADAM_EPS) + ADAM_WD * w_ref[...])


def _adamw(name, w, m, v, g):
    def body(w_ref, m_ref, v_ref, gin_ref, g_ref, d_ref, nm_ref, nv_ref):
        _adam_step(gin_ref[...], w_ref, m_ref, v_ref, g_ref, d_ref, nm_ref, nv_ref)

    return pl.pallas_call(body, name=name, out_shape=[jax.ShapeDtypeStruct(w.shape, F32)] * 4)(w, m, v, g)


def _adamw_reduced(name, w, m, v, own, land, chip, row0=0, into=None):
    rows_all, C = w.shape
    _, R, _ = land.shape

    def body(chip_ref, w_ref, m_ref, v_ref, own_ref, land_ref, *rest):
        g = jnp.where(chip_ref[0] == 0, own_ref[...], land_ref[0]).astype(F32)
        for q in range(1, N_CHIP):
            g = g + jnp.where(chip_ref[0] == q, own_ref[...], land_ref[q]).astype(F32)
        _adam_step(g, w_ref, m_ref, v_ref, *rest[-4:])

    if R % 16 == 0:
        br = _tile(R, max(16, (128 * 1024) // C))
        first = row0 // br
        grid = (R // br,)
        wsp = pl.BlockSpec((br, C), lambda i, chip_ref: (first + i, 0))
        osp = pl.BlockSpec((None, br, C), lambda i, chip_ref: (chip_ref[0], i, 0))
        lsp = pl.BlockSpec((N_CHIP, br, C), lambda i, chip_ref: (0, i, 0))
    else:
        bc = _tile(C, 512)
        grid = (C // bc,)
        wsp = pl.BlockSpec((R, bc), lambda i, chip_ref: (0, i))
        osp = pl.BlockSpec((None, R, bc), lambda i, chip_ref: (chip_ref[0], 0, i))
        lsp = pl.BlockSpec((N_CHIP, R, bc), lambda i, chip_ref: (0, 0, i))
    extra = [] if into is None else list(into)
    return pl.pallas_call(
        body, name=name,
        grid_spec=pltpu.PrefetchScalarGridSpec(
            num_scalar_prefetch=1, grid=grid, in_specs=[wsp, wsp, wsp, osp, lsp] + [ANY] * len(extra),
            out_specs=[wsp] * 4),
        out_shape=[jax.ShapeDtypeStruct((rows_all, C), F32)] * 4,
        input_output_aliases={6 + j: j for j in range(len(extra))}, compiler_params=_params("parallel"),
    )(chip, w, m, v, own, land, *extra)


def _sum_parts(name, parts, out_dtype):
    P, R, C = parts.shape
    br = _tile(R, max(8, (256 * 1024) // C)) if R % 8 == 0 else R

    def body(p_ref, o_ref):
        g = p_ref[0].astype(F32)
        for i in range(1, P):
            g = g + p_ref[i].astype(F32)
        o_ref[...] = g.astype(out_dtype)

    return pl.pallas_call(
        body, name=name, grid=(R // br,),
        in_specs=[pl.BlockSpec((P, br, C), lambda i: (0, i, 0))], out_specs=pl.BlockSpec((br, C), lambda i: (i, 0)),
        out_shape=jax.ShapeDtypeStruct((R, C), out_dtype), compiler_params=_params("parallel"),
    )(parts)


def _pair_sum(name, g, recv, core):
    _, R, C = g.shape

    def body(core_ref, g_ref, r_ref, o_ref):
        o_ref[...] = (g_ref[...] + r_ref[...]).astype(BF16)

    if R % 16 == 0:
        br = _tile(R, max(16, (256 * 1024) // C))
        blk, n = (None, br, C), R // br
        mine = lambda q, i, core_ref: (2 * q + core_ref[0], i, 0)
        same = lambda q, i, core_ref: (q, i, 0)
    else:
        bc = _tile(C, 512)
        blk, n = (None, R, bc), C // bc
        mine = lambda q, i, core_ref: (2 * q + core_ref[0], 0, i)
        same = lambda q, i, core_ref: (q, 0, i)
    return pl.pallas_call(
        body, name=name,
        grid_spec=pltpu.PrefetchScalarGridSpec(
            num_scalar_prefetch=1, grid=(N_CHIP, n),
            in_specs=[pl.BlockSpec(blk, mine), pl.BlockSpec(blk, same)], out_specs=pl.BlockSpec(blk, same)),
        out_shape=jax.ShapeDtypeStruct((N_CHIP, R, C), BF16), compiler_params=_params("parallel", "parallel"),
    )(core, g, recv)


def _chips_of(x, y):
    return [(1 - x, y), (x, 1 - y), (1 - x, 1 - y)]


def _position():
    x, y, c = lax.axis_index("x"), lax.axis_index("y"), lax.axis_index("c")
    return x, y, c, _chips_of(x, y)


def _all_gather(name, xs):
    n = len(xs)

    def body(*refs):
        x_refs, out_refs = refs[:n], refs[n:2 * n]
        send_sems, recv_sems, local_sems = refs[2 * n:]
        x, y, c, chips = _position()
        me, sibling = (x, y, c), (x, y, 1 - c)

        def copy(a, k, block, to, src=None):
            px, py, pc = block
            slot = out_refs[a].at[4 * px + 2 * py + pc]
            return pltpu.make_async_remote_copy(
                src_ref=slot if src is None else src, dst_ref=slot,
                send_sem=send_sems.at[a, k], recv_sem=recv_sems.at[a, k], device_id=to, device_id_type=MESH)

        mine = [pltpu.make_async_copy(x_refs[a], out_refs[a].at[4 * x + 2 * y + c], local_sems.at[a]) for a in range(n)]
        for cp in mine:
            cp.start()
        first = []
        for a in range(n):
            first.append(copy(a, 0, me, sibling, src=x_refs[a]))
            first += [copy(a, 1 + j, me, (*chip, c), src=x_refs[a]) for j, chip in enumerate(chips)]
        for cp in first:
            cp.start()
        passed = []
        for j, chip in enumerate(chips):
            for a in range(n):
                copy(a, 1 + j, (*chip, c), me).wait_recv()
                fwd = copy(a, 4 + j, (*chip, c), sibling)
                fwd.start()
                passed.append(fwd)
        for a in range(n):
            copy(a, 0, sibling, me).wait_recv()
            for j, chip in enumerate(chips):
                copy(a, 4 + j, (*chip, 1 - c), me).wait_recv()
        for cp in first + passed:
            cp.wait_send()
        for cp in mine:
            cp.wait()

    return pl.pallas_call(
        body, name=name, in_specs=[ANY] * n, out_specs=[ANY] * n,
        out_shape=[jax.ShapeDtypeStruct((N_DEV,) + x.shape, x.dtype) for x in xs],
        scratch_shapes=[pltpu.SemaphoreType.DMA((n, 7)), pltpu.SemaphoreType.DMA((n, 7)),
                        pltpu.SemaphoreType.DMA((n,))],
    )(*xs)


HBM = pl.BlockSpec(memory_space=pltpu.HBM)
SEM = pl.BlockSpec(memory_space=pltpu.SEMAPHORE)
EFFECT = pltpu.SideEffectType.DATAFLOW_SIDE_EFFECTING


def _peers(x, y, c):
    return [(x, y, 1 - c), (1 - x, y, c), (x, 1 - y, c), (1 - x, 1 - y, c),
            (1 - x, y, 1 - c), (x, 1 - y, 1 - c), (1 - x, 1 - y, 1 - c)]


def _gather_start(name, lands):
    n = len(lands)

    def body(*refs):
        land = refs[:n]
        send_sems, recv_sems = refs[n:2 * n], refs[2 * n:3 * n]
        x, y, c, _ = _position()
        mine = 4 * x + 2 * y + c
        for a in range(n):
            for k, peer in enumerate(_peers(x, y, c)):
                pltpu.make_async_remote_copy(
                    src_ref=land[a].at[mine], dst_ref=land[a].at[mine], send_sem=send_sems[a].at[k],
                    recv_sem=recv_sems[a].at[k], device_id=peer, device_id_type=MESH).start()

    res = pl.pallas_call(
        body, name=name, in_specs=[HBM] * n, out_specs=[SEM] * (2 * n) + [HBM] * n,
        out_shape=[pltpu.SemaphoreType.DMA((N_DEV - 1,))] * (2 * n) + [pltpu.HBM(l.shape, l.dtype) for l in lands],
        input_output_aliases={a: 2 * n + a for a in range(n)},
        compiler_params=pltpu.CompilerParams(has_side_effects=EFFECT),
    )(*[pltpu.with_memory_space_constraint(l, pltpu.HBM) for l in lands])
    return res[:n], res[n:2 * n], res[2 * n:]


def _gather_wait(name, lands, send_sems, recv_sems, after):
    n = len(lands)
    tail = [] if after is None else [after]

    def body(*refs):
        land = refs[:n]
        send_sems, recv_sems = refs[n:2 * n], refs[2 * n:3 * n]
        x, y, c, _ = _position()
        mine = 4 * x + 2 * y + c
        for a in range(n):
            for k, (px, py, pc) in enumerate(_peers(x, y, c)):
                cp = pltpu.make_async_remote_copy(
                    src_ref=land[a].at[mine], dst_ref=land[a].at[4 * px + 2 * py + pc], send_sem=send_sems[a].at[k],
                    recv_sem=recv_sems[a].at[k], device_id=(px, py, pc), device_id_type=MESH)
                cp.wait_send()
                cp.wait_recv()

    return pl.pallas_call(
        body, name=name, in_specs=[HBM] * n + [SEM] * (2 * n) + [ANY] * len(tail), out_specs=[HBM] * n,
        out_shape=[pltpu.HBM(l.shape, l.dtype) for l in lands], input_output_aliases={a: a for a in range(n)},
        compiler_params=pltpu.CompilerParams(has_side_effects=EFFECT),
    )(*lands, *send_sems, *recv_sems, *tail)


def _split_start(name, arrays, lands, plan):
    n = len(arrays)
    n_copies = len(plan(0, 0, 0))

    def body(*refs):
        src, land = refs[:n], refs[n:2 * n]
        send_sems, recv_sems = refs[2 * n:3 * n], refs[3 * n:4 * n]
        token = refs[6 * n]
        x, y, c, _ = _position()
        for a in range(n):
            for k, (src_block, land_block, peer) in enumerate(plan(x, y, c)):
                pltpu.make_async_remote_copy(
                    src_ref=src[a].at[src_block], dst_ref=land[a].at[land_block], send_sem=send_sems[a].at[k],
                    recv_sem=recv_sems[a].at[k], device_id=peer, device_id_type=MESH).start()
        token[...] = jnp.zeros_like(token)

    both = list(arrays) + list(lands)
    res = pl.pallas_call(
        body, name=name, in_specs=[HBM] * (2 * n),
        out_specs=[SEM] * (2 * n) + [HBM] * (2 * n) + [pl.BlockSpec(memory_space=pltpu.VMEM)],
        out_shape=[pltpu.SemaphoreType.DMA((n_copies,))] * (2 * n) + [pltpu.HBM(b.shape, b.dtype) for b in both]
        + [jax.ShapeDtypeStruct((8, LANES), F32)],
        input_output_aliases={i: 2 * n + i for i in range(2 * n)},
        compiler_params=pltpu.CompilerParams(has_side_effects=EFFECT),
    )(*[pltpu.with_memory_space_constraint(b, pltpu.HBM) for b in both])
    return res[:n], res[n:2 * n], res[2 * n:3 * n], res[3 * n:4 * n], res[4 * n]


def _split_wait(name, arrays, lands, send_sems, recv_sems, plan, after):
    n = len(arrays)
    tail = [] if after is None else [after]

    def body(*refs):
        src, land = refs[:n], refs[n:2 * n]
        send, recv = refs[2 * n:3 * n], refs[3 * n:4 * n]
        x, y, c, _ = _position()
        for a in range(n):
            for k, (src_block, land_block, peer) in enumerate(plan(x, y, c)):
                cp = pltpu.make_async_remote_copy(
                    src_ref=src[a].at[src_block], dst_ref=land[a].at[land_block], send_sem=send[a].at[k],
                    recv_sem=recv[a].at[k], device_id=peer, device_id_type=MESH)
                cp.wait_send()
                cp.wait_recv()

    both = list(arrays) + list(lands)
    res = pl.pallas_call(
        body, name=name, in_specs=[HBM] * (2 * n) + [SEM] * (2 * n) + [ANY] * len(tail), out_specs=[HBM] * (2 * n),
        out_shape=[pltpu.HBM(b.shape, b.dtype) for b in both], input_output_aliases={i: i for i in range(2 * n)},
        compiler_params=pltpu.CompilerParams(has_side_effects=EFFECT),
    )(*both, *send_sems, *recv_sems, *tail)
    return res[:n], res[n:]


def _to_other_core(x, y, c):
    return [(2 * q + (1 - c), q, (x, y, 1 - c)) for q in range(N_CHIP)]


def _to_other_chips(x, y, c):
    return [(2 * cx + cy, 2 * x + y, (cx, cy, c)) for cx, cy in _chips_of(x, y)]


def _from_other_chips(x, y, c):
    return [(2 * x + y, 2 * cx + cy, (cx, cy, c)) for cx, cy in _chips_of(x, y)]


def _mlp_fwd(tag, h, g, w1g, w2):
    T, D = h.shape
    F = w2.shape[0]
    (n,) = _rmsnorm_fwd(f"mlp{tag}_norm", h, [g])
    a, hid = _mm(f"mlp{tag}_up", "nn", n, w1g, T, F, D, b_groups=N_DEV, out_dtypes=(BF16, BF16),
                 epilogue=_ep_relu_square)
    out = _mm(f"mlp{tag}_down", "nn", hid, w2, T, D, F, extras=(h,), epilogue=_ep_add)
    return out, (n, a, hid)


def _mlp_bwd(tag, h, g, w1g, w2, saved, dy, behind=()):
    T, D = h.shape
    F = w2.shape[0]
    n, a, hid = saved
    dw2 = _mm(f"mlp{tag}_dw2", "tn", hid, dy, F, D, T, behind=behind)
    dpre = _mm(f"mlp{tag}_dpre", "nt", dy, w2, T, F, D, out_dtypes=(BF16,), extras=(a,), epilogue=_ep_times_2a)
    dw1 = _mm(f"mlp{tag}_dw1", "tn", n, dpre, D, F, T, out_groups=N_DEV)
    dn = _mm(f"mlp{tag}_dn", "nt", dpre, w1g, T, D, F, b_groups=N_DEV)
    dh, dg = _rmsnorm_bwd(f"mlp{tag}_norm_bwd", h, g, dn, dy)
    return dh, dg, dw1, dw2


def _local_step(x, target, W, fox_blk, arrive=lambda group, after: {}, sync=lambda tag, after, G: ()):
    T, D = x.shape
    H = D // HEAD_DIM
    G = {}
    W = dict(W)

    (ua,) = _rmsnorm_fwd("a_norm", x, [W["a_norm_g"]])
    qkv = _mm("a_qkv", "nt", ua, W["a_w_in_t"], T, 3 * D, D)
    fz = _mm("a_fz", "nt", ua, W["a_w_f_t"], T, LANES, D)
    qa, ka, va = _head_post("a_heads", qkv, D, [(0, W["a_q_g"]), (1, W["a_k_g"]), (2, None)])
    ccol = _gates_fwd(fz, W["a_b_f"], H)
    crow = ccol[:, :H].T
    oa, lse = _fox_fwd(qa, ka, va, ccol, crow, fox_blk)
    W.update(arrive("a_out", oa))
    h1 = _mm("a_out", "nn", oa, W["a_w_out"], T, D, D, extras=(x,), epilogue=_ep_add)
    W.update(arrive("middle", h1))
    h2, mlp0 = _mlp_fwd("0", h1, W["mlp_norm_g0"], W["mlp_w1_0"], W["mlp_w2_0"])
    nkv, ub = _rmsnorm_fwd("kv_b_norm", h2, [W["kv_norm_g"], W["b_norm_g"]])
    kv = _mm("kv_proj", "nn", nkv, W["kv_w"], T, 2 * D, D, b_groups=N_DEV)
    ks, vs = _head_post("kv_heads", kv, D, [(0, W["kv_k_g"]), (1, None)])
    qb_pre = _mm("b_q", "nn", ub, W["b_w_q"], T, D, D)
    (qb,) = _head_post("b_heads", qb_pre, D, [(0, W["b_q_g"])])
    onehot = (jnp.asarray(_rel_index())[:, None] == jnp.arange(N_REL_PAD)[None, :]).astype(F32)
    rel_pad = jnp.pad(W["b_rel"], ((0, 0), (0, N_REL_PAD - N_REL)))
    g_rows = _rel_rows(rel_pad, onehot.T).reshape(H, 1, ROLL_W)
    ob = _chunk_fwd(qb, ks, vs, g_rows)
    W.update(arrive("late", ob))
    h3 = _mm("b_out", "nn", ob, W["b_w_out"], T, D, D, extras=(h2,), epilogue=_ep_add)
    h4, mlp1 = _mlp_fwd("1", h3, W["mlp_norm_g1"], W["mlp_w1_1"], W["mlp_w2_1"])
    dy, loss = _loss_head(h4, target)

    dh3, G["mlp_norm_g1"], G["mlp_w1_1"], G["mlp_w2_1"] = _mlp_bwd(
        "1", h3, W["mlp_norm_g1"], W["mlp_w1_1"], W["mlp_w2_1"], mlp1, dy)
    dob = _mm("b_out_dx", "nt", dh3, W["b_w_out"], T, D, D, out_dtypes=(BF16,), behind=sync("mlp1", dh3, G))
    G["b_w_out"] = _mm("b_out_dw", "tn", ob, dh3, D, D, T, behind=sync("b_out_dx", dob, G))
    dqb, dks, dvs, ds_sum = _chunk_bwd(qb, ks, vs, g_rows, dob, ob)
    G["b_rel"] = _rel_grad(ds_sum, onehot)[:, 0, :]
    dqb_pre, (G["b_q_g"],) = _head_post_bwd("b_heads_bwd", qb_pre, D, [(0, W["b_q_g"], dqb)])
    G["b_w_q"] = _mm("b_q_dw", "tn", ub, dqb_pre, D, D, T)
    dub = _mm("b_q_dx", "nt", dqb_pre, W["b_w_q"], T, D, D)
    dkv, (G["kv_k_g"],) = _head_post_bwd("kv_heads_bwd", kv, D, [(0, W["kv_k_g"], dks), (1, None, dvs)])
    G["kv_w"] = _mm("kv_dw", "tn", nkv, dkv, D, 2 * D, T, out_groups=N_DEV)
    dnkv = _mm("kv_dx", "nt", dkv, W["kv_w"], T, D, 2 * D, b_groups=N_DEV, behind=sync("kv_dw", G["kv_w"], G))
    dh2, G["b_norm_g"] = _rmsnorm_bwd("b_norm_bwd", h2, W["b_norm_g"], dub, dh3)
    dh2, G["kv_norm_g"] = _rmsnorm_bwd("kv_norm_bwd", h2, W["kv_norm_g"], dnkv, dh2)
    dh1, G["mlp_norm_g0"], G["mlp_w1_0"], G["mlp_w2_0"] = _mlp_bwd(
        "0", h1, W["mlp_norm_g0"], W["mlp_w1_0"], W["mlp_w2_0"], mlp0, dh2, behind=sync("dh2", dh2, G))
    G["a_w_out"] = _mm("a_out_dw", "tn", oa, dh1, D, D, T)
    doa = _mm("a_out_dx", "nt", dh1, W["a_w_out"], T, D, D, out_dtypes=(BF16,), behind=sync("a_out_dw", G["a_w_out"], G))
    delta = _head_stat("fox_delta", doa, oa, behind=sync("a_out_dx", doa, G))
    dqa, dcq = _fox_bwd_dq(qa, ka, va, doa, ccol, crow, lse, delta, fox_blk)
    sync("fox_bwd_dq", dqa, G)
    dka, dva, dck = _fox_bwd_dkv(qa, ka, va, doa, ccol, crow, lse[:, :H].T, delta[:, :H].T, fox_blk)
    dfz, G["a_b_f"] = _gates_bwd(dcq, dck, fz, W["a_b_f"], H)
    dqkv, (G["a_q_g"], G["a_k_g"]) = _head_post_bwd(
        "a_heads_bwd", qkv, D, [(0, W["a_q_g"], dqa), (1, W["a_k_g"], dka), (2, None, dva)])
    G["a_w_qkv_t"] = _mm("a_qkv_dw", "tn", dqkv, ua, 3 * D, D, T)
    G["a_w_f_t"] = _mm("a_fz_dw", "tn", dfz, ua, LANES, D, T)
    dua = _mm("a_qkv_dx", "nn", dqkv, W["a_w_in_t"], T, D, 3 * D)
    dua = _mm("a_fz_dx", "nn", dfz, W["a_w_f_t"], T, D, LANES, extras=(dua,), epilogue=_ep_add)
    dx, G["a_norm_g"] = _rmsnorm_bwd("a_norm_bwd", x, W["a_norm_g"], dua, dh1)
    return loss, dx, G


def _pad_lanes(a):
    return jnp.pad(a, ((0, 0), (0, LANES - a.shape[1])))


def kernel(x, a_norm_g, a_w_in, a_b_f, a_q_g, a_k_g, a_w_out, mlp_norm_g, mlp_w1, mlp_w2, kv_norm_g, kv_w, kv_k_g, b_norm_g, b_w_q, b_q_g, b_rel, b_w_out, loss_target, m_a_norm_g, m_a_w_in, m_a_b_f, m_a_q_g, m_a_k_g, m_a_w_out, m_mlp_norm_g, m_mlp_w1, m_mlp_w2, m_kv_norm_g, m_kv_w, m_kv_k_g, m_b_norm_g, m_b_w_q, m_b_q_g, m_b_rel, m_b_w_out, v_a_norm_g, v_a_w_in, v_a_b_f, v_a_q_g, v_a_k_g, v_a_w_out, v_mlp_norm_g, v_mlp_w1, v_mlp_w2, v_kv_norm_g, v_kv_w, v_kv_k_g, v_b_norm_g, v_b_w_q, v_b_q_g, v_b_rel, v_b_w_out):
    T, D = x.shape[1], x.shape[2]
    H = D // HEAD_DIM
    F = mlp_w2.shape[1] * N_DEV
    me = 4 * lax.axis_index("x") + 2 * lax.axis_index("y") + lax.axis_index("c")
    core = lax.axis_index("c").astype(jnp.int32).reshape(1)
    n_in = a_w_in.shape[2]
    n_rel = b_rel.shape[2]

    small_in = jnp.concatenate([a_norm_g.reshape(1, -1), b_rel.reshape(1, -1)], axis=1)
    local = [jnp.swapaxes(a_w_in[0], 0, 1), a_w_out[0], mlp_w1[0], mlp_w2[0], kv_w, b_w_q[0], b_w_out[0],
             mlp_w1[1], mlp_w2[1]]
    sending = [small_in] + [w.astype(BF16) for w in local]
    lands = [lax.dynamic_update_slice(lax.empty((N_DEV,) + w.shape, w.dtype), w[None], (me,) + (0,) * w.ndim)
             for w in sending]
    send_sems, recv_sems, lands = _gather_start("gather_start", lands)
    groups = {"first": [0, 1], "a_out": [2], "middle": [3, 4, 5, 6], "late": [7, 8, 9]}

    def arrive(group, after):
        idx = groups[group]
        got = _gather_wait(f"gather_wait_{group}", [lands[i] for i in idx], [send_sems[i] for i in idx],
                           [recv_sems[i] for i in idx], after)
        if group == "first":
            g_small, w_in_t = got[0], got[1].reshape(N_DEV * n_in, D)
            rel = jnp.transpose(g_small[:, 0, D // N_DEV:].reshape(N_DEV, H, n_rel), (1, 0, 2))
            return {"a_norm_g": g_small[:, 0, :D // N_DEV].reshape(1, D), "b_rel": rel.reshape(H, N_DEV * n_rel),
                    "a_w_in_t": w_in_t, "a_w_f_t": jnp.pad(w_in_t[3 * D:], ((0, LANES - H), (0, 0)))}
        if group == "a_out":
            return {"a_w_out": got[0].reshape(D, D)}
        if group == "middle":
            return {"mlp_w1_0": got[0], "mlp_w2_0": got[1].reshape(F, D), "kv_w": got[2], "b_w_q": got[3].reshape(D, D)}
        return {"b_w_out": got[0].reshape(D, D), "mlp_w1_1": got[1], "mlp_w2_1": got[2].reshape(F, D)}

    W = {
        "a_b_f": _pad_lanes(a_b_f), "a_q_g": a_q_g, "a_k_g": a_k_g,
        "mlp_norm_g0": mlp_norm_g[0:1], "mlp_norm_g1": mlp_norm_g[1:2],
        "kv_norm_g": kv_norm_g.reshape(1, D), "kv_k_g": kv_k_g.reshape(1, HEAD_DIM),
        "b_norm_g": b_norm_g, "b_q_g": b_q_g, **arrive("first", None),
    }

    chip = (2 * lax.axis_index("x") + lax.axis_index("y")).astype(jnp.int32).reshape(1)
    flat = lambda a: a.reshape(-1, a.shape[-1])
    state = {"a_w_in": [jnp.swapaxes(a[0], 0, 1) for a in (a_w_in, m_a_w_in, v_a_w_in)]}
    for nm, w, m, v in [("a_w_out", a_w_out, m_a_w_out, v_a_w_out), ("kv_w", kv_w, m_kv_w, v_kv_w),
                        ("b_w_q", b_w_q, m_b_w_q, v_b_w_q), ("b_w_out", b_w_out, m_b_w_out, v_b_w_out),
                        ("mlp_w1", mlp_w1, m_mlp_w1, v_mlp_w1), ("mlp_w2", mlp_w2, m_mlp_w2, v_mlp_w2)]:
        state[nm] = [flat(w), flat(m), flat(v)]
    upd, flying = {}, {}

    def chunks(nm, G):
        if nm == "a_w_in":
            return jnp.concatenate([G["a_w_qkv_t"], G["a_w_f_t"][:H]], axis=0).reshape(N_DEV, n_in, D)
        g = G[nm]
        return g if g.ndim == 3 else g.reshape(N_DEV, g.shape[0] // N_DEV, g.shape[1])

    def to_core(tag, names, G):
        gs = [chunks(nm, G) for nm in names]
        lands = [lax.empty((N_CHIP,) + g.shape[1:], g.dtype) for g in gs]
        send, recv, gs, lands, token = _split_start(f"reduce_cores_start_{tag}", gs, lands, _to_other_core)
        flying[tag] = (names, send, recv, gs, lands)
        return (token,)

    def to_chips(tag, after):
        names, send, recv, gs, lands = flying[tag]
        gs, lands = _split_wait(f"reduce_cores_wait_{tag}", gs, lands, send, recv, _to_other_core, after)
        ps = [_pair_sum(f"pair_sum_{nm}", g, r, core) for nm, g, r in zip(names, gs, lands)]
        lands = [lax.empty(p.shape, p.dtype) for p in ps]
        send, recv, ps, lands, token = _split_start(f"reduce_chips_start_{tag}", ps, lands, _to_other_chips)
        flying[tag] = (names, send, recv, ps, lands)
        return (token,)

    def finish(tag, after):
        names, send, recv, ps, lands = flying.pop(tag)
        ps, lands = _split_wait(f"reduce_chips_wait_{tag}", ps, lands, send, recv, _from_other_chips, after)
        for nm, own, land in zip(names, ps, lands):
            if nm.startswith("mlp_w"):
                base, layer = nm[:-2], int(nm[-1])
                w, m, v = state[base]
                upd[base] = _adamw_reduced(f"adamw_{nm}", w, m, v, own, land, chip, row0=layer * (w.shape[0] // 2),
                                           into=upd.get(base))
            else:
                upd[nm] = _adamw_reduced(f"adamw_{nm}", *state[nm], own, land, chip)

    def sync(tag, after, G):
        if tag == "mlp1":
            return to_core("r1", ["mlp_w2_1", "mlp_w1_1"], G)
        if tag == "b_out_dx":
            return to_chips("r1", after)
        if tag == "kv_dw":
            return to_core("r2", ["b_w_out", "b_w_q", "kv_w"], G)
        if tag == "dh2":
            return to_chips("r2", after)
        if tag == "a_out_dw":
            return to_core("r3", ["mlp_w2_0", "mlp_w1_0", "a_w_out"], G)
        if tag == "a_out_dx":
            return to_chips("r3", after)
        if tag == "fox_bwd_dq":
            finish("r1", after)
            finish("r2", after)
        return ()

    loss_tile, dx, G = _local_step(x[0], loss_target[0], W, min(T, 512), arrive, sync)
    to_core("r4", ["a_w_in"], G)
    to_chips("r4", None)
    finish("r3", dx)

    def rows8(vecs):
        r = lax.broadcasted_iota(jnp.int32, (8, LANES), 0)
        tile = jnp.zeros((8, LANES), F32)
        for i, vec in enumerate(vecs):
            tile = jnp.where(r == i, vec, tile)
        return tile

    rows = lambda a: a.reshape(-1, LANES)
    packed = [rows(G["a_norm_g"]), rows(G["mlp_norm_g0"]), rows(G["mlp_norm_g1"]), rows(G["kv_norm_g"]),
              rows(G["b_norm_g"]), rows(G["b_rel"]),
              rows8([G["a_b_f"], G["a_q_g"], G["a_k_g"], G["kv_k_g"], G["b_q_g"], loss_tile[0:1]])]
    sizes = [p.shape[0] for p in packed]
    (all_small,) = _all_gather("gather_small_grads", [jnp.concatenate(packed, axis=0)])
    total = _sum_parts("sum_small_grads", all_small, F32)
    offs = np.concatenate([[0], np.cumsum(sizes)])
    s = [total[offs[i]:offs[i + 1]] for i in range(len(sizes))]
    loss = s[6][5, 0]
    g_a_norm = lax.dynamic_slice(s[0].reshape(1, D), (0, me * (D // N_DEV)), (1, D // N_DEV))
    g_rel = lax.dynamic_slice(s[5].reshape(H, N_REL_PAD), (0, me * n_rel), (H, n_rel))
    small = {
        "a_norm_g": (a_norm_g, m_a_norm_g, v_a_norm_g, g_a_norm),
        "a_b_f": (a_b_f, m_a_b_f, v_a_b_f, s[6][0:1, :H]),
        "a_q_g": (a_q_g, m_a_q_g, v_a_q_g, s[6][1:2]),
        "a_k_g": (a_k_g, m_a_k_g, v_a_k_g, s[6][2:3]),
        "mlp_norm_g": (mlp_norm_g, m_mlp_norm_g, v_mlp_norm_g,
                       jnp.concatenate([s[1].reshape(1, D), s[2].reshape(1, D)], axis=0)),
        "kv_norm_g": (kv_norm_g.reshape(1, D), m_kv_norm_g.reshape(1, D), v_kv_norm_g.reshape(1, D), s[3].reshape(1, D)),
        "kv_k_g": (kv_k_g.reshape(1, HEAD_DIM), m_kv_k_g.reshape(1, HEAD_DIM), v_kv_k_g.reshape(1, HEAD_DIM),
                   s[6][3:4]),
        "b_norm_g": (b_norm_g, m_b_norm_g, v_b_norm_g, s[4].reshape(1, D)),
        "b_q_g": (b_q_g, m_b_q_g, v_b_q_g, s[6][4:5]),
        "b_rel": (b_rel[0], m_b_rel[0], v_b_rel[0], g_rel),
    }
    for nm, (w, m, v, g) in small.items():
        upd[nm] = _adamw(f"adamw_{nm}", w, m, v, g)
    upd["kv_norm_g"] = [a.reshape(D) for a in upd["kv_norm_g"]]
    upd["kv_k_g"] = [a.reshape(HEAD_DIM) for a in upd["kv_k_g"]]
    upd["b_rel"] = [a[None] for a in upd["b_rel"]]
    finish("r4", total)
    upd["a_w_in"] = [jnp.swapaxes(o, 0, 1)[None] for o in upd["a_w_in"]]
    for nm, w in [("a_w_out", a_w_out), ("kv_w", kv_w), ("b_w_q", b_w_q), ("b_w_out", b_w_out), ("mlp_w1", mlp_w1),
                  ("mlp_w2", mlp_w2)]:
        upd[nm] = [o.reshape(w.shape) for o in upd[nm]]

    order = ["a_norm_g", "a_w_in", "a_b_f", "a_q_g", "a_k_g", "a_w_out", "mlp_norm_g", "mlp_w1", "mlp_w2",
             "kv_norm_g", "kv_w", "kv_k_g", "b_norm_g", "b_w_q", "b_q_g", "b_rel", "b_w_out"]
    return (loss, dx[None], *[upd[n][0] for n in order], *[upd[n][1] for n in order],
            *[upd[n][2] for n in order], *[upd[n][3] for n in order])
```

```python
import functools

import numpy as np
import jax
import jax.numpy as jnp
from jax import lax
from jax.experimental import pallas as pl
from jax.experimental.pallas import tpu as pltpu

F32 = jnp.float32
BF16 = jnp.bfloat16
HIGHEST = lax.Precision.HIGHEST
MESH = pl.DeviceIdType.MESH
ANY = pl.BlockSpec(memory_space=pl.ANY)

N_DEV = 8
N_CHIP = 4
LANES = 128
HEAD_DIM = 128
CHUNK = 64
N_PREV_CHUNKS = 8
REL_CLIP = 256
N_REL = REL_CLIP + CHUNK
N_REL_PAD = 384
WIN = N_PREV_CHUNKS * CHUNK
ROLL_W = 4 * WIN
EPS = 1e-6
NEG = -1e30
SCALE = HEAD_DIM ** -0.5
ADAM_LR, ADAM_B1, ADAM_B2, ADAM_EPS, ADAM_WD, ADAM_STEP = 0.001, 0.9, 0.999, 1e-08, 0.01, 10
V7X_VMEM_BYTES = 64 * 1024 * 1024
VMEM_LIMIT = V7X_VMEM_BYTES * 3 // 4

_NN = (((1,), (0,)), ((), ()))
_NT = (((1,), (1,)), ((), ()))
_TN = (((0,), (0,)), ((), ()))


def _params(*sem):
    return pltpu.CompilerParams(dimension_semantics=sem or None, vmem_limit_bytes=VMEM_LIMIT)


def _tile(n, pref):
    t = n
    while t > pref and t % 2 == 0:
        t //= 2
    return t


def _ep_store(acc, ex, outs):
    outs[0][...] = acc.astype(outs[0].dtype)


def _ep_add(acc, ex, outs):
    outs[0][...] = (acc + ex[0][...]).astype(outs[0].dtype)


def _ep_relu_square(acc, ex, outs):
    a = jnp.maximum(acc, 0.0)
    outs[0][...] = a.astype(outs[0].dtype)
    outs[1][...] = (a * a).astype(outs[1].dtype)


def _ep_times_2a(acc, ex, outs):
    outs[0][...] = (acc * (2.0 * ex[0][...].astype(F32))).astype(outs[0].dtype)


def _mm(name, kind, a, b, M, N, K, *, b_groups=0, out_groups=0, out_dtypes=(F32,), extras=(),
        epilogue=_ep_store, tm=1024, tn=1024, tk=2048, behind=()):
    tm, tn, tk = _tile(M, tm), _tile(N, tn), _tile(K, tk)
    if b_groups:
        cg = b.shape[-1]
        if kind == "nn":
            tn = min(tn, cg)
        else:
            tk = min(tk, cg)
    if out_groups:
        tn = min(tn, N // out_groups)
    grid = (M // tm, N // tn, K // tk)
    nk = grid[2]
    if kind == "nn":
        dims = _NN
        a_spec = pl.BlockSpec((tm, tk), lambda i, j, k: (i, k))
        if b_groups:
            npg = cg // tn
            b_spec = pl.BlockSpec((None, tk, tn), lambda i, j, k: (j // npg, k, j % npg))
        else:
            b_spec = pl.BlockSpec((tk, tn), lambda i, j, k: (k, j))
    elif kind == "nt":
        dims = _NT
        a_spec = pl.BlockSpec((tm, tk), lambda i, j, k: (i, k))
        if b_groups:
            npg = cg // tk
            b_spec = pl.BlockSpec((None, tn, tk), lambda i, j, k: (k // npg, j, k % npg))
        else:
            b_spec = pl.BlockSpec((tn, tk), lambda i, j, k: (j, k))
    else:
        dims = _TN
        a_spec = pl.BlockSpec((tk, tm), lambda i, j, k: (k, i))
        b_spec = pl.BlockSpec((tk, tn), lambda i, j, k: (k, j))
    tile_spec = pl.BlockSpec((tm, tn), lambda i, j, k: (i, j))
    if out_groups:
        ng = (N // out_groups) // tn
        out_spec = pl.BlockSpec((None, tm, tn), lambda i, j, k: (j // ng, i, j % ng))
        out_shape = [jax.ShapeDtypeStruct((out_groups, M, N // out_groups), d) for d in out_dtypes]
    else:
        out_spec = tile_spec
        out_shape = [jax.ShapeDtypeStruct((M, N), d) for d in out_dtypes]
    n_ex, n_out = len(extras), len(out_dtypes)

    def body(a_ref, b_ref, *rest):
        ex, outs, acc = rest[:n_ex], rest[-1 - n_out:-1], rest[-1]
        k = pl.program_id(2)

        @pl.when(k == 0)
        def _():
            acc[...] = jnp.zeros_like(acc)

        acc[...] += lax.dot_general(a_ref[...].astype(BF16), b_ref[...].astype(BF16), dims,
                                    preferred_element_type=F32)

        @pl.when(k == nk - 1)
        def _():
            epilogue(acc[...], ex, outs)

    res = pl.pallas_call(
        body, name=name, grid=grid,
        in_specs=[a_spec, b_spec] + [tile_spec] * n_ex + [ANY] * len(behind),
        out_specs=[out_spec] * n_out, out_shape=out_shape,
        scratch_shapes=[pltpu.VMEM((tm, tn), F32)],
        compiler_params=_params("parallel", "parallel", "arbitrary"),
    )(a, b, *extras, *behind)
    return res[0] if n_out == 1 else res


def _rmsnorm_fwd(name, x, gains):
    T, D = x.shape
    bt = _tile(T, 256)
    n = len(gains)

    def body(x_ref, *rest):
        xv = x_ref[...]
        y = xv * lax.rsqrt(jnp.mean(xv * xv, axis=-1, keepdims=True) + EPS)
        for g_ref, o_ref in zip(rest[:n], rest[n:]):
            o_ref[...] = (y * g_ref[...]).astype(BF16)

    row = pl.BlockSpec((bt, D), lambda i: (i, 0))
    gsp = pl.BlockSpec((1, D), lambda i: (0, 0))
    return pl.pallas_call(
        body, name=name, grid=(T // bt,), in_specs=[row] + [gsp] * n, out_specs=[row] * n,
        out_shape=[jax.ShapeDtypeStruct((T, D), BF16)] * n, compiler_params=_params("parallel"),
    )(x, *gains)


def _rmsnorm_bwd(name, x, g, dn, res):
    T, D = x.shape
    bt = _tile(T, 256)

    def body(x_ref, g_ref, dn_ref, res_ref, dx_ref, dx16_ref, dg_ref):
        @pl.when(pl.program_id(0) == 0)
        def _():
            dg_ref[...] = jnp.zeros_like(dg_ref)

        xv, dnv = x_ref[...], dn_ref[...]
        r = lax.rsqrt(jnp.mean(xv * xv, axis=-1, keepdims=True) + EPS)
        xhat = xv * r
        dyg = dnv * g_ref[...]
        dx = res_ref[...] + r * (dyg - xhat * jnp.mean(dyg * xhat, axis=-1, keepdims=True))
        dx_ref[...] = dx
        dx16_ref[...] = dx.astype(BF16)
        dg_ref[...] += jnp.sum(dnv * xhat, axis=0, keepdims=True)

    row = pl.BlockSpec((bt, D), lambda i: (i, 0))
    gsp = pl.BlockSpec((1, D), lambda i: (0, 0))
    return pl.pallas_call(
        body, name=name, grid=(T // bt,), in_specs=[row, gsp, row, row], out_specs=[row, row, gsp],
        out_shape=[jax.ShapeDtypeStruct((T, D), F32), jax.ShapeDtypeStruct((T, D), BF16),
                   jax.ShapeDtypeStruct((1, D), F32)],
        compiler_params=_params("arbitrary"),
    )(x, g, dn, res)


def _head_post(name, x, D, items):
    T = x.shape[0]
    H = D // HEAD_DIM
    bt = _tile(T, 256)
    gains = [g for _, g in items if g is not None]
    n_it, n_g = len(items), len(gains)

    def body(*refs):
        x_refs, g_refs, o_refs = refs[:n_it], refs[n_it:n_it + n_g], refs[n_it + n_g:]
        gi = 0
        for it, (_, g) in enumerate(items):
            if g is None:
                o_refs[it][...] = x_refs[it][...].astype(BF16)
                continue
            gv = g_refs[gi][...]
            gi += 1
            for h in range(H):
                sl = slice(h * HEAD_DIM, (h + 1) * HEAD_DIM)
                xs = x_refs[it][:, sl]
                r = lax.rsqrt(jnp.mean(xs * xs, axis=-1, keepdims=True) + EPS)
                o_refs[it][:, sl] = (xs * r * gv).astype(BF16)

    in_specs = [pl.BlockSpec((bt, D), functools.partial(lambda cb, i: (i, cb), cb)) for cb, _ in items]
    in_specs += [pl.BlockSpec((1, HEAD_DIM), lambda i: (0, 0))] * n_g
    row = pl.BlockSpec((bt, D), lambda i: (i, 0))
    res = pl.pallas_call(
        body, name=name, grid=(T // bt,), in_specs=in_specs, out_specs=[row] * n_it,
        out_shape=[jax.ShapeDtypeStruct((T, D), BF16)] * n_it, compiler_params=_params("parallel"),
    )(*([x] * n_it), *gains)
    return res


def _head_post_bwd(name, x, D, items):
    T = x.shape[0]
    H = D // HEAD_DIM
    bt = _tile(T, 128)
    gains = [g for _, g, _ in items if g is not None]
    n_it, n_g = len(items), len(gains)

    def body(*refs):
        x_refs, dy_refs = refs[:n_it], refs[n_it:2 * n_it]
        g_refs = refs[2 * n_it:2 * n_it + n_g]
        dx_ref = refs[2 * n_it + n_g]
        dg_refs = refs[2 * n_it + n_g + 1:]

        @pl.when(pl.program_id(0) == 0)
        def _():
            for r in dg_refs:
                r[...] = jnp.zeros_like(r)

        gi = 0
        for it, (_, g, _) in enumerate(items):
            if g is None:
                dx_ref[:, it * D:(it + 1) * D] = dy_refs[it][...].astype(BF16)
                continue
            gv = g_refs[gi][...]
            dg = jnp.zeros((1, HEAD_DIM), F32)
            for h in range(H):
                sl = slice(h * HEAD_DIM, (h + 1) * HEAD_DIM)
                xs, dy = x_refs[it][:, sl], dy_refs[it][:, sl]
                r = lax.rsqrt(jnp.mean(xs * xs, axis=-1, keepdims=True) + EPS)
                xhat = xs * r
                dyg = dy * gv
                dx = r * (dyg - xhat * jnp.mean(dyg * xhat, axis=-1, keepdims=True))
                dx_ref[:, it * D + h * HEAD_DIM:it * D + (h + 1) * HEAD_DIM] = dx.astype(BF16)
                dg = dg + jnp.sum(dy * xhat, axis=0, keepdims=True)
            dg_refs[gi][...] += dg
            gi += 1

    in_specs = [pl.BlockSpec((bt, D), functools.partial(lambda cb, i: (i, cb), cb)) for cb, _, _ in items]
    in_specs += [pl.BlockSpec((bt, D), lambda i: (i, 0))] * n_it
    gsp = pl.BlockSpec((1, HEAD_DIM), lambda i: (0, 0))
    in_specs += [gsp] * n_g
    res = pl.pallas_call(
        body, name=name, grid=(T // bt,), in_specs=in_specs,
        out_specs=[pl.BlockSpec((bt, n_it * D), lambda i: (i, 0))] + [gsp] * n_g,
        out_shape=[jax.ShapeDtypeStruct((T, n_it * D), BF16)] + [jax.ShapeDtypeStruct((1, HEAD_DIM), F32)] * n_g,
        compiler_params=_params("arbitrary"),
    )(*([x] * n_it), *[dy for _, _, dy in items], *gains)
    return res[0], list(res[1:])


def _gates_fwd(fz, bf, H):
    T = fz.shape[0]
    bt = _tile(T, 512)

    def body(fz_ref, bf_ref, c_ref, carry):
        @pl.when(pl.program_id(0) == 0)
        def _():
            carry[...] = jnp.zeros_like(carry)

        z = fz_ref[...] + bf_ref[...]
        logf = jnp.minimum(z, 0.0) - jnp.log(1.0 + jnp.exp(-jnp.abs(z)))
        lane = lax.broadcasted_iota(jnp.int32, (bt, LANES), 1)
        logf = jnp.where(lane < H, logf, 0.0)
        tri = (lax.broadcasted_iota(jnp.int32, (bt, bt), 0) >= lax.broadcasted_iota(jnp.int32, (bt, bt), 1))
        c = jnp.dot(tri.astype(F32), logf, precision=HIGHEST, preferred_element_type=F32) + carry[...]
        c_ref[...] = c
        carry[...] = c[bt - 1:bt, :]

    row = pl.BlockSpec((bt, LANES), lambda i: (i, 0))
    return pl.pallas_call(
        body, name="fox_gates_fwd", grid=(T // bt,),
        in_specs=[row, pl.BlockSpec((1, LANES), lambda i: (0, 0))], out_specs=row,
        out_shape=jax.ShapeDtypeStruct((T, LANES), F32), scratch_shapes=[pltpu.VMEM((1, LANES), F32)],
        compiler_params=_params("arbitrary"),
    )(fz, bf)


def _gates_bwd(dcq, dck, fz, bf, H):
    T = fz.shape[0]
    bt = _tile(T, 512)
    nb = T // bt

    def body(dcq_ref, dck_ref, fz_ref, bf_ref, dfz_ref, dbf_ref, carry):
        @pl.when(pl.program_id(0) == 0)
        def _():
            carry[...] = jnp.zeros_like(carry)
            dbf_ref[...] = jnp.zeros_like(dbf_ref)

        tri = (lax.broadcasted_iota(jnp.int32, (bt, bt), 0) <= lax.broadcasted_iota(jnp.int32, (bt, bt), 1))
        dc = dcq_ref[...] + dck_ref[...]
        dlogf = jnp.dot(tri.astype(F32), dc, precision=HIGHEST, preferred_element_type=F32) + carry[...]
        carry[...] = dlogf[0:1, :]
        z = fz_ref[...] + bf_ref[...]
        lane = lax.broadcasted_iota(jnp.int32, (bt, LANES), 1)
        dfz = jnp.where(lane < H, dlogf / (1.0 + jnp.exp(z)), 0.0)
        dfz_ref[...] = dfz
        dbf_ref[...] += jnp.sum(dfz, axis=0, keepdims=True)

    row = pl.BlockSpec((bt, LANES), lambda i: (nb - 1 - i, 0))
    one = pl.BlockSpec((1, LANES), lambda i: (0, 0))
    return pl.pallas_call(
        body, name="fox_gates_bwd", grid=(nb,), in_specs=[row, row, row, one], out_specs=[row, one],
        out_shape=[jax.ShapeDtypeStruct((T, LANES), F32), jax.ShapeDtypeStruct((1, LANES), F32)],
        scratch_shapes=[pltpu.VMEM((1, LANES), F32)], compiler_params=_params("arbitrary"),
    )(dcq, dck, fz, bf)


def _lane_pick(ref_value, h):
    lane = lax.broadcasted_iota(jnp.int32, (1, LANES), 1)
    return jnp.sum(jnp.where(lane == h, ref_value, 0.0), axis=1, keepdims=True)


def _lane_put(old, h, col):
    lane = lax.broadcasted_iota(jnp.int32, (1, LANES), 1)
    return jnp.where(lane == h, col, old)


FOX_HEADS = 4


def _causal(rows, cols, row0=0, transposed=False):
    row = row0 + lax.broadcasted_iota(jnp.int32, (rows, cols), 0)
    col = lax.broadcasted_iota(jnp.int32, (rows, cols), 1)
    return col >= row if transposed else row >= col


def _fox_fwd(q, k, v, ccol, crow, blk):
    T, D = q.shape
    H = D // HEAD_DIM
    nb = T // blk

    def body(q_ref, k_ref, v_ref, ccol_ref, crow_ref, o_ref, lse_ref, m_s, l_s, acc_s):
        qi, hg, kj = pl.program_id(0), pl.program_id(1), pl.program_id(2)

        @pl.when(kj == 0)
        def _():
            m_s[...] = jnp.full_like(m_s, NEG)
            l_s[...] = jnp.zeros_like(l_s)
            acc_s[...] = jnp.zeros_like(acc_s)

        @pl.when((kj == 0) & (hg == 0))
        def _():
            lse_ref[...] = jnp.zeros_like(lse_ref)

        def step(diagonal):
            heads = range(FOX_HEADS)
            sls = [slice(hh * HEAD_DIM, (hh + 1) * HEAD_DIM) for hh in heads]
            m_old = [m_s[hh] for hh in heads]
            l_old = [l_s[hh] for hh in heads]
            acc_old = [acc_s[:, sl] for sl in sls]
            s = []
            for hh in heads:
                h = hg * FOX_HEADS + hh
                bias = _lane_pick(ccol_ref[...], h) - crow_ref[pl.ds(h, 1), :]
                sc = lax.dot_general(q_ref[:, sls[hh]], k_ref[:, sls[hh]], _NT, preferred_element_type=F32) * SCALE + bias
                s.append(jnp.where(_causal(blk, blk), sc, NEG) if diagonal else sc)
            m_new = [jnp.maximum(m_old[hh], jnp.max(s[hh], axis=1, keepdims=True)) for hh in heads]
            p = [jnp.exp(s[hh] - m_new[hh]) for hh in heads]
            alpha = [jnp.exp(m_old[hh] - m_new[hh]) for hh in heads]
            pv = [jnp.dot(p[hh].astype(BF16), v_ref[:, sls[hh]], preferred_element_type=F32) for hh in heads]
            for hh in heads:
                l_s[hh] = alpha[hh] * l_old[hh] + jnp.sum(p[hh], axis=1, keepdims=True)
                acc_s[:, sls[hh]] = alpha[hh] * acc_old[hh] + pv[hh]
                m_s[hh] = m_new[hh]

        @pl.when(kj < qi)
        def _():
            step(False)

        @pl.when(kj == qi)
        def _():
            step(True)
            for hh in range(FOX_HEADS):
                sl = slice(hh * HEAD_DIM, (hh + 1) * HEAD_DIM)
                o_ref[:, sl] = (acc_s[:, sl] / l_s[hh]).astype(BF16)
                lse_ref[...] = _lane_put(lse_ref[...], hg * FOX_HEADS + hh, m_s[hh] + jnp.log(l_s[hh]))

    wide = FOX_HEADS * HEAD_DIM
    qsp = pl.BlockSpec((blk, wide), lambda qi, h, kj: (qi, h))
    ksp = pl.BlockSpec((blk, wide), lambda qi, h, kj: (jnp.minimum(kj, qi), h))
    stat = pl.BlockSpec((blk, LANES), lambda qi, h, kj: (qi, 0))
    return pl.pallas_call(
        body, name="fox_fwd", grid=(nb, H // FOX_HEADS, nb),
        in_specs=[qsp, ksp, ksp, stat, pl.BlockSpec((H, blk), lambda qi, h, kj: (0, jnp.minimum(kj, qi)))],
        out_specs=[qsp, stat],
        out_shape=[jax.ShapeDtypeStruct((T, D), BF16), jax.ShapeDtypeStruct((T, LANES), F32)],
        scratch_shapes=[pltpu.VMEM((FOX_HEADS, blk, 1), F32), pltpu.VMEM((FOX_HEADS, blk, 1), F32),
                        pltpu.VMEM((blk, wide), F32)],
        compiler_params=_params("arbitrary", "arbitrary", "arbitrary"),
    )(q, k, v, ccol, crow)


def _head_stat(name, a, b, behind=()):
    T, D = a.shape
    H = D // HEAD_DIM
    bt = _tile(T, 256)

    def body(a_ref, b_ref, *rest):
        o_ref = rest[-1]
        out = jnp.zeros((bt, LANES), F32)
        for h in range(H):
            sl = slice(h * HEAD_DIM, (h + 1) * HEAD_DIM)
            prod = a_ref[:, sl].astype(F32) * b_ref[:, sl].astype(F32)
            out = _lane_put(out, h, jnp.sum(prod, axis=1, keepdims=True))
        o_ref[...] = out

    row = pl.BlockSpec((bt, D), lambda i: (i, 0))
    return pl.pallas_call(
        body, name=name, grid=(T // bt,), in_specs=[row, row] + [ANY] * len(behind),
        out_specs=pl.BlockSpec((bt, LANES), lambda i: (i, 0)),
        out_shape=jax.ShapeDtypeStruct((T, LANES), F32), compiler_params=_params("parallel"),
    )(a, b, *behind)


def _fox_bwd_dq(q, k, v, do, ccol, crow, lse, delta, blk):
    T, D = q.shape
    H = D // HEAD_DIM
    nb = T // blk

    def body(q_ref, k_ref, v_ref, do_ref, ccol_ref, crow_ref, lse_ref, delta_ref, dq_ref, dc_ref, acc_s, dc_s):
        qi, hg, kj = pl.program_id(0), pl.program_id(1), pl.program_id(2)

        @pl.when(kj == 0)
        def _():
            acc_s[...] = jnp.zeros_like(acc_s)
            dc_s[...] = jnp.zeros_like(dc_s)

        @pl.when((kj == 0) & (hg == 0))
        def _():
            dc_ref[...] = jnp.zeros_like(dc_ref)

        def step(diagonal):
            heads = range(FOX_HEADS)
            sls = [slice(hh * HEAD_DIM, (hh + 1) * HEAD_DIM) for hh in heads]
            p, dp = [], []
            for hh in heads:
                h = hg * FOX_HEADS + hh
                bias = _lane_pick(ccol_ref[...], h) - crow_ref[pl.ds(h, 1), :]
                sc = lax.dot_general(q_ref[:, sls[hh]], k_ref[:, sls[hh]], _NT, preferred_element_type=F32) * SCALE + bias
                if diagonal:
                    sc = jnp.where(_causal(blk, blk), sc, NEG)
                p.append(jnp.exp(sc - _lane_pick(lse_ref[...], h)))
                dp.append(lax.dot_general(do_ref[:, sls[hh]], v_ref[:, sls[hh]], _NT, preferred_element_type=F32)
                          - _lane_pick(delta_ref[...], h))
            ds = [p[hh] * dp[hh] for hh in heads]
            for hh in heads:
                acc_s[:, sls[hh]] += jnp.dot(ds[hh].astype(BF16), k_ref[:, sls[hh]], preferred_element_type=F32)
                dc_s[hh] += jnp.sum(ds[hh], axis=1, keepdims=True)

        @pl.when(kj < qi)
        def _():
            step(False)

        @pl.when(kj == qi)
        def _():
            step(True)
            dq_ref[...] = acc_s[...] * SCALE
            for hh in range(FOX_HEADS):
                dc_ref[...] = _lane_put(dc_ref[...], hg * FOX_HEADS + hh, dc_s[hh])

    wide = FOX_HEADS * HEAD_DIM
    qsp = pl.BlockSpec((blk, wide), lambda qi, h, kj: (qi, h))
    ksp = pl.BlockSpec((blk, wide), lambda qi, h, kj: (jnp.minimum(kj, qi), h))
    stat = pl.BlockSpec((blk, LANES), lambda qi, h, kj: (qi, 0))
    return pl.pallas_call(
        body, name="fox_bwd_dq", grid=(nb, H // FOX_HEADS, nb),
        in_specs=[qsp, ksp, ksp, qsp, stat, pl.BlockSpec((H, blk), lambda qi, h, kj: (0, jnp.minimum(kj, qi))),
                  stat, stat],
        out_specs=[qsp, stat],
        out_shape=[jax.ShapeDtypeStruct((T, D), F32), jax.ShapeDtypeStruct((T, LANES), F32)],
        scratch_shapes=[pltpu.VMEM((blk, wide), F32), pltpu.VMEM((FOX_HEADS, blk, 1), F32)],
        compiler_params=_params("arbitrary", "arbitrary", "arbitrary"),
    )(q, k, v, do, ccol, crow, lse, delta)


def _fox_bwd_dkv(q, k, v, do, ccol, crow, lse_row, delta_row, blk):
    T, D = q.shape
    H = D // HEAD_DIM
    nb = T // blk

    def body(q_ref, k_ref, v_ref, do_ref, ccol_ref, crow_ref, lse_ref, delta_ref, dk_ref, dv_ref, dc_ref,
             dk_s, dv_s, dc_s):
        kj, hg, qi = pl.program_id(0), pl.program_id(1), pl.program_id(2)

        @pl.when(qi == 0)
        def _():
            dk_s[...] = jnp.zeros_like(dk_s)
            dv_s[...] = jnp.zeros_like(dv_s)
            dc_s[...] = jnp.zeros_like(dc_s)

        @pl.when((qi == 0) & (hg == 0))
        def _():
            dc_ref[...] = jnp.zeros_like(dc_ref)

        def step(diagonal):
            heads = range(FOX_HEADS)
            sls = [slice(hh * HEAD_DIM, (hh + 1) * HEAD_DIM) for hh in heads]
            pt, dpt = [], []
            for hh in heads:
                h = hg * FOX_HEADS + hh
                bias = crow_ref[pl.ds(h, 1), :] - _lane_pick(ccol_ref[...], h)
                st = lax.dot_general(k_ref[:, sls[hh]], q_ref[:, sls[hh]], _NT, preferred_element_type=F32) * SCALE + bias
                if diagonal:
                    st = jnp.where(_causal(blk, blk, transposed=True), st, NEG)
                pt.append(jnp.exp(st - lse_ref[pl.ds(h, 1), :]))
                dpt.append(lax.dot_general(v_ref[:, sls[hh]], do_ref[:, sls[hh]], _NT, preferred_element_type=F32)
                           - delta_ref[pl.ds(h, 1), :])
            dst = [pt[hh] * dpt[hh] for hh in heads]
            for hh in heads:
                dv_s[:, sls[hh]] += jnp.dot(pt[hh].astype(BF16), do_ref[:, sls[hh]], preferred_element_type=F32)
                dk_s[:, sls[hh]] += jnp.dot(dst[hh].astype(BF16), q_ref[:, sls[hh]], preferred_element_type=F32)
                dc_s[hh] -= jnp.sum(dst[hh], axis=1, keepdims=True)

        @pl.when(qi > kj)
        def _():
            step(False)

        @pl.when(qi == kj)
        def _():
            step(True)

        @pl.when(qi == nb - 1)
        def _():
            dk_ref[...] = dk_s[...] * SCALE
            dv_ref[...] = dv_s[...]
            for hh in range(FOX_HEADS):
                dc_ref[...] = _lane_put(dc_ref[...], hg * FOX_HEADS + hh, dc_s[hh])

    wide = FOX_HEADS * HEAD_DIM
    ksp = pl.BlockSpec((blk, wide), lambda kj, h, qi: (kj, h))
    qsp = pl.BlockSpec((blk, wide), lambda kj, h, qi: (jnp.maximum(qi, kj), h))
    kstat = pl.BlockSpec((blk, LANES), lambda kj, h, qi: (kj, 0))
    qrow = pl.BlockSpec((H, blk), lambda kj, h, qi: (0, jnp.maximum(qi, kj)))
    return pl.pallas_call(
        body, name="fox_bwd_dkv", grid=(nb, H // FOX_HEADS, nb),
        in_specs=[qsp, ksp, ksp, qsp, kstat, qrow, qrow, qrow],
        out_specs=[ksp, ksp, kstat],
        out_shape=[jax.ShapeDtypeStruct((T, D), F32), jax.ShapeDtypeStruct((T, D), F32),
                   jax.ShapeDtypeStruct((T, LANES), F32)],
        scratch_shapes=[pltpu.VMEM((blk, wide), F32), pltpu.VMEM((blk, wide), F32),
                        pltpu.VMEM((FOX_HEADS, blk, 1), F32)],
        compiler_params=_params("arbitrary", "arbitrary", "arbitrary"),
    )(q, k, v, do, ccol, crow, lse_row, delta_row)


def _rel_index():
    j = np.arange(ROLL_W)
    u = np.where(j < 2 * WIN, j, j - ROLL_W)
    return np.clip(WIN - u, -(CHUNK - 1), REL_CLIP) + (CHUNK - 1)


def _band_bias(g_row, bias_s):
    base = jnp.broadcast_to(g_row, (CHUNK, ROLL_W))
    rowb = lax.broadcasted_iota(jnp.int32, (CHUNK, ROLL_W), 0)
    for bit in range(CHUNK.bit_length() - 1):
        base = jnp.where((rowb >> bit) & 1 == 1, pltpu.roll(base, 1 << bit, 1), base)
    kc = lax.broadcasted_iota(jnp.int32, (CHUNK, 2 * WIN), 1) // CHUNK
    for a in range(N_PREV_CHUNKS):
        part = base if a == 0 else pltpu.roll(base, CHUNK * a, 1)
        valid = (kc >= a) & (kc <= a + N_PREV_CHUNKS)
        bias_s[a * CHUNK:(a + 1) * CHUNK, :] = jnp.where(valid, part[:, :2 * WIN], NEG)


def _chunk_probs(q, kp, kc, bias_s, qi):
    sp = lax.dot_general(q, kp, _NT, preferred_element_type=F32) * SCALE + bias_s[:, :WIN]
    sc = lax.dot_general(q, kc, _NT, preferred_element_type=F32) * SCALE + bias_s[:, WIN:]
    sp = jnp.where(qi > 0, sp, NEG)
    m = jnp.maximum(jnp.max(sp, axis=1, keepdims=True), jnp.max(sc, axis=1, keepdims=True))
    pp, pc = jnp.exp(sp - m), jnp.exp(sc - m)
    inv = 1.0 / (jnp.sum(pp, axis=1, keepdims=True) + jnp.sum(pc, axis=1, keepdims=True))
    return pp * inv, pc * inv


def _chunk_fwd(q, k, v, g_rows):
    T, D = q.shape
    H = D // HEAD_DIM
    nq = T // WIN

    def body(q_ref, kp_ref, kc_ref, vp_ref, vc_ref, g_ref, o_ref, bias_s):
        qi = pl.program_id(1)

        @pl.when(qi == 0)
        def _():
            _band_bias(g_ref[...], bias_s)

        pp, pc = _chunk_probs(q_ref[...], kp_ref[...], kc_ref[...], bias_s, qi)
        o = jnp.dot(pp.astype(BF16), vp_ref[...], preferred_element_type=F32)
        o += jnp.dot(pc.astype(BF16), vc_ref[...], preferred_element_type=F32)
        o_ref[...] = o.astype(BF16)

    cur = pl.BlockSpec((WIN, HEAD_DIM), lambda h, qi: (qi, h))
    prev = pl.BlockSpec((WIN, HEAD_DIM), lambda h, qi: (jnp.maximum(qi - 1, 0), h))
    return pl.pallas_call(
        body, name="chunk_fwd", grid=(H, nq),
        in_specs=[cur, prev, cur, prev, cur, pl.BlockSpec((None, 1, ROLL_W), lambda h, qi: (h, 0, 0))],
        out_specs=cur, out_shape=jax.ShapeDtypeStruct((T, D), BF16),
        scratch_shapes=[pltpu.VMEM((WIN, 2 * WIN), F32)],
        compiler_params=_params("arbitrary", "arbitrary"),
    )(q, k, k, v, v, g_rows)


def _chunk_bwd(q, k, v, g_rows, do, o):
    T, D = q.shape
    H = D // HEAD_DIM
    nq = T // WIN

    def body(q_ref, kp_ref, kc_ref, vp_ref, vc_ref, g_ref, do_ref, o_ref,
             dq_ref, dk_ref, dv_ref, ds_ref, bias_s, ck_s, cv_s):
        qi = pl.program_id(1)

        @pl.when(qi == 0)
        def _():
            _band_bias(g_ref[...], bias_s)
            ds_ref[...] = jnp.zeros_like(ds_ref)
            ck_s[...] = jnp.zeros_like(ck_s)
            cv_s[...] = jnp.zeros_like(cv_s)

        @pl.when(qi < nq)
        def _():
            qv, dov = q_ref[...], do_ref[...]
            pp, pc = _chunk_probs(qv, kp_ref[...], kc_ref[...], bias_s, qi)
            delta = jnp.sum(dov.astype(F32) * o_ref[...].astype(F32), axis=1, keepdims=True)
            dsp = pp * (lax.dot_general(dov, vp_ref[...], _NT, preferred_element_type=F32) - delta)
            dsc = pc * (lax.dot_general(dov, vc_ref[...], _NT, preferred_element_type=F32) - delta)
            dsp16, dsc16 = dsp.astype(BF16), dsc.astype(BF16)
            dq = jnp.dot(dsp16, kp_ref[...], preferred_element_type=F32)
            dq += jnp.dot(dsc16, kc_ref[...], preferred_element_type=F32)
            dq_ref[...] = dq * SCALE
            ds_ref[:, :WIN] += dsp
            ds_ref[:, WIN:] += dsc
            dk_ref[...] = ck_s[...] + lax.dot_general(dsp16, qv, _TN, preferred_element_type=F32) * SCALE
            dv_ref[...] = cv_s[...] + lax.dot_general(pp.astype(BF16), dov, _TN, preferred_element_type=F32)
            ck_s[...] = lax.dot_general(dsc16, qv, _TN, preferred_element_type=F32) * SCALE
            cv_s[...] = lax.dot_general(pc.astype(BF16), dov, _TN, preferred_element_type=F32)

        @pl.when(qi == nq)
        def _():
            dk_ref[...] = ck_s[...]
            dv_ref[...] = cv_s[...]

    cur = pl.BlockSpec((WIN, HEAD_DIM), lambda h, qi: (jnp.minimum(qi, nq - 1), h))
    prev = pl.BlockSpec((WIN, HEAD_DIM), lambda h, qi: (jnp.clip(qi - 1, 0, nq - 2), h))
    late = pl.BlockSpec((WIN, HEAD_DIM), lambda h, qi: (jnp.maximum(qi - 1, 0), h))
    return pl.pallas_call(
        body, name="chunk_bwd", grid=(H, nq + 1),
        in_specs=[cur, prev, cur, prev, cur, pl.BlockSpec((None, 1, ROLL_W), lambda h, qi: (h, 0, 0)), cur, cur],
        out_specs=[cur, late, late, pl.BlockSpec((None, WIN, 2 * WIN), lambda h, qi: (h, 0, 0))],
        out_shape=[jax.ShapeDtypeStruct((T, D), F32)] * 3 + [jax.ShapeDtypeStruct((H, WIN, 2 * WIN), F32)],
        scratch_shapes=[pltpu.VMEM((WIN, 2 * WIN), F32), pltpu.VMEM((WIN, HEAD_DIM), F32),
                        pltpu.VMEM((WIN, HEAD_DIM), F32)],
        compiler_params=_params("arbitrary", "arbitrary"),
    )(q, k, k, v, v, g_rows, do, o)


def _rel_rows(rel, onehot_t):
    H = rel.shape[0]

    def body(rel_ref, oh_ref, out_ref):
        out_ref[...] = jnp.dot(rel_ref[...], oh_ref[...], precision=HIGHEST, preferred_element_type=F32)

    return pl.pallas_call(body, name="rel_rows", out_shape=jax.ShapeDtypeStruct((H, ROLL_W), F32))(rel, onehot_t)


def _rel_grad(ds_sum, onehot):
    H = ds_sum.shape[0]

    def body(ds_ref, oh_ref, out_ref):
        y = jnp.zeros((CHUNK, ROLL_W), F32)
        pad = jnp.zeros((CHUNK, ROLL_W - 2 * WIN), F32)
        for a in range(N_PREV_CHUNKS):
            part = jnp.concatenate([ds_ref[a * CHUNK:(a + 1) * CHUNK, :], pad], axis=1)
            y = y + (part if a == 0 else pltpu.roll(part, ROLL_W - CHUNK * a, 1))
        rowb = lax.broadcasted_iota(jnp.int32, (CHUNK, ROLL_W), 0)
        for bit in range(CHUNK.bit_length() - 1):
            y = jnp.where((rowb >> bit) & 1 == 1, pltpu.roll(y, ROLL_W - (1 << bit), 1), y)
        diag = jnp.broadcast_to(jnp.sum(y, axis=0, keepdims=True), (8, ROLL_W))
        out_ref[...] = jnp.dot(diag, oh_ref[...], precision=HIGHEST, preferred_element_type=F32)

    return pl.pallas_call(
        body, name="rel_grad", grid=(H,),
        in_specs=[pl.BlockSpec((None, WIN, 2 * WIN), lambda h: (h, 0, 0)),
                  pl.BlockSpec((ROLL_W, N_REL_PAD), lambda h: (0, 0))],
        out_specs=pl.BlockSpec((None, 8, N_REL_PAD), lambda h: (h, 0, 0)),
        out_shape=jax.ShapeDtypeStruct((H, 8, N_REL_PAD), F32), compiler_params=_params("parallel"),
    )(ds_sum, onehot)


def _loss_head(y, target):
    T, D = y.shape
    bt = _tile(T, 256)

    def body(y_ref, t_ref, dy_ref, dy16_ref, loss_ref):
        @pl.when(pl.program_id(0) == 0)
        def _():
            loss_ref[...] = jnp.zeros_like(loss_ref)

        err = y_ref[...] - t_ref[...]
        dy = err * (1.0 / D)
        dy_ref[...] = dy
        dy16_ref[...] = dy.astype(BF16)
        loss_ref[...] += 0.5 * jnp.sum(jnp.mean(err * err, axis=-1, keepdims=True))

    row = pl.BlockSpec((bt, D), lambda i: (i, 0))
    one = pl.BlockSpec((8, LANES), lambda i: (0, 0))
    return pl.pallas_call(
        body, name="loss_head", grid=(T // bt,), in_specs=[row, row], out_specs=[row, row, one],
        out_shape=[jax.ShapeDtypeStruct((T, D), F32), jax.ShapeDtypeStruct((T, D), BF16),
                   jax.ShapeDtypeStruct((8, LANES), F32)],
        compiler_params=_params("arbitrary"),
    )(y, target)


def _adam_step(g, w_ref, m_ref, v_ref, g_ref, d_ref, nm_ref, nv_ref):
    nm = ADAM_B1 * m_ref[...] + (1.0 - ADAM_B1) * g
    nv = ADAM_B2 * v_ref[...] + (1.0 - ADAM_B2) * (g * g)
    m_hat = nm / (1.0 - ADAM_B1 ** ADAM_STEP)
    v_hat = nv / (1.0 - ADAM_B2 ** ADAM_STEP)
    g_ref[...] = g
    nm_ref[...] = nm
    nv_ref[...] = nv
    d_ref[...] = -ADAM_LR * (m_hat / (jnp.sqrt(v_hat) + ADAM_EPS) + ADAM_WD * w_ref[...])


def _adamw(name, w, m, v, g):
    def body(w_ref, m_ref, v_ref, gin_ref, g_ref, d_ref, nm_ref, nv_ref):
        _adam_step(gin_ref[...], w_ref, m_ref, v_ref, g_ref, d_ref, nm_ref, nv_ref)

    return pl.pallas_call(body, name=name, out_shape=[jax.ShapeDtypeStruct(w.shape, F32)] * 4)(w, m, v, g)


def _adamw_reduced(name, w, m, v, own, land, chip, row0=0, into=None):
    rows_all, C = w.shape
    _, R, _ = land.shape

    def body(chip_ref, w_ref, m_ref, v_ref, own_ref, land_ref, *rest):
        g = jnp.where(chip_ref[0] == 0, own_ref[...], land_ref[0]).astype(F32)
        for q in range(1, N_CHIP):
            g = g + jnp.where(chip_ref[0] == q, own_ref[...], land_ref[q]).astype(F32)
        _adam_step(g, w_ref, m_ref, v_ref, *rest[-4:])

    if R % 16 == 0:
        br = _tile(R, max(16, (128 * 1024) // C))
        first = row0 // br
        grid = (R // br,)
        wsp = pl.BlockSpec((br, C), lambda i, chip_ref: (first + i, 0))
        osp = pl.BlockSpec((None, br, C), lambda i, chip_ref: (chip_ref[0], i, 0))
        lsp = pl.BlockSpec((N_CHIP, br, C), lambda i, chip_ref: (0, i, 0))
    else:
        bc = _tile(C, 512)
        grid = (C // bc,)
        wsp = pl.BlockSpec((R, bc), lambda i, chip_ref: (0, i))
        osp = pl.BlockSpec((None, R, bc), lambda i, chip_ref: (chip_ref[0], 0, i))
        lsp = pl.BlockSpec((N_CHIP, R, bc), lambda i, chip_ref: (0, 0, i))
    extra = [] if into is None else list(into)
    return pl.pallas_call(
        body, name=name,
        grid_spec=pltpu.PrefetchScalarGridSpec(
            num_scalar_prefetch=1, grid=grid, in_specs=[wsp, wsp, wsp, osp, lsp] + [ANY] * len(extra),
            out_specs=[wsp] * 4),
        out_shape=[jax.ShapeDtypeStruct((rows_all, C), F32)] * 4,
        input_output_aliases={6 + j: j for j in range(len(extra))}, compiler_params=_params("parallel"),
    )(chip, w, m, v, own, land, *extra)


def _sum_parts(name, parts, out_dtype):
    P, R, C = parts.shape
    br = _tile(R, max(8, (256 * 1024) // C)) if R % 8 == 0 else R

    def body(p_ref, o_ref):
        g = p_ref[0].astype(F32)
        for i in range(1, P):
            g = g + p_ref[i].astype(F32)
        o_ref[...] = g.astype(out_dtype)

    return pl.pallas_call(
        body, name=name, grid=(R // br,),
        in_specs=[pl.BlockSpec((P, br, C), lambda i: (0, i, 0))], out_specs=pl.BlockSpec((br, C), lambda i: (i, 0)),
        out_shape=jax.ShapeDtypeStruct((R, C), out_dtype), compiler_params=_params("parallel"),
    )(parts)


def _pair_sum(name, g, recv, core):
    _, R, C = g.shape

    def body(core_ref, g_ref, r_ref, o_ref):
        o_ref[...] = (g_ref[...] + r_ref[...]).astype(BF16)

    if R % 16 == 0:
        br = _tile(R, max(16, (256 * 1024) // C))
        blk, n = (None, br, C), R // br
        mine = lambda q, i, core_ref: (2 * q + core_ref[0], i, 0)
        same = lambda q, i, core_ref: (q, i, 0)
    else:
        bc = _tile(C, 512)
        blk, n = (None, R, bc), C // bc
        mine = lambda q, i, core_ref: (2 * q + core_ref[0], 0, i)
        same = lambda q, i, core_ref: (q, 0, i)
    return pl.pallas_call(
        body, name=name,
        grid_spec=pltpu.PrefetchScalarGridSpec(
            num_scalar_prefetch=1, grid=(N_CHIP, n),
            in_specs=[pl.BlockSpec(blk, mine), pl.BlockSpec(blk, same)], out_specs=pl.BlockSpec(blk, same)),
        out_shape=jax.ShapeDtypeStruct((N_CHIP, R, C), BF16), compiler_params=_params("parallel", "parallel"),
    )(core, g, recv)


def _chips_of(x, y):
    return [(1 - x, y), (x, 1 - y), (1 - x, 1 - y)]


def _position():
    x, y, c = lax.axis_index("x"), lax.axis_index("y"), lax.axis_index("c")
    return x, y, c, _chips_of(x, y)


def _all_gather(name, xs):
    n = len(xs)

    def body(*refs):
        x_refs, out_refs = refs[:n], refs[n:2 * n]
        send_sems, recv_sems, local_sems = refs[2 * n:]
        x, y, c, chips = _position()
        me, sibling = (x, y, c), (x, y, 1 - c)

        def copy(a, k, block, to, src=None):
            px, py, pc = block
            slot = out_refs[a].at[4 * px + 2 * py + pc]
            return pltpu.make_async_remote_copy(
                src_ref=slot if src is None else src, dst_ref=slot,
                send_sem=send_sems.at[a, k], recv_sem=recv_sems.at[a, k], device_id=to, device_id_type=MESH)

        mine = [pltpu.make_async_copy(x_refs[a], out_refs[a].at[4 * x + 2 * y + c], local_sems.at[a]) for a in range(n)]
        for cp in mine:
            cp.start()
        first = []
        for a in range(n):
            first.append(copy(a, 0, me, sibling, src=x_refs[a]))
            first += [copy(a, 1 + j, me, (*chip, c), src=x_refs[a]) for j, chip in enumerate(chips)]
        for cp in first:
            cp.start()
        passed = []
        for j, chip in enumerate(chips):
            for a in range(n):
                copy(a, 1 + j, (*chip, c), me).wait_recv()
                fwd = copy(a, 4 + j, (*chip, c), sibling)
                fwd.start()
                passed.append(fwd)
        for a in range(n):
            copy(a, 0, sibling, me).wait_recv()
            for j, chip in enumerate(chips):
                copy(a, 4 + j, (*chip, 1 - c), me).wait_recv()
        for cp in first + passed:
            cp.wait_send()
        for cp in mine:
            cp.wait()

    return pl.pallas_call(
        body, name=name, in_specs=[ANY] * n, out_specs=[ANY] * n,
        out_shape=[jax.ShapeDtypeStruct((N_DEV,) + x.shape, x.dtype) for x in xs],
        scratch_shapes=[pltpu.SemaphoreType.DMA((n, 7)), pltpu.SemaphoreType.DMA((n, 7)),
                        pltpu.SemaphoreType.DMA((n,))],
    )(*xs)


HBM = pl.BlockSpec(memory_space=pltpu.HBM)
SEM = pl.BlockSpec(memory_space=pltpu.SEMAPHORE)
EFFECT = pltpu.SideEffectType.DATAFLOW_SIDE_EFFECTING


def _peers(x, y, c):
    return [(x, y, 1 - c), (1 - x, y, c), (x, 1 - y, c), (1 - x, 1 - y, c),
            (1 - x, y, 1 - c), (x, 1 - y, 1 - c), (1 - x, 1 - y, 1 - c)]


def _gather_start(name, lands):
    n = len(lands)

    def body(*refs):
        land = refs[:n]
        send_sems, recv_sems = refs[n:2 * n], refs[2 * n:3 * n]
        token = refs[4 * n]
        x, y, c, _ = _position()
        mine = 4 * x + 2 * y + c
        for a in range(n):
            for k, peer in enumerate(_peers(x, y, c)):
                pltpu.make_async_remote_copy(
                    src_ref=land[a].at[mine], dst_ref=land[a].at[mine], send_sem=send_sems[a].at[k],
                    recv_sem=recv_sems[a].at[k], device_id=peer, device_id_type=MESH).start()
        token[...] = jnp.zeros_like(token)

    res = pl.pallas_call(
        body, name=name, in_specs=[HBM] * n,
        out_specs=[SEM] * (2 * n) + [HBM] * n + [pl.BlockSpec(memory_space=pltpu.VMEM)],
        out_shape=[pltpu.SemaphoreType.DMA((N_DEV - 1,))] * (2 * n) + [pltpu.HBM(l.shape, l.dtype) for l in lands]
        + [jax.ShapeDtypeStruct((8, LANES), F32)],
        input_output_aliases={a: 2 * n + a for a in range(n)},
        compiler_params=pltpu.CompilerParams(has_side_effects=EFFECT),
    )(*[pltpu.with_memory_space_constraint(l, pltpu.HBM) for l in lands])
    return res[:n], res[n:2 * n], res[2 * n:3 * n], res[3 * n]


def _gather_wait(name, lands, send_sems, recv_sems, after):
    n = len(lands)
    tail = [] if after is None else [after]

    def body(*refs):
        land = refs[:n]
        send_sems, recv_sems = refs[n:2 * n], refs[2 * n:3 * n]
        x, y, c, _ = _position()
        mine = 4 * x + 2 * y + c
        for a in range(n):
            for k, (px, py, pc) in enumerate(_peers(x, y, c)):
                cp = pltpu.make_async_remote_copy(
                    src_ref=land[a].at[mine], dst_ref=land[a].at[4 * px + 2 * py + pc], send_sem=send_sems[a].at[k],
                    recv_sem=recv_sems[a].at[k], device_id=(px, py, pc), device_id_type=MESH)
                cp.wait_send()
                cp.wait_recv()

    return pl.pallas_call(
        body, name=name, in_specs=[HBM] * n + [SEM] * (2 * n) + [ANY] * len(tail), out_specs=[HBM] * n,
        out_shape=[pltpu.HBM(l.shape, l.dtype) for l in lands], input_output_aliases={a: a for a in range(n)},
        compiler_params=pltpu.CompilerParams(has_side_effects=EFFECT),
    )(*lands, *send_sems, *recv_sems, *tail)


def _split_start(name, arrays, lands, plan):
    n = len(arrays)
    n_copies = len(plan(0, 0, 0))

    def body(*refs):
        src, land = refs[:n], refs[n:2 * n]
        send_sems, recv_sems = refs[2 * n:3 * n], refs[3 * n:4 * n]
        token = refs[6 * n]
        x, y, c, _ = _position()
        for a in range(n):
            for k, (src_block, land_block, peer) in enumerate(plan(x, y, c)):
                pltpu.make_async_remote_copy(
                    src_ref=src[a].at[src_block], dst_ref=land[a].at[land_block], send_sem=send_sems[a].at[k],
                    recv_sem=recv_sems[a].at[k], device_id=peer, device_id_type=MESH).start()
        token[...] = jnp.zeros_like(token)

    both = list(arrays) + list(lands)
    res = pl.pallas_call(
        body, name=name, in_specs=[HBM] * (2 * n),
        out_specs=[SEM] * (2 * n) + [HBM] * (2 * n) + [pl.BlockSpec(memory_space=pltpu.VMEM)],
        out_shape=[pltpu.SemaphoreType.DMA((n_copies,))] * (2 * n) + [pltpu.HBM(b.shape, b.dtype) for b in both]
        + [jax.ShapeDtypeStruct((8, LANES), F32)],
        input_output_aliases={i: 2 * n + i for i in range(2 * n)},
        compiler_params=pltpu.CompilerParams(has_side_effects=EFFECT),
    )(*[pltpu.with_memory_space_constraint(b, pltpu.HBM) for b in both])
    return res[:n], res[n:2 * n], res[2 * n:3 * n], res[3 * n:4 * n], res[4 * n]


def _split_wait(name, arrays, lands, send_sems, recv_sems, plan, after):
    n = len(arrays)
    tail = [] if after is None else [after]

    def body(*refs):
        src, land = refs[:n], refs[n:2 * n]
        send, recv = refs[2 * n:3 * n], refs[3 * n:4 * n]
        x, y, c, _ = _position()
        for a in range(n):
            for k, (src_block, land_block, peer) in enumerate(plan(x, y, c)):
                cp = pltpu.make_async_remote_copy(
                    src_ref=src[a].at[src_block], dst_ref=land[a].at[land_block], send_sem=send[a].at[k],
                    recv_sem=recv[a].at[k], device_id=peer, device_id_type=MESH)
                cp.wait_send()
                cp.wait_recv()

    both = list(arrays) + list(lands)
    res = pl.pallas_call(
        body, name=name, in_specs=[HBM] * (2 * n) + [SEM] * (2 * n) + [ANY] * len(tail), out_specs=[HBM] * (2 * n),
        out_shape=[pltpu.HBM(b.shape, b.dtype) for b in both], input_output_aliases={i: i for i in range(2 * n)},
        compiler_params=pltpu.CompilerParams(has_side_effects=EFFECT),
    )(*both, *send_sems, *recv_sems, *tail)
    return res[:n], res[n:]


def _to_other_core(x, y, c):
    return [(2 * q + (1 - c), q, (x, y, 1 - c)) for q in range(N_CHIP)]


def _to_other_chips(x, y, c):
    return [(2 * cx + cy, 2 * x + y, (cx, cy, c)) for cx, cy in _chips_of(x, y)]


def _from_other_chips(x, y, c):
    return [(2 * x + y, 2 * cx + cy, (cx, cy, c)) for cx, cy in _chips_of(x, y)]


def _mlp_fwd(tag, h, g, w1g, w2):
    T, D = h.shape
    F = w2.shape[0]
    (n,) = _rmsnorm_fwd(f"mlp{tag}_norm", h, [g])
    a, hid = _mm(f"mlp{tag}_up", "nn", n, w1g, T, F, D, b_groups=N_DEV, out_dtypes=(BF16, BF16),
                 epilogue=_ep_relu_square)
    out = _mm(f"mlp{tag}_down", "nn", hid, w2, T, D, F, extras=(h,), epilogue=_ep_add)
    return out, (n, a, hid)


def _mlp_bwd(tag, h, g, w1g, w2, saved, dy, dy16, behind=()):
    T, D = h.shape
    F = w2.shape[0]
    n, a, hid = saved
    dw2 = _mm(f"mlp{tag}_dw2", "tn", hid, dy16, F, D, T, behind=behind)
    dpre = _mm(f"mlp{tag}_dpre", "nt", dy16, w2, T, F, D, out_dtypes=(BF16,), extras=(a,), epilogue=_ep_times_2a)
    dw1 = _mm(f"mlp{tag}_dw1", "tn", n, dpre, D, F, T, out_groups=N_DEV)
    dn = _mm(f"mlp{tag}_dn", "nt", dpre, w1g, T, D, F, b_groups=N_DEV)
    dh, dh16, dg = _rmsnorm_bwd(f"mlp{tag}_norm_bwd", h, g, dn, dy)
    return dh, dh16, dg, dw1, dw2


def _local_step(x, target, W, fox_blk, arrive=lambda group, after: {}, sync=lambda tag, after, G: ()):
    T, D = x.shape
    H = D // HEAD_DIM
    G = {}
    W = dict(W)

    (ua,) = _rmsnorm_fwd("a_norm", x, [W["a_norm_g"]])
    qkv = _mm("a_qkv", "nt", ua, W["a_w_in_t"], T, 3 * D, D)
    fz = _mm("a_fz", "nt", ua, W["a_w_f_t"], T, LANES, D)
    qa, ka, va = _head_post("a_heads", qkv, D, [(0, W["a_q_g"]), (1, W["a_k_g"]), (2, None)])
    ccol = _gates_fwd(fz, W["a_b_f"], H)
    crow = ccol[:, :H].T
    oa, lse = _fox_fwd(qa, ka, va, ccol, crow, fox_blk)
    W.update(arrive("a_out", oa))
    h1 = _mm("a_out", "nn", oa, W["a_w_out"], T, D, D, extras=(x,), epilogue=_ep_add)
    W.update(arrive("middle", h1))
    h2, mlp0 = _mlp_fwd("0", h1, W["mlp_norm_g0"], W["mlp_w1_0"], W["mlp_w2_0"])
    W.update(arrive("kv_q", h2))
    nkv, ub = _rmsnorm_fwd("kv_b_norm", h2, [W["kv_norm_g"], W["b_norm_g"]])
    kv = _mm("kv_proj", "nn", nkv, W["kv_w"], T, 2 * D, D, b_groups=N_DEV)
    ks, vs = _head_post("kv_heads", kv, D, [(0, W["kv_k_g"]), (1, None)])
    qb_pre = _mm("b_q", "nn", ub, W["b_w_q"], T, D, D)
    (qb,) = _head_post("b_heads", qb_pre, D, [(0, W["b_q_g"])])
    onehot = (jnp.asarray(_rel_index())[:, None] == jnp.arange(N_REL_PAD)[None, :]).astype(F32)
    rel_pad = jnp.pad(W["b_rel"], ((0, 0), (0, N_REL_PAD - N_REL)))
    g_rows = _rel_rows(rel_pad, onehot.T).reshape(H, 1, ROLL_W)
    ob = _chunk_fwd(qb, ks, vs, g_rows)
    W.update(arrive("late", ob))
    h3 = _mm("b_out", "nn", ob, W["b_w_out"], T, D, D, extras=(h2,), epilogue=_ep_add)
    h4, mlp1 = _mlp_fwd("1", h3, W["mlp_norm_g1"], W["mlp_w1_1"], W["mlp_w2_1"])
    dy, dy16, loss = _loss_head(h4, target)

    dh3, dh3_16, G["mlp_norm_g1"], G["mlp_w1_1"], G["mlp_w2_1"] = _mlp_bwd(
        "1", h3, W["mlp_norm_g1"], W["mlp_w1_1"], W["mlp_w2_1"], mlp1, dy, dy16)
    dob = _mm("b_out_dx", "nt", dh3_16, W["b_w_out"], T, D, D, out_dtypes=(BF16,), behind=sync("mlp1", dh3, G))
    G["b_w_out"] = _mm("b_out_dw", "tn", ob, dh3_16, D, D, T, behind=sync("b_out_dx", dob, G))
    dqb, dks, dvs, ds_sum = _chunk_bwd(qb, ks, vs, g_rows, dob, ob)
    G["b_rel"] = _rel_grad(ds_sum, onehot)[:, 0, :]
    dqb_pre, (G["b_q_g"],) = _head_post_bwd("b_heads_bwd", qb_pre, D, [(0, W["b_q_g"], dqb)])
    G["b_w_q"] = _mm("b_q_dw", "tn", ub, dqb_pre, D, D, T)
    dub = _mm("b_q_dx", "nt", dqb_pre, W["b_w_q"], T, D, D)
    dkv, (G["kv_k_g"],) = _head_post_bwd("kv_heads_bwd", kv, D, [(0, W["kv_k_g"], dks), (1, None, dvs)])
    G["kv_w"] = _mm("kv_dw", "tn", nkv, dkv, D, 2 * D, T, out_groups=N_DEV)
    dnkv = _mm("kv_dx", "nt", dkv, W["kv_w"], T, D, 2 * D, b_groups=N_DEV, behind=sync("kv_dw", G["kv_w"], G))
    dh2, _, G["b_norm_g"] = _rmsnorm_bwd("b_norm_bwd", h2, W["b_norm_g"], dub, dh3)
    dh2, dh2_16, G["kv_norm_g"] = _rmsnorm_bwd("kv_norm_bwd", h2, W["kv_norm_g"], dnkv, dh2)
    dh1, dh1_16, G["mlp_norm_g0"], G["mlp_w1_0"], G["mlp_w2_0"] = _mlp_bwd(
        "0", h1, W["mlp_norm_g0"], W["mlp_w1_0"], W["mlp_w2_0"], mlp0, dh2, dh2_16, behind=sync("dh2", dh2, G))
    G["a_w_out"] = _mm("a_out_dw", "tn", oa, dh1_16, D, D, T)
    doa = _mm("a_out_dx", "nt", dh1_16, W["a_w_out"], T, D, D, out_dtypes=(BF16,),
              behind=sync("a_out_dw", G["a_w_out"], G))
    delta = _head_stat("fox_delta", doa, oa, behind=sync("a_out_dx", doa, G))
    dqa, dcq = _fox_bwd_dq(qa, ka, va, doa, ccol, crow, lse, delta, fox_blk)
    sync("fox_bwd_dq", dqa, G)
    dka, dva, dck = _fox_bwd_dkv(qa, ka, va, doa, ccol, crow, lse[:, :H].T, delta[:, :H].T, fox_blk)
    dfz, G["a_b_f"] = _gates_bwd(dcq, dck, fz, W["a_b_f"], H)
    dqkv, (G["a_q_g"], G["a_k_g"]) = _head_post_bwd(
        "a_heads_bwd", qkv, D, [(0, W["a_q_g"], dqa), (1, W["a_k_g"], dka), (2, None, dva)])
    G["a_w_qkv_t"] = _mm("a_qkv_dw", "tn", dqkv, ua, 3 * D, D, T)
    G["a_w_f_t"] = _mm("a_fz_dw", "tn", dfz, ua, LANES, D, T)
    dua = _mm("a_qkv_dx", "nn", dqkv, W["a_w_in_t"], T, D, 3 * D, behind=sync("a_in_dw", G["a_w_f_t"], G))
    dua = _mm("a_fz_dx", "nn", dfz, W["a_w_f_t"], T, D, LANES, extras=(dua,), epilogue=_ep_add,
              behind=sync("a_qkv_dx", dua, G))
    dx, _, G["a_norm_g"] = _rmsnorm_bwd("a_norm_bwd", x, W["a_norm_g"], dua, dh1)
    return loss, dx, G


def _pad_lanes(a):
    return jnp.pad(a, ((0, 0), (0, LANES - a.shape[1])))


def kernel(x, a_norm_g, a_w_in, a_b_f, a_q_g, a_k_g, a_w_out, mlp_norm_g, mlp_w1, mlp_w2, kv_norm_g, kv_w, kv_k_g, b_norm_g, b_w_q, b_q_g, b_rel, b_w_out, loss_target, m_a_norm_g, m_a_w_in, m_a_b_f, m_a_q_g, m_a_k_g, m_a_w_out, m_mlp_norm_g, m_mlp_w1, m_mlp_w2, m_kv_norm_g, m_kv_w, m_kv_k_g, m_b_norm_g, m_b_w_q, m_b_q_g, m_b_rel, m_b_w_out, v_a_norm_g, v_a_w_in, v_a_b_f, v_a_q_g, v_a_k_g, v_a_w_out, v_mlp_norm_g, v_mlp_w1, v_mlp_w2, v_kv_norm_g, v_kv_w, v_kv_k_g, v_b_norm_g, v_b_w_q, v_b_q_g, v_b_rel, v_b_w_out):
    T, D = x.shape[1], x.shape[2]
    H = D // HEAD_DIM
    F = mlp_w2.shape[1] * N_DEV
    me = 4 * lax.axis_index("x") + 2 * lax.axis_index("y") + lax.axis_index("c")
    core = lax.axis_index("c").astype(jnp.int32).reshape(1)
    n_in = a_w_in.shape[2]
    n_rel = b_rel.shape[2]

    small_in = jnp.concatenate([a_norm_g.reshape(1, -1), b_rel.reshape(1, -1)], axis=1)
    local = [jnp.swapaxes(a_w_in[0], 0, 1), a_w_out[0], mlp_w1[0], mlp_w2[0], kv_w, b_w_q[0], b_w_out[0],
             mlp_w1[1], mlp_w2[1]]
    def landing(w):
        return lax.dynamic_update_slice(lax.empty((N_DEV,) + w.shape, w.dtype), w[None], (me,) + (0,) * w.ndim)

    send_sems, recv_sems, lands, started = _gather_start(
        "gather_start_first", [landing(small_in), landing(local[0].astype(BF16))])
    more = _gather_start("gather_start_rest", [landing((w + started[0, 0]).astype(BF16)) for w in local[1:]])
    send_sems, recv_sems, lands = send_sems + more[0], recv_sems + more[1], lands + more[2]
    groups = {"first": [0, 1], "a_out": [2], "middle": [3, 4], "kv_q": [5, 6], "late": [7, 8, 9]}

    def arrive(group, after):
        idx = groups[group]
        got = _gather_wait(f"gather_wait_{group}", [lands[i] for i in idx], [send_sems[i] for i in idx],
                           [recv_sems[i] for i in idx], after)
        if group == "first":
            g_small, w_in_t = got[0], got[1].reshape(N_DEV * n_in, D)
            rel = jnp.transpose(g_small[:, 0, D // N_DEV:].reshape(N_DEV, H, n_rel), (1, 0, 2))
            return {"a_norm_g": g_small[:, 0, :D // N_DEV].reshape(1, D), "b_rel": rel.reshape(H, N_DEV * n_rel),
                    "a_w_in_t": w_in_t, "a_w_f_t": jnp.pad(w_in_t[3 * D:], ((0, LANES - H), (0, 0)))}
        if group == "a_out":
            return {"a_w_out": got[0].reshape(D, D)}
        if group == "middle":
            return {"mlp_w1_0": got[0], "mlp_w2_0": got[1].reshape(F, D)}
        if group == "kv_q":
            return {"kv_w": got[0], "b_w_q": got[1].reshape(D, D)}
        return {"b_w_out": got[0].reshape(D, D), "mlp_w1_1": got[1], "mlp_w2_1": got[2].reshape(F, D)}

    W = {
        "a_b_f": _pad_lanes(a_b_f), "a_q_g": a_q_g, "a_k_g": a_k_g,
        "mlp_norm_g0": mlp_norm_g[0:1], "mlp_norm_g1": mlp_norm_g[1:2],
        "kv_norm_g": kv_norm_g.reshape(1, D), "kv_k_g": kv_k_g.reshape(1, HEAD_DIM),
        "b_norm_g": b_norm_g, "b_q_g": b_q_g, **arrive("first", more[3]),
    }

    chip = (2 * lax.axis_index("x") + lax.axis_index("y")).astype(jnp.int32).reshape(1)
    flat = lambda a: a.reshape(-1, a.shape[-1])
    state = {"a_w_in": [jnp.swapaxes(a[0], 0, 1) for a in (a_w_in, m_a_w_in, v_a_w_in)]}
    for nm, w, m, v in [("a_w_out", a_w_out, m_a_w_out, v_a_w_out), ("kv_w", kv_w, m_kv_w, v_kv_w),
                        ("b_w_q", b_w_q, m_b_w_q, v_b_w_q), ("b_w_out", b_w_out, m_b_w_out, v_b_w_out),
                        ("mlp_w1", mlp_w1, m_mlp_w1, v_mlp_w1), ("mlp_w2", mlp_w2, m_mlp_w2, v_mlp_w2)]:
        state[nm] = [flat(w), flat(m), flat(v)]
    upd, flying = {}, {}

    def chunks(nm, G):
        if nm == "a_w_in":
            return jnp.concatenate([G["a_w_qkv_t"], G["a_w_f_t"][:H]], axis=0).reshape(N_DEV, n_in, D)
        g = G[nm]
        return g if g.ndim == 3 else g.reshape(N_DEV, g.shape[0] // N_DEV, g.shape[1])

    def to_core(tag, names, G):
        gs = [chunks(nm, G) for nm in names]
        lands = [lax.empty((N_CHIP,) + g.shape[1:], g.dtype) for g in gs]
        send, recv, gs, lands, token = _split_start(f"reduce_cores_start_{tag}", gs, lands, _to_other_core)
        flying[tag] = (names, send, recv, gs, lands)
        return (token,)

    def to_chips(tag, after):
        names, send, recv, gs, lands = flying[tag]
        gs, lands = _split_wait(f"reduce_cores_wait_{tag}", gs, lands, send, recv, _to_other_core, after)
        ps = [_pair_sum(f"pair_sum_{nm}", g, r, core) for nm, g, r in zip(names, gs, lands)]
        lands = [lax.empty(p.shape, p.dtype) for p in ps]
        send, recv, ps, lands, token = _split_start(f"reduce_chips_start_{tag}", ps, lands, _to_other_chips)
        flying[tag] = (names, send, recv, ps, lands)
        return (token,)

    def finish(tag, after):
        names, send, recv, ps, lands = flying.pop(tag)
        ps, lands = _split_wait(f"reduce_chips_wait_{tag}", ps, lands, send, recv, _from_other_chips, after)
        for nm, own, land in zip(names, ps, lands):
            if nm.startswith("mlp_w"):
                base, layer = nm[:-2], int(nm[-1])
                w, m, v = state[base]
                upd[base] = _adamw_reduced(f"adamw_{nm}", w, m, v, own, land, chip, row0=layer * (w.shape[0] // 2),
                                           into=upd.get(base))
            else:
                upd[nm] = _adamw_reduced(f"adamw_{nm}", *state[nm], own, land, chip)

    def sync(tag, after, G):
        if tag == "mlp1":
            return to_core("r1", ["mlp_w2_1", "mlp_w1_1"], G)
        if tag == "b_out_dx":
            return to_chips("r1", after)
        if tag == "kv_dw":
            return to_core("r2", ["b_w_out", "b_w_q", "kv_w"], G)
        if tag == "dh2":
            return to_chips("r2", after)
        if tag == "a_out_dw":
            return to_core("r3", ["mlp_w2_0", "mlp_w1_0", "a_w_out"], G)
        if tag == "a_out_dx":
            return to_chips("r3", after)
        if tag == "fox_bwd_dq":
            finish("r1", after)
            finish("r2", after)
        if tag == "a_in_dw":
            return to_core("r4", ["a_w_in"], G)
        if tag == "a_qkv_dx":
            return to_chips("r4", after)
        return ()

    loss_tile, dx, G = _local_step(x[0], loss_target[0], W, min(T, 512), arrive, sync)
    finish("r3", dx)

    def rows8(vecs):
        r = lax.broadcasted_iota(jnp.int32, (8, LANES), 0)
        tile = jnp.zeros((8, LANES), F32)
        for i, vec in enumerate(vecs):
            tile = jnp.where(r == i, vec, tile)
        return tile

    rows = lambda a: a.reshape(-1, LANES)
    packed = [rows(G["a_norm_g"]), rows(G["mlp_norm_g0"]), rows(G["mlp_norm_g1"]), rows(G["kv_norm_g"]),
              rows(G["b_norm_g"]), rows(G["b_rel"]),
              rows8([G["a_b_f"], G["a_q_g"], G["a_k_g"], G["kv_k_g"], G["b_q_g"], loss_tile[0:1]])]
    sizes = [p.shape[0] for p in packed]
    (all_small,) = _all_gather("gather_small_grads", [jnp.concatenate(packed, axis=0)])
    total = _sum_parts("sum_small_grads", all_small, F32)
    offs = np.concatenate([[0], np.cumsum(sizes)])
    s = [total[offs[i]:offs[i + 1]] for i in range(len(sizes))]
    loss = s[6][5, 0]
    g_a_norm = lax.dynamic_slice(s[0].reshape(1, D), (0, me * (D // N_DEV)), (1, D // N_DEV))
    g_rel = lax.dynamic_slice(s[5].reshape(H, N_REL_PAD), (0, me * n_rel), (H, n_rel))
    small = {
        "a_norm_g": (a_norm_g, m_a_norm_g, v_a_norm_g, g_a_norm),
        "a_b_f": (a_b_f, m_a_b_f, v_a_b_f, s[6][0:1, :H]),
        "a_q_g": (a_q_g, m_a_q_g, v_a_q_g, s[6][1:2]),
        "a_k_g": (a_k_g, m_a_k_g, v_a_k_g, s[6][2:3]),
        "mlp_norm_g": (mlp_norm_g, m_mlp_norm_g, v_mlp_norm_g,
                       jnp.concatenate([s[1].reshape(1, D), s[2].reshape(1, D)], axis=0)),
        "kv_norm_g": (kv_norm_g.reshape(1, D), m_kv_norm_g.reshape(1, D), v_kv_norm_g.reshape(1, D), s[3].reshape(1, D)),
        "kv_k_g": (kv_k_g.reshape(1, HEAD_DIM), m_kv_k_g.reshape(1, HEAD_DIM), v_kv_k_g.reshape(1, HEAD_DIM),
                   s[6][3:4]),
        "b_norm_g": (b_norm_g, m_b_norm_g, v_b_norm_g, s[4].reshape(1, D)),
        "b_q_g": (b_q_g, m_b_q_g, v_b_q_g, s[6][4:5]),
        "b_rel": (b_rel[0], m_b_rel[0], v_b_rel[0], g_rel),
    }
    for nm, (w, m, v, g) in small.items():
        upd[nm] = _adamw(f"adamw_{nm}", w, m, v, g)
    upd["kv_norm_g"] = [a.reshape(D) for a in upd["kv_norm_g"]]
    upd["kv_k_g"] = [a.reshape(HEAD_DIM) for a in upd["kv_k_g"]]
    upd["b_rel"] = [a[None] for a in upd["b_rel"]]
    finish("r4", total)
    upd["a_w_in"] = [jnp.swapaxes(o, 0, 1)[None] for o in upd["a_w_in"]]
    for nm, w in [("a_w_out", a_w_out), ("kv_w", kv_w), ("b_w_q", b_w_q), ("b_w_out", b_w_out), ("mlp_w1", mlp_w1),
                  ("mlp_w2", mlp_w2)]:
        upd[nm] = [o.reshape(w.shape) for o in upd[nm]]

    order = ["a_norm_g", "a_w_in", "a_b_f", "a_q_g", "a_k_g", "a_w_out", "mlp_norm_g", "mlp_w1", "mlp_w2",
             "kv_norm_g", "kv_w", "kv_k_g", "b_norm_g", "b_w_q", "b_q_g", "b_rel", "b_w_out"]
    return (loss, dx[None], *[upd[n][0] for n in order], *[upd[n][1] for n in order],
            *[upd[n][2] for n in order], *[upd[n][3] for n in order])
```

```python
import functools

import numpy as np
import jax
import jax.numpy as jnp
from jax import lax
from jax.experimental import pallas as pl
from jax.experimental.pallas import tpu as pltpu

F32 = jnp.float32
BF16 = jnp.bfloat16
HIGHEST = lax.Precision.HIGHEST
MESH = pl.DeviceIdType.MESH
ANY = pl.BlockSpec(memory_space=pl.ANY)

N_DEV = 8
N_CHIP = 4
LANES = 128
HEAD_DIM = 128
CHUNK = 64
N_PREV_CHUNKS = 8
REL_CLIP = 256
N_REL = REL_CLIP + CHUNK
N_REL_PAD = 384
WIN = N_PREV_CHUNKS * CHUNK
ROLL_W = 4 * WIN
EPS = 1e-6
NEG = -1e30
SCALE = HEAD_DIM ** -0.5
ADAM_LR, ADAM_B1, ADAM_B2, ADAM_EPS, ADAM_WD, ADAM_STEP = 0.001, 0.9, 0.999, 1e-08, 0.01, 10
V7X_VMEM_BYTES = 64 * 1024 * 1024
VMEM_LIMIT = V7X_VMEM_BYTES * 3 // 4

_NN = (((1,), (0,)), ((), ()))
_NT = (((1,), (1,)), ((), ()))
_TN = (((0,), (0,)), ((), ()))


def _params(*sem):
    return pltpu.CompilerParams(dimension_semantics=sem or None, vmem_limit_bytes=VMEM_LIMIT)


def _tile(n, pref):
    t = n
    while t > pref and t % 2 == 0:
        t //= 2
    return t


def _ep_store(acc, ex, outs):
    outs[0][...] = acc.astype(outs[0].dtype)


def _ep_add(acc, ex, outs):
    outs[0][...] = (acc + ex[0][...]).astype(outs[0].dtype)


def _ep_relu_square(acc, ex, outs):
    a = jnp.maximum(acc, 0.0)
    outs[0][...] = a.astype(outs[0].dtype)
    outs[1][...] = (a * a).astype(outs[1].dtype)


def _ep_times_2a(acc, ex, outs):
    outs[0][...] = (acc * (2.0 * ex[0][...].astype(F32))).astype(outs[0].dtype)


def _mm(name, kind, a, b, M, N, K, *, b_groups=0, out_groups=0, out_dtypes=(F32,), extras=(),
        epilogue=_ep_store, tm=1024, tn=1024, tk=2048, behind=()):
    tm, tn, tk = _tile(M, tm), _tile(N, tn), _tile(K, tk)
    if b_groups:
        cg = b.shape[-1]
        if kind == "nn":
            tn = min(tn, cg)
        else:
            tk = min(tk, cg)
    if out_groups:
        tn = min(tn, N // out_groups)
    grid = (M // tm, N // tn, K // tk)
    nk = grid[2]
    if kind == "nn":
        dims = _NN
        a_spec = pl.BlockSpec((tm, tk), lambda i, j, k: (i, k))
        if b_groups:
            npg = cg // tn
            b_spec = pl.BlockSpec((None, tk, tn), lambda i, j, k: (j // npg, k, j % npg))
        else:
            b_spec = pl.BlockSpec((tk, tn), lambda i, j, k: (k, j))
    elif kind == "nt":
        dims = _NT
        a_spec = pl.BlockSpec((tm, tk), lambda i, j, k: (i, k))
        if b_groups:
            npg = cg // tk
            b_spec = pl.BlockSpec((None, tn, tk), lambda i, j, k: (k // npg, j, k % npg))
        else:
            b_spec = pl.BlockSpec((tn, tk), lambda i, j, k: (j, k))
    else:
        dims = _TN
        a_spec = pl.BlockSpec((tk, tm), lambda i, j, k: (k, i))
        b_spec = pl.BlockSpec((tk, tn), lambda i, j, k: (k, j))
    tile_spec = pl.BlockSpec((tm, tn), lambda i, j, k: (i, j))
    if out_groups:
        ng = (N // out_groups) // tn
        out_spec = pl.BlockSpec((None, tm, tn), lambda i, j, k: (j // ng, i, j % ng))
        out_shape = [jax.ShapeDtypeStruct((out_groups, M, N // out_groups), d) for d in out_dtypes]
    else:
        out_spec = tile_spec
        out_shape = [jax.ShapeDtypeStruct((M, N), d) for d in out_dtypes]
    n_ex, n_out = len(extras), len(out_dtypes)

    def body(a_ref, b_ref, *rest):
        ex, outs, acc = rest[:n_ex], rest[-1 - n_out:-1], rest[-1]
        k = pl.program_id(2)

        @pl.when(k == 0)
        def _():
            acc[...] = jnp.zeros_like(acc)

        acc[...] += lax.dot_general(a_ref[...].astype(BF16), b_ref[...].astype(BF16), dims,
                                    preferred_element_type=F32)

        @pl.when(k == nk - 1)
        def _():
            epilogue(acc[...], ex, outs)

    res = pl.pallas_call(
        body, name=name, grid=grid,
        in_specs=[a_spec, b_spec] + [tile_spec] * n_ex + [ANY] * len(behind),
        out_specs=[out_spec] * n_out, out_shape=out_shape,
        scratch_shapes=[pltpu.VMEM((tm, tn), F32)],
        compiler_params=_params("parallel", "parallel", "arbitrary"),
    )(a, b, *extras, *behind)
    return res[0] if n_out == 1 else res


def _rmsnorm_fwd(name, x, gains):
    T, D = x.shape
    bt = _tile(T, 256)
    n = len(gains)

    def body(x_ref, *rest):
        xv = x_ref[...]
        y = xv * lax.rsqrt(jnp.mean(xv * xv, axis=-1, keepdims=True) + EPS)
        for g_ref, o_ref in zip(rest[:n], rest[n:]):
            o_ref[...] = (y * g_ref[...]).astype(BF16)

    row = pl.BlockSpec((bt, D), lambda i: (i, 0))
    gsp = pl.BlockSpec((1, D), lambda i: (0, 0))
    return pl.pallas_call(
        body, name=name, grid=(T // bt,), in_specs=[row] + [gsp] * n, out_specs=[row] * n,
        out_shape=[jax.ShapeDtypeStruct((T, D), BF16)] * n, compiler_params=_params("parallel"),
    )(x, *gains)


def _rmsnorm_bwd(name, x, g, dn, res):
    T, D = x.shape
    bt = _tile(T, 256)

    def body(x_ref, g_ref, dn_ref, res_ref, dx_ref, dx16_ref, dg_ref):
        @pl.when(pl.program_id(0) == 0)
        def _():
            dg_ref[...] = jnp.zeros_like(dg_ref)

        xv, dnv = x_ref[...], dn_ref[...]
        r = lax.rsqrt(jnp.mean(xv * xv, axis=-1, keepdims=True) + EPS)
        xhat = xv * r
        dyg = dnv * g_ref[...]
        dx = res_ref[...] + r * (dyg - xhat * jnp.mean(dyg * xhat, axis=-1, keepdims=True))
        dx_ref[...] = dx
        dx16_ref[...] = dx.astype(BF16)
        dg_ref[...] += jnp.sum(dnv * xhat, axis=0, keepdims=True)

    row = pl.BlockSpec((bt, D), lambda i: (i, 0))
    gsp = pl.BlockSpec((1, D), lambda i: (0, 0))
    return pl.pallas_call(
        body, name=name, grid=(T // bt,), in_specs=[row, gsp, row, row], out_specs=[row, row, gsp],
        out_shape=[jax.ShapeDtypeStruct((T, D), F32), jax.ShapeDtypeStruct((T, D), BF16),
                   jax.ShapeDtypeStruct((1, D), F32)],
        compiler_params=_params("arbitrary"),
    )(x, g, dn, res)


def _head_post(name, x, D, items):
    T = x.shape[0]
    H = D // HEAD_DIM
    bt = _tile(T, 256)
    gains = [g for _, g in items if g is not None]
    n_it, n_g = len(items), len(gains)

    def body(*refs):
        x_refs, g_refs, o_refs = refs[:n_it], refs[n_it:n_it + n_g], refs[n_it + n_g:]
        gi = 0
        for it, (_, g) in enumerate(items):
            if g is None:
                o_refs[it][...] = x_refs[it][...].astype(BF16)
                continue
            gv = g_refs[gi][...]
            gi += 1
            for h in range(H):
                sl = slice(h * HEAD_DIM, (h + 1) * HEAD_DIM)
                xs = x_refs[it][:, sl]
                r = lax.rsqrt(jnp.mean(xs * xs, axis=-1, keepdims=True) + EPS)
                o_refs[it][:, sl] = (xs * r * gv).astype(BF16)

    in_specs = [pl.BlockSpec((bt, D), functools.partial(lambda cb, i: (i, cb), cb)) for cb, _ in items]
    in_specs += [pl.BlockSpec((1, HEAD_DIM), lambda i: (0, 0))] * n_g
    row = pl.BlockSpec((bt, D), lambda i: (i, 0))
    res = pl.pallas_call(
        body, name=name, grid=(T // bt,), in_specs=in_specs, out_specs=[row] * n_it,
        out_shape=[jax.ShapeDtypeStruct((T, D), BF16)] * n_it, compiler_params=_params("parallel"),
    )(*([x] * n_it), *gains)
    return res


def _head_post_bwd(name, x, D, items):
    T = x.shape[0]
    H = D // HEAD_DIM
    bt = _tile(T, 128)
    gains = [g for _, g, _ in items if g is not None]
    n_it, n_g = len(items), len(gains)

    def body(*refs):
        x_refs, dy_refs = refs[:n_it], refs[n_it:2 * n_it]
        g_refs = refs[2 * n_it:2 * n_it + n_g]
        dx_ref = refs[2 * n_it + n_g]
        dg_refs = refs[2 * n_it + n_g + 1:]

        @pl.when(pl.program_id(0) == 0)
        def _():
            for r in dg_refs:
                r[...] = jnp.zeros_like(r)

        gi = 0
        for it, (_, g, _) in enumerate(items):
            if g is None:
                dx_ref[:, it * D:(it + 1) * D] = dy_refs[it][...].astype(BF16)
                continue
            gv = g_refs[gi][...]
            dg = jnp.zeros((1, HEAD_DIM), F32)
            for h in range(H):
                sl = slice(h * HEAD_DIM, (h + 1) * HEAD_DIM)
                xs, dy = x_refs[it][:, sl], dy_refs[it][:, sl]
                r = lax.rsqrt(jnp.mean(xs * xs, axis=-1, keepdims=True) + EPS)
                xhat = xs * r
                dyg = dy * gv
                dx = r * (dyg - xhat * jnp.mean(dyg * xhat, axis=-1, keepdims=True))
                dx_ref[:, it * D + h * HEAD_DIM:it * D + (h + 1) * HEAD_DIM] = dx.astype(BF16)
                dg = dg + jnp.sum(dy * xhat, axis=0, keepdims=True)
            dg_refs[gi][...] += dg
            gi += 1

    in_specs = [pl.BlockSpec((bt, D), functools.partial(lambda cb, i: (i, cb), cb)) for cb, _, _ in items]
    in_specs += [pl.BlockSpec((bt, D), lambda i: (i, 0))] * n_it
    gsp = pl.BlockSpec((1, HEAD_DIM), lambda i: (0, 0))
    in_specs += [gsp] * n_g
    res = pl.pallas_call(
        body, name=name, grid=(T // bt,), in_specs=in_specs,
        out_specs=[pl.BlockSpec((bt, n_it * D), lambda i: (i, 0))] + [gsp] * n_g,
        out_shape=[jax.ShapeDtypeStruct((T, n_it * D), BF16)] + [jax.ShapeDtypeStruct((1, HEAD_DIM), F32)] * n_g,
        compiler_params=_params("arbitrary"),
    )(*([x] * n_it), *[dy for _, _, dy in items], *gains)
    return res[0], list(res[1:])


def _gates_fwd(fz, bf, H):
    T = fz.shape[0]
    bt = _tile(T, 512)

    def body(fz_ref, bf_ref, c_ref, carry):
        @pl.when(pl.program_id(0) == 0)
        def _():
            carry[...] = jnp.zeros_like(carry)

        z = fz_ref[...] + bf_ref[...]
        logf = jnp.minimum(z, 0.0) - jnp.log(1.0 + jnp.exp(-jnp.abs(z)))
        lane = lax.broadcasted_iota(jnp.int32, (bt, LANES), 1)
        logf = jnp.where(lane < H, logf, 0.0)
        tri = (lax.broadcasted_iota(jnp.int32, (bt, bt), 0) >= lax.broadcasted_iota(jnp.int32, (bt, bt), 1))
        c = jnp.dot(tri.astype(F32), logf, precision=HIGHEST, preferred_element_type=F32) + carry[...]
        c_ref[...] = c
        carry[...] = c[bt - 1:bt, :]

    row = pl.BlockSpec((bt, LANES), lambda i: (i, 0))
    return pl.pallas_call(
        body, name="fox_gates_fwd", grid=(T // bt,),
        in_specs=[row, pl.BlockSpec((1, LANES), lambda i: (0, 0))], out_specs=row,
        out_shape=jax.ShapeDtypeStruct((T, LANES), F32), scratch_shapes=[pltpu.VMEM((1, LANES), F32)],
        compiler_params=_params("arbitrary"),
    )(fz, bf)


def _gates_bwd(dcq, dck, fz, bf, H):
    T = fz.shape[0]
    bt = _tile(T, 512)
    nb = T // bt

    def body(dcq_ref, dck_ref, fz_ref, bf_ref, dfz_ref, dbf_ref, carry):
        @pl.when(pl.program_id(0) == 0)
        def _():
            carry[...] = jnp.zeros_like(carry)
            dbf_ref[...] = jnp.zeros_like(dbf_ref)

        tri = (lax.broadcasted_iota(jnp.int32, (bt, bt), 0) <= lax.broadcasted_iota(jnp.int32, (bt, bt), 1))
        dc = dcq_ref[...] + dck_ref[...]
        dlogf = jnp.dot(tri.astype(F32), dc, precision=HIGHEST, preferred_element_type=F32) + carry[...]
        carry[...] = dlogf[0:1, :]
        z = fz_ref[...] + bf_ref[...]
        lane = lax.broadcasted_iota(jnp.int32, (bt, LANES), 1)
        dfz = jnp.where(lane < H, dlogf / (1.0 + jnp.exp(z)), 0.0)
        dfz_ref[...] = dfz
        dbf_ref[...] += jnp.sum(dfz, axis=0, keepdims=True)

    row = pl.BlockSpec((bt, LANES), lambda i: (nb - 1 - i, 0))
    one = pl.BlockSpec((1, LANES), lambda i: (0, 0))
    return pl.pallas_call(
        body, name="fox_gates_bwd", grid=(nb,), in_specs=[row, row, row, one], out_specs=[row, one],
        out_shape=[jax.ShapeDtypeStruct((T, LANES), F32), jax.ShapeDtypeStruct((1, LANES), F32)],
        scratch_shapes=[pltpu.VMEM((1, LANES), F32)], compiler_params=_params("arbitrary"),
    )(dcq, dck, fz, bf)


def _lane_pick(ref_value, h):
    lane = lax.broadcasted_iota(jnp.int32, (1, LANES), 1)
    return jnp.sum(jnp.where(lane == h, ref_value, 0.0), axis=1, keepdims=True)


def _lane_put(old, h, col):
    lane = lax.broadcasted_iota(jnp.int32, (1, LANES), 1)
    return jnp.where(lane == h, col, old)


FOX_HEADS = 4


def _causal(rows, cols, row0=0, transposed=False):
    row = row0 + lax.broadcasted_iota(jnp.int32, (rows, cols), 0)
    col = lax.broadcasted_iota(jnp.int32, (rows, cols), 1)
    return col >= row if transposed else row >= col


def _fox_fwd(q, k, v, ccol, crow, blk):
    T, D = q.shape
    H = D // HEAD_DIM
    nb = T // blk

    def body(q_ref, k_ref, v_ref, ccol_ref, crow_ref, o_ref, lse_ref, m_s, l_s, acc_s):
        qi, hg, kj = pl.program_id(0), pl.program_id(1), pl.program_id(2)

        @pl.when(kj == 0)
        def _():
            m_s[...] = jnp.full_like(m_s, NEG)
            l_s[...] = jnp.zeros_like(l_s)
            acc_s[...] = jnp.zeros_like(acc_s)

        @pl.when((kj == 0) & (hg == 0))
        def _():
            lse_ref[...] = jnp.zeros_like(lse_ref)

        def step(diagonal):
            heads = range(FOX_HEADS)
            sls = [slice(hh * HEAD_DIM, (hh + 1) * HEAD_DIM) for hh in heads]
            m_old = [m_s[hh] for hh in heads]
            l_old = [l_s[hh] for hh in heads]
            acc_old = [acc_s[:, sl] for sl in sls]
            s = []
            for hh in heads:
                h = hg * FOX_HEADS + hh
                bias = _lane_pick(ccol_ref[...], h) - crow_ref[pl.ds(h, 1), :]
                sc = lax.dot_general(q_ref[:, sls[hh]], k_ref[:, sls[hh]], _NT, preferred_element_type=F32) * SCALE + bias
                s.append(jnp.where(_causal(blk, blk), sc, NEG) if diagonal else sc)
            m_new = [jnp.maximum(m_old[hh], jnp.max(s[hh], axis=1, keepdims=True)) for hh in heads]
            p = [jnp.exp(s[hh] - m_new[hh]) for hh in heads]
            alpha = [jnp.exp(m_old[hh] - m_new[hh]) for hh in heads]
            pv = [jnp.dot(p[hh].astype(BF16), v_ref[:, sls[hh]], preferred_element_type=F32) for hh in heads]
            for hh in heads:
                l_s[hh] = alpha[hh] * l_old[hh] + jnp.sum(p[hh], axis=1, keepdims=True)
                acc_s[:, sls[hh]] = alpha[hh] * acc_old[hh] + pv[hh]
                m_s[hh] = m_new[hh]

        @pl.when(kj < qi)
        def _():
            step(False)

        @pl.when(kj == qi)
        def _():
            step(True)
            for hh in range(FOX_HEADS):
                sl = slice(hh * HEAD_DIM, (hh + 1) * HEAD_DIM)
                o_ref[:, sl] = (acc_s[:, sl] / l_s[hh]).astype(BF16)
                lse_ref[...] = _lane_put(lse_ref[...], hg * FOX_HEADS + hh, m_s[hh] + jnp.log(l_s[hh]))

    wide = FOX_HEADS * HEAD_DIM
    qsp = pl.BlockSpec((blk, wide), lambda qi, h, kj: (qi, h))
    ksp = pl.BlockSpec((blk, wide), lambda qi, h, kj: (jnp.minimum(kj, qi), h))
    stat = pl.BlockSpec((blk, LANES), lambda qi, h, kj: (qi, 0))
    return pl.pallas_call(
        body, name="fox_fwd", grid=(nb, H // FOX_HEADS, nb),
        in_specs=[qsp, ksp, ksp, stat, pl.BlockSpec((H, blk), lambda qi, h, kj: (0, jnp.minimum(kj, qi)))],
        out_specs=[qsp, stat],
        out_shape=[jax.ShapeDtypeStruct((T, D), BF16), jax.ShapeDtypeStruct((T, LANES), F32)],
        scratch_shapes=[pltpu.VMEM((FOX_HEADS, blk, 1), F32), pltpu.VMEM((FOX_HEADS, blk, 1), F32),
                        pltpu.VMEM((blk, wide), F32)],
        compiler_params=_params("arbitrary", "arbitrary", "arbitrary"),
    )(q, k, v, ccol, crow)


def _head_stat(name, a, b, behind=()):
    T, D = a.shape
    H = D // HEAD_DIM
    bt = _tile(T, 256)

    def body(a_ref, b_ref, *rest):
        o_ref = rest[-1]
        out = jnp.zeros((bt, LANES), F32)
        for h in range(H):
            sl = slice(h * HEAD_DIM, (h + 1) * HEAD_DIM)
            prod = a_ref[:, sl].astype(F32) * b_ref[:, sl].astype(F32)
            out = _lane_put(out, h, jnp.sum(prod, axis=1, keepdims=True))
        o_ref[...] = out

    row = pl.BlockSpec((bt, D), lambda i: (i, 0))
    return pl.pallas_call(
        body, name=name, grid=(T // bt,), in_specs=[row, row] + [ANY] * len(behind),
        out_specs=pl.BlockSpec((bt, LANES), lambda i: (i, 0)),
        out_shape=jax.ShapeDtypeStruct((T, LANES), F32), compiler_params=_params("parallel"),
    )(a, b, *behind)


def _fox_bwd_dq(q, k, v, do, ccol, crow, lse, delta, blk):
    T, D = q.shape
    H = D // HEAD_DIM
    nb = T // blk

    def body(q_ref, k_ref, v_ref, do_ref, ccol_ref, crow_ref, lse_ref, delta_ref, dq_ref, dc_ref, acc_s, dc_s):
        qi, hg, kj = pl.program_id(0), pl.program_id(1), pl.program_id(2)

        @pl.when(kj == 0)
        def _():
            acc_s[...] = jnp.zeros_like(acc_s)
            dc_s[...] = jnp.zeros_like(dc_s)

        @pl.when((kj == 0) & (hg == 0))
        def _():
            dc_ref[...] = jnp.zeros_like(dc_ref)

        def step(diagonal):
            heads = range(FOX_HEADS)
            sls = [slice(hh * HEAD_DIM, (hh + 1) * HEAD_DIM) for hh in heads]
            p, dp = [], []
            for hh in heads:
                h = hg * FOX_HEADS + hh
                bias = _lane_pick(ccol_ref[...], h) - crow_ref[pl.ds(h, 1), :]
                sc = lax.dot_general(q_ref[:, sls[hh]], k_ref[:, sls[hh]], _NT, preferred_element_type=F32) * SCALE + bias
                if diagonal:
                    sc = jnp.where(_causal(blk, blk), sc, NEG)
                p.append(jnp.exp(sc - _lane_pick(lse_ref[...], h)))
                dp.append(lax.dot_general(do_ref[:, sls[hh]], v_ref[:, sls[hh]], _NT, preferred_element_type=F32)
                          - _lane_pick(delta_ref[...], h))
            ds = [p[hh] * dp[hh] for hh in heads]
            for hh in heads:
                acc_s[:, sls[hh]] += jnp.dot(ds[hh].astype(BF16), k_ref[:, sls[hh]], preferred_element_type=F32)
                dc_s[hh] += jnp.sum(ds[hh], axis=1, keepdims=True)

        @pl.when(kj < qi)
        def _():
            step(False)

        @pl.when(kj == qi)
        def _():
            step(True)
            dq_ref[...] = acc_s[...] * SCALE
            for hh in range(FOX_HEADS):
                dc_ref[...] = _lane_put(dc_ref[...], hg * FOX_HEADS + hh, dc_s[hh])

    wide = FOX_HEADS * HEAD_DIM
    qsp = pl.BlockSpec((blk, wide), lambda qi, h, kj: (qi, h))
    ksp = pl.BlockSpec((blk, wide), lambda qi, h, kj: (jnp.minimum(kj, qi), h))
    stat = pl.BlockSpec((blk, LANES), lambda qi, h, kj: (qi, 0))
    return pl.pallas_call(
        body, name="fox_bwd_dq", grid=(nb, H // FOX_HEADS, nb),
        in_specs=[qsp, ksp, ksp, qsp, stat, pl.BlockSpec((H, blk), lambda qi, h, kj: (0, jnp.minimum(kj, qi))),
                  stat, stat],
        out_specs=[qsp, stat],
        out_shape=[jax.ShapeDtypeStruct((T, D), F32), jax.ShapeDtypeStruct((T, LANES), F32)],
        scratch_shapes=[pltpu.VMEM((blk, wide), F32), pltpu.VMEM((FOX_HEADS, blk, 1), F32)],
        compiler_params=_params("arbitrary", "arbitrary", "arbitrary"),
    )(q, k, v, do, ccol, crow, lse, delta)


def _fox_bwd_dkv(q, k, v, do, ccol, crow, lse_row, delta_row, blk):
    T, D = q.shape
    H = D // HEAD_DIM
    nb = T // blk

    def body(q_ref, k_ref, v_ref, do_ref, ccol_ref, crow_ref, lse_ref, delta_ref, dk_ref, dv_ref, dc_ref,
             dk_s, dv_s, dc_s):
        kj, hg, qi = pl.program_id(0), pl.program_id(1), pl.program_id(2)

        @pl.when(qi == 0)
        def _():
            dk_s[...] = jnp.zeros_like(dk_s)
            dv_s[...] = jnp.zeros_like(dv_s)
            dc_s[...] = jnp.zeros_like(dc_s)

        @pl.when((qi == 0) & (hg == 0))
        def _():
            dc_ref[...] = jnp.zeros_like(dc_ref)

        def step(diagonal):
            heads = range(FOX_HEADS)
            sls = [slice(hh * HEAD_DIM, (hh + 1) * HEAD_DIM) for hh in heads]
            pt, dpt = [], []
            for hh in heads:
                h = hg * FOX_HEADS + hh
                bias = crow_ref[pl.ds(h, 1), :] - _lane_pick(ccol_ref[...], h)
                st = lax.dot_general(k_ref[:, sls[hh]], q_ref[:, sls[hh]], _NT, preferred_element_type=F32) * SCALE + bias
                if diagonal:
                    st = jnp.where(_causal(blk, blk, transposed=True), st, NEG)
                pt.append(jnp.exp(st - lse_ref[pl.ds(h, 1), :]))
                dpt.append(lax.dot_general(v_ref[:, sls[hh]], do_ref[:, sls[hh]], _NT, preferred_element_type=F32)
                           - delta_ref[pl.ds(h, 1), :])
            dst = [pt[hh] * dpt[hh] for hh in heads]
            for hh in heads:
                dv_s[:, sls[hh]] += jnp.dot(pt[hh].astype(BF16), do_ref[:, sls[hh]], preferred_element_type=F32)
                dk_s[:, sls[hh]] += jnp.dot(dst[hh].astype(BF16), q_ref[:, sls[hh]], preferred_element_type=F32)
                dc_s[hh] -= jnp.sum(dst[hh], axis=1, keepdims=True)

        @pl.when(qi > kj)
        def _():
            step(False)

        @pl.when(qi == kj)
        def _():
            step(True)

        @pl.when(qi == nb - 1)
        def _():
            dk_ref[...] = dk_s[...] * SCALE
            dv_ref[...] = dv_s[...]
            for hh in range(FOX_HEADS):
                dc_ref[...] = _lane_put(dc_ref[...], hg * FOX_HEADS + hh, dc_s[hh])

    wide = FOX_HEADS * HEAD_DIM
    ksp = pl.BlockSpec((blk, wide), lambda kj, h, qi: (kj, h))
    qsp = pl.BlockSpec((blk, wide), lambda kj, h, qi: (jnp.maximum(qi, kj), h))
    kstat = pl.BlockSpec((blk, LANES), lambda kj, h, qi: (kj, 0))
    qrow = pl.BlockSpec((H, blk), lambda kj, h, qi: (0, jnp.maximum(qi, kj)))
    return pl.pallas_call(
        body, name="fox_bwd_dkv", grid=(nb, H // FOX_HEADS, nb),
        in_specs=[qsp, ksp, ksp, qsp, kstat, qrow, qrow, qrow],
        out_specs=[ksp, ksp, kstat],
        out_shape=[jax.ShapeDtypeStruct((T, D), F32), jax.ShapeDtypeStruct((T, D), F32),
                   jax.ShapeDtypeStruct((T, LANES), F32)],
        scratch_shapes=[pltpu.VMEM((blk, wide), F32), pltpu.VMEM((blk, wide), F32),
                        pltpu.VMEM((FOX_HEADS, blk, 1), F32)],
        compiler_params=_params("arbitrary", "arbitrary", "arbitrary"),
    )(q, k, v, do, ccol, crow, lse_row, delta_row)


def _rel_index():
    j = np.arange(ROLL_W)
    u = np.where(j < 2 * WIN, j, j - ROLL_W)
    return np.clip(WIN - u, -(CHUNK - 1), REL_CLIP) + (CHUNK - 1)


def _band_bias(g_row, bias_s):
    base = jnp.broadcast_to(g_row, (CHUNK, ROLL_W))
    rowb = lax.broadcasted_iota(jnp.int32, (CHUNK, ROLL_W), 0)
    for bit in range(CHUNK.bit_length() - 1):
        base = jnp.where((rowb >> bit) & 1 == 1, pltpu.roll(base, 1 << bit, 1), base)
    kc = lax.broadcasted_iota(jnp.int32, (CHUNK, 2 * WIN), 1) // CHUNK
    for a in range(N_PREV_CHUNKS):
        part = base if a == 0 else pltpu.roll(base, CHUNK * a, 1)
        valid = (kc >= a) & (kc <= a + N_PREV_CHUNKS)
        bias_s[a * CHUNK:(a + 1) * CHUNK, :] = jnp.where(valid, part[:, :2 * WIN], NEG)


def _chunk_probs(q, kp, kc, bias_s, qi):
    sp = lax.dot_general(q, kp, _NT, preferred_element_type=F32) * SCALE + bias_s[:, :WIN]
    sc = lax.dot_general(q, kc, _NT, preferred_element_type=F32) * SCALE + bias_s[:, WIN:]
    sp = jnp.where(qi > 0, sp, NEG)
    m = jnp.maximum(jnp.max(sp, axis=1, keepdims=True), jnp.max(sc, axis=1, keepdims=True))
    pp, pc = jnp.exp(sp - m), jnp.exp(sc - m)
    inv = 1.0 / (jnp.sum(pp, axis=1, keepdims=True) + jnp.sum(pc, axis=1, keepdims=True))
    return pp * inv, pc * inv


def _chunk_fwd(q, k, v, g_rows):
    T, D = q.shape
    H = D // HEAD_DIM
    nq = T // WIN

    def body(q_ref, kp_ref, kc_ref, vp_ref, vc_ref, g_ref, o_ref, bias_s):
        qi = pl.program_id(1)

        @pl.when(qi == 0)
        def _():
            _band_bias(g_ref[...], bias_s)

        pp, pc = _chunk_probs(q_ref[...], kp_ref[...], kc_ref[...], bias_s, qi)
        o = jnp.dot(pp.astype(BF16), vp_ref[...], preferred_element_type=F32)
        o += jnp.dot(pc.astype(BF16), vc_ref[...], preferred_element_type=F32)
        o_ref[...] = o.astype(BF16)

    cur = pl.BlockSpec((WIN, HEAD_DIM), lambda h, qi: (qi, h))
    prev = pl.BlockSpec((WIN, HEAD_DIM), lambda h, qi: (jnp.maximum(qi - 1, 0), h))
    return pl.pallas_call(
        body, name="chunk_fwd", grid=(H, nq),
        in_specs=[cur, prev, cur, prev, cur, pl.BlockSpec((None, 1, ROLL_W), lambda h, qi: (h, 0, 0))],
        out_specs=cur, out_shape=jax.ShapeDtypeStruct((T, D), BF16),
        scratch_shapes=[pltpu.VMEM((WIN, 2 * WIN), F32)],
        compiler_params=_params("arbitrary", "arbitrary"),
    )(q, k, k, v, v, g_rows)


def _chunk_bwd(q, k, v, g_rows, do, o):
    T, D = q.shape
    H = D // HEAD_DIM
    nq = T // WIN

    def body(q_ref, kp_ref, kc_ref, vp_ref, vc_ref, g_ref, do_ref, o_ref,
             dq_ref, dk_ref, dv_ref, ds_ref, bias_s, ck_s, cv_s):
        qi = pl.program_id(1)

        @pl.when(qi == 0)
        def _():
            _band_bias(g_ref[...], bias_s)
            ds_ref[...] = jnp.zeros_like(ds_ref)
            ck_s[...] = jnp.zeros_like(ck_s)
            cv_s[...] = jnp.zeros_like(cv_s)

        @pl.when(qi < nq)
        def _():
            qv, dov = q_ref[...], do_ref[...]
            pp, pc = _chunk_probs(qv, kp_ref[...], kc_ref[...], bias_s, qi)
            delta = jnp.sum(dov.astype(F32) * o_ref[...].astype(F32), axis=1, keepdims=True)
            dsp = pp * (lax.dot_general(dov, vp_ref[...], _NT, preferred_element_type=F32) - delta)
            dsc = pc * (lax.dot_general(dov, vc_ref[...], _NT, preferred_element_type=F32) - delta)
            dsp16, dsc16 = dsp.astype(BF16), dsc.astype(BF16)
            dq = jnp.dot(dsp16, kp_ref[...], preferred_element_type=F32)
            dq += jnp.dot(dsc16, kc_ref[...], preferred_element_type=F32)
            dq_ref[...] = dq * SCALE
            ds_ref[:, :WIN] += dsp
            ds_ref[:, WIN:] += dsc
            dk_ref[...] = ck_s[...] + lax.dot_general(dsp16, qv, _TN, preferred_element_type=F32) * SCALE
            dv_ref[...] = cv_s[...] + lax.dot_general(pp.astype(BF16), dov, _TN, preferred_element_type=F32)
            ck_s[...] = lax.dot_general(dsc16, qv, _TN, preferred_element_type=F32) * SCALE
            cv_s[...] = lax.dot_general(pc.astype(BF16), dov, _TN, preferred_element_type=F32)

        @pl.when(qi == nq)
        def _():
            dk_ref[...] = ck_s[...]
            dv_ref[...] = cv_s[...]

    cur = pl.BlockSpec((WIN, HEAD_DIM), lambda h, qi: (jnp.minimum(qi, nq - 1), h))
    prev = pl.BlockSpec((WIN, HEAD_DIM), lambda h, qi: (jnp.clip(qi - 1, 0, nq - 2), h))
    late = pl.BlockSpec((WIN, HEAD_DIM), lambda h, qi: (jnp.maximum(qi - 1, 0), h))
    return pl.pallas_call(
        body, name="chunk_bwd", grid=(H, nq + 1),
        in_specs=[cur, prev, cur, prev, cur, pl.BlockSpec((None, 1, ROLL_W), lambda h, qi: (h, 0, 0)), cur, cur],
        out_specs=[cur, late, late, pl.BlockSpec((None, WIN, 2 * WIN), lambda h, qi: (h, 0, 0))],
        out_shape=[jax.ShapeDtypeStruct((T, D), F32)] * 3 + [jax.ShapeDtypeStruct((H, WIN, 2 * WIN), F32)],
        scratch_shapes=[pltpu.VMEM((WIN, 2 * WIN), F32), pltpu.VMEM((WIN, HEAD_DIM), F32),
                        pltpu.VMEM((WIN, HEAD_DIM), F32)],
        compiler_params=_params("arbitrary", "arbitrary"),
    )(q, k, k, v, v, g_rows, do, o)


def _rel_rows(rel, onehot_t):
    H = rel.shape[0]

    def body(rel_ref, oh_ref, out_ref):
        out_ref[...] = jnp.dot(rel_ref[...], oh_ref[...], precision=HIGHEST, preferred_element_type=F32)

    return pl.pallas_call(body, name="rel_rows", out_shape=jax.ShapeDtypeStruct((H, ROLL_W), F32))(rel, onehot_t)


def _rel_grad(ds_sum, onehot):
    H = ds_sum.shape[0]

    def body(ds_ref, oh_ref, out_ref):
        y = jnp.zeros((CHUNK, ROLL_W), F32)
        pad = jnp.zeros((CHUNK, ROLL_W - 2 * WIN), F32)
        for a in range(N_PREV_CHUNKS):
            part = jnp.concatenate([ds_ref[a * CHUNK:(a + 1) * CHUNK, :], pad], axis=1)
            y = y + (part if a == 0 else pltpu.roll(part, ROLL_W - CHUNK * a, 1))
        rowb = lax.broadcasted_iota(jnp.int32, (CHUNK, ROLL_W), 0)
        for bit in range(CHUNK.bit_length() - 1):
            y = jnp.where((rowb >> bit) & 1 == 1, pltpu.roll(y, ROLL_W - (1 << bit), 1), y)
        diag = jnp.broadcast_to(jnp.sum(y, axis=0, keepdims=True), (8, ROLL_W))
        out_ref[...] = jnp.dot(diag, oh_ref[...], precision=HIGHEST, preferred_element_type=F32)

    return pl.pallas_call(
        body, name="rel_grad", grid=(H,),
        in_specs=[pl.BlockSpec((None, WIN, 2 * WIN), lambda h: (h, 0, 0)),
                  pl.BlockSpec((ROLL_W, N_REL_PAD), lambda h: (0, 0))],
        out_specs=pl.BlockSpec((None, 8, N_REL_PAD), lambda h: (h, 0, 0)),
        out_shape=jax.ShapeDtypeStruct((H, 8, N_REL_PAD), F32), compiler_params=_params("parallel"),
    )(ds_sum, onehot)


def _loss_head(y, target):
    T, D = y.shape
    bt = _tile(T, 256)

    def body(y_ref, t_ref, dy_ref, dy16_ref, loss_ref):
        @pl.when(pl.program_id(0) == 0)
        def _():
            loss_ref[...] = jnp.zeros_like(loss_ref)

        err = y_ref[...] - t_ref[...]
        dy = err * (1.0 / D)
        dy_ref[...] = dy
        dy16_ref[...] = dy.astype(BF16)
        loss_ref[...] += 0.5 * jnp.sum(jnp.mean(err * err, axis=-1, keepdims=True))

    row = pl.BlockSpec((bt, D), lambda i: (i, 0))
    one = pl.BlockSpec((8, LANES), lambda i: (0, 0))
    return pl.pallas_call(
        body, name="loss_head", grid=(T // bt,), in_specs=[row, row], out_specs=[row, row, one],
        out_shape=[jax.ShapeDtypeStruct((T, D), F32), jax.ShapeDtypeStruct((T, D), BF16),
                   jax.ShapeDtypeStruct((8, LANES), F32)],
        compiler_params=_params("arbitrary"),
    )(y, target)


def _adam_step(g, w_ref, m_ref, v_ref, g_ref, d_ref, nm_ref, nv_ref):
    nm = ADAM_B1 * m_ref[...] + (1.0 - ADAM_B1) * g
    nv = ADAM_B2 * v_ref[...] + (1.0 - ADAM_B2) * (g * g)
    m_hat = nm / (1.0 - ADAM_B1 ** ADAM_STEP)
    v_hat = nv / (1.0 - ADAM_B2 ** ADAM_STEP)
    g_ref[...] = g
    nm_ref[...] = nm
    nv_ref[...] = nv
    d_ref[...] = -ADAM_LR * (m_hat / (jnp.sqrt(v_hat) + ADAM_EPS) + ADAM_WD * w_ref[...])


def _adamw(name, w, m, v, g):
    def body(w_ref, m_ref, v_ref, gin_ref, g_ref, d_ref, nm_ref, nv_ref):
        _adam_step(gin_ref[...], w_ref, m_ref, v_ref, g_ref, d_ref, nm_ref, nv_ref)

    return pl.pallas_call(body, name=name, out_shape=[jax.ShapeDtypeStruct(w.shape, F32)] * 4)(w, m, v, g)


def _adamw_reduced(name, w, m, v, own, land, chip, row0=0, into=None):
    rows_all, C = w.shape
    _, R, _ = land.shape

    def body(chip_ref, w_ref, m_ref, v_ref, own_ref, land_ref, *rest):
        g = jnp.where(chip_ref[0] == 0, own_ref[...], land_ref[0]).astype(F32)
        for q in range(1, N_CHIP):
            g = g + jnp.where(chip_ref[0] == q, own_ref[...], land_ref[q]).astype(F32)
        _adam_step(g, w_ref, m_ref, v_ref, *rest[-4:])

    if R % 16 == 0:
        br = _tile(R, max(16, (128 * 1024) // C))
        first = row0 // br
        grid = (R // br,)
        wsp = pl.BlockSpec((br, C), lambda i, chip_ref: (first + i, 0))
        osp = pl.BlockSpec((None, br, C), lambda i, chip_ref: (chip_ref[0], i, 0))
        lsp = pl.BlockSpec((N_CHIP, br, C), lambda i, chip_ref: (0, i, 0))
    else:
        bc = _tile(C, 512)
        grid = (C // bc,)
        wsp = pl.BlockSpec((R, bc), lambda i, chip_ref: (0, i))
        osp = pl.BlockSpec((None, R, bc), lambda i, chip_ref: (chip_ref[0], 0, i))
        lsp = pl.BlockSpec((N_CHIP, R, bc), lambda i, chip_ref: (0, 0, i))
    extra = [] if into is None else list(into)
    return pl.pallas_call(
        body, name=name,
        grid_spec=pltpu.PrefetchScalarGridSpec(
            num_scalar_prefetch=1, grid=grid, in_specs=[wsp, wsp, wsp, osp, lsp] + [ANY] * len(extra),
            out_specs=[wsp] * 4),
        out_shape=[jax.ShapeDtypeStruct((rows_all, C), F32)] * 4,
        input_output_aliases={6 + j: j for j in range(len(extra))}, compiler_params=_params("parallel"),
    )(chip, w, m, v, own, land, *extra)


def _sum_parts(name, parts, out_dtype):
    P, R, C = parts.shape
    br = _tile(R, max(8, (256 * 1024) // C)) if R % 8 == 0 else R

    def body(p_ref, o_ref):
        g = p_ref[0].astype(F32)
        for i in range(1, P):
            g = g + p_ref[i].astype(F32)
        o_ref[...] = g.astype(out_dtype)

    return pl.pallas_call(
        body, name=name, grid=(R // br,),
        in_specs=[pl.BlockSpec((P, br, C), lambda i: (0, i, 0))], out_specs=pl.BlockSpec((br, C), lambda i: (i, 0)),
        out_shape=jax.ShapeDtypeStruct((R, C), out_dtype), compiler_params=_params("parallel"),
    )(parts)


def _pair_sum(name, g, recv, core):
    _, R, C = g.shape

    def body(core_ref, g_ref, r_ref, o_ref):
        o_ref[...] = (g_ref[...] + r_ref[...]).astype(BF16)

    if R % 16 == 0:
        br = _tile(R, max(16, (256 * 1024) // C))
        blk, n = (None, br, C), R // br
        mine = lambda q, i, core_ref: (2 * q + core_ref[0], i, 0)
        same = lambda q, i, core_ref: (q, i, 0)
    else:
        bc = _tile(C, 512)
        blk, n = (None, R, bc), C // bc
        mine = lambda q, i, core_ref: (2 * q + core_ref[0], 0, i)
        same = lambda q, i, core_ref: (q, 0, i)
    return pl.pallas_call(
        body, name=name,
        grid_spec=pltpu.PrefetchScalarGridSpec(
            num_scalar_prefetch=1, grid=(N_CHIP, n),
            in_specs=[pl.BlockSpec(blk, mine), pl.BlockSpec(blk, same)], out_specs=pl.BlockSpec(blk, same)),
        out_shape=jax.ShapeDtypeStruct((N_CHIP, R, C), BF16), compiler_params=_params("parallel", "parallel"),
    )(core, g, recv)


def _chips_of(x, y):
    return [(1 - x, y), (x, 1 - y), (1 - x, 1 - y)]


def _position():
    x, y, c = lax.axis_index("x"), lax.axis_index("y"), lax.axis_index("c")
    return x, y, c, _chips_of(x, y)


def _all_gather(name, xs):
    n = len(xs)

    def body(*refs):
        x_refs, out_refs = refs[:n], refs[n:2 * n]
        send_sems, recv_sems, local_sems = refs[2 * n:]
        x, y, c, chips = _position()
        me, sibling = (x, y, c), (x, y, 1 - c)

        def copy(a, k, block, to, src=None):
            px, py, pc = block
            slot = out_refs[a].at[4 * px + 2 * py + pc]
            return pltpu.make_async_remote_copy(
                src_ref=slot if src is None else src, dst_ref=slot,
                send_sem=send_sems.at[a, k], recv_sem=recv_sems.at[a, k], device_id=to, device_id_type=MESH)

        mine = [pltpu.make_async_copy(x_refs[a], out_refs[a].at[4 * x + 2 * y + c], local_sems.at[a]) for a in range(n)]
        for cp in mine:
            cp.start()
        first = []
        for a in range(n):
            first.append(copy(a, 0, me, sibling, src=x_refs[a]))
            first += [copy(a, 1 + j, me, (*chip, c), src=x_refs[a]) for j, chip in enumerate(chips)]
        for cp in first:
            cp.start()
        passed = []
        for j, chip in enumerate(chips):
            for a in range(n):
                copy(a, 1 + j, (*chip, c), me).wait_recv()
                fwd = copy(a, 4 + j, (*chip, c), sibling)
                fwd.start()
                passed.append(fwd)
        for a in range(n):
            copy(a, 0, sibling, me).wait_recv()
            for j, chip in enumerate(chips):
                copy(a, 4 + j, (*chip, 1 - c), me).wait_recv()
        for cp in first + passed:
            cp.wait_send()
        for cp in mine:
            cp.wait()

    return pl.pallas_call(
        body, name=name, in_specs=[ANY] * n, out_specs=[ANY] * n,
        out_shape=[jax.ShapeDtypeStruct((N_DEV,) + x.shape, x.dtype) for x in xs],
        scratch_shapes=[pltpu.SemaphoreType.DMA((n, 7)), pltpu.SemaphoreType.DMA((n, 7)),
                        pltpu.SemaphoreType.DMA((n,))],
    )(*xs)


HBM = pl.BlockSpec(memory_space=pltpu.HBM)
SEM = pl.BlockSpec(memory_space=pltpu.SEMAPHORE)
EFFECT = pltpu.SideEffectType.DATAFLOW_SIDE_EFFECTING


def _own_block_out(x, y, c):
    mine = 4 * x + 2 * y + c
    return [(mine, (x, y, 1 - c))] + [(mine, (cx, cy, c)) for cx, cy in _chips_of(x, y)]


def _own_blocks_in(x, y, c):
    return [(4 * x + 2 * y + 1 - c, (x, y, 1 - c))] + [(4 * cx + 2 * cy + c, (cx, cy, c)) for cx, cy in _chips_of(x, y)]


def _passed_on_out(x, y, c):
    return [(4 * cx + 2 * cy + c, (x, y, 1 - c)) for cx, cy in _chips_of(x, y)]


def _passed_on_in(x, y, c):
    return [(4 * cx + 2 * cy + 1 - c, (x, y, 1 - c)) for cx, cy in _chips_of(x, y)]


def _gather_start(name, lands, plan):
    n = len(lands)
    n_copies = len(plan(0, 0, 0))

    def body(*refs):
        land = refs[:n]
        send_sems, recv_sems = refs[n:2 * n], refs[2 * n:3 * n]
        token = refs[4 * n]
        x, y, c, _ = _position()
        for a in range(n):
            for k, (block, peer) in enumerate(plan(x, y, c)):
                pltpu.make_async_remote_copy(
                    src_ref=land[a].at[block], dst_ref=land[a].at[block], send_sem=send_sems[a].at[k],
                    recv_sem=recv_sems[a].at[k], device_id=peer, device_id_type=MESH).start()
        token[...] = jnp.zeros_like(token)

    res = pl.pallas_call(
        body, name=name, in_specs=[HBM] * n,
        out_specs=[SEM] * (2 * n) + [HBM] * n + [pl.BlockSpec(memory_space=pltpu.VMEM)],
        out_shape=[pltpu.SemaphoreType.DMA((n_copies,))] * (2 * n) + [pltpu.HBM(l.shape, l.dtype) for l in lands]
        + [jax.ShapeDtypeStruct((8, LANES), F32)],
        input_output_aliases={a: 2 * n + a for a in range(n)},
        compiler_params=pltpu.CompilerParams(has_side_effects=EFFECT),
    )(*[pltpu.with_memory_space_constraint(l, pltpu.HBM) for l in lands])
    return res[:n], res[n:2 * n], res[2 * n:3 * n], res[3 * n]


def _gather_wait(name, lands, send_sems, recv_sems, plan, after):
    n = len(lands)
    tail = [] if after is None else [after]

    def body(*refs):
        land = refs[:n]
        send, recv = refs[n:2 * n], refs[2 * n:3 * n]
        x, y, c, _ = _position()
        for a in range(n):
            for k, (block, peer) in enumerate(plan(x, y, c)):
                cp = pltpu.make_async_remote_copy(
                    src_ref=land[a].at[block], dst_ref=land[a].at[block], send_sem=send[a].at[k],
                    recv_sem=recv[a].at[k], device_id=peer, device_id_type=MESH)
                cp.wait_send()
                cp.wait_recv()

    return pl.pallas_call(
        body, name=name, in_specs=[HBM] * n + [SEM] * (2 * n) + [ANY] * len(tail), out_specs=[HBM] * n,
        out_shape=[pltpu.HBM(l.shape, l.dtype) for l in lands], input_output_aliases={a: a for a in range(n)},
        compiler_params=pltpu.CompilerParams(has_side_effects=EFFECT),
    )(*lands, *send_sems, *recv_sems, *tail)


def _split_start(name, arrays, lands, plan):
    n = len(arrays)
    n_copies = len(plan(0, 0, 0))

    def body(*refs):
        src, land = refs[:n], refs[n:2 * n]
        send_sems, recv_sems = refs[2 * n:3 * n], refs[3 * n:4 * n]
        token = refs[6 * n]
        x, y, c, _ = _position()
        for a in range(n):
            for k, (src_block, land_block, peer) in enumerate(plan(x, y, c)):
                pltpu.make_async_remote_copy(
                    src_ref=src[a].at[src_block], dst_ref=land[a].at[land_block], send_sem=send_sems[a].at[k],
                    recv_sem=recv_sems[a].at[k], device_id=peer, device_id_type=MESH).start()
        token[...] = jnp.zeros_like(token)

    both = list(arrays) + list(lands)
    res = pl.pallas_call(
        body, name=name, in_specs=[HBM] * (2 * n),
        out_specs=[SEM] * (2 * n) + [HBM] * (2 * n) + [pl.BlockSpec(memory_space=pltpu.VMEM)],
        out_shape=[pltpu.SemaphoreType.DMA((n_copies,))] * (2 * n) + [pltpu.HBM(b.shape, b.dtype) for b in both]
        + [jax.ShapeDtypeStruct((8, LANES), F32)],
        input_output_aliases={i: 2 * n + i for i in range(2 * n)},
        compiler_params=pltpu.CompilerParams(has_side_effects=EFFECT),
    )(*[pltpu.with_memory_space_constraint(b, pltpu.HBM) for b in both])
    return res[:n], res[n:2 * n], res[2 * n:3 * n], res[3 * n:4 * n], res[4 * n]


def _split_wait(name, arrays, lands, send_sems, recv_sems, plan, after):
    n = len(arrays)
    tail = [] if after is None else [after]

    def body(*refs):
        src, land = refs[:n], refs[n:2 * n]
        send, recv = refs[2 * n:3 * n], refs[3 * n:4 * n]
        x, y, c, _ = _position()
        for a in range(n):
            for k, (src_block, land_block, peer) in enumerate(plan(x, y, c)):
                cp = pltpu.make_async_remote_copy(
                    src_ref=src[a].at[src_block], dst_ref=land[a].at[land_block], send_sem=send[a].at[k],
                    recv_sem=recv[a].at[k], device_id=peer, device_id_type=MESH)
                cp.wait_send()
                cp.wait_recv()

    both = list(arrays) + list(lands)
    res = pl.pallas_call(
        body, name=name, in_specs=[HBM] * (2 * n) + [SEM] * (2 * n) + [ANY] * len(tail), out_specs=[HBM] * (2 * n),
        out_shape=[pltpu.HBM(b.shape, b.dtype) for b in both], input_output_aliases={i: i for i in range(2 * n)},
        compiler_params=pltpu.CompilerParams(has_side_effects=EFFECT),
    )(*both, *send_sems, *recv_sems, *tail)
    return res[:n], res[n:]


def _to_other_core(x, y, c):
    return [(2 * q + (1 - c), q, (x, y, 1 - c)) for q in range(N_CHIP)]


def _to_other_chips(x, y, c):
    return [(2 * cx + cy, 2 * x + y, (cx, cy, c)) for cx, cy in _chips_of(x, y)]


def _from_other_chips(x, y, c):
    return [(2 * x + y, 2 * cx + cy, (cx, cy, c)) for cx, cy in _chips_of(x, y)]


def _mlp_fwd(tag, h, g, w1g, w2):
    T, D = h.shape
    F = w2.shape[0]
    (n,) = _rmsnorm_fwd(f"mlp{tag}_norm", h, [g])
    a, hid = _mm(f"mlp{tag}_up", "nn", n, w1g, T, F, D, b_groups=N_DEV, out_dtypes=(BF16, BF16),
                 epilogue=_ep_relu_square)
    out = _mm(f"mlp{tag}_down", "nn", hid, w2, T, D, F, extras=(h,), epilogue=_ep_add)
    return out, (n, a, hid)


def _mlp_bwd(tag, h, g, w1g, w2, saved, dy, dy16, behind=()):
    T, D = h.shape
    F = w2.shape[0]
    n, a, hid = saved
    dw2 = _mm(f"mlp{tag}_dw2", "tn", hid, dy16, F, D, T, behind=behind)
    dpre = _mm(f"mlp{tag}_dpre", "nt", dy16, w2, T, F, D, out_dtypes=(BF16,), extras=(a,), epilogue=_ep_times_2a)
    dw1 = _mm(f"mlp{tag}_dw1", "tn", n, dpre, D, F, T, out_groups=N_DEV)
    dn = _mm(f"mlp{tag}_dn", "nt", dpre, w1g, T, D, F, b_groups=N_DEV)
    dh, dh16, dg = _rmsnorm_bwd(f"mlp{tag}_norm_bwd", h, g, dn, dy)
    return dh, dh16, dg, dw1, dw2


def _local_step(x, target, W, fox_blk, arrive=lambda group, after: {}, sync=lambda tag, after, G: ()):
    T, D = x.shape
    H = D // HEAD_DIM
    G = {}
    W = dict(W)

    (ua,) = _rmsnorm_fwd("a_norm", x, [W["a_norm_g"]])
    qkv = _mm("a_qkv", "nt", ua, W["a_w_in_t"], T, 3 * D, D)
    fz = _mm("a_fz", "nt", ua, W["a_w_f_t"], T, LANES, D)
    qa, ka, va = _head_post("a_heads", qkv, D, [(0, W["a_q_g"]), (1, W["a_k_g"]), (2, None)])
    ccol = _gates_fwd(fz, W["a_b_f"], H)
    crow = ccol[:, :H].T
    oa, lse = _fox_fwd(qa, ka, va, ccol, crow, fox_blk)
    W.update(arrive("a_out", oa))
    h1 = _mm("a_out", "nn", oa, W["a_w_out"], T, D, D, extras=(x,), epilogue=_ep_add)
    W.update(arrive("middle", h1))
    h2, mlp0 = _mlp_fwd("0", h1, W["mlp_norm_g0"], W["mlp_w1_0"], W["mlp_w2_0"])
    W.update(arrive("kv_q", h2))
    nkv, ub = _rmsnorm_fwd("kv_b_norm", h2, [W["kv_norm_g"], W["b_norm_g"]])
    kv = _mm("kv_proj", "nn", nkv, W["kv_w"], T, 2 * D, D, b_groups=N_DEV)
    ks, vs = _head_post("kv_heads", kv, D, [(0, W["kv_k_g"]), (1, None)])
    qb_pre = _mm("b_q", "nn", ub, W["b_w_q"], T, D, D)
    (qb,) = _head_post("b_heads", qb_pre, D, [(0, W["b_q_g"])])
    onehot = (jnp.asarray(_rel_index())[:, None] == jnp.arange(N_REL_PAD)[None, :]).astype(F32)
    rel_pad = jnp.pad(W["b_rel"], ((0, 0), (0, N_REL_PAD - N_REL)))
    g_rows = _rel_rows(rel_pad, onehot.T).reshape(H, 1, ROLL_W)
    ob = _chunk_fwd(qb, ks, vs, g_rows)
    W.update(arrive("late", ob))
    h3 = _mm("b_out", "nn", ob, W["b_w_out"], T, D, D, extras=(h2,), epilogue=_ep_add)
    h4, mlp1 = _mlp_fwd("1", h3, W["mlp_norm_g1"], W["mlp_w1_1"], W["mlp_w2_1"])
    dy, dy16, loss = _loss_head(h4, target)

    dh3, dh3_16, G["mlp_norm_g1"], G["mlp_w1_1"], G["mlp_w2_1"] = _mlp_bwd(
        "1", h3, W["mlp_norm_g1"], W["mlp_w1_1"], W["mlp_w2_1"], mlp1, dy, dy16)
    dob = _mm("b_out_dx", "nt", dh3_16, W["b_w_out"], T, D, D, out_dtypes=(BF16,), behind=sync("mlp1", dh3, G))
    G["b_w_out"] = _mm("b_out_dw", "tn", ob, dh3_16, D, D, T, behind=sync("b_out_dx", dob, G))
    dqb, dks, dvs, ds_sum = _chunk_bwd(qb, ks, vs, g_rows, dob, ob)
    G["b_rel"] = _rel_grad(ds_sum, onehot)[:, 0, :]
    dqb_pre, (G["b_q_g"],) = _head_post_bwd("b_heads_bwd", qb_pre, D, [(0, W["b_q_g"], dqb)])
    G["b_w_q"] = _mm("b_q_dw", "tn", ub, dqb_pre, D, D, T)
    dub = _mm("b_q_dx", "nt", dqb_pre, W["b_w_q"], T, D, D)
    dkv, (G["kv_k_g"],) = _head_post_bwd("kv_heads_bwd", kv, D, [(0, W["kv_k_g"], dks), (1, None, dvs)])
    G["kv_w"] = _mm("kv_dw", "tn", nkv, dkv, D, 2 * D, T, out_groups=N_DEV)
    dnkv = _mm("kv_dx", "nt", dkv, W["kv_w"], T, D, 2 * D, b_groups=N_DEV, behind=sync("kv_dw", G["kv_w"], G))
    dh2, _, G["b_norm_g"] = _rmsnorm_bwd("b_norm_bwd", h2, W["b_norm_g"], dub, dh3)
    dh2, dh2_16, G["kv_norm_g"] = _rmsnorm_bwd("kv_norm_bwd", h2, W["kv_norm_g"], dnkv, dh2)
    dh1, dh1_16, G["mlp_norm_g0"], G["mlp_w1_0"], G["mlp_w2_0"] = _mlp_bwd(
        "0", h1, W["mlp_norm_g0"], W["mlp_w1_0"], W["mlp_w2_0"], mlp0, dh2, dh2_16, behind=sync("dh2", dh2, G))
    G["a_w_out"] = _mm("a_out_dw", "tn", oa, dh1_16, D, D, T)
    doa = _mm("a_out_dx", "nt", dh1_16, W["a_w_out"], T, D, D, out_dtypes=(BF16,),
              behind=sync("a_out_dw", G["a_w_out"], G))
    delta = _head_stat("fox_delta", doa, oa, behind=sync("a_out_dx", doa, G))
    dqa, dcq = _fox_bwd_dq(qa, ka, va, doa, ccol, crow, lse, delta, fox_blk)
    sync("fox_bwd_dq", dqa, G)
    dka, dva, dck = _fox_bwd_dkv(qa, ka, va, doa, ccol, crow, lse[:, :H].T, delta[:, :H].T, fox_blk)
    dfz, G["a_b_f"] = _gates_bwd(dcq, dck, fz, W["a_b_f"], H)
    dqkv, (G["a_q_g"], G["a_k_g"]) = _head_post_bwd(
        "a_heads_bwd", qkv, D, [(0, W["a_q_g"], dqa), (1, W["a_k_g"], dka), (2, None, dva)])
    G["a_w_qkv_t"] = _mm("a_qkv_dw", "tn", dqkv, ua, 3 * D, D, T)
    G["a_w_f_t"] = _mm("a_fz_dw", "tn", dfz, ua, LANES, D, T)
    dua = _mm("a_qkv_dx", "nn", dqkv, W["a_w_in_t"], T, D, 3 * D, behind=sync("a_in_dw", G["a_w_f_t"], G))
    dua = _mm("a_fz_dx", "nn", dfz, W["a_w_f_t"], T, D, LANES, extras=(dua,), epilogue=_ep_add,
              behind=sync("a_qkv_dx", dua, G))
    dx, _, G["a_norm_g"] = _rmsnorm_bwd("a_norm_bwd", x, W["a_norm_g"], dua, dh1)
    return loss, dx, G


def _pad_lanes(a):
    return jnp.pad(a, ((0, 0), (0, LANES - a.shape[1])))


def kernel(x, a_norm_g, a_w_in, a_b_f, a_q_g, a_k_g, a_w_out, mlp_norm_g, mlp_w1, mlp_w2, kv_norm_g, kv_w, kv_k_g, b_norm_g, b_w_q, b_q_g, b_rel, b_w_out, loss_target, m_a_norm_g, m_a_w_in, m_a_b_f, m_a_q_g, m_a_k_g, m_a_w_out, m_mlp_norm_g, m_mlp_w1, m_mlp_w2, m_kv_norm_g, m_kv_w, m_kv_k_g, m_b_norm_g, m_b_w_q, m_b_q_g, m_b_rel, m_b_w_out, v_a_norm_g, v_a_w_in, v_a_b_f, v_a_q_g, v_a_k_g, v_a_w_out, v_mlp_norm_g, v_mlp_w1, v_mlp_w2, v_kv_norm_g, v_kv_w, v_kv_k_g, v_b_norm_g, v_b_w_q, v_b_q_g, v_b_rel, v_b_w_out):
    T, D = x.shape[1], x.shape[2]
    H = D // HEAD_DIM
    F = mlp_w2.shape[1] * N_DEV
    me = 4 * lax.axis_index("x") + 2 * lax.axis_index("y") + lax.axis_index("c")
    core = lax.axis_index("c").astype(jnp.int32).reshape(1)
    n_in = a_w_in.shape[2]
    n_rel = b_rel.shape[2]

    small_in = jnp.concatenate([a_norm_g.reshape(1, -1), b_rel.reshape(1, -1)], axis=1)
    local = [jnp.swapaxes(a_w_in[0], 0, 1), a_w_out[0], mlp_w1[0], mlp_w2[0], kv_w, b_w_q[0], b_w_out[0],
             mlp_w1[1], mlp_w2[1]]
    def landing(w):
        return lax.dynamic_update_slice(lax.empty((N_DEV,) + w.shape, w.dtype), w[None], (me,) + (0,) * w.ndim)

    send1, recv1, lands, started = _gather_start(
        "gather_start_first", [landing(small_in), landing(local[0].astype(BF16))], _own_block_out)
    more = _gather_start("gather_start_rest", [landing((w + started[0, 0]).astype(BF16)) for w in local[1:]],
                         _own_block_out)
    send1, recv1, lands, sems2 = send1 + more[0], recv1 + more[1], list(lands + more[2]), {}

    def pass_on(tag, idx, after):
        got = _gather_wait(f"gather_wait_{tag}", [lands[i] for i in idx], [send1[i] for i in idx],
                           [recv1[i] for i in idx], _own_blocks_in, after)
        send, recv, got, _ = _gather_start(f"gather_pass_{tag}", got, _passed_on_out)
        for k, i in enumerate(idx):
            lands[i], sems2[i] = got[k], (send[k], recv[k])

    def landed(tag, idx, after):
        got = _gather_wait(f"gather_landed_{tag}", [lands[i] for i in idx], [sems2[i][0] for i in idx],
                           [sems2[i][1] for i in idx], _passed_on_in, after)
        for k, i in enumerate(idx):
            lands[i] = got[k]
        return got

    def arrive(group, after):
        if group == "first":
            pass_on("first", [0, 1], after)
            g_small, w_in_t = landed("first", [0, 1], None)
            w_in_t = w_in_t.reshape(N_DEV * n_in, D)
            rel = jnp.transpose(g_small[:, 0, D // N_DEV:].reshape(N_DEV, H, n_rel), (1, 0, 2))
            return {"a_norm_g": g_small[:, 0, :D // N_DEV].reshape(1, D), "b_rel": rel.reshape(H, N_DEV * n_rel),
                    "a_w_in_t": w_in_t, "a_w_f_t": jnp.pad(w_in_t[3 * D:], ((0, LANES - H), (0, 0)))}
        if group == "a_out":
            pass_on("early", [2, 3, 4], after)
            return {"a_w_out": landed("a_out", [2], None)[0].reshape(D, D)}
        if group == "middle":
            got = landed("middle", [3, 4], after)
            return {"mlp_w1_0": got[0], "mlp_w2_0": got[1].reshape(F, D)}
        if group == "kv_q":
            pass_on("late", [5, 6, 7, 8, 9], after)
            got = landed("kv_q", [5, 6], None)
            return {"kv_w": got[0], "b_w_q": got[1].reshape(D, D)}
        got = landed("late", [7, 8, 9], after)
        return {"b_w_out": got[0].reshape(D, D), "mlp_w1_1": got[1], "mlp_w2_1": got[2].reshape(F, D)}

    W = {
        "a_b_f": _pad_lanes(a_b_f), "a_q_g": a_q_g, "a_k_g": a_k_g,
        "mlp_norm_g0": mlp_norm_g[0:1], "mlp_norm_g1": mlp_norm_g[1:2],
        "kv_norm_g": kv_norm_g.reshape(1, D), "kv_k_g": kv_k_g.reshape(1, HEAD_DIM),
        "b_norm_g": b_norm_g, "b_q_g": b_q_g, **arrive("first", more[3]),
    }

    chip = (2 * lax.axis_index("x") + lax.axis_index("y")).astype(jnp.int32).reshape(1)
    flat = lambda a: a.reshape(-1, a.shape[-1])
    state = {"a_w_in": [jnp.swapaxes(a[0], 0, 1) for a in (a_w_in, m_a_w_in, v_a_w_in)]}
    for nm, w, m, v in [("a_w_out", a_w_out, m_a_w_out, v_a_w_out), ("kv_w", kv_w, m_kv_w, v_kv_w),
                        ("b_w_q", b_w_q, m_b_w_q, v_b_w_q), ("b_w_out", b_w_out, m_b_w_out, v_b_w_out),
                        ("mlp_w1", mlp_w1, m_mlp_w1, v_mlp_w1), ("mlp_w2", mlp_w2, m_mlp_w2, v_mlp_w2)]:
        state[nm] = [flat(w), flat(m), flat(v)]
    upd, flying = {}, {}

    def chunks(nm, G):
        if nm == "a_w_in":
            return jnp.concatenate([G["a_w_qkv_t"], G["a_w_f_t"][:H]], axis=0).reshape(N_DEV, n_in, D)
        g = G[nm]
        return g if g.ndim == 3 else g.reshape(N_DEV, g.shape[0] // N_DEV, g.shape[1])

    def to_core(tag, names, G):
        gs = [chunks(nm, G) for nm in names]
        lands = [lax.empty((N_CHIP,) + g.shape[1:], g.dtype) for g in gs]
        send, recv, gs, lands, token = _split_start(f"reduce_cores_start_{tag}", gs, lands, _to_other_core)
        flying[tag] = (names, send, recv, gs, lands)
        return (token,)

    def to_chips(tag, after):
        names, send, recv, gs, lands = flying[tag]
        gs, lands = _split_wait(f"reduce_cores_wait_{tag}", gs, lands, send, recv, _to_other_core, after)
        ps = [_pair_sum(f"pair_sum_{nm}", g, r, core) for nm, g, r in zip(names, gs, lands)]
        lands = [lax.empty(p.shape, p.dtype) for p in ps]
        send, recv, ps, lands, token = _split_start(f"reduce_chips_start_{tag}", ps, lands, _to_other_chips)
        flying[tag] = (names, send, recv, ps, lands)
        return (token,)

    def finish(tag, after):
        names, send, recv, ps, lands = flying.pop(tag)
        ps, lands = _split_wait(f"reduce_chips_wait_{tag}", ps, lands, send, recv, _from_other_chips, after)
        for nm, own, land in zip(names, ps, lands):
            if nm.startswith("mlp_w"):
                base, layer = nm[:-2], int(nm[-1])
                w, m, v = state[base]
                upd[base] = _adamw_reduced(f"adamw_{nm}", w, m, v, own, land, chip, row0=layer * (w.shape[0] // 2),
                                           into=upd.get(base))
            else:
                upd[nm] = _adamw_reduced(f"adamw_{nm}", *state[nm], own, land, chip)

    def sync(tag, after, G):
        if tag == "mlp1":
            return to_core("r1", ["mlp_w2_1", "mlp_w1_1"], G)
        if tag == "b_out_dx":
            return to_chips("r1", after)
        if tag == "kv_dw":
            return to_core("r2", ["b_w_out", "b_w_q", "kv_w"], G)
        if tag == "dh2":
            return to_chips("r2", after)
        if tag == "a_out_dw":
            return to_core("r3", ["mlp_w2_0", "mlp_w1_0", "a_w_out"], G)
        if tag == "a_out_dx":
            return to_chips("r3", after)
        if tag == "fox_bwd_dq":
            finish("r1", after)
            finish("r2", after)
        if tag == "a_in_dw":
            return to_core("r4", ["a_w_in"], G)
        if tag == "a_qkv_dx":
            return to_chips("r4", after)
        return ()

    loss_tile, dx, G = _local_step(x[0], loss_target[0], W, min(T, 512), arrive, sync)
    finish("r3", dx)

    def rows8(vecs):
        r = lax.broadcasted_iota(jnp.int32, (8, LANES), 0)
        tile = jnp.zeros((8, LANES), F32)
        for i, vec in enumerate(vecs):
            tile = jnp.where(r == i, vec, tile)
        return tile

    rows = lambda a: a.reshape(-1, LANES)
    packed = [rows(G["a_norm_g"]), rows(G["mlp_norm_g0"]), rows(G["mlp_norm_g1"]), rows(G["kv_norm_g"]),
              rows(G["b_norm_g"]), rows(G["b_rel"]),
              rows8([G["a_b_f"], G["a_q_g"], G["a_k_g"], G["kv_k_g"], G["b_q_g"], loss_tile[0:1]])]
    sizes = [p.shape[0] for p in packed]
    (all_small,) = _all_gather("gather_small_grads", [jnp.concatenate(packed, axis=0)])
    total = _sum_parts("sum_small_grads", all_small, F32)
    offs = np.concatenate([[0], np.cumsum(sizes)])
    s = [total[offs[i]:offs[i + 1]] for i in range(len(sizes))]
    loss = s[6][5, 0]
    g_a_norm = lax.dynamic_slice(s[0].reshape(1, D), (0, me * (D // N_DEV)), (1, D // N_DEV))
    g_rel = lax.dynamic_slice(s[5].reshape(H, N_REL_PAD), (0, me * n_rel), (H, n_rel))
    small = {
        "a_norm_g": (a_norm_g, m_a_norm_g, v_a_norm_g, g_a_norm),
        "a_b_f": (a_b_f, m_a_b_f, v_a_b_f, s[6][0:1, :H]),
        "a_q_g": (a_q_g, m_a_q_g, v_a_q_g, s[6][1:2]),
        "a_k_g": (a_k_g, m_a_k_g, v_a_k_g, s[6][2:3]),
        "mlp_norm_g": (mlp_norm_g, m_mlp_norm_g, v_mlp_norm_g,
                       jnp.concatenate([s[1].reshape(1, D), s[2].reshape(1, D)], axis=0)),
        "kv_norm_g": (kv_norm_g.reshape(1, D), m_kv_norm_g.reshape(1, D), v_kv_norm_g.reshape(1, D), s[3].reshape(1, D)),
        "kv_k_g": (kv_k_g.reshape(1, HEAD_DIM), m_kv_k_g.reshape(1, HEAD_DIM), v_kv_k_g.reshape(1, HEAD_DIM),
                   s[6][3:4]),
        "b_norm_g": (b_norm_g, m_b_norm_g, v_b_norm_g, s[4].reshape(1, D)),
        "b_q_g": (b_q_g, m_b_q_g, v_b_q_g, s[6][4:5]),
        "b_rel": (b_rel[0], m_b_rel[0], v_b_rel[0], g_rel),
    }
    for nm, (w, m, v, g) in small.items():
        upd[nm] = _adamw(f"adamw_{nm}", w, m, v, g)
    upd["kv_norm_g"] = [a.reshape(D) for a in upd["kv_norm_g"]]
    upd["kv_k_g"] = [a.reshape(HEAD_DIM) for a in upd["kv_k_g"]]
    upd["b_rel"] = [a[None] for a in upd["b_rel"]]
    finish("r4", total)
    upd["a_w_in"] = [jnp.swapaxes(o, 0, 1)[None] for o in upd["a_w_in"]]
    for nm, w in [("a_w_out", a_w_out), ("kv_w", kv_w), ("b_w_q", b_w_q), ("b_w_out", b_w_out), ("mlp_w1", mlp_w1),
                  ("mlp_w2", mlp_w2)]:
        upd[nm] = [o.reshape(w.shape) for o in upd[nm]]

    order = ["a_norm_g", "a_w_in", "a_b_f", "a_q_g", "a_k_g", "a_w_out", "mlp_norm_g", "mlp_w1", "mlp_w2",
             "kv_norm_g", "kv_w", "kv_k_g", "b_norm_g", "b_w_q", "b_q_g", "b_rel", "b_w_out"]
    return (loss, dx[None], *[upd[n][0] for n in order], *[upd[n][1] for n in order],
            *[upd[n][2] for n in order], *[upd[n][3] for n in order])
```

```python
import functools

import numpy as np
import jax
import jax.numpy as jnp
from jax import lax
from jax.experimental import pallas as pl
from jax.experimental.pallas import tpu as pltpu

F32 = jnp.float32
BF16 = jnp.bfloat16
HIGHEST = lax.Precision.HIGHEST
MESH = pl.DeviceIdType.MESH
ANY = pl.BlockSpec(memory_space=pl.ANY)

N_DEV = 8
N_CHIP = 4
LANES = 128
HEAD_DIM = 128
CHUNK = 64
N_PREV_CHUNKS = 8
REL_CLIP = 256
N_REL = REL_CLIP + CHUNK
N_REL_PAD = 384
WIN = N_PREV_CHUNKS * CHUNK
ROLL_W = 4 * WIN
EPS = 1e-6
NEG = -1e30
SCALE = HEAD_DIM ** -0.5
ADAM_LR, ADAM_B1, ADAM_B2, ADAM_EPS, ADAM_WD, ADAM_STEP = 0.001, 0.9, 0.999, 1e-08, 0.01, 10
V7X_VMEM_BYTES = 64 * 1024 * 1024
VMEM_LIMIT = V7X_VMEM_BYTES * 3 // 4

_NN = (((1,), (0,)), ((), ()))
_NT = (((1,), (1,)), ((), ()))
_TN = (((0,), (0,)), ((), ()))


def _params(*sem):
    return pltpu.CompilerParams(dimension_semantics=sem or None, vmem_limit_bytes=VMEM_LIMIT)


def _tile(n, pref):
    t = n
    while t > pref and t % 2 == 0:
        t //= 2
    return t


def _ep_store(acc, ex, outs):
    outs[0][...] = acc.astype(outs[0].dtype)


def _ep_add(acc, ex, outs):
    outs[0][...] = (acc + ex[0][...]).astype(outs[0].dtype)


def _ep_relu_square(acc, ex, outs):
    a = jnp.maximum(acc, 0.0)
    outs[0][...] = a.astype(outs[0].dtype)
    outs[1][...] = (a * a).astype(outs[1].dtype)


def _ep_times_2a(acc, ex, outs):
    outs[0][...] = (acc * (2.0 * ex[0][...].astype(F32))).astype(outs[0].dtype)


def _mm(name, kind, a, b, M, N, K, *, b_groups=0, out_groups=0, out_dtypes=(F32,), extras=(),
        epilogue=_ep_store, tm=1024, tn=1024, tk=2048, behind=()):
    tm, tn, tk = _tile(M, tm), _tile(N, tn), _tile(K, tk)
    if b_groups:
        cg = b.shape[-1]
        if kind == "nn":
            tn = min(tn, cg)
        else:
            tk = min(tk, cg)
    if out_groups:
        tn = min(tn, N // out_groups)
    grid = (M // tm, N // tn, K // tk)
    nk = grid[2]
    if kind == "nn":
        dims = _NN
        a_spec = pl.BlockSpec((tm, tk), lambda i, j, k: (i, k))
        if b_groups:
            npg = cg // tn
            b_spec = pl.BlockSpec((None, tk, tn), lambda i, j, k: (j // npg, k, j % npg))
        else:
            b_spec = pl.BlockSpec((tk, tn), lambda i, j, k: (k, j))
    elif kind == "nt":
        dims = _NT
        a_spec = pl.BlockSpec((tm, tk), lambda i, j, k: (i, k))
        if b_groups:
            npg = cg // tk
            b_spec = pl.BlockSpec((None, tn, tk), lambda i, j, k: (k // npg, j, k % npg))
        else:
            b_spec = pl.BlockSpec((tn, tk), lambda i, j, k: (j, k))
    else:
        dims = _TN
        a_spec = pl.BlockSpec((tk, tm), lambda i, j, k: (k, i))
        b_spec = pl.BlockSpec((tk, tn), lambda i, j, k: (k, j))
    tile_spec = pl.BlockSpec((tm, tn), lambda i, j, k: (i, j))
    if out_groups:
        ng = (N // out_groups) // tn
        out_spec = pl.BlockSpec((None, tm, tn), lambda i, j, k: (j // ng, i, j % ng))
        out_shape = [jax.ShapeDtypeStruct((out_groups, M, N // out_groups), d) for d in out_dtypes]
    else:
        out_spec = tile_spec
        out_shape = [jax.ShapeDtypeStruct((M, N), d) for d in out_dtypes]
    n_ex, n_out = len(extras), len(out_dtypes)

    def body(a_ref, b_ref, *rest):
        ex, outs, acc = rest[:n_ex], rest[-1 - n_out:-1], rest[-1]
        k = pl.program_id(2)

        @pl.when(k == 0)
        def _():
            acc[...] = jnp.zeros_like(acc)

        acc[...] += lax.dot_general(a_ref[...].astype(BF16), b_ref[...].astype(BF16), dims,
                                    preferred_element_type=F32)

        @pl.when(k == nk - 1)
        def _():
            epilogue(acc[...], ex, outs)

    res = pl.pallas_call(
        body, name=name, grid=grid,
        in_specs=[a_spec, b_spec] + [tile_spec] * n_ex + [ANY] * len(behind),
        out_specs=[out_spec] * n_out, out_shape=out_shape,
        scratch_shapes=[pltpu.VMEM((tm, tn), F32)],
        compiler_params=_params("parallel", "parallel", "arbitrary"),
    )(a, b, *extras, *behind)
    return res[0] if n_out == 1 else res


def _rmsnorm_fwd(name, x, gains):
    T, D = x.shape
    bt = _tile(T, 256)
    n = len(gains)

    def body(x_ref, *rest):
        xv = x_ref[...]
        y = xv * lax.rsqrt(jnp.mean(xv * xv, axis=-1, keepdims=True) + EPS)
        for g_ref, o_ref in zip(rest[:n], rest[n:]):
            o_ref[...] = (y * g_ref[...]).astype(BF16)

    row = pl.BlockSpec((bt, D), lambda i: (i, 0))
    gsp = pl.BlockSpec((1, D), lambda i: (0, 0))
    return pl.pallas_call(
        body, name=name, grid=(T // bt,), in_specs=[row] + [gsp] * n, out_specs=[row] * n,
        out_shape=[jax.ShapeDtypeStruct((T, D), BF16)] * n, compiler_params=_params("parallel"),
    )(x, *gains)


def _rmsnorm_bwd(name, x, g, dn, res):
    T, D = x.shape
    bt = _tile(T, 256)

    def body(x_ref, g_ref, dn_ref, res_ref, dx_ref, dx16_ref, dg_ref):
        @pl.when(pl.program_id(0) == 0)
        def _():
            dg_ref[...] = jnp.zeros_like(dg_ref)

        xv, dnv = x_ref[...], dn_ref[...]
        r = lax.rsqrt(jnp.mean(xv * xv, axis=-1, keepdims=True) + EPS)
        xhat = xv * r
        dyg = dnv * g_ref[...]
        dx = res_ref[...] + r * (dyg - xhat * jnp.mean(dyg * xhat, axis=-1, keepdims=True))
        dx_ref[...] = dx
        dx16_ref[...] = dx.astype(BF16)
        dg_ref[...] += jnp.sum(dnv * xhat, axis=0, keepdims=True)

    row = pl.BlockSpec((bt, D), lambda i: (i, 0))
    gsp = pl.BlockSpec((1, D), lambda i: (0, 0))
    return pl.pallas_call(
        body, name=name, grid=(T // bt,), in_specs=[row, gsp, row, row], out_specs=[row, row, gsp],
        out_shape=[jax.ShapeDtypeStruct((T, D), F32), jax.ShapeDtypeStruct((T, D), BF16),
                   jax.ShapeDtypeStruct((1, D), F32)],
        compiler_params=_params("arbitrary"),
    )(x, g, dn, res)


def _head_post(name, x, D, items):
    T = x.shape[0]
    H = D // HEAD_DIM
    bt = _tile(T, 256)
    gains = [g for _, g in items if g is not None]
    n_it, n_g = len(items), len(gains)

    def body(*refs):
        x_refs, g_refs, o_refs = refs[:n_it], refs[n_it:n_it + n_g], refs[n_it + n_g:]
        gi = 0
        for it, (_, g) in enumerate(items):
            if g is None:
                o_refs[it][...] = x_refs[it][...].astype(BF16)
                continue
            gv = g_refs[gi][...]
            gi += 1
            for h in range(H):
                sl = slice(h * HEAD_DIM, (h + 1) * HEAD_DIM)
                xs = x_refs[it][:, sl]
                r = lax.rsqrt(jnp.mean(xs * xs, axis=-1, keepdims=True) + EPS)
                o_refs[it][:, sl] = (xs * r * gv).astype(BF16)

    in_specs = [pl.BlockSpec((bt, D), functools.partial(lambda cb, i: (i, cb), cb)) for cb, _ in items]
    in_specs += [pl.BlockSpec((1, HEAD_DIM), lambda i: (0, 0))] * n_g
    row = pl.BlockSpec((bt, D), lambda i: (i, 0))
    res = pl.pallas_call(
        body, name=name, grid=(T // bt,), in_specs=in_specs, out_specs=[row] * n_it,
        out_shape=[jax.ShapeDtypeStruct((T, D), BF16)] * n_it, compiler_params=_params("parallel"),
    )(*([x] * n_it), *gains)
    return res


def _head_post_bwd(name, x, D, items):
    T = x.shape[0]
    H = D // HEAD_DIM
    bt = _tile(T, 128)
    gains = [g for _, g, _ in items if g is not None]
    n_it, n_g = len(items), len(gains)

    def body(*refs):
        x_refs, dy_refs = refs[:n_it], refs[n_it:2 * n_it]
        g_refs = refs[2 * n_it:2 * n_it + n_g]
        dx_ref = refs[2 * n_it + n_g]
        dg_refs = refs[2 * n_it + n_g + 1:]

        @pl.when(pl.program_id(0) == 0)
        def _():
            for r in dg_refs:
                r[...] = jnp.zeros_like(r)

        gi = 0
        for it, (_, g, _) in enumerate(items):
            if g is None:
                dx_ref[:, it * D:(it + 1) * D] = dy_refs[it][...].astype(BF16)
                continue
            gv = g_refs[gi][...]
            dg = jnp.zeros((1, HEAD_DIM), F32)
            for h in range(H):
                sl = slice(h * HEAD_DIM, (h + 1) * HEAD_DIM)
                xs, dy = x_refs[it][:, sl], dy_refs[it][:, sl]
                r = lax.rsqrt(jnp.mean(xs * xs, axis=-1, keepdims=True) + EPS)
                xhat = xs * r
                dyg = dy * gv
                dx = r * (dyg - xhat * jnp.mean(dyg * xhat, axis=-1, keepdims=True))
                dx_ref[:, it * D + h * HEAD_DIM:it * D + (h + 1) * HEAD_DIM] = dx.astype(BF16)
                dg = dg + jnp.sum(dy * xhat, axis=0, keepdims=True)
            dg_refs[gi][...] += dg
            gi += 1

    in_specs = [pl.BlockSpec((bt, D), functools.partial(lambda cb, i: (i, cb), cb)) for cb, _, _ in items]
    in_specs += [pl.BlockSpec((bt, D), lambda i: (i, 0))] * n_it
    gsp = pl.BlockSpec((1, HEAD_DIM), lambda i: (0, 0))
    in_specs += [gsp] * n_g
    res = pl.pallas_call(
        body, name=name, grid=(T // bt,), in_specs=in_specs,
        out_specs=[pl.BlockSpec((bt, n_it * D), lambda i: (i, 0))] + [gsp] * n_g,
        out_shape=[jax.ShapeDtypeStruct((T, n_it * D), BF16)] + [jax.ShapeDtypeStruct((1, HEAD_DIM), F32)] * n_g,
        compiler_params=_params("arbitrary"),
    )(*([x] * n_it), *[dy for _, _, dy in items], *gains)
    return res[0], list(res[1:])


def _gates_fwd(fz, bf, H):
    T = fz.shape[0]
    bt = _tile(T, 512)

    def body(fz_ref, bf_ref, c_ref, carry):
        @pl.when(pl.program_id(0) == 0)
        def _():
            carry[...] = jnp.zeros_like(carry)

        z = fz_ref[...] + bf_ref[...]
        logf = jnp.minimum(z, 0.0) - jnp.log(1.0 + jnp.exp(-jnp.abs(z)))
        lane = lax.broadcasted_iota(jnp.int32, (bt, LANES), 1)
        logf = jnp.where(lane < H, logf, 0.0)
        tri = (lax.broadcasted_iota(jnp.int32, (bt, bt), 0) >= lax.broadcasted_iota(jnp.int32, (bt, bt), 1))
        c = jnp.dot(tri.astype(F32), logf, precision=HIGHEST, preferred_element_type=F32) + carry[...]
        c_ref[...] = c
        carry[...] = c[bt - 1:bt, :]

    row = pl.BlockSpec((bt, LANES), lambda i: (i, 0))
    return pl.pallas_call(
        body, name="fox_gates_fwd", grid=(T // bt,),
        in_specs=[row, pl.BlockSpec((1, LANES), lambda i: (0, 0))], out_specs=row,
        out_shape=jax.ShapeDtypeStruct((T, LANES), F32), scratch_shapes=[pltpu.VMEM((1, LANES), F32)],
        compiler_params=_params("arbitrary"),
    )(fz, bf)


def _gates_bwd(dcq, dck, fz, bf, H):
    T = fz.shape[0]
    bt = _tile(T, 512)
    nb = T // bt

    def body(dcq_ref, dck_ref, fz_ref, bf_ref, dfz_ref, dbf_ref, carry):
        @pl.when(pl.program_id(0) == 0)
        def _():
            carry[...] = jnp.zeros_like(carry)
            dbf_ref[...] = jnp.zeros_like(dbf_ref)

        tri = (lax.broadcasted_iota(jnp.int32, (bt, bt), 0) <= lax.broadcasted_iota(jnp.int32, (bt, bt), 1))
        dc = dcq_ref[...] + dck_ref[...]
        dlogf = jnp.dot(tri.astype(F32), dc, precision=HIGHEST, preferred_element_type=F32) + carry[...]
        carry[...] = dlogf[0:1, :]
        z = fz_ref[...] + bf_ref[...]
        lane = lax.broadcasted_iota(jnp.int32, (bt, LANES), 1)
        dfz = jnp.where(lane < H, dlogf / (1.0 + jnp.exp(z)), 0.0)
        dfz_ref[...] = dfz
        dbf_ref[...] += jnp.sum(dfz, axis=0, keepdims=True)

    row = pl.BlockSpec((bt, LANES), lambda i: (nb - 1 - i, 0))
    one = pl.BlockSpec((1, LANES), lambda i: (0, 0))
    return pl.pallas_call(
        body, name="fox_gates_bwd", grid=(nb,), in_specs=[row, row, row, one], out_specs=[row, one],
        out_shape=[jax.ShapeDtypeStruct((T, LANES), F32), jax.ShapeDtypeStruct((1, LANES), F32)],
        scratch_shapes=[pltpu.VMEM((1, LANES), F32)], compiler_params=_params("arbitrary"),
    )(dcq, dck, fz, bf)


def _lane_pick(ref_value, h):
    lane = lax.broadcasted_iota(jnp.int32, (1, LANES), 1)
    return jnp.sum(jnp.where(lane == h, ref_value, 0.0), axis=1, keepdims=True)


def _lane_put(old, h, col):
    lane = lax.broadcasted_iota(jnp.int32, (1, LANES), 1)
    return jnp.where(lane == h, col, old)


FOX_HEADS = 4


def _causal(rows, cols, row0=0, transposed=False):
    row = row0 + lax.broadcasted_iota(jnp.int32, (rows, cols), 0)
    col = lax.broadcasted_iota(jnp.int32, (rows, cols), 1)
    return col >= row if transposed else row >= col


def _triangle(nb, groups, keys_outermost=False):
    if keys_outermost:
        steps = [(qi, g, kj) for kj in range(nb) for g in range(groups) for qi in range(kj, nb)]
    else:
        steps = [(qi, g, kj) for qi in range(nb) for g in range(groups) for kj in range(qi + 1)]
    return [jnp.asarray(np.array(col, np.int32)) for col in zip(*steps)]


def _fox_fwd(q, k, v, ccol, crow, blk):
    T, D = q.shape
    H = D // HEAD_DIM
    nb = T // blk
    steps = _triangle(nb, H // FOX_HEADS)

    def body(qi_ref, hg_ref, kj_ref, q_ref, k_ref, v_ref, ccol_ref, crow_ref, o_ref, lse_ref, m_s, l_s, acc_s):
        step_id = pl.program_id(0)
        qi, hg, kj = qi_ref[step_id], hg_ref[step_id], kj_ref[step_id]

        @pl.when(kj == 0)
        def _():
            m_s[...] = jnp.full_like(m_s, NEG)
            l_s[...] = jnp.zeros_like(l_s)
            acc_s[...] = jnp.zeros_like(acc_s)

        @pl.when((kj == 0) & (hg == 0))
        def _():
            lse_ref[...] = jnp.zeros_like(lse_ref)

        def step(diagonal):
            heads = range(FOX_HEADS)
            sls = [slice(hh * HEAD_DIM, (hh + 1) * HEAD_DIM) for hh in heads]
            m_old = [m_s[hh] for hh in heads]
            l_old = [l_s[hh] for hh in heads]
            acc_old = [acc_s[:, sl] for sl in sls]
            s = []
            for hh in heads:
                h = hg * FOX_HEADS + hh
                bias = _lane_pick(ccol_ref[...], h) - crow_ref[pl.ds(h, 1), :]
                sc = lax.dot_general(q_ref[:, sls[hh]], k_ref[:, sls[hh]], _NT, preferred_element_type=F32) * SCALE + bias
                s.append(jnp.where(_causal(blk, blk), sc, NEG) if diagonal else sc)
            m_new = [jnp.maximum(m_old[hh], jnp.max(s[hh], axis=1, keepdims=True)) for hh in heads]
            p = [jnp.exp(s[hh] - m_new[hh]) for hh in heads]
            alpha = [jnp.exp(m_old[hh] - m_new[hh]) for hh in heads]
            pv = [jnp.dot(p[hh].astype(BF16), v_ref[:, sls[hh]], preferred_element_type=F32) for hh in heads]
            for hh in heads:
                l_s[hh] = alpha[hh] * l_old[hh] + jnp.sum(p[hh], axis=1, keepdims=True)
                acc_s[:, sls[hh]] = alpha[hh] * acc_old[hh] + pv[hh]
                m_s[hh] = m_new[hh]

        @pl.when(kj < qi)
        def _():
            step(False)

        @pl.when(kj == qi)
        def _():
            step(True)
            for hh in range(FOX_HEADS):
                sl = slice(hh * HEAD_DIM, (hh + 1) * HEAD_DIM)
                o_ref[:, sl] = (acc_s[:, sl] / l_s[hh]).astype(BF16)
                lse_ref[...] = _lane_put(lse_ref[...], hg * FOX_HEADS + hh, m_s[hh] + jnp.log(l_s[hh]))

    wide = FOX_HEADS * HEAD_DIM
    qsp = pl.BlockSpec((blk, wide), lambda s, qi, hg, kj: (qi[s], hg[s]))
    ksp = pl.BlockSpec((blk, wide), lambda s, qi, hg, kj: (kj[s], hg[s]))
    stat = pl.BlockSpec((blk, LANES), lambda s, qi, hg, kj: (qi[s], 0))
    return pl.pallas_call(
        body, name="fox_fwd",
        grid_spec=pltpu.PrefetchScalarGridSpec(
            num_scalar_prefetch=3, grid=(steps[0].shape[0],),
            in_specs=[qsp, ksp, ksp, stat, pl.BlockSpec((H, blk), lambda s, qi, hg, kj: (0, kj[s]))],
            out_specs=[qsp, stat],
            scratch_shapes=[pltpu.VMEM((FOX_HEADS, blk, 1), F32), pltpu.VMEM((FOX_HEADS, blk, 1), F32),
                            pltpu.VMEM((blk, wide), F32)]),
        out_shape=[jax.ShapeDtypeStruct((T, D), BF16), jax.ShapeDtypeStruct((T, LANES), F32)],
        compiler_params=_params("arbitrary"),
    )(*steps, q, k, v, ccol, crow)


def _head_stat(name, a, b, behind=()):
    T, D = a.shape
    H = D // HEAD_DIM
    bt = _tile(T, 256)

    def body(a_ref, b_ref, *rest):
        o_ref = rest[-1]
        out = jnp.zeros((bt, LANES), F32)
        for h in range(H):
            sl = slice(h * HEAD_DIM, (h + 1) * HEAD_DIM)
            prod = a_ref[:, sl].astype(F32) * b_ref[:, sl].astype(F32)
            out = _lane_put(out, h, jnp.sum(prod, axis=1, keepdims=True))
        o_ref[...] = out

    row = pl.BlockSpec((bt, D), lambda i: (i, 0))
    return pl.pallas_call(
        body, name=name, grid=(T // bt,), in_specs=[row, row] + [ANY] * len(behind),
        out_specs=pl.BlockSpec((bt, LANES), lambda i: (i, 0)),
        out_shape=jax.ShapeDtypeStruct((T, LANES), F32), compiler_params=_params("parallel"),
    )(a, b, *behind)


def _fox_bwd_dq(q, k, v, do, ccol, crow, lse, delta, blk):
    T, D = q.shape
    H = D // HEAD_DIM
    nb = T // blk

    steps = _triangle(nb, H // FOX_HEADS)

    def body(qi_ref, hg_ref, kj_ref, q_ref, k_ref, v_ref, do_ref, ccol_ref, crow_ref, lse_ref, delta_ref,
             dq_ref, dc_ref, acc_s, dc_s):
        step_id = pl.program_id(0)
        qi, hg, kj = qi_ref[step_id], hg_ref[step_id], kj_ref[step_id]

        @pl.when(kj == 0)
        def _():
            acc_s[...] = jnp.zeros_like(acc_s)
            dc_s[...] = jnp.zeros_like(dc_s)

        @pl.when((kj == 0) & (hg == 0))
        def _():
            dc_ref[...] = jnp.zeros_like(dc_ref)

        def step(diagonal):
            heads = range(FOX_HEADS)
            sls = [slice(hh * HEAD_DIM, (hh + 1) * HEAD_DIM) for hh in heads]
            p, dp = [], []
            for hh in heads:
                h = hg * FOX_HEADS + hh
                bias = _lane_pick(ccol_ref[...] - lse_ref[...], h) - crow_ref[pl.ds(h, 1), :]
                sc = lax.dot_general(q_ref[:, sls[hh]], k_ref[:, sls[hh]], _NT, preferred_element_type=F32) * SCALE + bias
                if diagonal:
                    sc = jnp.where(_causal(blk, blk), sc, NEG)
                p.append(jnp.exp(sc))
                dp.append(lax.dot_general(do_ref[:, sls[hh]], v_ref[:, sls[hh]], _NT, preferred_element_type=F32)
                          - _lane_pick(delta_ref[...], h))
            ds = [p[hh] * dp[hh] for hh in heads]
            for hh in heads:
                acc_s[:, sls[hh]] += jnp.dot(ds[hh].astype(BF16), k_ref[:, sls[hh]], preferred_element_type=F32)
                dc_s[hh] += jnp.sum(ds[hh], axis=1, keepdims=True)

        @pl.when(kj < qi)
        def _():
            step(False)

        @pl.when(kj == qi)
        def _():
            step(True)
            dq_ref[...] = acc_s[...] * SCALE
            for hh in range(FOX_HEADS):
                dc_ref[...] = _lane_put(dc_ref[...], hg * FOX_HEADS + hh, dc_s[hh])

    wide = FOX_HEADS * HEAD_DIM
    qsp = pl.BlockSpec((blk, wide), lambda s, qi, hg, kj: (qi[s], hg[s]))
    ksp = pl.BlockSpec((blk, wide), lambda s, qi, hg, kj: (kj[s], hg[s]))
    stat = pl.BlockSpec((blk, LANES), lambda s, qi, hg, kj: (qi[s], 0))
    return pl.pallas_call(
        body, name="fox_bwd_dq",
        grid_spec=pltpu.PrefetchScalarGridSpec(
            num_scalar_prefetch=3, grid=(steps[0].shape[0],),
            in_specs=[qsp, ksp, ksp, qsp, stat, pl.BlockSpec((H, blk), lambda s, qi, hg, kj: (0, kj[s])), stat, stat],
            out_specs=[qsp, stat],
            scratch_shapes=[pltpu.VMEM((blk, wide), F32), pltpu.VMEM((FOX_HEADS, blk, 1), F32)]),
        out_shape=[jax.ShapeDtypeStruct((T, D), F32), jax.ShapeDtypeStruct((T, LANES), F32)],
        compiler_params=_params("arbitrary"),
    )(*steps, q, k, v, do, ccol, crow, lse, delta)


def _fox_bwd_dkv(q, k, v, do, ccol, crow, lse_row, delta_row, blk):
    T, D = q.shape
    H = D // HEAD_DIM
    nb = T // blk

    steps = _triangle(nb, H // FOX_HEADS, keys_outermost=True)

    def body(qi_ref, hg_ref, kj_ref, q_ref, k_ref, v_ref, do_ref, ccol_ref, crow_ref, lse_ref, delta_ref,
             dk_ref, dv_ref, dc_ref, dk_s, dv_s, dc_s):
        step_id = pl.program_id(0)
        qi, hg, kj = qi_ref[step_id], hg_ref[step_id], kj_ref[step_id]

        @pl.when(qi == kj)
        def _():
            dk_s[...] = jnp.zeros_like(dk_s)
            dv_s[...] = jnp.zeros_like(dv_s)
            dc_s[...] = jnp.zeros_like(dc_s)

        @pl.when((qi == kj) & (hg == 0))
        def _():
            dc_ref[...] = jnp.zeros_like(dc_ref)

        def step(diagonal):
            heads = range(FOX_HEADS)
            sls = [slice(hh * HEAD_DIM, (hh + 1) * HEAD_DIM) for hh in heads]
            pt, dpt = [], []
            for hh in heads:
                h = hg * FOX_HEADS + hh
                bias = (crow_ref[pl.ds(h, 1), :] - lse_ref[pl.ds(h, 1), :]) - _lane_pick(ccol_ref[...], h)
                st = lax.dot_general(k_ref[:, sls[hh]], q_ref[:, sls[hh]], _NT, preferred_element_type=F32) * SCALE + bias
                if diagonal:
                    st = jnp.where(_causal(blk, blk, transposed=True), st, NEG)
                pt.append(jnp.exp(st))
                dpt.append(lax.dot_general(v_ref[:, sls[hh]], do_ref[:, sls[hh]], _NT, preferred_element_type=F32)
                           - delta_ref[pl.ds(h, 1), :])
            dst = [pt[hh] * dpt[hh] for hh in heads]
            for hh in heads:
                dv_s[:, sls[hh]] += jnp.dot(pt[hh].astype(BF16), do_ref[:, sls[hh]], preferred_element_type=F32)
                dk_s[:, sls[hh]] += jnp.dot(dst[hh].astype(BF16), q_ref[:, sls[hh]], preferred_element_type=F32)
                dc_s[hh] -= jnp.sum(dst[hh], axis=1, keepdims=True)

        @pl.when(qi > kj)
        def _():
            step(False)

        @pl.when(qi == kj)
        def _():
            step(True)

        @pl.when(qi == nb - 1)
        def _():
            dk_ref[...] = dk_s[...] * SCALE
            dv_ref[...] = dv_s[...]
            for hh in range(FOX_HEADS):
                dc_ref[...] = _lane_put(dc_ref[...], hg * FOX_HEADS + hh, dc_s[hh])

    wide = FOX_HEADS * HEAD_DIM
    ksp = pl.BlockSpec((blk, wide), lambda s, qi, hg, kj: (kj[s], hg[s]))
    qsp = pl.BlockSpec((blk, wide), lambda s, qi, hg, kj: (qi[s], hg[s]))
    kstat = pl.BlockSpec((blk, LANES), lambda s, qi, hg, kj: (kj[s], 0))
    qrow = pl.BlockSpec((H, blk), lambda s, qi, hg, kj: (0, qi[s]))
    return pl.pallas_call(
        body, name="fox_bwd_dkv",
        grid_spec=pltpu.PrefetchScalarGridSpec(
            num_scalar_prefetch=3, grid=(steps[0].shape[0],),
            in_specs=[qsp, ksp, ksp, qsp, kstat, qrow, qrow, qrow], out_specs=[ksp, ksp, kstat],
            scratch_shapes=[pltpu.VMEM((blk, wide), F32), pltpu.VMEM((blk, wide), F32),
                            pltpu.VMEM((FOX_HEADS, blk, 1), F32)]),
        out_shape=[jax.ShapeDtypeStruct((T, D), F32), jax.ShapeDtypeStruct((T, D), F32),
                   jax.ShapeDtypeStruct((T, LANES), F32)],
        compiler_params=_params("arbitrary"),
    )(*steps, q, k, v, do, ccol, crow, lse_row, delta_row)


def _rel_index():
    j = np.arange(ROLL_W)
    u = np.where(j < 2 * WIN, j, j - ROLL_W)
    return np.clip(WIN - u, -(CHUNK - 1), REL_CLIP) + (CHUNK - 1)


def _band_bias(g_row, bias_s):
    base = jnp.broadcast_to(g_row, (CHUNK, ROLL_W))
    rowb = lax.broadcasted_iota(jnp.int32, (CHUNK, ROLL_W), 0)
    for bit in range(CHUNK.bit_length() - 1):
        base = jnp.where((rowb >> bit) & 1 == 1, pltpu.roll(base, 1 << bit, 1), base)
    kc = lax.broadcasted_iota(jnp.int32, (CHUNK, 2 * WIN), 1) // CHUNK
    for a in range(N_PREV_CHUNKS):
        part = base if a == 0 else pltpu.roll(base, CHUNK * a, 1)
        valid = (kc >= a) & (kc <= a + N_PREV_CHUNKS)
        bias_s[a * CHUNK:(a + 1) * CHUNK, :] = jnp.where(valid, part[:, :2 * WIN], NEG)


def _chunk_probs(q, kp, kc, bias_s, qi):
    sp = lax.dot_general(q, kp, _NT, preferred_element_type=F32) * SCALE + bias_s[:, :WIN]
    sc = lax.dot_general(q, kc, _NT, preferred_element_type=F32) * SCALE + bias_s[:, WIN:]
    sp = jnp.where(qi > 0, sp, NEG)
    m = jnp.maximum(jnp.max(sp, axis=1, keepdims=True), jnp.max(sc, axis=1, keepdims=True))
    pp, pc = jnp.exp(sp - m), jnp.exp(sc - m)
    inv = 1.0 / (jnp.sum(pp, axis=1, keepdims=True) + jnp.sum(pc, axis=1, keepdims=True))
    return pp * inv, pc * inv


def _chunk_fwd(q, k, v, g_rows):
    T, D = q.shape
    H = D // HEAD_DIM
    nq = T // WIN

    def body(q_ref, kp_ref, kc_ref, vp_ref, vc_ref, g_ref, o_ref, bias_s):
        qi = pl.program_id(1)

        @pl.when(qi == 0)
        def _():
            _band_bias(g_ref[...], bias_s)

        pp, pc = _chunk_probs(q_ref[...], kp_ref[...], kc_ref[...], bias_s, qi)
        o = jnp.dot(pp.astype(BF16), vp_ref[...], preferred_element_type=F32)
        o += jnp.dot(pc.astype(BF16), vc_ref[...], preferred_element_type=F32)
        o_ref[...] = o.astype(BF16)

    cur = pl.BlockSpec((WIN, HEAD_DIM), lambda h, qi: (qi, h))
    prev = pl.BlockSpec((WIN, HEAD_DIM), lambda h, qi: (jnp.maximum(qi - 1, 0), h))
    return pl.pallas_call(
        body, name="chunk_fwd", grid=(H, nq),
        in_specs=[cur, prev, cur, prev, cur, pl.BlockSpec((None, 1, ROLL_W), lambda h, qi: (h, 0, 0))],
        out_specs=cur, out_shape=jax.ShapeDtypeStruct((T, D), BF16),
        scratch_shapes=[pltpu.VMEM((WIN, 2 * WIN), F32)],
        compiler_params=_params("arbitrary", "arbitrary"),
    )(q, k, k, v, v, g_rows)


def _chunk_bwd(q, k, v, g_rows, do, o):
    T, D = q.shape
    H = D // HEAD_DIM
    nq = T // WIN

    def body(q_ref, kp_ref, kc_ref, vp_ref, vc_ref, g_ref, do_ref, o_ref,
             dq_ref, dk_ref, dv_ref, ds_ref, bias_s, ck_s, cv_s):
        qi = pl.program_id(1)

        @pl.when(qi == 0)
        def _():
            _band_bias(g_ref[...], bias_s)
            ds_ref[...] = jnp.zeros_like(ds_ref)
            ck_s[...] = jnp.zeros_like(ck_s)
            cv_s[...] = jnp.zeros_like(cv_s)

        @pl.when(qi < nq)
        def _():
            qv, dov = q_ref[...], do_ref[...]
            pp, pc = _chunk_probs(qv, kp_ref[...], kc_ref[...], bias_s, qi)
            delta = jnp.sum(dov.astype(F32) * o_ref[...].astype(F32), axis=1, keepdims=True)
            dsp = pp * (lax.dot_general(dov, vp_ref[...], _NT, preferred_element_type=F32) - delta)
            dsc = pc * (lax.dot_general(dov, vc_ref[...], _NT, preferred_element_type=F32) - delta)
            dsp16, dsc16 = dsp.astype(BF16), dsc.astype(BF16)
            dq = jnp.dot(dsp16, kp_ref[...], preferred_element_type=F32)
            dq += jnp.dot(dsc16, kc_ref[...], preferred_element_type=F32)
            dq_ref[...] = dq * SCALE
            ds_ref[:, :WIN] += dsp
            ds_ref[:, WIN:] += dsc
            dk_ref[...] = ck_s[...] + lax.dot_general(dsp16, qv, _TN, preferred_element_type=F32) * SCALE
            dv_ref[...] = cv_s[...] + lax.dot_general(pp.astype(BF16), dov, _TN, preferred_element_type=F32)
            ck_s[...] = lax.dot_general(dsc16, qv, _TN, preferred_element_type=F32) * SCALE
            cv_s[...] = lax.dot_general(pc.astype(BF16), dov, _TN, preferred_element_type=F32)

        @pl.when(qi == nq)
        def _():
            dk_ref[...] = ck_s[...]
            dv_ref[...] = cv_s[...]

    cur = pl.BlockSpec((WIN, HEAD_DIM), lambda h, qi: (jnp.minimum(qi, nq - 1), h))
    prev = pl.BlockSpec((WIN, HEAD_DIM), lambda h, qi: (jnp.clip(qi - 1, 0, nq - 2), h))
    late = pl.BlockSpec((WIN, HEAD_DIM), lambda h, qi: (jnp.maximum(qi - 1, 0), h))
    return pl.pallas_call(
        body, name="chunk_bwd", grid=(H, nq + 1),
        in_specs=[cur, prev, cur, prev, cur, pl.BlockSpec((None, 1, ROLL_W), lambda h, qi: (h, 0, 0)), cur, cur],
        out_specs=[cur, late, late, pl.BlockSpec((None, WIN, 2 * WIN), lambda h, qi: (h, 0, 0))],
        out_shape=[jax.ShapeDtypeStruct((T, D), F32)] * 3 + [jax.ShapeDtypeStruct((H, WIN, 2 * WIN), F32)],
        scratch_shapes=[pltpu.VMEM((WIN, 2 * WIN), F32), pltpu.VMEM((WIN, HEAD_DIM), F32),
                        pltpu.VMEM((WIN, HEAD_DIM), F32)],
        compiler_params=_params("arbitrary", "arbitrary"),
    )(q, k, k, v, v, g_rows, do, o)


def _rel_rows(rel, onehot_t):
    H = rel.shape[0]

    def body(rel_ref, oh_ref, out_ref):
        out_ref[...] = jnp.dot(rel_ref[...], oh_ref[...], precision=HIGHEST, preferred_element_type=F32)

    return pl.pallas_call(body, name="rel_rows", out_shape=jax.ShapeDtypeStruct((H, ROLL_W), F32))(rel, onehot_t)


def _rel_grad(ds_sum, onehot):
    H = ds_sum.shape[0]

    def body(ds_ref, oh_ref, out_ref):
        y = jnp.zeros((CHUNK, ROLL_W), F32)
        pad = jnp.zeros((CHUNK, ROLL_W - 2 * WIN), F32)
        for a in range(N_PREV_CHUNKS):
            part = jnp.concatenate([ds_ref[a * CHUNK:(a + 1) * CHUNK, :], pad], axis=1)
            y = y + (part if a == 0 else pltpu.roll(part, ROLL_W - CHUNK * a, 1))
        rowb = lax.broadcasted_iota(jnp.int32, (CHUNK, ROLL_W), 0)
        for bit in range(CHUNK.bit_length() - 1):
            y = jnp.where((rowb >> bit) & 1 == 1, pltpu.roll(y, ROLL_W - (1 << bit), 1), y)
        diag = jnp.broadcast_to(jnp.sum(y, axis=0, keepdims=True), (8, ROLL_W))
        out_ref[...] = jnp.dot(diag, oh_ref[...], precision=HIGHEST, preferred_element_type=F32)

    return pl.pallas_call(
        body, name="rel_grad", grid=(H,),
        in_specs=[pl.BlockSpec((None, WIN, 2 * WIN), lambda h: (h, 0, 0)),
                  pl.BlockSpec((ROLL_W, N_REL_PAD), lambda h: (0, 0))],
        out_specs=pl.BlockSpec((None, 8, N_REL_PAD), lambda h: (h, 0, 0)),
        out_shape=jax.ShapeDtypeStruct((H, 8, N_REL_PAD), F32), compiler_params=_params("parallel"),
    )(ds_sum, onehot)


def _loss_head(y, target):
    T, D = y.shape
    bt = _tile(T, 256)

    def body(y_ref, t_ref, dy_ref, dy16_ref, loss_ref):
        @pl.when(pl.program_id(0) == 0)
        def _():
            loss_ref[...] = jnp.zeros_like(loss_ref)

        err = y_ref[...] - t_ref[...]
        dy = err * (1.0 / D)
        dy_ref[...] = dy
        dy16_ref[...] = dy.astype(BF16)
        loss_ref[...] += 0.5 * jnp.sum(jnp.mean(err * err, axis=-1, keepdims=True))

    row = pl.BlockSpec((bt, D), lambda i: (i, 0))
    one = pl.BlockSpec((8, LANES), lambda i: (0, 0))
    return pl.pallas_call(
        body, name="loss_head", grid=(T // bt,), in_specs=[row, row], out_specs=[row, row, one],
        out_shape=[jax.ShapeDtypeStruct((T, D), F32), jax.ShapeDtypeStruct((T, D), BF16),
                   jax.ShapeDtypeStruct((8, LANES), F32)],
        compiler_params=_params("arbitrary"),
    )(y, target)


def _adam_step(g, w_ref, m_ref, v_ref, g_ref, d_ref, nm_ref, nv_ref):
    nm = ADAM_B1 * m_ref[...] + (1.0 - ADAM_B1) * g
    nv = ADAM_B2 * v_ref[...] + (1.0 - ADAM_B2) * (g * g)
    m_hat = nm / (1.0 - ADAM_B1 ** ADAM_STEP)
    v_hat = nv / (1.0 - ADAM_B2 ** ADAM_STEP)
    g_ref[...] = g
    nm_ref[...] = nm
    nv_ref[...] = nv
    d_ref[...] = -ADAM_LR * (m_hat / (jnp.sqrt(v_hat) + ADAM_EPS) + ADAM_WD * w_ref[...])


def _adamw(name, w, m, v, g):
    def body(w_ref, m_ref, v_ref, gin_ref, g_ref, d_ref, nm_ref, nv_ref):
        _adam_step(gin_ref[...], w_ref, m_ref, v_ref, g_ref, d_ref, nm_ref, nv_ref)

    return pl.pallas_call(body, name=name, out_shape=[jax.ShapeDtypeStruct(w.shape, F32)] * 4)(w, m, v, g)


def _adamw_reduced(name, w, m, v, own, land, chip, row0=0, into=None):
    rows_all, C = w.shape
    _, R, _ = land.shape

    def body(chip_ref, w_ref, m_ref, v_ref, own_ref, land_ref, *rest):
        g = jnp.where(chip_ref[0] == 0, own_ref[...], land_ref[0]).astype(F32)
        for q in range(1, N_CHIP):
            g = g + jnp.where(chip_ref[0] == q, own_ref[...], land_ref[q]).astype(F32)
        _adam_step(g, w_ref, m_ref, v_ref, *rest[-4:])

    if R % 16 == 0:
        br = _tile(R, max(16, (128 * 1024) // C))
        first = row0 // br
        grid = (R // br,)
        wsp = pl.BlockSpec((br, C), lambda i, chip_ref: (first + i, 0))
        osp = pl.BlockSpec((None, br, C), lambda i, chip_ref: (chip_ref[0], i, 0))
        lsp = pl.BlockSpec((N_CHIP, br, C), lambda i, chip_ref: (0, i, 0))
    else:
        bc = _tile(C, 512)
        grid = (C // bc,)
        wsp = pl.BlockSpec((R, bc), lambda i, chip_ref: (0, i))
        osp = pl.BlockSpec((None, R, bc), lambda i, chip_ref: (chip_ref[0], 0, i))
        lsp = pl.BlockSpec((N_CHIP, R, bc), lambda i, chip_ref: (0, 0, i))
    extra = [] if into is None else list(into)
    return pl.pallas_call(
        body, name=name,
        grid_spec=pltpu.PrefetchScalarGridSpec(
            num_scalar_prefetch=1, grid=grid, in_specs=[wsp, wsp, wsp, osp, lsp] + [ANY] * len(extra),
            out_specs=[wsp] * 4),
        out_shape=[jax.ShapeDtypeStruct((rows_all, C), F32)] * 4,
        input_output_aliases={6 + j: j for j in range(len(extra))}, compiler_params=_params("parallel"),
    )(chip, w, m, v, own, land, *extra)


def _sum_parts(name, parts, out_dtype):
    P, R, C = parts.shape
    br = _tile(R, max(8, (256 * 1024) // C)) if R % 8 == 0 else R

    def body(p_ref, o_ref):
        g = p_ref[0].astype(F32)
        for i in range(1, P):
            g = g + p_ref[i].astype(F32)
        o_ref[...] = g.astype(out_dtype)

    return pl.pallas_call(
        body, name=name, grid=(R // br,),
        in_specs=[pl.BlockSpec((P, br, C), lambda i: (0, i, 0))], out_specs=pl.BlockSpec((br, C), lambda i: (i, 0)),
        out_shape=jax.ShapeDtypeStruct((R, C), out_dtype), compiler_params=_params("parallel"),
    )(parts)


def _pair_sum(name, g, recv, core):
    _, R, C = g.shape

    def body(core_ref, g_ref, r_ref, o_ref):
        o_ref[...] = (g_ref[...] + r_ref[...]).astype(BF16)

    if R % 16 == 0:
        br = _tile(R, max(16, (256 * 1024) // C))
        blk, n = (None, br, C), R // br
        mine = lambda q, i, core_ref: (2 * q + core_ref[0], i, 0)
        same = lambda q, i, core_ref: (q, i, 0)
    else:
        bc = _tile(C, 512)
        blk, n = (None, R, bc), C // bc
        mine = lambda q, i, core_ref: (2 * q + core_ref[0], 0, i)
        same = lambda q, i, core_ref: (q, 0, i)
    return pl.pallas_call(
        body, name=name,
        grid_spec=pltpu.PrefetchScalarGridSpec(
            num_scalar_prefetch=1, grid=(N_CHIP, n),
            in_specs=[pl.BlockSpec(blk, mine), pl.BlockSpec(blk, same)], out_specs=pl.BlockSpec(blk, same)),
        out_shape=jax.ShapeDtypeStruct((N_CHIP, R, C), BF16), compiler_params=_params("parallel", "parallel"),
    )(core, g, recv)


def _chips_of(x, y):
    return [(1 - x, y), (x, 1 - y), (1 - x, 1 - y)]


def _position():
    x, y, c = lax.axis_index("x"), lax.axis_index("y"), lax.axis_index("c")
    return x, y, c, _chips_of(x, y)


def _all_gather(name, xs):
    n = len(xs)

    def body(*refs):
        x_refs, out_refs = refs[:n], refs[n:2 * n]
        send_sems, recv_sems, local_sems = refs[2 * n:]
        x, y, c, chips = _position()
        me, sibling = (x, y, c), (x, y, 1 - c)

        def copy(a, k, block, to, src=None):
            px, py, pc = block
            slot = out_refs[a].at[4 * px + 2 * py + pc]
            return pltpu.make_async_remote_copy(
                src_ref=slot if src is None else src, dst_ref=slot,
                send_sem=send_sems.at[a, k], recv_sem=recv_sems.at[a, k], device_id=to, device_id_type=MESH)

        mine = [pltpu.make_async_copy(x_refs[a], out_refs[a].at[4 * x + 2 * y + c], local_sems.at[a]) for a in range(n)]
        for cp in mine:
            cp.start()
        first = []
        for a in range(n):
            first.append(copy(a, 0, me, sibling, src=x_refs[a]))
            first += [copy(a, 1 + j, me, (*chip, c), src=x_refs[a]) for j, chip in enumerate(chips)]
        for cp in first:
            cp.start()
        passed = []
        for j, chip in enumerate(chips):
            for a in range(n):
                copy(a, 1 + j, (*chip, c), me).wait_recv()
                fwd = copy(a, 4 + j, (*chip, c), sibling)
                fwd.start()
                passed.append(fwd)
        for a in range(n):
            copy(a, 0, sibling, me).wait_recv()
            for j, chip in enumerate(chips):
                copy(a, 4 + j, (*chip, 1 - c), me).wait_recv()
        for cp in first + passed:
            cp.wait_send()
        for cp in mine:
            cp.wait()

    return pl.pallas_call(
        body, name=name, in_specs=[ANY] * n, out_specs=[ANY] * n,
        out_shape=[jax.ShapeDtypeStruct((N_DEV,) + x.shape, x.dtype) for x in xs],
        scratch_shapes=[pltpu.SemaphoreType.DMA((n, 7)), pltpu.SemaphoreType.DMA((n, 7)),
                        pltpu.SemaphoreType.DMA((n,))],
    )(*xs)


HBM = pl.BlockSpec(memory_space=pltpu.HBM)
SEM = pl.BlockSpec(memory_space=pltpu.SEMAPHORE)
EFFECT = pltpu.SideEffectType.DATAFLOW_SIDE_EFFECTING


def _own_block_out(x, y, c):
    mine = 4 * x + 2 * y + c
    return [(mine, (x, y, 1 - c))] + [(mine, (cx, cy, c)) for cx, cy in _chips_of(x, y)]


def _own_blocks_in(x, y, c):
    return [(4 * x + 2 * y + 1 - c, (x, y, 1 - c))] + [(4 * cx + 2 * cy + c, (cx, cy, c)) for cx, cy in _chips_of(x, y)]


def _passed_on_out(x, y, c):
    return [(4 * cx + 2 * cy + c, (x, y, 1 - c)) for cx, cy in _chips_of(x, y)]


def _passed_on_in(x, y, c):
    return [(4 * cx + 2 * cy + 1 - c, (x, y, 1 - c)) for cx, cy in _chips_of(x, y)]


def _gather_start(name, lands, plan):
    n = len(lands)
    n_copies = len(plan(0, 0, 0))

    def body(*refs):
        land = refs[:n]
        send_sems, recv_sems = refs[n:2 * n], refs[2 * n:3 * n]
        token = refs[4 * n]
        x, y, c, _ = _position()
        for a in range(n):
            for k, (block, peer) in enumerate(plan(x, y, c)):
                pltpu.make_async_remote_copy(
                    src_ref=land[a].at[block], dst_ref=land[a].at[block], send_sem=send_sems[a].at[k],
                    recv_sem=recv_sems[a].at[k], device_id=peer, device_id_type=MESH).start()
        token[...] = jnp.zeros_like(token)

    res = pl.pallas_call(
        body, name=name, in_specs=[HBM] * n,
        out_specs=[SEM] * (2 * n) + [HBM] * n + [pl.BlockSpec(memory_space=pltpu.VMEM)],
        out_shape=[pltpu.SemaphoreType.DMA((n_copies,))] * (2 * n) + [pltpu.HBM(l.shape, l.dtype) for l in lands]
        + [jax.ShapeDtypeStruct((8, LANES), F32)],
        input_output_aliases={a: 2 * n + a for a in range(n)},
        compiler_params=pltpu.CompilerParams(has_side_effects=EFFECT),
    )(*[pltpu.with_memory_space_constraint(l, pltpu.HBM) for l in lands])
    return res[:n], res[n:2 * n], res[2 * n:3 * n], res[3 * n]


def _gather_wait(name, lands, send_sems, recv_sems, plan, after):
    n = len(lands)
    tail = [] if after is None else [after]

    def body(*refs):
        land = refs[:n]
        send, recv = refs[n:2 * n], refs[2 * n:3 * n]
        x, y, c, _ = _position()
        for a in range(n):
            for k, (block, peer) in enumerate(plan(x, y, c)):
                cp = pltpu.make_async_remote_copy(
                    src_ref=land[a].at[block], dst_ref=land[a].at[block], send_sem=send[a].at[k],
                    recv_sem=recv[a].at[k], device_id=peer, device_id_type=MESH)
                cp.wait_send()
                cp.wait_recv()

    return pl.pallas_call(
        body, name=name, in_specs=[HBM] * n + [SEM] * (2 * n) + [ANY] * len(tail), out_specs=[HBM] * n,
        out_shape=[pltpu.HBM(l.shape, l.dtype) for l in lands], input_output_aliases={a: a for a in range(n)},
        compiler_params=pltpu.CompilerParams(has_side_effects=EFFECT),
    )(*lands, *send_sems, *recv_sems, *tail)


def _split_start(name, arrays, lands, plan):
    n = len(arrays)
    n_copies = len(plan(0, 0, 0))

    def body(*refs):
        src, land = refs[:n], refs[n:2 * n]
        send_sems, recv_sems = refs[2 * n:3 * n], refs[3 * n:4 * n]
        token = refs[6 * n]
        x, y, c, _ = _position()
        for a in range(n):
            for k, (src_block, land_block, peer) in enumerate(plan(x, y, c)):
                pltpu.make_async_remote_copy(
                    src_ref=src[a].at[src_block], dst_ref=land[a].at[land_block], send_sem=send_sems[a].at[k],
                    recv_sem=recv_sems[a].at[k], device_id=peer, device_id_type=MESH).start()
        token[...] = jnp.zeros_like(token)

    both = list(arrays) + list(lands)
    res = pl.pallas_call(
        body, name=name, in_specs=[HBM] * (2 * n),
        out_specs=[SEM] * (2 * n) + [HBM] * (2 * n) + [pl.BlockSpec(memory_space=pltpu.VMEM)],
        out_shape=[pltpu.SemaphoreType.DMA((n_copies,))] * (2 * n) + [pltpu.HBM(b.shape, b.dtype) for b in both]
        + [jax.ShapeDtypeStruct((8, LANES), F32)],
        input_output_aliases={i: 2 * n + i for i in range(2 * n)},
        compiler_params=pltpu.CompilerParams(has_side_effects=EFFECT),
    )(*[pltpu.with_memory_space_constraint(b, pltpu.HBM) for b in both])
    return res[:n], res[n:2 * n], res[2 * n:3 * n], res[3 * n:4 * n], res[4 * n]


def _split_wait(name, arrays, lands, send_sems, recv_sems, plan, after):
    n = len(arrays)
    tail = [] if after is None else [after]

    def body(*refs):
        src, land = refs[:n], refs[n:2 * n]
        send, recv = refs[2 * n:3 * n], refs[3 * n:4 * n]
        x, y, c, _ = _position()
        for a in range(n):
            for k, (src_block, land_block, peer) in enumerate(plan(x, y, c)):
                cp = pltpu.make_async_remote_copy(
                    src_ref=src[a].at[src_block], dst_ref=land[a].at[land_block], send_sem=send[a].at[k],
                    recv_sem=recv[a].at[k], device_id=peer, device_id_type=MESH)
                cp.wait_send()
                cp.wait_recv()

    both = list(arrays) + list(lands)
    res = pl.pallas_call(
        body, name=name, in_specs=[HBM] * (2 * n) + [SEM] * (2 * n) + [ANY] * len(tail), out_specs=[HBM] * (2 * n),
        out_shape=[pltpu.HBM(b.shape, b.dtype) for b in both], input_output_aliases={i: i for i in range(2 * n)},
        compiler_params=pltpu.CompilerParams(has_side_effects=EFFECT),
    )(*both, *send_sems, *recv_sems, *tail)
    return res[:n], res[n:]


def _to_other_core(x, y, c):
    return [(2 * q + (1 - c), q, (x, y, 1 - c)) for q in range(N_CHIP)]


def _to_other_chips(x, y, c):
    return [(2 * cx + cy, 2 * x + y, (cx, cy, c)) for cx, cy in _chips_of(x, y)]


def _from_other_chips(x, y, c):
    return [(2 * x + y, 2 * cx + cy, (cx, cy, c)) for cx, cy in _chips_of(x, y)]


def _mlp_fwd(tag, h, g, w1g, w2):
    T, D = h.shape
    F = w2.shape[0]
    (n,) = _rmsnorm_fwd(f"mlp{tag}_norm", h, [g])
    a, hid = _mm(f"mlp{tag}_up", "nn", n, w1g, T, F, D, b_groups=N_DEV, out_dtypes=(BF16, BF16),
                 epilogue=_ep_relu_square)
    out = _mm(f"mlp{tag}_down", "nn", hid, w2, T, D, F, extras=(h,), epilogue=_ep_add)
    return out, (n, a, hid)


def _mlp_bwd(tag, h, g, w1g, w2, saved, dy, dy16, behind=()):
    T, D = h.shape
    F = w2.shape[0]
    n, a, hid = saved
    dw2 = _mm(f"mlp{tag}_dw2", "tn", hid, dy16, F, D, T, behind=behind)
    dpre = _mm(f"mlp{tag}_dpre", "nt", dy16, w2, T, F, D, out_dtypes=(BF16,), extras=(a,), epilogue=_ep_times_2a)
    dw1 = _mm(f"mlp{tag}_dw1", "tn", n, dpre, D, F, T, out_groups=N_DEV)
    dn = _mm(f"mlp{tag}_dn", "nt", dpre, w1g, T, D, F, b_groups=N_DEV)
    dh, dh16, dg = _rmsnorm_bwd(f"mlp{tag}_norm_bwd", h, g, dn, dy)
    return dh, dh16, dg, dw1, dw2


def _local_step(x, target, W, fox_blk, arrive=lambda group, after: {}, sync=lambda tag, after, G: ()):
    T, D = x.shape
    H = D // HEAD_DIM
    G = {}
    W = dict(W)

    (ua,) = _rmsnorm_fwd("a_norm", x, [W["a_norm_g"]])
    qkv = _mm("a_qkv", "nt", ua, W["a_w_in_t"], T, 3 * D, D)
    fz = _mm("a_fz", "nt", ua, W["a_w_f_t"], T, LANES, D)
    qa, ka, va = _head_post("a_heads", qkv, D, [(0, W["a_q_g"]), (1, W["a_k_g"]), (2, None)])
    ccol = _gates_fwd(fz, W["a_b_f"], H)
    crow = ccol[:, :H].T
    oa, lse = _fox_fwd(qa, ka, va, ccol, crow, fox_blk)
    W.update(arrive("a_out", oa))
    h1 = _mm("a_out", "nn", oa, W["a_w_out"], T, D, D, extras=(x,), epilogue=_ep_add)
    W.update(arrive("middle", h1))
    h2, mlp0 = _mlp_fwd("0", h1, W["mlp_norm_g0"], W["mlp_w1_0"], W["mlp_w2_0"])
    W.update(arrive("kv_q", h2))
    nkv, ub = _rmsnorm_fwd("kv_b_norm", h2, [W["kv_norm_g"], W["b_norm_g"]])
    kv = _mm("kv_proj", "nn", nkv, W["kv_w"], T, 2 * D, D, b_groups=N_DEV)
    ks, vs = _head_post("kv_heads", kv, D, [(0, W["kv_k_g"]), (1, None)])
    qb_pre = _mm("b_q", "nn", ub, W["b_w_q"], T, D, D)
    (qb,) = _head_post("b_heads", qb_pre, D, [(0, W["b_q_g"])])
    onehot = (jnp.asarray(_rel_index())[:, None] == jnp.arange(N_REL_PAD)[None, :]).astype(F32)
    rel_pad = jnp.pad(W["b_rel"], ((0, 0), (0, N_REL_PAD - N_REL)))
    g_rows = _rel_rows(rel_pad, onehot.T).reshape(H, 1, ROLL_W)
    ob = _chunk_fwd(qb, ks, vs, g_rows)
    W.update(arrive("late", ob))
    h3 = _mm("b_out", "nn", ob, W["b_w_out"], T, D, D, extras=(h2,), epilogue=_ep_add)
    h4, mlp1 = _mlp_fwd("1", h3, W["mlp_norm_g1"], W["mlp_w1_1"], W["mlp_w2_1"])
    dy, dy16, loss = _loss_head(h4, target)

    dh3, dh3_16, G["mlp_norm_g1"], G["mlp_w1_1"], G["mlp_w2_1"] = _mlp_bwd(
        "1", h3, W["mlp_norm_g1"], W["mlp_w1_1"], W["mlp_w2_1"], mlp1, dy, dy16)
    dob = _mm("b_out_dx", "nt", dh3_16, W["b_w_out"], T, D, D, out_dtypes=(BF16,), behind=sync("mlp1", dh3, G))
    G["b_w_out"] = _mm("b_out_dw", "tn", ob, dh3_16, D, D, T, behind=sync("b_out_dx", dob, G))
    dqb, dks, dvs, ds_sum = _chunk_bwd(qb, ks, vs, g_rows, dob, ob)
    G["b_rel"] = _rel_grad(ds_sum, onehot)[:, 0, :]
    dqb_pre, (G["b_q_g"],) = _head_post_bwd("b_heads_bwd", qb_pre, D, [(0, W["b_q_g"], dqb)])
    G["b_w_q"] = _mm("b_q_dw", "tn", ub, dqb_pre, D, D, T)
    dub = _mm("b_q_dx", "nt", dqb_pre, W["b_w_q"], T, D, D)
    dkv, (G["kv_k_g"],) = _head_post_bwd("kv_heads_bwd", kv, D, [(0, W["kv_k_g"], dks), (1, None, dvs)])
    G["kv_w"] = _mm("kv_dw", "tn", nkv, dkv, D, 2 * D, T, out_groups=N_DEV)
    dnkv = _mm("kv_dx", "nt", dkv, W["kv_w"], T, D, 2 * D, b_groups=N_DEV, behind=sync("kv_dw", G["kv_w"], G))
    dh2, _, G["b_norm_g"] = _rmsnorm_bwd("b_norm_bwd", h2, W["b_norm_g"], dub, dh3)
    dh2, dh2_16, G["kv_norm_g"] = _rmsnorm_bwd("kv_norm_bwd", h2, W["kv_norm_g"], dnkv, dh2)
    dh1, dh1_16, G["mlp_norm_g0"], G["mlp_w1_0"], G["mlp_w2_0"] = _mlp_bwd(
        "0", h1, W["mlp_norm_g0"], W["mlp_w1_0"], W["mlp_w2_0"], mlp0, dh2, dh2_16, behind=sync("dh2", dh2, G))
    G["a_w_out"] = _mm("a_out_dw", "tn", oa, dh1_16, D, D, T, behind=sync("mlp0", dh1, G))
    doa = _mm("a_out_dx", "nt", dh1_16, W["a_w_out"], T, D, D, out_dtypes=(BF16,))
    delta = _head_stat("fox_delta", doa, oa, behind=sync("a_out_dx", doa, G))
    dqa, dcq = _fox_bwd_dq(qa, ka, va, doa, ccol, crow, lse, delta, fox_blk)
    sync("fox_bwd_dq", dqa, G)
    dka, dva, dck = _fox_bwd_dkv(qa, ka, va, doa, ccol, crow, lse[:, :H].T, delta[:, :H].T, fox_blk)
    dfz, G["a_b_f"] = _gates_bwd(dcq, dck, fz, W["a_b_f"], H)
    dqkv, (G["a_q_g"], G["a_k_g"]) = _head_post_bwd(
        "a_heads_bwd", qkv, D, [(0, W["a_q_g"], dqa), (1, W["a_k_g"], dka), (2, None, dva)])
    G["a_w_qkv_t"] = _mm("a_qkv_dw", "tn", dqkv, ua, 3 * D, D, T)
    G["a_w_f_t"] = _mm("a_fz_dw", "tn", dfz, ua, LANES, D, T)
    dua = _mm("a_qkv_dx", "nn", dqkv, W["a_w_in_t"], T, D, 3 * D, behind=sync("a_in_dw", G["a_w_f_t"], G))
    dua = _mm("a_fz_dx", "nn", dfz, W["a_w_f_t"], T, D, LANES, extras=(dua,), epilogue=_ep_add,
              behind=sync("a_qkv_dx", dua, G))
    dx, _, G["a_norm_g"] = _rmsnorm_bwd("a_norm_bwd", x, W["a_norm_g"], dua, dh1)
    return loss, dx, G


def _pad_lanes(a):
    return jnp.pad(a, ((0, 0), (0, LANES - a.shape[1])))


def kernel(x, a_norm_g, a_w_in, a_b_f, a_q_g, a_k_g, a_w_out, mlp_norm_g, mlp_w1, mlp_w2, kv_norm_g, kv_w, kv_k_g, b_norm_g, b_w_q, b_q_g, b_rel, b_w_out, loss_target, m_a_norm_g, m_a_w_in, m_a_b_f, m_a_q_g, m_a_k_g, m_a_w_out, m_mlp_norm_g, m_mlp_w1, m_mlp_w2, m_kv_norm_g, m_kv_w, m_kv_k_g, m_b_norm_g, m_b_w_q, m_b_q_g, m_b_rel, m_b_w_out, v_a_norm_g, v_a_w_in, v_a_b_f, v_a_q_g, v_a_k_g, v_a_w_out, v_mlp_norm_g, v_mlp_w1, v_mlp_w2, v_kv_norm_g, v_kv_w, v_kv_k_g, v_b_norm_g, v_b_w_q, v_b_q_g, v_b_rel, v_b_w_out):
    T, D = x.shape[1], x.shape[2]
    H = D // HEAD_DIM
    F = mlp_w2.shape[1] * N_DEV
    me = 4 * lax.axis_index("x") + 2 * lax.axis_index("y") + lax.axis_index("c")
    core = lax.axis_index("c").astype(jnp.int32).reshape(1)
    n_in = a_w_in.shape[2]
    n_rel = b_rel.shape[2]

    small_in = jnp.concatenate([a_norm_g.reshape(1, -1), b_rel.reshape(1, -1)], axis=1)
    local = [jnp.swapaxes(a_w_in[0], 0, 1), a_w_out[0], mlp_w1[0], mlp_w2[0], kv_w, b_w_q[0], b_w_out[0],
             mlp_w1[1], mlp_w2[1]]
    def landing(w):
        return lax.dynamic_update_slice(lax.empty((N_DEV,) + w.shape, w.dtype), w[None], (me,) + (0,) * w.ndim)

    send1, recv1, lands, started = _gather_start(
        "gather_start_first", [landing(small_in), landing(local[0].astype(BF16))], _own_block_out)
    more = _gather_start("gather_start_rest", [landing((w + started[0, 0]).astype(BF16)) for w in local[1:]],
                         _own_block_out)
    send1, recv1, lands, sems2 = send1 + more[0], recv1 + more[1], list(lands + more[2]), {}

    def pass_on(tag, idx, after):
        got = _gather_wait(f"gather_wait_{tag}", [lands[i] for i in idx], [send1[i] for i in idx],
                           [recv1[i] for i in idx], _own_blocks_in, after)
        send, recv, got, _ = _gather_start(f"gather_pass_{tag}", got, _passed_on_out)
        for k, i in enumerate(idx):
            lands[i], sems2[i] = got[k], (send[k], recv[k])

    def landed(tag, idx, after):
        got = _gather_wait(f"gather_landed_{tag}", [lands[i] for i in idx], [sems2[i][0] for i in idx],
                           [sems2[i][1] for i in idx], _passed_on_in, after)
        for k, i in enumerate(idx):
            lands[i] = got[k]
        return got

    def arrive(group, after):
        if group == "first":
            pass_on("first", [0, 1], after)
            g_small, w_in_t = landed("first", [0, 1], None)
            w_in_t = w_in_t.reshape(N_DEV * n_in, D)
            rel = jnp.transpose(g_small[:, 0, D // N_DEV:].reshape(N_DEV, H, n_rel), (1, 0, 2))
            return {"a_norm_g": g_small[:, 0, :D // N_DEV].reshape(1, D), "b_rel": rel.reshape(H, N_DEV * n_rel),
                    "a_w_in_t": w_in_t, "a_w_f_t": jnp.pad(w_in_t[3 * D:], ((0, LANES - H), (0, 0)))}
        if group == "a_out":
            pass_on("early", [2, 3, 4], after)
            return {"a_w_out": landed("a_out", [2], None)[0].reshape(D, D)}
        if group == "middle":
            got = landed("middle", [3, 4], after)
            return {"mlp_w1_0": got[0], "mlp_w2_0": got[1].reshape(F, D)}
        if group == "kv_q":
            pass_on("late", [5, 6, 7, 8, 9], after)
            got = landed("kv_q", [5, 6], None)
            return {"kv_w": got[0], "b_w_q": got[1].reshape(D, D)}
        got = landed("late", [7, 8, 9], after)
        return {"b_w_out": got[0].reshape(D, D), "mlp_w1_1": got[1], "mlp_w2_1": got[2].reshape(F, D)}

    W = {
        "a_b_f": _pad_lanes(a_b_f), "a_q_g": a_q_g, "a_k_g": a_k_g,
        "mlp_norm_g0": mlp_norm_g[0:1], "mlp_norm_g1": mlp_norm_g[1:2],
        "kv_norm_g": kv_norm_g.reshape(1, D), "kv_k_g": kv_k_g.reshape(1, HEAD_DIM),
        "b_norm_g": b_norm_g, "b_q_g": b_q_g, **arrive("first", more[3]),
    }

    chip = (2 * lax.axis_index("x") + lax.axis_index("y")).astype(jnp.int32).reshape(1)
    flat = lambda a: a.reshape(-1, a.shape[-1])
    state = {"a_w_in": [jnp.swapaxes(a[0], 0, 1) for a in (a_w_in, m_a_w_in, v_a_w_in)]}
    for nm, w, m, v in [("a_w_out", a_w_out, m_a_w_out, v_a_w_out), ("kv_w", kv_w, m_kv_w, v_kv_w),
                        ("b_w_q", b_w_q, m_b_w_q, v_b_w_q), ("b_w_out", b_w_out, m_b_w_out, v_b_w_out),
                        ("mlp_w1", mlp_w1, m_mlp_w1, v_mlp_w1), ("mlp_w2", mlp_w2, m_mlp_w2, v_mlp_w2)]:
        state[nm] = [flat(w), flat(m), flat(v)]
    upd, flying = {}, {}

    def chunks(nm, G):
        if nm == "a_w_in":
            return jnp.concatenate([G["a_w_qkv_t"], G["a_w_f_t"][:H]], axis=0).reshape(N_DEV, n_in, D)
        g = G[nm]
        return g if g.ndim == 3 else g.reshape(N_DEV, g.shape[0] // N_DEV, g.shape[1])

    def to_core(tag, names, G):
        gs = [chunks(nm, G) for nm in names]
        lands = [lax.empty((N_CHIP,) + g.shape[1:], g.dtype) for g in gs]
        send, recv, gs, lands, token = _split_start(f"reduce_cores_start_{tag}", gs, lands, _to_other_core)
        flying[tag] = (names, send, recv, gs, lands)
        return (token,)

    def to_chips(tag, after):
        names, send, recv, gs, lands = flying[tag]
        gs, lands = _split_wait(f"reduce_cores_wait_{tag}", gs, lands, send, recv, _to_other_core, after)
        ps = [_pair_sum(f"pair_sum_{nm}", g, r, core) for nm, g, r in zip(names, gs, lands)]
        lands = [lax.empty(p.shape, p.dtype) for p in ps]
        send, recv, ps, lands, token = _split_start(f"reduce_chips_start_{tag}", ps, lands, _to_other_chips)
        flying[tag] = (names, send, recv, ps, lands)
        return (token,)

    def finish(tag, after):
        names, send, recv, ps, lands = flying.pop(tag)
        ps, lands = _split_wait(f"reduce_chips_wait_{tag}", ps, lands, send, recv, _from_other_chips, after)
        for nm, own, land in zip(names, ps, lands):
            if nm.startswith("mlp_w"):
                base, layer = nm[:-2], int(nm[-1])
                w, m, v = state[base]
                upd[base] = _adamw_reduced(f"adamw_{nm}", w, m, v, own, land, chip, row0=layer * (w.shape[0] // 2),
                                           into=upd.get(base))
            else:
                upd[nm] = _adamw_reduced(f"adamw_{nm}", *state[nm], own, land, chip)

    def sync(tag, after, G):
        if tag == "mlp1":
            return to_core("r1", ["mlp_w2_1", "mlp_w1_1"], G)
        if tag == "b_out_dx":
            return to_chips("r1", after)
        if tag == "kv_dw":
            return to_core("r2", ["b_w_out", "b_w_q", "kv_w"], G)
        if tag == "dh2":
            return to_chips("r2", after)
        if tag == "mlp0":
            return to_core("r3", ["mlp_w2_0", "mlp_w1_0"], G)
        if tag == "a_out_dx":
            return to_chips("r3", after)
        if tag == "fox_bwd_dq":
            finish("r1", after)
            finish("r2", after)
        if tag == "a_in_dw":
            return to_core("r4", ["a_w_out", "a_w_in"], G)
        if tag == "a_qkv_dx":
            return to_chips("r4", after)
        return ()

    loss_tile, dx, G = _local_step(x[0], loss_target[0], W, min(T, 512), arrive, sync)
    finish("r3", dx)

    def rows8(vecs):
        r = lax.broadcasted_iota(jnp.int32, (8, LANES), 0)
        tile = jnp.zeros((8, LANES), F32)
        for i, vec in enumerate(vecs):
            tile = jnp.where(r == i, vec, tile)
        return tile

    rows = lambda a: a.reshape(-1, LANES)
    packed = [rows(G["a_norm_g"]), rows(G["mlp_norm_g0"]), rows(G["mlp_norm_g1"]), rows(G["kv_norm_g"]),
              rows(G["b_norm_g"]), rows(G["b_rel"]),
              rows8([G["a_b_f"], G["a_q_g"], G["a_k_g"], G["kv_k_g"], G["b_q_g"], loss_tile[0:1]])]
    sizes = [p.shape[0] for p in packed]
    (all_small,) = _all_gather("gather_small_grads", [jnp.concatenate(packed, axis=0)])
    total = _sum_parts("sum_small_grads", all_small, F32)
    offs = np.concatenate([[0], np.cumsum(sizes)])
    s = [total[offs[i]:offs[i + 1]] for i in range(len(sizes))]
    loss = s[6][5, 0]
    g_a_norm = lax.dynamic_slice(s[0].reshape(1, D), (0, me * (D // N_DEV)), (1, D // N_DEV))
    g_rel = lax.dynamic_slice(s[5].reshape(H, N_REL_PAD), (0, me * n_rel), (H, n_rel))
    small = {
        "a_norm_g": (a_norm_g, m_a_norm_g, v_a_norm_g, g_a_norm),
        "a_b_f": (a_b_f, m_a_b_f, v_a_b_f, s[6][0:1, :H]),
        "a_q_g": (a_q_g, m_a_q_g, v_a_q_g, s[6][1:2]),
        "a_k_g": (a_k_g, m_a_k_g, v_a_k_g, s[6][2:3]),
        "mlp_norm_g": (mlp_norm_g, m_mlp_norm_g, v_mlp_norm_g,
                       jnp.concatenate([s[1].reshape(1, D), s[2].reshape(1, D)], axis=0)),
        "kv_norm_g": (kv_norm_g.reshape(1, D), m_kv_norm_g.reshape(1, D), v_kv_norm_g.reshape(1, D), s[3].reshape(1, D)),
        "kv_k_g": (kv_k_g.reshape(1, HEAD_DIM), m_kv_k_g.reshape(1, HEAD_DIM), v_kv_k_g.reshape(1, HEAD_DIM),
                   s[6][3:4]),
        "b_norm_g": (b_norm_g, m_b_norm_g, v_b_norm_g, s[4].reshape(1, D)),
        "b_q_g": (b_q_g, m_b_q_g, v_b_q_g, s[6][4:5]),
        "b_rel": (b_rel[0], m_b_rel[0], v_b_rel[0], g_rel),
    }
    for nm, (w, m, v, g) in small.items():
        upd[nm] = _adamw(f"adamw_{nm}", w, m, v, g)
    upd["kv_norm_g"] = [a.reshape(D) for a in upd["kv_norm_g"]]
    upd["kv_k_g"] = [a.reshape(HEAD_DIM) for a in upd["kv_k_g"]]
    upd["b_rel"] = [a[None] for a in upd["b_rel"]]
    finish("r4", total)
    upd["a_w_in"] = [jnp.swapaxes(o, 0, 1)[None] for o in upd["a_w_in"]]
    for nm, w in [("a_w_out", a_w_out), ("kv_w", kv_w), ("b_w_q", b_w_q), ("b_w_out", b_w_out), ("mlp_w1", mlp_w1),
                  ("mlp_w2", mlp_w2)]:
        upd[nm] = [o.reshape(w.shape) for o in upd[nm]]

    order = ["a_norm_g", "a_w_in", "a_b_f", "a_q_g", "a_k_g", "a_w_out", "mlp_norm_g", "mlp_w1", "mlp_w2",
             "kv_norm_g", "kv_w", "kv_k_g", "b_norm_g", "b_w_q", "b_q_g", "b_rel", "b_w_out"]
    return (loss, dx[None], *[upd[n][0] for n in order], *[upd[n][1] for n in order],
            *[upd[n][2] for n in order], *[upd[n][3] for n in order])
```

```python
import functools

import numpy as np
import jax
import jax.numpy as jnp
from jax import lax
from jax.experimental import pallas as pl
from jax.experimental.pallas import tpu as pltpu

F32 = jnp.float32
BF16 = jnp.bfloat16
HIGHEST = lax.Precision.HIGHEST
MESH = pl.DeviceIdType.MESH
ANY = pl.BlockSpec(memory_space=pl.ANY)

N_DEV = 8
N_CHIP = 4
LANES = 128
HEAD_DIM = 128
CHUNK = 64
N_PREV_CHUNKS = 8
REL_CLIP = 256
N_REL = REL_CLIP + CHUNK
N_REL_PAD = 384
WIN = N_PREV_CHUNKS * CHUNK
ROLL_W = 4 * WIN
EPS = 1e-6
NEG = -1e30
SCALE = HEAD_DIM ** -0.5
ADAM_LR, ADAM_B1, ADAM_B2, ADAM_EPS, ADAM_WD, ADAM_STEP = 0.001, 0.9, 0.999, 1e-08, 0.01, 10
V7X_VMEM_BYTES = 64 * 1024 * 1024
VMEM_LIMIT = V7X_VMEM_BYTES * 3 // 4

_NN = (((1,), (0,)), ((), ()))
_NT = (((1,), (1,)), ((), ()))
_TN = (((0,), (0,)), ((), ()))


def _params(*sem):
    return pltpu.CompilerParams(dimension_semantics=sem or None, vmem_limit_bytes=VMEM_LIMIT)


def _tile(n, pref):
    t = n
    while t > pref and t % 2 == 0:
        t //= 2
    return t


def _ep_store(acc, ex, outs):
    outs[0][...] = acc.astype(outs[0].dtype)


def _ep_add(acc, ex, outs):
    outs[0][...] = (acc + ex[0][...]).astype(outs[0].dtype)


def _ep_relu_square(acc, ex, outs):
    a = jnp.maximum(acc, 0.0)
    outs[0][...] = a.astype(outs[0].dtype)
    outs[1][...] = (a * a).astype(outs[1].dtype)


def _ep_times_2a(acc, ex, outs):
    outs[0][...] = (acc * (2.0 * ex[0][...].astype(F32))).astype(outs[0].dtype)


def _mm(name, kind, a, b, M, N, K, *, b_groups=0, out_groups=0, out_dtypes=(F32,), extras=(),
        epilogue=_ep_store, tm=1024, tn=1024, tk=2048, behind=()):
    tm, tn, tk = _tile(M, tm), _tile(N, tn), _tile(K, tk)
    if b_groups:
        cg = b.shape[-1]
        if kind == "nn":
            tn = min(tn, cg)
        else:
            tk = min(tk, cg)
    if out_groups:
        tn = min(tn, N // out_groups)
    grid = (M // tm, N // tn, K // tk)
    nk = grid[2]
    if kind == "nn":
        dims = _NN
        a_spec = pl.BlockSpec((tm, tk), lambda i, j, k: (i, k))
        if b_groups:
            npg = cg // tn
            b_spec = pl.BlockSpec((None, tk, tn), lambda i, j, k: (j // npg, k, j % npg))
        else:
            b_spec = pl.BlockSpec((tk, tn), lambda i, j, k: (k, j))
    elif kind == "nt":
        dims = _NT
        a_spec = pl.BlockSpec((tm, tk), lambda i, j, k: (i, k))
        if b_groups:
            npg = cg // tk
            b_spec = pl.BlockSpec((None, tn, tk), lambda i, j, k: (k // npg, j, k % npg))
        else:
            b_spec = pl.BlockSpec((tn, tk), lambda i, j, k: (j, k))
    else:
        dims = _TN
        a_spec = pl.BlockSpec((tk, tm), lambda i, j, k: (k, i))
        b_spec = pl.BlockSpec((tk, tn), lambda i, j, k: (k, j))
    tile_spec = pl.BlockSpec((tm, tn), lambda i, j, k: (i, j))
    if out_groups:
        ng = (N // out_groups) // tn
        out_spec = pl.BlockSpec((None, tm, tn), lambda i, j, k: (j // ng, i, j % ng))
        out_shape = [jax.ShapeDtypeStruct((out_groups, M, N // out_groups), d) for d in out_dtypes]
    else:
        out_spec = tile_spec
        out_shape = [jax.ShapeDtypeStruct((M, N), d) for d in out_dtypes]
    n_ex, n_out = len(extras), len(out_dtypes)

    def body(a_ref, b_ref, *rest):
        ex, outs, acc = rest[:n_ex], rest[-1 - n_out:-1], rest[-1]
        k = pl.program_id(2)

        @pl.when(k == 0)
        def _():
            acc[...] = jnp.zeros_like(acc)

        acc[...] += lax.dot_general(a_ref[...].astype(BF16), b_ref[...].astype(BF16), dims,
                                    preferred_element_type=F32)

        @pl.when(k == nk - 1)
        def _():
            epilogue(acc[...], ex, outs)

    res = pl.pallas_call(
        body, name=name, grid=grid,
        in_specs=[a_spec, b_spec] + [tile_spec] * n_ex + [ANY] * len(behind),
        out_specs=[out_spec] * n_out, out_shape=out_shape,
        scratch_shapes=[pltpu.VMEM((tm, tn), F32)],
        compiler_params=_params("parallel", "parallel", "arbitrary"),
    )(a, b, *extras, *behind)
    return res[0] if n_out == 1 else res


def _rmsnorm_fwd(name, x, gains):
    T, D = x.shape
    bt = _tile(T, 256)
    n = len(gains)

    def body(x_ref, *rest):
        xv = x_ref[...]
        y = xv * lax.rsqrt(jnp.mean(xv * xv, axis=-1, keepdims=True) + EPS)
        for g_ref, o_ref in zip(rest[:n], rest[n:]):
            o_ref[...] = (y * g_ref[...]).astype(BF16)

    row = pl.BlockSpec((bt, D), lambda i: (i, 0))
    gsp = pl.BlockSpec((1, D), lambda i: (0, 0))
    return pl.pallas_call(
        body, name=name, grid=(T // bt,), in_specs=[row] + [gsp] * n, out_specs=[row] * n,
        out_shape=[jax.ShapeDtypeStruct((T, D), BF16)] * n, compiler_params=_params("parallel"),
    )(x, *gains)


def _rmsnorm_bwd(name, x, g, dn, res):
    T, D = x.shape
    bt = _tile(T, 256)

    def body(x_ref, g_ref, dn_ref, res_ref, dx_ref, dx16_ref, dg_ref):
        @pl.when(pl.program_id(0) == 0)
        def _():
            dg_ref[...] = jnp.zeros_like(dg_ref)

        xv, dnv = x_ref[...], dn_ref[...]
        r = lax.rsqrt(jnp.mean(xv * xv, axis=-1, keepdims=True) + EPS)
        xhat = xv * r
        dyg = dnv * g_ref[...]
        dx = res_ref[...] + r * (dyg - xhat * jnp.mean(dyg * xhat, axis=-1, keepdims=True))
        dx_ref[...] = dx
        dx16_ref[...] = dx.astype(BF16)
        dg_ref[...] += jnp.sum(dnv * xhat, axis=0, keepdims=True)

    row = pl.BlockSpec((bt, D), lambda i: (i, 0))
    gsp = pl.BlockSpec((1, D), lambda i: (0, 0))
    return pl.pallas_call(
        body, name=name, grid=(T // bt,), in_specs=[row, gsp, row, row], out_specs=[row, row, gsp],
        out_shape=[jax.ShapeDtypeStruct((T, D), F32), jax.ShapeDtypeStruct((T, D), BF16),
                   jax.ShapeDtypeStruct((1, D), F32)],
        compiler_params=_params("arbitrary"),
    )(x, g, dn, res)


def _head_post(name, x, D, items):
    T = x.shape[0]
    H = D // HEAD_DIM
    bt = _tile(T, 256)
    gains = [g for _, g in items if g is not None]
    n_it, n_g = len(items), len(gains)

    def body(*refs):
        x_refs, g_refs, o_refs = refs[:n_it], refs[n_it:n_it + n_g], refs[n_it + n_g:]
        gi = 0
        for it, (_, g) in enumerate(items):
            if g is None:
                o_refs[it][...] = x_refs[it][...].astype(BF16)
                continue
            gv = g_refs[gi][...]
            gi += 1
            for h in range(H):
                sl = slice(h * HEAD_DIM, (h + 1) * HEAD_DIM)
                xs = x_refs[it][:, sl]
                r = lax.rsqrt(jnp.mean(xs * xs, axis=-1, keepdims=True) + EPS)
                o_refs[it][:, sl] = (xs * r * gv).astype(BF16)

    in_specs = [pl.BlockSpec((bt, D), functools.partial(lambda cb, i: (i, cb), cb)) for cb, _ in items]
    in_specs += [pl.BlockSpec((1, HEAD_DIM), lambda i: (0, 0))] * n_g
    row = pl.BlockSpec((bt, D), lambda i: (i, 0))
    res = pl.pallas_call(
        body, name=name, grid=(T // bt,), in_specs=in_specs, out_specs=[row] * n_it,
        out_shape=[jax.ShapeDtypeStruct((T, D), BF16)] * n_it, compiler_params=_params("parallel"),
    )(*([x] * n_it), *gains)
    return res


def _head_post_bwd(name, x, D, items):
    T = x.shape[0]
    H = D // HEAD_DIM
    bt = _tile(T, 128)
    gains = [g for _, g, _ in items if g is not None]
    n_it, n_g = len(items), len(gains)

    def body(*refs):
        x_refs, dy_refs = refs[:n_it], refs[n_it:2 * n_it]
        g_refs = refs[2 * n_it:2 * n_it + n_g]
        dx_ref = refs[2 * n_it + n_g]
        dg_refs = refs[2 * n_it + n_g + 1:]

        @pl.when(pl.program_id(0) == 0)
        def _():
            for r in dg_refs:
                r[...] = jnp.zeros_like(r)

        gi = 0
        for it, (_, g, _) in enumerate(items):
            if g is None:
                dx_ref[:, it * D:(it + 1) * D] = dy_refs[it][...].astype(BF16)
                continue
            gv = g_refs[gi][...]
            dg = jnp.zeros((1, HEAD_DIM), F32)
            for h in range(H):
                sl = slice(h * HEAD_DIM, (h + 1) * HEAD_DIM)
                xs, dy = x_refs[it][:, sl], dy_refs[it][:, sl]
                r = lax.rsqrt(jnp.mean(xs * xs, axis=-1, keepdims=True) + EPS)
                xhat = xs * r
                dyg = dy * gv
                dx = r * (dyg - xhat * jnp.mean(dyg * xhat, axis=-1, keepdims=True))
                dx_ref[:, it * D + h * HEAD_DIM:it * D + (h + 1) * HEAD_DIM] = dx.astype(BF16)
                dg = dg + jnp.sum(dy * xhat, axis=0, keepdims=True)
            dg_refs[gi][...] += dg
            gi += 1

    in_specs = [pl.BlockSpec((bt, D), functools.partial(lambda cb, i: (i, cb), cb)) for cb, _, _ in items]
    in_specs += [pl.BlockSpec((bt, D), lambda i: (i, 0))] * n_it
    gsp = pl.BlockSpec((1, HEAD_DIM), lambda i: (0, 0))
    in_specs += [gsp] * n_g
    res = pl.pallas_call(
        body, name=name, grid=(T // bt,), in_specs=in_specs,
        out_specs=[pl.BlockSpec((bt, n_it * D), lambda i: (i, 0))] + [gsp] * n_g,
        out_shape=[jax.ShapeDtypeStruct((T, n_it * D), BF16)] + [jax.ShapeDtypeStruct((1, HEAD_DIM), F32)] * n_g,
        compiler_params=_params("arbitrary"),
    )(*([x] * n_it), *[dy for _, _, dy in items], *gains)
    return res[0], list(res[1:])


def _gates_fwd(fz, bf, H):
    T = fz.shape[0]
    bt = _tile(T, 512)

    def body(fz_ref, bf_ref, c_ref, carry):
        @pl.when(pl.program_id(0) == 0)
        def _():
            carry[...] = jnp.zeros_like(carry)

        z = fz_ref[...] + bf_ref[...]
        logf = jnp.minimum(z, 0.0) - jnp.log(1.0 + jnp.exp(-jnp.abs(z)))
        lane = lax.broadcasted_iota(jnp.int32, (bt, LANES), 1)
        logf = jnp.where(lane < H, logf, 0.0)
        tri = (lax.broadcasted_iota(jnp.int32, (bt, bt), 0) >= lax.broadcasted_iota(jnp.int32, (bt, bt), 1))
        c = jnp.dot(tri.astype(F32), logf, precision=HIGHEST, preferred_element_type=F32) + carry[...]
        c_ref[...] = c
        carry[...] = c[bt - 1:bt, :]

    row = pl.BlockSpec((bt, LANES), lambda i: (i, 0))
    return pl.pallas_call(
        body, name="fox_gates_fwd", grid=(T // bt,),
        in_specs=[row, pl.BlockSpec((1, LANES), lambda i: (0, 0))], out_specs=row,
        out_shape=jax.ShapeDtypeStruct((T, LANES), F32), scratch_shapes=[pltpu.VMEM((1, LANES), F32)],
        compiler_params=_params("arbitrary"),
    )(fz, bf)


def _gates_bwd(dcq, dck, fz, bf, H):
    T = fz.shape[0]
    bt = _tile(T, 512)
    nb = T // bt

    def body(dcq_ref, dck_ref, fz_ref, bf_ref, dfz_ref, dbf_ref, carry):
        @pl.when(pl.program_id(0) == 0)
        def _():
            carry[...] = jnp.zeros_like(carry)
            dbf_ref[...] = jnp.zeros_like(dbf_ref)

        tri = (lax.broadcasted_iota(jnp.int32, (bt, bt), 0) <= lax.broadcasted_iota(jnp.int32, (bt, bt), 1))
        dc = dcq_ref[...] + dck_ref[...]
        dlogf = jnp.dot(tri.astype(F32), dc, precision=HIGHEST, preferred_element_type=F32) + carry[...]
        carry[...] = dlogf[0:1, :]
        z = fz_ref[...] + bf_ref[...]
        lane = lax.broadcasted_iota(jnp.int32, (bt, LANES), 1)
        dfz = jnp.where(lane < H, dlogf / (1.0 + jnp.exp(z)), 0.0)
        dfz_ref[...] = dfz
        dbf_ref[...] += jnp.sum(dfz, axis=0, keepdims=True)

    row = pl.BlockSpec((bt, LANES), lambda i: (nb - 1 - i, 0))
    one = pl.BlockSpec((1, LANES), lambda i: (0, 0))
    return pl.pallas_call(
        body, name="fox_gates_bwd", grid=(nb,), in_specs=[row, row, row, one], out_specs=[row, one],
        out_shape=[jax.ShapeDtypeStruct((T, LANES), F32), jax.ShapeDtypeStruct((1, LANES), F32)],
        scratch_shapes=[pltpu.VMEM((1, LANES), F32)], compiler_params=_params("arbitrary"),
    )(dcq, dck, fz, bf)


def _lane_pick(ref_value, h):
    lane = lax.broadcasted_iota(jnp.int32, (1, LANES), 1)
    return jnp.sum(jnp.where(lane == h, ref_value, 0.0), axis=1, keepdims=True)


def _lane_put(old, h, col):
    lane = lax.broadcasted_iota(jnp.int32, (1, LANES), 1)
    return jnp.where(lane == h, col, old)


FOX_HEADS = 4


def _causal(rows, cols, row0=0, transposed=False):
    row = row0 + lax.broadcasted_iota(jnp.int32, (rows, cols), 0)
    col = lax.broadcasted_iota(jnp.int32, (rows, cols), 1)
    return col >= row if transposed else row >= col


def _triangle(nb, groups, keys_outermost=False):
    if keys_outermost:
        steps = [(qi, g, kj) for kj in range(nb) for g in range(groups) for qi in range(kj, nb)]
    else:
        steps = [(qi, g, kj) for qi in range(nb) for g in range(groups) for kj in range(qi + 1)]
    return [jnp.asarray(np.array(col, np.int32)) for col in zip(*steps)]


def _fox_fwd(q, k, v, ccol, crow, blk):
    T, D = q.shape
    H = D // HEAD_DIM
    nb = T // blk
    steps = _triangle(nb, H // FOX_HEADS)

    def body(qi_ref, hg_ref, kj_ref, q_ref, k_ref, v_ref, ccol_ref, crow_ref, o_ref, lse_ref, m_s, l_s, acc_s):
        step_id = pl.program_id(0)
        qi, hg, kj = qi_ref[step_id], hg_ref[step_id], kj_ref[step_id]

        @pl.when(kj == 0)
        def _():
            m_s[...] = jnp.full_like(m_s, NEG)
            l_s[...] = jnp.zeros_like(l_s)
            acc_s[...] = jnp.zeros_like(acc_s)

        @pl.when((kj == 0) & (hg == 0))
        def _():
            lse_ref[...] = jnp.zeros_like(lse_ref)

        def step(diagonal):
            heads = range(FOX_HEADS)
            sls = [slice(hh * HEAD_DIM, (hh + 1) * HEAD_DIM) for hh in heads]
            m_old = [m_s[hh] for hh in heads]
            l_old = [l_s[hh] for hh in heads]
            acc_old = [acc_s[:, sl] for sl in sls]
            s = []
            for hh in heads:
                h = hg * FOX_HEADS + hh
                bias = _lane_pick(ccol_ref[...], h) - crow_ref[pl.ds(h, 1), :]
                sc = lax.dot_general(q_ref[:, sls[hh]], k_ref[:, sls[hh]], _NT, preferred_element_type=F32) * SCALE + bias
                s.append(jnp.where(_causal(blk, blk), sc, NEG) if diagonal else sc)
            m_new = [jnp.maximum(m_old[hh], jnp.max(s[hh], axis=1, keepdims=True)) for hh in heads]
            p = [jnp.exp(s[hh] - m_new[hh]) for hh in heads]
            alpha = [jnp.exp(m_old[hh] - m_new[hh]) for hh in heads]
            pv = [jnp.dot(p[hh].astype(BF16), v_ref[:, sls[hh]], preferred_element_type=F32) for hh in heads]
            for hh in heads:
                l_s[hh] = alpha[hh] * l_old[hh] + jnp.sum(p[hh], axis=1, keepdims=True)
                acc_s[:, sls[hh]] = alpha[hh] * acc_old[hh] + pv[hh]
                m_s[hh] = m_new[hh]

        @pl.when(kj < qi)
        def _():
            step(False)

        @pl.when(kj == qi)
        def _():
            step(True)
            for hh in range(FOX_HEADS):
                sl = slice(hh * HEAD_DIM, (hh + 1) * HEAD_DIM)
                o_ref[:, sl] = (acc_s[:, sl] / l_s[hh]).astype(BF16)
                lse_ref[...] = _lane_put(lse_ref[...], hg * FOX_HEADS + hh, m_s[hh] + jnp.log(l_s[hh]))

    wide = FOX_HEADS * HEAD_DIM
    qsp = pl.BlockSpec((blk, wide), lambda s, qi, hg, kj: (qi[s], hg[s]))
    ksp = pl.BlockSpec((blk, wide), lambda s, qi, hg, kj: (kj[s], hg[s]))
    stat = pl.BlockSpec((blk, LANES), lambda s, qi, hg, kj: (qi[s], 0))
    return pl.pallas_call(
        body, name="fox_fwd",
        grid_spec=pltpu.PrefetchScalarGridSpec(
            num_scalar_prefetch=3, grid=(steps[0].shape[0],),
            in_specs=[qsp, ksp, ksp, stat, pl.BlockSpec((H, blk), lambda s, qi, hg, kj: (0, kj[s]))],
            out_specs=[qsp, stat],
            scratch_shapes=[pltpu.VMEM((FOX_HEADS, blk, 1), F32), pltpu.VMEM((FOX_HEADS, blk, 1), F32),
                            pltpu.VMEM((blk, wide), F32)]),
        out_shape=[jax.ShapeDtypeStruct((T, D), BF16), jax.ShapeDtypeStruct((T, LANES), F32)],
        compiler_params=_params("arbitrary"),
    )(*steps, q, k, v, ccol, crow)


def _head_stat(name, a, b, behind=()):
    T, D = a.shape
    H = D // HEAD_DIM
    bt = _tile(T, 256)

    def body(a_ref, b_ref, *rest):
        o_ref = rest[-1]
        out = jnp.zeros((bt, LANES), F32)
        for h in range(H):
            sl = slice(h * HEAD_DIM, (h + 1) * HEAD_DIM)
            prod = a_ref[:, sl].astype(F32) * b_ref[:, sl].astype(F32)
            out = _lane_put(out, h, jnp.sum(prod, axis=1, keepdims=True))
        o_ref[...] = out

    row = pl.BlockSpec((bt, D), lambda i: (i, 0))
    return pl.pallas_call(
        body, name=name, grid=(T // bt,), in_specs=[row, row] + [ANY] * len(behind),
        out_specs=pl.BlockSpec((bt, LANES), lambda i: (i, 0)),
        out_shape=jax.ShapeDtypeStruct((T, LANES), F32), compiler_params=_params("parallel"),
    )(a, b, *behind)


def _fox_bwd_dq(q, k, v, do, ccol, crow, lse, delta, blk):
    T, D = q.shape
    H = D // HEAD_DIM
    nb = T // blk

    steps = _triangle(nb, H // FOX_HEADS)

    def body(qi_ref, hg_ref, kj_ref, q_ref, k_ref, v_ref, do_ref, ccol_ref, crow_ref, lse_ref, delta_ref,
             dq_ref, dc_ref, acc_s, dc_s):
        step_id = pl.program_id(0)
        qi, hg, kj = qi_ref[step_id], hg_ref[step_id], kj_ref[step_id]

        @pl.when(kj == 0)
        def _():
            acc_s[...] = jnp.zeros_like(acc_s)
            dc_s[...] = jnp.zeros_like(dc_s)

        @pl.when((kj == 0) & (hg == 0))
        def _():
            dc_ref[...] = jnp.zeros_like(dc_ref)

        def step(diagonal):
            heads = range(FOX_HEADS)
            sls = [slice(hh * HEAD_DIM, (hh + 1) * HEAD_DIM) for hh in heads]
            p, dp = [], []
            for hh in heads:
                h = hg * FOX_HEADS + hh
                bias = _lane_pick(ccol_ref[...] - lse_ref[...], h) - crow_ref[pl.ds(h, 1), :]
                sc = lax.dot_general(q_ref[:, sls[hh]], k_ref[:, sls[hh]], _NT, preferred_element_type=F32) * SCALE + bias
                if diagonal:
                    sc = jnp.where(_causal(blk, blk), sc, NEG)
                p.append(jnp.exp(sc))
                dp.append(lax.dot_general(do_ref[:, sls[hh]], v_ref[:, sls[hh]], _NT, preferred_element_type=F32)
                          - _lane_pick(delta_ref[...], h))
            ds = [p[hh] * dp[hh] for hh in heads]
            for hh in heads:
                acc_s[:, sls[hh]] += jnp.dot(ds[hh].astype(BF16), k_ref[:, sls[hh]], preferred_element_type=F32)
                dc_s[hh] += jnp.sum(ds[hh], axis=1, keepdims=True)

        @pl.when(kj < qi)
        def _():
            step(False)

        @pl.when(kj == qi)
        def _():
            step(True)
            dq_ref[...] = acc_s[...] * SCALE
            for hh in range(FOX_HEADS):
                dc_ref[...] = _lane_put(dc_ref[...], hg * FOX_HEADS + hh, dc_s[hh])

    wide = FOX_HEADS * HEAD_DIM
    qsp = pl.BlockSpec((blk, wide), lambda s, qi, hg, kj: (qi[s], hg[s]))
    ksp = pl.BlockSpec((blk, wide), lambda s, qi, hg, kj: (kj[s], hg[s]))
    stat = pl.BlockSpec((blk, LANES), lambda s, qi, hg, kj: (qi[s], 0))
    return pl.pallas_call(
        body, name="fox_bwd_dq",
        grid_spec=pltpu.PrefetchScalarGridSpec(
            num_scalar_prefetch=3, grid=(steps[0].shape[0],),
            in_specs=[qsp, ksp, ksp, qsp, stat, pl.BlockSpec((H, blk), lambda s, qi, hg, kj: (0, kj[s])), stat, stat],
            out_specs=[qsp, stat],
            scratch_shapes=[pltpu.VMEM((blk, wide), F32), pltpu.VMEM((FOX_HEADS, blk, 1), F32)]),
        out_shape=[jax.ShapeDtypeStruct((T, D), F32), jax.ShapeDtypeStruct((T, LANES), F32)],
        compiler_params=_params("arbitrary"),
    )(*steps, q, k, v, do, ccol, crow, lse, delta)


def _fox_bwd_dkv(q, k, v, do, ccol, crow, lse_row, delta_row, blk, behind=()):
    T, D = q.shape
    H = D // HEAD_DIM
    nb = T // blk

    steps = _triangle(nb, H // FOX_HEADS, keys_outermost=True)

    def body(qi_ref, hg_ref, kj_ref, q_ref, k_ref, v_ref, do_ref, ccol_ref, crow_ref, lse_ref, delta_ref, *rest):
        dk_ref, dv_ref, dc_ref, dk_s, dv_s, dc_s = rest[-6:]
        step_id = pl.program_id(0)
        qi, hg, kj = qi_ref[step_id], hg_ref[step_id], kj_ref[step_id]

        @pl.when(qi == kj)
        def _():
            dk_s[...] = jnp.zeros_like(dk_s)
            dv_s[...] = jnp.zeros_like(dv_s)
            dc_s[...] = jnp.zeros_like(dc_s)

        @pl.when((qi == kj) & (hg == 0))
        def _():
            dc_ref[...] = jnp.zeros_like(dc_ref)

        def step(diagonal):
            heads = range(FOX_HEADS)
            sls = [slice(hh * HEAD_DIM, (hh + 1) * HEAD_DIM) for hh in heads]
            pt, dpt = [], []
            for hh in heads:
                h = hg * FOX_HEADS + hh
                bias = (crow_ref[pl.ds(h, 1), :] - lse_ref[pl.ds(h, 1), :]) - _lane_pick(ccol_ref[...], h)
                st = lax.dot_general(k_ref[:, sls[hh]], q_ref[:, sls[hh]], _NT, preferred_element_type=F32) * SCALE + bias
                if diagonal:
                    st = jnp.where(_causal(blk, blk, transposed=True), st, NEG)
                pt.append(jnp.exp(st))
                dpt.append(lax.dot_general(v_ref[:, sls[hh]], do_ref[:, sls[hh]], _NT, preferred_element_type=F32)
                           - delta_ref[pl.ds(h, 1), :])
            dst = [pt[hh] * dpt[hh] for hh in heads]
            for hh in heads:
                dv_s[:, sls[hh]] += jnp.dot(pt[hh].astype(BF16), do_ref[:, sls[hh]], preferred_element_type=F32)
                dk_s[:, sls[hh]] += jnp.dot(dst[hh].astype(BF16), q_ref[:, sls[hh]], preferred_element_type=F32)
                dc_s[hh] -= jnp.sum(dst[hh], axis=1, keepdims=True)

        @pl.when(qi > kj)
        def _():
            step(False)

        @pl.when(qi == kj)
        def _():
            step(True)

        @pl.when(qi == nb - 1)
        def _():
            dk_ref[...] = dk_s[...] * SCALE
            dv_ref[...] = dv_s[...]
            for hh in range(FOX_HEADS):
                dc_ref[...] = _lane_put(dc_ref[...], hg * FOX_HEADS + hh, dc_s[hh])

    wide = FOX_HEADS * HEAD_DIM
    ksp = pl.BlockSpec((blk, wide), lambda s, qi, hg, kj: (kj[s], hg[s]))
    qsp = pl.BlockSpec((blk, wide), lambda s, qi, hg, kj: (qi[s], hg[s]))
    kstat = pl.BlockSpec((blk, LANES), lambda s, qi, hg, kj: (kj[s], 0))
    qrow = pl.BlockSpec((H, blk), lambda s, qi, hg, kj: (0, qi[s]))
    return pl.pallas_call(
        body, name="fox_bwd_dkv",
        grid_spec=pltpu.PrefetchScalarGridSpec(
            num_scalar_prefetch=3, grid=(steps[0].shape[0],),
            in_specs=[qsp, ksp, ksp, qsp, kstat, qrow, qrow, qrow] + [ANY] * len(behind), out_specs=[ksp, ksp, kstat],
            scratch_shapes=[pltpu.VMEM((blk, wide), F32), pltpu.VMEM((blk, wide), F32),
                            pltpu.VMEM((FOX_HEADS, blk, 1), F32)]),
        out_shape=[jax.ShapeDtypeStruct((T, D), F32), jax.ShapeDtypeStruct((T, D), F32),
                   jax.ShapeDtypeStruct((T, LANES), F32)],
        compiler_params=_params("arbitrary"),
    )(*steps, q, k, v, do, ccol, crow, lse_row, delta_row, *behind)


def _rel_index():
    j = np.arange(ROLL_W)
    u = np.where(j < 2 * WIN, j, j - ROLL_W)
    return np.clip(WIN - u, -(CHUNK - 1), REL_CLIP) + (CHUNK - 1)


def _band_bias(g_row, bias_s):
    base = jnp.broadcast_to(g_row, (CHUNK, ROLL_W))
    rowb = lax.broadcasted_iota(jnp.int32, (CHUNK, ROLL_W), 0)
    for bit in range(CHUNK.bit_length() - 1):
        base = jnp.where((rowb >> bit) & 1 == 1, pltpu.roll(base, 1 << bit, 1), base)
    kc = lax.broadcasted_iota(jnp.int32, (CHUNK, 2 * WIN), 1) // CHUNK
    for a in range(N_PREV_CHUNKS):
        part = base if a == 0 else pltpu.roll(base, CHUNK * a, 1)
        valid = (kc >= a) & (kc <= a + N_PREV_CHUNKS)
        bias_s[a * CHUNK:(a + 1) * CHUNK, :] = jnp.where(valid, part[:, :2 * WIN], NEG)


HALF = WIN // 2
SPAN = WIN + HALF


def _half(r):
    return slice(r * HALF, (r + 1) * HALF), slice(r * HALF, r * HALF + SPAN)


def _chunk_probs(q, kw, bias_s, qi, r):
    rows, keys = _half(r)
    s = lax.dot_general(q[rows], kw[keys], _NT, preferred_element_type=F32) * SCALE + bias_s[rows, keys]
    key = r * HALF + lax.broadcasted_iota(jnp.int32, (HALF, SPAN), 1)
    s = jnp.where((qi > 0) | (key >= WIN), s, NEG)
    p = jnp.exp(s - jnp.max(s, axis=1, keepdims=True))
    return p * (1.0 / jnp.sum(p, axis=1, keepdims=True))


def _chunk_fwd(q, k, v, g_rows):
    T, D = q.shape
    H = D // HEAD_DIM
    nq = T // WIN

    def body(q_ref, kp_ref, kc_ref, vp_ref, vc_ref, g_ref, o_ref, bias_s):
        qi = pl.program_id(1)

        @pl.when(qi == 0)
        def _():
            _band_bias(g_ref[...], bias_s)

        qv = q_ref[...]
        kw = jnp.concatenate([kp_ref[...], kc_ref[...]], axis=0)
        vw = jnp.concatenate([vp_ref[...], vc_ref[...]], axis=0)
        p = [_chunk_probs(qv, kw, bias_s, qi, r) for r in range(2)]
        for r in range(2):
            rows, keys = _half(r)
            o_ref[rows, :] = jnp.dot(p[r].astype(BF16), vw[keys], preferred_element_type=F32).astype(BF16)

    cur = pl.BlockSpec((WIN, HEAD_DIM), lambda h, qi: (qi, h))
    prev = pl.BlockSpec((WIN, HEAD_DIM), lambda h, qi: (jnp.maximum(qi - 1, 0), h))
    return pl.pallas_call(
        body, name="chunk_fwd", grid=(H, nq),
        in_specs=[cur, prev, cur, prev, cur, pl.BlockSpec((None, 1, ROLL_W), lambda h, qi: (h, 0, 0))],
        out_specs=cur, out_shape=jax.ShapeDtypeStruct((T, D), BF16),
        scratch_shapes=[pltpu.VMEM((WIN, 2 * WIN), F32)],
        compiler_params=_params("arbitrary", "arbitrary"),
    )(q, k, k, v, v, g_rows)


def _chunk_bwd(q, k, v, g_rows, do, o):
    T, D = q.shape
    H = D // HEAD_DIM
    nq = T // WIN

    def body(q_ref, kp_ref, kc_ref, vp_ref, vc_ref, g_ref, do_ref, o_ref,
             dq_ref, dk_ref, dv_ref, ds_ref, bias_s, ck_s, cv_s):
        qi = pl.program_id(1)

        @pl.when(qi == 0)
        def _():
            _band_bias(g_ref[...], bias_s)
            ds_ref[...] = jnp.zeros_like(ds_ref)
            ck_s[...] = jnp.zeros_like(ck_s)
            cv_s[...] = jnp.zeros_like(cv_s)

        @pl.when(qi < nq)
        def _():
            qv, dov = q_ref[...], do_ref[...]
            kw = jnp.concatenate([kp_ref[...], kc_ref[...]], axis=0)
            vw = jnp.concatenate([vp_ref[...], vc_ref[...]], axis=0)
            delta = jnp.sum(dov.astype(F32) * o_ref[...].astype(F32), axis=1, keepdims=True)
            p = [_chunk_probs(qv, kw, bias_s, qi, r) for r in range(2)]
            ds, dk, dv = [], [], []
            for r in range(2):
                rows, keys = _half(r)
                dp = lax.dot_general(dov[rows], vw[keys], _NT, preferred_element_type=F32) - delta[rows]
                ds.append(p[r] * dp)
            for r in range(2):
                rows, keys = _half(r)
                ds16 = ds[r].astype(BF16)
                dq_ref[rows, :] = jnp.dot(ds16, kw[keys], preferred_element_type=F32) * SCALE
                ds_ref[rows, keys] += ds[r]
                dk.append(lax.dot_general(ds16, qv[rows], _TN, preferred_element_type=F32) * SCALE)
                dv.append(lax.dot_general(p[r].astype(BF16), dov[rows], _TN, preferred_element_type=F32))

            def split(parts):
                lo, hi = parts
                prev_blk = jnp.concatenate([lo[:HALF], lo[HALF:WIN] + hi[:HALF]], axis=0)
                own_blk = jnp.concatenate([lo[WIN:] + hi[HALF:WIN], hi[WIN:]], axis=0)
                return prev_blk, own_blk

            dk_prev, dk_own = split(dk)
            dv_prev, dv_own = split(dv)
            dk_ref[...] = ck_s[...] + dk_prev
            dv_ref[...] = cv_s[...] + dv_prev
            ck_s[...] = dk_own
            cv_s[...] = dv_own

        @pl.when(qi == nq)
        def _():
            dk_ref[...] = ck_s[...]
            dv_ref[...] = cv_s[...]

    cur = pl.BlockSpec((WIN, HEAD_DIM), lambda h, qi: (jnp.minimum(qi, nq - 1), h))
    prev = pl.BlockSpec((WIN, HEAD_DIM), lambda h, qi: (jnp.clip(qi - 1, 0, nq - 2), h))
    late = pl.BlockSpec((WIN, HEAD_DIM), lambda h, qi: (jnp.maximum(qi - 1, 0), h))
    return pl.pallas_call(
        body, name="chunk_bwd", grid=(H, nq + 1),
        in_specs=[cur, prev, cur, prev, cur, pl.BlockSpec((None, 1, ROLL_W), lambda h, qi: (h, 0, 0)), cur, cur],
        out_specs=[cur, late, late, pl.BlockSpec((None, WIN, 2 * WIN), lambda h, qi: (h, 0, 0))],
        out_shape=[jax.ShapeDtypeStruct((T, D), F32)] * 3 + [jax.ShapeDtypeStruct((H, WIN, 2 * WIN), F32)],
        scratch_shapes=[pltpu.VMEM((WIN, 2 * WIN), F32), pltpu.VMEM((WIN, HEAD_DIM), F32),
                        pltpu.VMEM((WIN, HEAD_DIM), F32)],
        compiler_params=_params("arbitrary", "arbitrary"),
    )(q, k, k, v, v, g_rows, do, o)


def _rel_rows(rel, onehot_t):
    H = rel.shape[0]

    def body(rel_ref, oh_ref, out_ref):
        out_ref[...] = jnp.dot(rel_ref[...], oh_ref[...], precision=HIGHEST, preferred_element_type=F32)

    return pl.pallas_call(body, name="rel_rows", out_shape=jax.ShapeDtypeStruct((H, ROLL_W), F32))(rel, onehot_t)


def _rel_grad(ds_sum, onehot):
    H = ds_sum.shape[0]

    def body(ds_ref, oh_ref, out_ref):
        y = jnp.zeros((CHUNK, ROLL_W), F32)
        pad = jnp.zeros((CHUNK, ROLL_W - 2 * WIN), F32)
        for a in range(N_PREV_CHUNKS):
            part = jnp.concatenate([ds_ref[a * CHUNK:(a + 1) * CHUNK, :], pad], axis=1)
            y = y + (part if a == 0 else pltpu.roll(part, ROLL_W - CHUNK * a, 1))
        rowb = lax.broadcasted_iota(jnp.int32, (CHUNK, ROLL_W), 0)
        for bit in range(CHUNK.bit_length() - 1):
            y = jnp.where((rowb >> bit) & 1 == 1, pltpu.roll(y, ROLL_W - (1 << bit), 1), y)
        diag = jnp.broadcast_to(jnp.sum(y, axis=0, keepdims=True), (8, ROLL_W))
        out_ref[...] = jnp.dot(diag, oh_ref[...], precision=HIGHEST, preferred_element_type=F32)

    return pl.pallas_call(
        body, name="rel_grad", grid=(H,),
        in_specs=[pl.BlockSpec((None, WIN, 2 * WIN), lambda h: (h, 0, 0)),
                  pl.BlockSpec((ROLL_W, N_REL_PAD), lambda h: (0, 0))],
        out_specs=pl.BlockSpec((None, 8, N_REL_PAD), lambda h: (h, 0, 0)),
        out_shape=jax.ShapeDtypeStruct((H, 8, N_REL_PAD), F32), compiler_params=_params("parallel"),
    )(ds_sum, onehot)


def _loss_head(y, target):
    T, D = y.shape
    bt = _tile(T, 256)

    def body(y_ref, t_ref, dy_ref, dy16_ref, loss_ref):
        @pl.when(pl.program_id(0) == 0)
        def _():
            loss_ref[...] = jnp.zeros_like(loss_ref)

        err = y_ref[...] - t_ref[...]
        dy = err * (1.0 / D)
        dy_ref[...] = dy
        dy16_ref[...] = dy.astype(BF16)
        loss_ref[...] += 0.5 * jnp.sum(jnp.mean(err * err, axis=-1, keepdims=True))

    row = pl.BlockSpec((bt, D), lambda i: (i, 0))
    one = pl.BlockSpec((8, LANES), lambda i: (0, 0))
    return pl.pallas_call(
        body, name="loss_head", grid=(T // bt,), in_specs=[row, row], out_specs=[row, row, one],
        out_shape=[jax.ShapeDtypeStruct((T, D), F32), jax.ShapeDtypeStruct((T, D), BF16),
                   jax.ShapeDtypeStruct((8, LANES), F32)],
        compiler_params=_params("arbitrary"),
    )(y, target)


def _adam_step(g, w_ref, m_ref, v_ref, g_ref, d_ref, nm_ref, nv_ref):
    nm = ADAM_B1 * m_ref[...] + (1.0 - ADAM_B1) * g
    nv = ADAM_B2 * v_ref[...] + (1.0 - ADAM_B2) * (g * g)
    m_hat = nm / (1.0 - ADAM_B1 ** ADAM_STEP)
    v_hat = nv / (1.0 - ADAM_B2 ** ADAM_STEP)
    g_ref[...] = g
    nm_ref[...] = nm
    nv_ref[...] = nv
    d_ref[...] = -ADAM_LR * (m_hat / (jnp.sqrt(v_hat) + ADAM_EPS) + ADAM_WD * w_ref[...])


def _adamw(name, w, m, v, g):
    def body(w_ref, m_ref, v_ref, gin_ref, g_ref, d_ref, nm_ref, nv_ref):
        _adam_step(gin_ref[...], w_ref, m_ref, v_ref, g_ref, d_ref, nm_ref, nv_ref)

    return pl.pallas_call(body, name=name, out_shape=[jax.ShapeDtypeStruct(w.shape, F32)] * 4)(w, m, v, g)


def _adamw_reduced(name, w, m, v, own, land, chip, row0=0, into=None):
    rows_all, C = w.shape
    _, R, _ = land.shape

    def body(chip_ref, w_ref, m_ref, v_ref, own_ref, land_ref, *rest):
        g = jnp.where(chip_ref[0] == 0, own_ref[...], land_ref[0]).astype(F32)
        for q in range(1, N_CHIP):
            g = g + jnp.where(chip_ref[0] == q, own_ref[...], land_ref[q]).astype(F32)
        _adam_step(g, w_ref, m_ref, v_ref, *rest[-4:])

    if R % 16 == 0:
        br = _tile(R, max(16, (128 * 1024) // C))
        first = row0 // br
        grid = (R // br,)
        wsp = pl.BlockSpec((br, C), lambda i, chip_ref: (first + i, 0))
        osp = pl.BlockSpec((None, br, C), lambda i, chip_ref: (chip_ref[0], i, 0))
        lsp = pl.BlockSpec((N_CHIP, br, C), lambda i, chip_ref: (0, i, 0))
    else:
        bc = _tile(C, 512)
        grid = (C // bc,)
        wsp = pl.BlockSpec((R, bc), lambda i, chip_ref: (0, i))
        osp = pl.BlockSpec((None, R, bc), lambda i, chip_ref: (chip_ref[0], 0, i))
        lsp = pl.BlockSpec((N_CHIP, R, bc), lambda i, chip_ref: (0, 0, i))
    extra = [] if into is None else list(into)
    return pl.pallas_call(
        body, name=name,
        grid_spec=pltpu.PrefetchScalarGridSpec(
            num_scalar_prefetch=1, grid=grid, in_specs=[wsp, wsp, wsp, osp, lsp] + [ANY] * len(extra),
            out_specs=[wsp] * 4),
        out_shape=[jax.ShapeDtypeStruct((rows_all, C), F32)] * 4,
        input_output_aliases={6 + j: j for j in range(len(extra))}, compiler_params=_params("parallel"),
    )(chip, w, m, v, own, land, *extra)


def _sum_parts(name, parts, out_dtype):
    P, R, C = parts.shape
    br = _tile(R, max(8, (256 * 1024) // C)) if R % 8 == 0 else R

    def body(p_ref, o_ref):
        g = p_ref[0].astype(F32)
        for i in range(1, P):
            g = g + p_ref[i].astype(F32)
        o_ref[...] = g.astype(out_dtype)

    return pl.pallas_call(
        body, name=name, grid=(R // br,),
        in_specs=[pl.BlockSpec((P, br, C), lambda i: (0, i, 0))], out_specs=pl.BlockSpec((br, C), lambda i: (i, 0)),
        out_shape=jax.ShapeDtypeStruct((R, C), out_dtype), compiler_params=_params("parallel"),
    )(parts)


def _pair_sum(name, g, recv, core):
    _, R, C = g.shape

    def body(core_ref, g_ref, r_ref, o_ref):
        o_ref[...] = (g_ref[...] + r_ref[...]).astype(BF16)

    if R % 16 == 0:
        br = _tile(R, max(16, (256 * 1024) // C))
        blk, n = (None, br, C), R // br
        mine = lambda q, i, core_ref: (2 * q + core_ref[0], i, 0)
        same = lambda q, i, core_ref: (q, i, 0)
    else:
        bc = _tile(C, 512)
        blk, n = (None, R, bc), C // bc
        mine = lambda q, i, core_ref: (2 * q + core_ref[0], 0, i)
        same = lambda q, i, core_ref: (q, 0, i)
    return pl.pallas_call(
        body, name=name,
        grid_spec=pltpu.PrefetchScalarGridSpec(
            num_scalar_prefetch=1, grid=(N_CHIP, n),
            in_specs=[pl.BlockSpec(blk, mine), pl.BlockSpec(blk, same)], out_specs=pl.BlockSpec(blk, same)),
        out_shape=jax.ShapeDtypeStruct((N_CHIP, R, C), BF16), compiler_params=_params("parallel", "parallel"),
    )(core, g, recv)


def _chips_of(x, y):
    return [(1 - x, y), (x, 1 - y), (1 - x, 1 - y)]


def _position():
    x, y, c = lax.axis_index("x"), lax.axis_index("y"), lax.axis_index("c")
    return x, y, c, _chips_of(x, y)


def _all_gather(name, xs):
    n = len(xs)

    def body(*refs):
        x_refs, out_refs = refs[:n], refs[n:2 * n]
        send_sems, recv_sems, local_sems = refs[2 * n:]
        x, y, c, chips = _position()
        me, sibling = (x, y, c), (x, y, 1 - c)

        def copy(a, k, block, to, src=None):
            px, py, pc = block
            slot = out_refs[a].at[4 * px + 2 * py + pc]
            return pltpu.make_async_remote_copy(
                src_ref=slot if src is None else src, dst_ref=slot,
                send_sem=send_sems.at[a, k], recv_sem=recv_sems.at[a, k], device_id=to, device_id_type=MESH)

        mine = [pltpu.make_async_copy(x_refs[a], out_refs[a].at[4 * x + 2 * y + c], local_sems.at[a]) for a in range(n)]
        for cp in mine:
            cp.start()
        first = []
        for a in range(n):
            first.append(copy(a, 0, me, sibling, src=x_refs[a]))
            first += [copy(a, 1 + j, me, (*chip, c), src=x_refs[a]) for j, chip in enumerate(chips)]
        for cp in first:
            cp.start()
        passed = []
        for j, chip in enumerate(chips):
            for a in range(n):
                copy(a, 1 + j, (*chip, c), me).wait_recv()
                fwd = copy(a, 4 + j, (*chip, c), sibling)
                fwd.start()
                passed.append(fwd)
        for a in range(n):
            copy(a, 0, sibling, me).wait_recv()
            for j, chip in enumerate(chips):
                copy(a, 4 + j, (*chip, 1 - c), me).wait_recv()
        for cp in first + passed:
            cp.wait_send()
        for cp in mine:
            cp.wait()

    return pl.pallas_call(
        body, name=name, in_specs=[ANY] * n, out_specs=[ANY] * n,
        out_shape=[jax.ShapeDtypeStruct((N_DEV,) + x.shape, x.dtype) for x in xs],
        scratch_shapes=[pltpu.SemaphoreType.DMA((n, 7)), pltpu.SemaphoreType.DMA((n, 7)),
                        pltpu.SemaphoreType.DMA((n,))],
    )(*xs)


HBM = pl.BlockSpec(memory_space=pltpu.HBM)
SEM = pl.BlockSpec(memory_space=pltpu.SEMAPHORE)
EFFECT = pltpu.SideEffectType.DATAFLOW_SIDE_EFFECTING


def _own_block_out(x, y, c):
    mine = 4 * x + 2 * y + c
    return [(mine, (x, y, 1 - c))] + [(mine, (cx, cy, c)) for cx, cy in _chips_of(x, y)]


def _own_blocks_in(x, y, c):
    return [(4 * x + 2 * y + 1 - c, (x, y, 1 - c))] + [(4 * cx + 2 * cy + c, (cx, cy, c)) for cx, cy in _chips_of(x, y)]


def _passed_on_out(x, y, c):
    return [(4 * cx + 2 * cy + c, (x, y, 1 - c)) for cx, cy in _chips_of(x, y)]


def _passed_on_in(x, y, c):
    return [(4 * cx + 2 * cy + 1 - c, (x, y, 1 - c)) for cx, cy in _chips_of(x, y)]


def _gather_start(name, lands, plan):
    n = len(lands)
    n_copies = len(plan(0, 0, 0))

    def body(*refs):
        land = refs[:n]
        send_sems, recv_sems = refs[n:2 * n], refs[2 * n:3 * n]
        token = refs[4 * n]
        x, y, c, _ = _position()
        for a in range(n):
            for k, (block, peer) in enumerate(plan(x, y, c)):
                pltpu.make_async_remote_copy(
                    src_ref=land[a].at[block], dst_ref=land[a].at[block], send_sem=send_sems[a].at[k],
                    recv_sem=recv_sems[a].at[k], device_id=peer, device_id_type=MESH).start()
        token[...] = jnp.zeros_like(token)

    res = pl.pallas_call(
        body, name=name, in_specs=[HBM] * n,
        out_specs=[SEM] * (2 * n) + [HBM] * n + [pl.BlockSpec(memory_space=pltpu.VMEM)],
        out_shape=[pltpu.SemaphoreType.DMA((n_copies,))] * (2 * n) + [pltpu.HBM(l.shape, l.dtype) for l in lands]
        + [jax.ShapeDtypeStruct((8, LANES), F32)],
        input_output_aliases={a: 2 * n + a for a in range(n)},
        compiler_params=pltpu.CompilerParams(has_side_effects=EFFECT),
    )(*[pltpu.with_memory_space_constraint(l, pltpu.HBM) for l in lands])
    return res[:n], res[n:2 * n], res[2 * n:3 * n], res[3 * n]


def _gather_wait(name, lands, send_sems, recv_sems, plan, after):
    n = len(lands)
    tail = [] if after is None else [after]

    def body(*refs):
        land = refs[:n]
        send, recv = refs[n:2 * n], refs[2 * n:3 * n]
        x, y, c, _ = _position()
        for a in range(n):
            for k, (block, peer) in enumerate(plan(x, y, c)):
                cp = pltpu.make_async_remote_copy(
                    src_ref=land[a].at[block], dst_ref=land[a].at[block], send_sem=send[a].at[k],
                    recv_sem=recv[a].at[k], device_id=peer, device_id_type=MESH)
                cp.wait_send()
                cp.wait_recv()

    return pl.pallas_call(
        body, name=name, in_specs=[HBM] * n + [SEM] * (2 * n) + [ANY] * len(tail), out_specs=[HBM] * n,
        out_shape=[pltpu.HBM(l.shape, l.dtype) for l in lands], input_output_aliases={a: a for a in range(n)},
        compiler_params=pltpu.CompilerParams(has_side_effects=EFFECT),
    )(*lands, *send_sems, *recv_sems, *tail)


def _split_start(name, arrays, lands, plan):
    n = len(arrays)
    n_copies = len(plan(0, 0, 0))

    def body(*refs):
        src, land = refs[:n], refs[n:2 * n]
        send_sems, recv_sems = refs[2 * n:3 * n], refs[3 * n:4 * n]
        token = refs[6 * n]
        x, y, c, _ = _position()
        for a in range(n):
            for k, (src_block, land_block, peer) in enumerate(plan(x, y, c)):
                pltpu.make_async_remote_copy(
                    src_ref=src[a].at[src_block], dst_ref=land[a].at[land_block], send_sem=send_sems[a].at[k],
                    recv_sem=recv_sems[a].at[k], device_id=peer, device_id_type=MESH).start()
        token[...] = jnp.zeros_like(token)

    both = list(arrays) + list(lands)
    res = pl.pallas_call(
        body, name=name, in_specs=[HBM] * (2 * n),
        out_specs=[SEM] * (2 * n) + [HBM] * (2 * n) + [pl.BlockSpec(memory_space=pltpu.VMEM)],
        out_shape=[pltpu.SemaphoreType.DMA((n_copies,))] * (2 * n) + [pltpu.HBM(b.shape, b.dtype) for b in both]
        + [jax.ShapeDtypeStruct((8, LANES), F32)],
        input_output_aliases={i: 2 * n + i for i in range(2 * n)},
        compiler_params=pltpu.CompilerParams(has_side_effects=EFFECT),
    )(*[pltpu.with_memory_space_constraint(b, pltpu.HBM) for b in both])
    return res[:n], res[n:2 * n], res[2 * n:3 * n], res[3 * n:4 * n], res[4 * n]


def _split_wait(name, arrays, lands, send_sems, recv_sems, plan, after):
    n = len(arrays)
    tail = [] if after is None else [after]

    def body(*refs):
        src, land = refs[:n], refs[n:2 * n]
        send, recv = refs[2 * n:3 * n], refs[3 * n:4 * n]
        x, y, c, _ = _position()
        for a in range(n):
            for k, (src_block, land_block, peer) in enumerate(plan(x, y, c)):
                cp = pltpu.make_async_remote_copy(
                    src_ref=src[a].at[src_block], dst_ref=land[a].at[land_block], send_sem=send[a].at[k],
                    recv_sem=recv[a].at[k], device_id=peer, device_id_type=MESH)
                cp.wait_send()
                cp.wait_recv()

    both = list(arrays) + list(lands)
    res = pl.pallas_call(
        body, name=name, in_specs=[HBM] * (2 * n) + [SEM] * (2 * n) + [ANY] * len(tail), out_specs=[HBM] * (2 * n),
        out_shape=[pltpu.HBM(b.shape, b.dtype) for b in both], input_output_aliases={i: i for i in range(2 * n)},
        compiler_params=pltpu.CompilerParams(has_side_effects=EFFECT),
    )(*both, *send_sems, *recv_sems, *tail)
    return res[:n], res[n:]


def _to_other_core(x, y, c):
    return [(2 * q + (1 - c), q, (x, y, 1 - c)) for q in range(N_CHIP)]


def _to_other_chips(x, y, c):
    return [(2 * cx + cy, 2 * x + y, (cx, cy, c)) for cx, cy in _chips_of(x, y)]


def _from_other_chips(x, y, c):
    return [(2 * x + y, 2 * cx + cy, (cx, cy, c)) for cx, cy in _chips_of(x, y)]


def _mlp_fwd(tag, h, g, w1g, w2):
    T, D = h.shape
    F = w2.shape[0]
    (n,) = _rmsnorm_fwd(f"mlp{tag}_norm", h, [g])
    a, hid = _mm(f"mlp{tag}_up", "nn", n, w1g, T, F, D, b_groups=N_DEV, out_dtypes=(BF16, BF16),
                 epilogue=_ep_relu_square)
    out = _mm(f"mlp{tag}_down", "nn", hid, w2, T, D, F, extras=(h,), epilogue=_ep_add)
    return out, (n, a, hid)


def _mlp_bwd(tag, h, g, w1g, w2, saved, dy, dy16, behind=()):
    T, D = h.shape
    F = w2.shape[0]
    n, a, hid = saved
    dw2 = _mm(f"mlp{tag}_dw2", "tn", hid, dy16, F, D, T, behind=behind)
    dpre = _mm(f"mlp{tag}_dpre", "nt", dy16, w2, T, F, D, out_dtypes=(BF16,), extras=(a,), epilogue=_ep_times_2a)
    dw1 = _mm(f"mlp{tag}_dw1", "tn", n, dpre, D, F, T, out_groups=N_DEV)
    dn = _mm(f"mlp{tag}_dn", "nt", dpre, w1g, T, D, F, b_groups=N_DEV)
    dh, dh16, dg = _rmsnorm_bwd(f"mlp{tag}_norm_bwd", h, g, dn, dy)
    return dh, dh16, dg, dw1, dw2


def _local_step(x, target, W, fox_blk, arrive=lambda group, after: {}, sync=lambda tag, after, G: ()):
    T, D = x.shape
    H = D // HEAD_DIM
    G = {}
    W = dict(W)

    (ua,) = _rmsnorm_fwd("a_norm", x, [W["a_norm_g"]])
    qkv = _mm("a_qkv", "nt", ua, W["a_w_in_t"], T, 3 * D, D)
    fz = _mm("a_fz", "nt", ua, W["a_w_f_t"], T, LANES, D)
    qa, ka, va = _head_post("a_heads", qkv, D, [(0, W["a_q_g"]), (1, W["a_k_g"]), (2, None)])
    ccol = _gates_fwd(fz, W["a_b_f"], H)
    crow = ccol[:, :H].T
    oa, lse = _fox_fwd(qa, ka, va, ccol, crow, fox_blk)
    W.update(arrive("a_out", oa))
    h1 = _mm("a_out", "nn", oa, W["a_w_out"], T, D, D, extras=(x,), epilogue=_ep_add)
    W.update(arrive("middle", h1))
    h2, mlp0 = _mlp_fwd("0", h1, W["mlp_norm_g0"], W["mlp_w1_0"], W["mlp_w2_0"])
    W.update(arrive("kv_q", h2))
    nkv, ub = _rmsnorm_fwd("kv_b_norm", h2, [W["kv_norm_g"], W["b_norm_g"]])
    kv = _mm("kv_proj", "nn", nkv, W["kv_w"], T, 2 * D, D, b_groups=N_DEV)
    ks, vs = _head_post("kv_heads", kv, D, [(0, W["kv_k_g"]), (1, None)])
    qb_pre = _mm("b_q", "nn", ub, W["b_w_q"], T, D, D)
    (qb,) = _head_post("b_heads", qb_pre, D, [(0, W["b_q_g"])])
    onehot = (jnp.asarray(_rel_index())[:, None] == jnp.arange(N_REL_PAD)[None, :]).astype(F32)
    rel_pad = jnp.pad(W["b_rel"], ((0, 0), (0, N_REL_PAD - N_REL)))
    g_rows = _rel_rows(rel_pad, onehot.T).reshape(H, 1, ROLL_W)
    ob = _chunk_fwd(qb, ks, vs, g_rows)
    W.update(arrive("late", ob))
    h3 = _mm("b_out", "nn", ob, W["b_w_out"], T, D, D, extras=(h2,), epilogue=_ep_add)
    h4, mlp1 = _mlp_fwd("1", h3, W["mlp_norm_g1"], W["mlp_w1_1"], W["mlp_w2_1"])
    dy, dy16, loss = _loss_head(h4, target)

    dh3, dh3_16, G["mlp_norm_g1"], G["mlp_w1_1"], G["mlp_w2_1"] = _mlp_bwd(
        "1", h3, W["mlp_norm_g1"], W["mlp_w1_1"], W["mlp_w2_1"], mlp1, dy, dy16)
    dob = _mm("b_out_dx", "nt", dh3_16, W["b_w_out"], T, D, D, out_dtypes=(BF16,), behind=sync("mlp1", dh3, G))
    G["b_w_out"] = _mm("b_out_dw", "tn", ob, dh3_16, D, D, T, behind=sync("b_out_dx", dob, G))
    dqb, dks, dvs, ds_sum = _chunk_bwd(qb, ks, vs, g_rows, dob, ob)
    G["b_rel"] = _rel_grad(ds_sum, onehot)[:, 0, :]
    dqb_pre, (G["b_q_g"],) = _head_post_bwd("b_heads_bwd", qb_pre, D, [(0, W["b_q_g"], dqb)])
    G["b_w_q"] = _mm("b_q_dw", "tn", ub, dqb_pre, D, D, T)
    dub = _mm("b_q_dx", "nt", dqb_pre, W["b_w_q"], T, D, D)
    dkv, (G["kv_k_g"],) = _head_post_bwd("kv_heads_bwd", kv, D, [(0, W["kv_k_g"], dks), (1, None, dvs)])
    G["kv_w"] = _mm("kv_dw", "tn", nkv, dkv, D, 2 * D, T, out_groups=N_DEV)
    dnkv = _mm("kv_dx", "nt", dkv, W["kv_w"], T, D, 2 * D, b_groups=N_DEV, behind=sync("kv_dw", G["kv_w"], G))
    dh2, _, G["b_norm_g"] = _rmsnorm_bwd("b_norm_bwd", h2, W["b_norm_g"], dub, dh3)
    dh2, dh2_16, G["kv_norm_g"] = _rmsnorm_bwd("kv_norm_bwd", h2, W["kv_norm_g"], dnkv, dh2)
    dh1, dh1_16, G["mlp_norm_g0"], G["mlp_w1_0"], G["mlp_w2_0"] = _mlp_bwd(
        "0", h1, W["mlp_norm_g0"], W["mlp_w1_0"], W["mlp_w2_0"], mlp0, dh2, dh2_16, behind=sync("dh2", dh2, G))
    G["a_w_out"] = _mm("a_out_dw", "tn", oa, dh1_16, D, D, T, behind=sync("mlp0", dh1, G))
    doa = _mm("a_out_dx", "nt", dh1_16, W["a_w_out"], T, D, D, out_dtypes=(BF16,))
    delta = _head_stat("fox_delta", doa, oa)
    dqa, dcq = _fox_bwd_dq(qa, ka, va, doa, ccol, crow, lse, delta, fox_blk)
    dka, dva, dck = _fox_bwd_dkv(qa, ka, va, doa, ccol, crow, lse[:, :H].T, delta[:, :H].T, fox_blk,
                                 behind=sync("fox_bwd_dq", dqa, G))
    dfz, G["a_b_f"] = _gates_bwd(dcq, dck, fz, W["a_b_f"], H)
    dqkv, (G["a_q_g"], G["a_k_g"]) = _head_post_bwd(
        "a_heads_bwd", qkv, D, [(0, W["a_q_g"], dqa), (1, W["a_k_g"], dka), (2, None, dva)])
    G["a_w_qkv_t"] = _mm("a_qkv_dw", "tn", dqkv, ua, 3 * D, D, T)
    G["a_w_f_t"] = _mm("a_fz_dw", "tn", dfz, ua, LANES, D, T)
    dua = _mm("a_qkv_dx", "nn", dqkv, W["a_w_in_t"], T, D, 3 * D, behind=sync("a_in_dw", G["a_w_f_t"], G))
    dua = _mm("a_fz_dx", "nn", dfz, W["a_w_f_t"], T, D, LANES, extras=(dua,), epilogue=_ep_add,
              behind=sync("a_qkv_dx", dua, G))
    dx, _, G["a_norm_g"] = _rmsnorm_bwd("a_norm_bwd", x, W["a_norm_g"], dua, dh1)
    return loss, dx, G


def _pad_lanes(a):
    return jnp.pad(a, ((0, 0), (0, LANES - a.shape[1])))


def kernel(x, a_norm_g, a_w_in, a_b_f, a_q_g, a_k_g, a_w_out, mlp_norm_g, mlp_w1, mlp_w2, kv_norm_g, kv_w, kv_k_g, b_norm_g, b_w_q, b_q_g, b_rel, b_w_out, loss_target, m_a_norm_g, m_a_w_in, m_a_b_f, m_a_q_g, m_a_k_g, m_a_w_out, m_mlp_norm_g, m_mlp_w1, m_mlp_w2, m_kv_norm_g, m_kv_w, m_kv_k_g, m_b_norm_g, m_b_w_q, m_b_q_g, m_b_rel, m_b_w_out, v_a_norm_g, v_a_w_in, v_a_b_f, v_a_q_g, v_a_k_g, v_a_w_out, v_mlp_norm_g, v_mlp_w1, v_mlp_w2, v_kv_norm_g, v_kv_w, v_kv_k_g, v_b_norm_g, v_b_w_q, v_b_q_g, v_b_rel, v_b_w_out):
    T, D = x.shape[1], x.shape[2]
    H = D // HEAD_DIM
    F = mlp_w2.shape[1] * N_DEV
    me = 4 * lax.axis_index("x") + 2 * lax.axis_index("y") + lax.axis_index("c")
    core = lax.axis_index("c").astype(jnp.int32).reshape(1)
    n_in = a_w_in.shape[2]
    n_rel = b_rel.shape[2]

    small_in = jnp.concatenate([a_norm_g.reshape(1, -1), b_rel.reshape(1, -1)], axis=1)
    local = [jnp.swapaxes(a_w_in[0], 0, 1), a_w_out[0], mlp_w1[0], mlp_w2[0], kv_w, b_w_q[0], b_w_out[0],
             mlp_w1[1], mlp_w2[1]]
    def landing(w):
        return lax.dynamic_update_slice(lax.empty((N_DEV,) + w.shape, w.dtype), w[None], (me,) + (0,) * w.ndim)

    send1, recv1, lands, started = _gather_start(
        "gather_start_first", [landing(small_in), landing(local[0].astype(BF16))], _own_block_out)
    more = _gather_start("gather_start_rest", [landing((w + started[0, 0]).astype(BF16)) for w in local[1:]],
                         _own_block_out)
    send1, recv1, lands, sems2 = send1 + more[0], recv1 + more[1], list(lands + more[2]), {}

    def pass_on(tag, idx, after):
        got = _gather_wait(f"gather_wait_{tag}", [lands[i] for i in idx], [send1[i] for i in idx],
                           [recv1[i] for i in idx], _own_blocks_in, after)
        send, recv, got, _ = _gather_start(f"gather_pass_{tag}", got, _passed_on_out)
        for k, i in enumerate(idx):
            lands[i], sems2[i] = got[k], (send[k], recv[k])

    def landed(tag, idx, after):
        got = _gather_wait(f"gather_landed_{tag}", [lands[i] for i in idx], [sems2[i][0] for i in idx],
                           [sems2[i][1] for i in idx], _passed_on_in, after)
        for k, i in enumerate(idx):
            lands[i] = got[k]
        return got

    def arrive(group, after):
        if group == "first":
            pass_on("first", [0, 1], after)
            g_small, w_in_t = landed("first", [0, 1], None)
            w_in_t = w_in_t.reshape(N_DEV * n_in, D)
            rel = jnp.transpose(g_small[:, 0, D // N_DEV:].reshape(N_DEV, H, n_rel), (1, 0, 2))
            return {"a_norm_g": g_small[:, 0, :D // N_DEV].reshape(1, D), "b_rel": rel.reshape(H, N_DEV * n_rel),
                    "a_w_in_t": w_in_t, "a_w_f_t": jnp.pad(w_in_t[3 * D:], ((0, LANES - H), (0, 0)))}
        if group == "a_out":
            pass_on("early", [2, 3, 4], after)
            return {"a_w_out": landed("a_out", [2], None)[0].reshape(D, D)}
        if group == "middle":
            got = landed("middle", [3, 4], after)
            return {"mlp_w1_0": got[0], "mlp_w2_0": got[1].reshape(F, D)}
        if group == "kv_q":
            pass_on("late", [5, 6, 7, 8, 9], after)
            got = landed("kv_q", [5, 6], None)
            return {"kv_w": got[0], "b_w_q": got[1].reshape(D, D)}
        got = landed("late", [7, 8, 9], after)
        return {"b_w_out": got[0].reshape(D, D), "mlp_w1_1": got[1], "mlp_w2_1": got[2].reshape(F, D)}

    W = {
        "a_b_f": _pad_lanes(a_b_f), "a_q_g": a_q_g, "a_k_g": a_k_g,
        "mlp_norm_g0": mlp_norm_g[0:1], "mlp_norm_g1": mlp_norm_g[1:2],
        "kv_norm_g": kv_norm_g.reshape(1, D), "kv_k_g": kv_k_g.reshape(1, HEAD_DIM),
        "b_norm_g": b_norm_g, "b_q_g": b_q_g, **arrive("first", more[3]),
    }

    chip = (2 * lax.axis_index("x") + lax.axis_index("y")).astype(jnp.int32).reshape(1)
    flat = lambda a: a.reshape(-1, a.shape[-1])
    state = {"a_w_in": [jnp.swapaxes(a[0], 0, 1) for a in (a_w_in, m_a_w_in, v_a_w_in)]}
    for nm, w, m, v in [("a_w_out", a_w_out, m_a_w_out, v_a_w_out), ("kv_w", kv_w, m_kv_w, v_kv_w),
                        ("b_w_q", b_w_q, m_b_w_q, v_b_w_q), ("b_w_out", b_w_out, m_b_w_out, v_b_w_out),
                        ("mlp_w1", mlp_w1, m_mlp_w1, v_mlp_w1), ("mlp_w2", mlp_w2, m_mlp_w2, v_mlp_w2)]:
        state[nm] = [flat(w), flat(m), flat(v)]
    upd, flying = {}, {}

    def chunks(nm, G):
        if nm == "a_w_in":
            return jnp.concatenate([G["a_w_qkv_t"], G["a_w_f_t"][:H]], axis=0).reshape(N_DEV, n_in, D)
        g = G[nm]
        return g if g.ndim == 3 else g.reshape(N_DEV, g.shape[0] // N_DEV, g.shape[1])

    def to_core(tag, names, G):
        gs = [chunks(nm, G) for nm in names]
        lands = [lax.empty((N_CHIP,) + g.shape[1:], g.dtype) for g in gs]
        send, recv, gs, lands, token = _split_start(f"reduce_cores_start_{tag}", gs, lands, _to_other_core)
        flying[tag] = (names, send, recv, gs, lands)
        return (token,)

    def to_chips(tag, after):
        names, send, recv, gs, lands = flying[tag]
        gs, lands = _split_wait(f"reduce_cores_wait_{tag}", gs, lands, send, recv, _to_other_core, after)
        ps = [_pair_sum(f"pair_sum_{nm}", g, r, core) for nm, g, r in zip(names, gs, lands)]
        lands = [lax.empty(p.shape, p.dtype) for p in ps]
        send, recv, ps, lands, token = _split_start(f"reduce_chips_start_{tag}", ps, lands, _to_other_chips)
        flying[tag] = (names, send, recv, ps, lands)
        return (token,)

    def finish(tag, after):
        names, send, recv, ps, lands = flying.pop(tag)
        ps, lands = _split_wait(f"reduce_chips_wait_{tag}", ps, lands, send, recv, _from_other_chips, after)
        for nm, own, land in zip(names, ps, lands):
            if nm.startswith("mlp_w"):
                base, layer = nm[:-2], int(nm[-1])
                w, m, v = state[base]
                upd[base] = _adamw_reduced(f"adamw_{nm}", w, m, v, own, land, chip, row0=layer * (w.shape[0] // 2),
                                           into=upd.get(base))
            else:
                upd[nm] = _adamw_reduced(f"adamw_{nm}", *state[nm], own, land, chip)

    def sync(tag, after, G):
        if tag == "mlp1":
            return to_core("r1", ["mlp_w2_1", "mlp_w1_1"], G)
        if tag == "b_out_dx":
            return to_chips("r1", after)
        if tag == "kv_dw":
            return to_core("r2", ["b_w_out", "b_w_q", "kv_w"], G)
        if tag == "dh2":
            return to_chips("r2", after)
        if tag == "mlp0":
            return to_core("r3", ["mlp_w2_0", "mlp_w1_0"], G)
        if tag == "fox_bwd_dq":
            finish("r1", after)
            finish("r2", after)
            return to_chips("r3", after)
        if tag == "a_in_dw":
            return to_core("r4", ["a_w_out", "a_w_in"], G)
        if tag == "a_qkv_dx":
            return to_chips("r4", after)
        return ()

    loss_tile, dx, G = _local_step(x[0], loss_target[0], W, min(T, 512), arrive, sync)
    finish("r3", dx)

    def rows8(vecs):
        r = lax.broadcasted_iota(jnp.int32, (8, LANES), 0)
        tile = jnp.zeros((8, LANES), F32)
        for i, vec in enumerate(vecs):
            tile = jnp.where(r == i, vec, tile)
        return tile

    rows = lambda a: a.reshape(-1, LANES)
    packed = [rows(G["a_norm_g"]), rows(G["mlp_norm_g0"]), rows(G["mlp_norm_g1"]), rows(G["kv_norm_g"]),
              rows(G["b_norm_g"]), rows(G["b_rel"]),
              rows8([G["a_b_f"], G["a_q_g"], G["a_k_g"], G["kv_k_g"], G["b_q_g"], loss_tile[0:1]])]
    sizes = [p.shape[0] for p in packed]
    (all_small,) = _all_gather("gather_small_grads", [jnp.concatenate(packed, axis=0)])
    total = _sum_parts("sum_small_grads", all_small, F32)
    offs = np.concatenate([[0], np.cumsum(sizes)])
    s = [total[offs[i]:offs[i + 1]] for i in range(len(sizes))]
    loss = s[6][5, 0]
    g_a_norm = lax.dynamic_slice(s[0].reshape(1, D), (0, me * (D // N_DEV)), (1, D // N_DEV))
    g_rel = lax.dynamic_slice(s[5].reshape(H, N_REL_PAD), (0, me * n_rel), (H, n_rel))
    small = {
        "a_norm_g": (a_norm_g, m_a_norm_g, v_a_norm_g, g_a_norm),
        "a_b_f": (a_b_f, m_a_b_f, v_a_b_f, s[6][0:1, :H]),
        "a_q_g": (a_q_g, m_a_q_g, v_a_q_g, s[6][1:2]),
        "a_k_g": (a_k_g, m_a_k_g, v_a_k_g, s[6][2:3]),
        "mlp_norm_g": (mlp_norm_g, m_mlp_norm_g, v_mlp_norm_g,
                       jnp.concatenate([s[1].reshape(1, D), s[2].reshape(1, D)], axis=0)),
        "kv_norm_g": (kv_norm_g.reshape(1, D), m_kv_norm_g.reshape(1, D), v_kv_norm_g.reshape(1, D), s[3].reshape(1, D)),
        "kv_k_g": (kv_k_g.reshape(1, HEAD_DIM), m_kv_k_g.reshape(1, HEAD_DIM), v_kv_k_g.reshape(1, HEAD_DIM),
                   s[6][3:4]),
        "b_norm_g": (b_norm_g, m_b_norm_g, v_b_norm_g, s[4].reshape(1, D)),
        "b_q_g": (b_q_g, m_b_q_g, v_b_q_g, s[6][4:5]),
        "b_rel": (b_rel[0], m_b_rel[0], v_b_rel[0], g_rel),
    }
    for nm, (w, m, v, g) in small.items():
        upd[nm] = _adamw(f"adamw_{nm}", w, m, v, g)
    upd["kv_norm_g"] = [a.reshape(D) for a in upd["kv_norm_g"]]
    upd["kv_k_g"] = [a.reshape(HEAD_DIM) for a in upd["kv_k_g"]]
    upd["b_rel"] = [a[None] for a in upd["b_rel"]]
    finish("r4", total)
    upd["a_w_in"] = [jnp.swapaxes(o, 0, 1)[None] for o in upd["a_w_in"]]
    for nm, w in [("a_w_out", a_w_out), ("kv_w", kv_w), ("b_w_q", b_w_q), ("b_w_out", b_w_out), ("mlp_w1", mlp_w1),
                  ("mlp_w2", mlp_w2)]:
        upd[nm] = [o.reshape(w.shape) for o in upd[nm]]

    order = ["a_norm_g", "a_w_in", "a_b_f", "a_q_g", "a_k_g", "a_w_out", "mlp_norm_g", "mlp_w1", "mlp_w2",
             "kv_norm_g", "kv_w", "kv_k_g", "b_norm_g", "b_w_q", "b_q_g", "b_rel", "b_w_out"]
    return (loss, dx[None], *[upd[n][0] for n in order], *[upd[n][1] for n in order],
            *[upd[n][2] for n in order], *[upd[n][3] for n in order])
```

```python
import functools

import numpy as np
import jax
import jax.numpy as jnp
from jax import lax
from jax.experimental import pallas as pl
from jax.experimental.pallas import tpu as pltpu

F32 = jnp.float32
BF16 = jnp.bfloat16
HIGHEST = lax.Precision.HIGHEST
MESH = pl.DeviceIdType.MESH
ANY = pl.BlockSpec(memory_space=pl.ANY)

N_DEV = 8
N_CHIP = 4
LANES = 128
HEAD_DIM = 128
CHUNK = 64
N_PREV_CHUNKS = 8
REL_CLIP = 256
N_REL = REL_CLIP + CHUNK
N_REL_PAD = 384
WIN = N_PREV_CHUNKS * CHUNK
ROLL_W = 4 * WIN
EPS = 1e-6
NEG = -1e30
SCALE = HEAD_DIM ** -0.5
ADAM_LR, ADAM_B1, ADAM_B2, ADAM_EPS, ADAM_WD, ADAM_STEP = 0.001, 0.9, 0.999, 1e-08, 0.01, 10
V7X_VMEM_BYTES = 64 * 1024 * 1024
VMEM_LIMIT = V7X_VMEM_BYTES * 3 // 4

_NN = (((1,), (0,)), ((), ()))
_NT = (((1,), (1,)), ((), ()))
_TN = (((0,), (0,)), ((), ()))


def _params(*sem):
    return pltpu.CompilerParams(dimension_semantics=sem or None, vmem_limit_bytes=VMEM_LIMIT)


def _tile(n, pref):
    t = n
    while t > pref and t % 2 == 0:
        t //= 2
    return t


def _ep_store(acc, ex, outs):
    outs[0][...] = acc.astype(outs[0].dtype)


def _ep_add(acc, ex, outs):
    outs[0][...] = (acc + ex[0][...]).astype(outs[0].dtype)


def _ep_relu_square(acc, ex, outs):
    a = jnp.maximum(acc, 0.0)
    outs[0][...] = a.astype(outs[0].dtype)
    outs[1][...] = (a * a).astype(outs[1].dtype)


def _ep_times_2a(acc, ex, outs):
    outs[0][...] = (acc * (2.0 * ex[0][...].astype(F32))).astype(outs[0].dtype)


def _mm(name, kind, a, b, M, N, K, *, b_groups=0, out_groups=0, out_dtypes=(F32,), extras=(),
        epilogue=_ep_store, tm=1024, tn=1024, tk=2048, behind=()):
    tm, tn, tk = _tile(M, tm), _tile(N, tn), _tile(K, tk)
    if b_groups:
        cg = b.shape[-1]
        if kind == "nn":
            tn = min(tn, cg)
        else:
            tk = min(tk, cg)
    if out_groups:
        tn = min(tn, N // out_groups)
    grid = (M // tm, N // tn, K // tk)
    nk = grid[2]
    if kind == "nn":
        dims = _NN
        a_spec = pl.BlockSpec((tm, tk), lambda i, j, k: (i, k))
        if b_groups:
            npg = cg // tn
            b_spec = pl.BlockSpec((None, tk, tn), lambda i, j, k: (j // npg, k, j % npg))
        else:
            b_spec = pl.BlockSpec((tk, tn), lambda i, j, k: (k, j))
    elif kind == "nt":
        dims = _NT
        a_spec = pl.BlockSpec((tm, tk), lambda i, j, k: (i, k))
        if b_groups:
            npg = cg // tk
            b_spec = pl.BlockSpec((None, tn, tk), lambda i, j, k: (k // npg, j, k % npg))
        else:
            b_spec = pl.BlockSpec((tn, tk), lambda i, j, k: (j, k))
    else:
        dims = _TN
        a_spec = pl.BlockSpec((tk, tm), lambda i, j, k: (k, i))
        b_spec = pl.BlockSpec((tk, tn), lambda i, j, k: (k, j))
    tile_spec = pl.BlockSpec((tm, tn), lambda i, j, k: (i, j))
    if out_groups:
        ng = (N // out_groups) // tn
        out_spec = pl.BlockSpec((None, tm, tn), lambda i, j, k: (j // ng, i, j % ng))
        out_shape = [jax.ShapeDtypeStruct((out_groups, M, N // out_groups), d) for d in out_dtypes]
    else:
        out_spec = tile_spec
        out_shape = [jax.ShapeDtypeStruct((M, N), d) for d in out_dtypes]
    n_ex, n_out = len(extras), len(out_dtypes)

    def body(a_ref, b_ref, *rest):
        ex, outs, acc = rest[:n_ex], rest[-1 - n_out:-1], rest[-1]
        k = pl.program_id(2)

        @pl.when(k == 0)
        def _():
            acc[...] = jnp.zeros_like(acc)

        acc[...] += lax.dot_general(a_ref[...].astype(BF16), b_ref[...].astype(BF16), dims,
                                    preferred_element_type=F32)

        @pl.when(k == nk - 1)
        def _():
            epilogue(acc[...], ex, outs)

    res = pl.pallas_call(
        body, name=name, grid=grid,
        in_specs=[a_spec, b_spec] + [tile_spec] * n_ex + [ANY] * len(behind),
        out_specs=[out_spec] * n_out, out_shape=out_shape,
        scratch_shapes=[pltpu.VMEM((tm, tn), F32)],
        compiler_params=_params("parallel", "parallel", "arbitrary"),
    )(a, b, *extras, *behind)
    return res[0] if n_out == 1 else res


def _rmsnorm_fwd(name, x, gains):
    T, D = x.shape
    bt = _tile(T, 256)
    n = len(gains)

    def body(x_ref, *rest):
        xv = x_ref[...]
        y = xv * lax.rsqrt(jnp.mean(xv * xv, axis=-1, keepdims=True) + EPS)
        for g_ref, o_ref in zip(rest[:n], rest[n:]):
            o_ref[...] = (y * g_ref[...]).astype(BF16)

    row = pl.BlockSpec((bt, D), lambda i: (i, 0))
    gsp = pl.BlockSpec((1, D), lambda i: (0, 0))
    return pl.pallas_call(
        body, name=name, grid=(T // bt,), in_specs=[row] + [gsp] * n, out_specs=[row] * n,
        out_shape=[jax.ShapeDtypeStruct((T, D), BF16)] * n, compiler_params=_params("parallel"),
    )(x, *gains)


def _rmsnorm_bwd(name, x, g, dn, res):
    T, D = x.shape
    bt = _tile(T, 256)

    def body(x_ref, g_ref, dn_ref, res_ref, dx_ref, dx16_ref, dg_ref):
        @pl.when(pl.program_id(0) == 0)
        def _():
            dg_ref[...] = jnp.zeros_like(dg_ref)

        xv, dnv = x_ref[...], dn_ref[...]
        r = lax.rsqrt(jnp.mean(xv * xv, axis=-1, keepdims=True) + EPS)
        xhat = xv * r
        dyg = dnv * g_ref[...]
        dx = res_ref[...] + r * (dyg - xhat * jnp.mean(dyg * xhat, axis=-1, keepdims=True))
        dx_ref[...] = dx
        dx16_ref[...] = dx.astype(BF16)
        dg_ref[...] += jnp.sum(dnv * xhat, axis=0, keepdims=True)

    row = pl.BlockSpec((bt, D), lambda i: (i, 0))
    gsp = pl.BlockSpec((1, D), lambda i: (0, 0))
    return pl.pallas_call(
        body, name=name, grid=(T // bt,), in_specs=[row, gsp, row, row], out_specs=[row, row, gsp],
        out_shape=[jax.ShapeDtypeStruct((T, D), F32), jax.ShapeDtypeStruct((T, D), BF16),
                   jax.ShapeDtypeStruct((1, D), F32)],
        compiler_params=_params("arbitrary"),
    )(x, g, dn, res)


def _head_post(name, x, D, items):
    T = x.shape[0]
    H = D // HEAD_DIM
    bt = _tile(T, 256)
    gains = [g for _, g in items if g is not None]
    n_it, n_g = len(items), len(gains)

    def body(*refs):
        x_refs, g_refs, o_refs = refs[:n_it], refs[n_it:n_it + n_g], refs[n_it + n_g:]
        gi = 0
        for it, (_, g) in enumerate(items):
            if g is None:
                o_refs[it][...] = x_refs[it][...].astype(BF16)
                continue
            gv = g_refs[gi][...]
            gi += 1
            for h in range(H):
                sl = slice(h * HEAD_DIM, (h + 1) * HEAD_DIM)
                xs = x_refs[it][:, sl]
                r = lax.rsqrt(jnp.mean(xs * xs, axis=-1, keepdims=True) + EPS)
                o_refs[it][:, sl] = (xs * r * gv).astype(BF16)

    in_specs = [pl.BlockSpec((bt, D), functools.partial(lambda cb, i: (i, cb), cb)) for cb, _ in items]
    in_specs += [pl.BlockSpec((1, HEAD_DIM), lambda i: (0, 0))] * n_g
    row = pl.BlockSpec((bt, D), lambda i: (i, 0))
    res = pl.pallas_call(
        body, name=name, grid=(T // bt,), in_specs=in_specs, out_specs=[row] * n_it,
        out_shape=[jax.ShapeDtypeStruct((T, D), BF16)] * n_it, compiler_params=_params("parallel"),
    )(*([x] * n_it), *gains)
    return res


def _head_post_bwd(name, x, D, items):
    T = x.shape[0]
    H = D // HEAD_DIM
    bt = _tile(T, 128)
    gains = [g for _, g, _ in items if g is not None]
    n_it, n_g = len(items), len(gains)

    def body(*refs):
        x_refs, dy_refs = refs[:n_it], refs[n_it:2 * n_it]
        g_refs = refs[2 * n_it:2 * n_it + n_g]
        dx_ref = refs[2 * n_it + n_g]
        dg_refs = refs[2 * n_it + n_g + 1:]

        @pl.when(pl.program_id(0) == 0)
        def _():
            for r in dg_refs:
                r[...] = jnp.zeros_like(r)

        gi = 0
        for it, (_, g, _) in enumerate(items):
            if g is None:
                dx_ref[:, it * D:(it + 1) * D] = dy_refs[it][...].astype(BF16)
                continue
            gv = g_refs[gi][...]
            dg = jnp.zeros((1, HEAD_DIM), F32)
            for h in range(H):
                sl = slice(h * HEAD_DIM, (h + 1) * HEAD_DIM)
                xs, dy = x_refs[it][:, sl], dy_refs[it][:, sl]
                r = lax.rsqrt(jnp.mean(xs * xs, axis=-1, keepdims=True) + EPS)
                xhat = xs * r
                dyg = dy * gv
                dx = r * (dyg - xhat * jnp.mean(dyg * xhat, axis=-1, keepdims=True))
                dx_ref[:, it * D + h * HEAD_DIM:it * D + (h + 1) * HEAD_DIM] = dx.astype(BF16)
                dg = dg + jnp.sum(dy * xhat, axis=0, keepdims=True)
            dg_refs[gi][...] += dg
            gi += 1

    in_specs = [pl.BlockSpec((bt, D), functools.partial(lambda cb, i: (i, cb), cb)) for cb, _, _ in items]
    in_specs += [pl.BlockSpec((bt, D), lambda i: (i, 0))] * n_it
    gsp = pl.BlockSpec((1, HEAD_DIM), lambda i: (0, 0))
    in_specs += [gsp] * n_g
    res = pl.pallas_call(
        body, name=name, grid=(T // bt,), in_specs=in_specs,
        out_specs=[pl.BlockSpec((bt, n_it * D), lambda i: (i, 0))] + [gsp] * n_g,
        out_shape=[jax.ShapeDtypeStruct((T, n_it * D), BF16)] + [jax.ShapeDtypeStruct((1, HEAD_DIM), F32)] * n_g,
        compiler_params=_params("arbitrary"),
    )(*([x] * n_it), *[dy for _, _, dy in items], *gains)
    return res[0], list(res[1:])


def _gates_fwd(fz, bf, H):
    T = fz.shape[0]
    bt = _tile(T, 512)

    def body(fz_ref, bf_ref, c_ref, carry):
        @pl.when(pl.program_id(0) == 0)
        def _():
            carry[...] = jnp.zeros_like(carry)

        z = fz_ref[...] + bf_ref[...]
        logf = jnp.minimum(z, 0.0) - jnp.log(1.0 + jnp.exp(-jnp.abs(z)))
        lane = lax.broadcasted_iota(jnp.int32, (bt, LANES), 1)
        logf = jnp.where(lane < H, logf, 0.0)
        tri = (lax.broadcasted_iota(jnp.int32, (bt, bt), 0) >= lax.broadcasted_iota(jnp.int32, (bt, bt), 1))
        c = jnp.dot(tri.astype(F32), logf, precision=HIGHEST, preferred_element_type=F32) + carry[...]
        c_ref[...] = c
        carry[...] = c[bt - 1:bt, :]

    row = pl.BlockSpec((bt, LANES), lambda i: (i, 0))
    return pl.pallas_call(
        body, name="fox_gates_fwd", grid=(T // bt,),
        in_specs=[row, pl.BlockSpec((1, LANES), lambda i: (0, 0))], out_specs=row,
        out_shape=jax.ShapeDtypeStruct((T, LANES), F32), scratch_shapes=[pltpu.VMEM((1, LANES), F32)],
        compiler_params=_params("arbitrary"),
    )(fz, bf)


def _gates_bwd(dcq, dck, fz, bf, H):
    T = fz.shape[0]
    bt = _tile(T, 512)
    nb = T // bt

    def body(dcq_ref, dck_ref, fz_ref, bf_ref, dfz_ref, dbf_ref, carry):
        @pl.when(pl.program_id(0) == 0)
        def _():
            carry[...] = jnp.zeros_like(carry)
            dbf_ref[...] = jnp.zeros_like(dbf_ref)

        tri = (lax.broadcasted_iota(jnp.int32, (bt, bt), 0) <= lax.broadcasted_iota(jnp.int32, (bt, bt), 1))
        dc = dcq_ref[...] + dck_ref[...]
        dlogf = jnp.dot(tri.astype(F32), dc, precision=HIGHEST, preferred_element_type=F32) + carry[...]
        carry[...] = dlogf[0:1, :]
        z = fz_ref[...] + bf_ref[...]
        lane = lax.broadcasted_iota(jnp.int32, (bt, LANES), 1)
        dfz = jnp.where(lane < H, dlogf / (1.0 + jnp.exp(z)), 0.0)
        dfz_ref[...] = dfz
        dbf_ref[...] += jnp.sum(dfz, axis=0, keepdims=True)

    row = pl.BlockSpec((bt, LANES), lambda i: (nb - 1 - i, 0))
    one = pl.BlockSpec((1, LANES), lambda i: (0, 0))
    return pl.pallas_call(
        body, name="fox_gates_bwd", grid=(nb,), in_specs=[row, row, row, one], out_specs=[row, one],
        out_shape=[jax.ShapeDtypeStruct((T, LANES), F32), jax.ShapeDtypeStruct((1, LANES), F32)],
        scratch_shapes=[pltpu.VMEM((1, LANES), F32)], compiler_params=_params("arbitrary"),
    )(dcq, dck, fz, bf)


def _lane_pick(ref_value, h):
    lane = lax.broadcasted_iota(jnp.int32, (1, LANES), 1)
    return jnp.sum(jnp.where(lane == h, ref_value, 0.0), axis=1, keepdims=True)


def _lane_put(old, h, col):
    lane = lax.broadcasted_iota(jnp.int32, (1, LANES), 1)
    return jnp.where(lane == h, col, old)


FOX_HEADS = 4


def _causal(rows, cols, row0=0, transposed=False):
    row = row0 + lax.broadcasted_iota(jnp.int32, (rows, cols), 0)
    col = lax.broadcasted_iota(jnp.int32, (rows, cols), 1)
    return col >= row if transposed else row >= col


def _triangle(nb, groups, keys_outermost=False):
    if keys_outermost:
        steps = [(qi, g, kj) for kj in range(nb) for g in range(groups) for qi in range(kj, nb)]
    else:
        steps = [(qi, g, kj) for qi in range(nb) for g in range(groups) for kj in range(qi + 1)]
    return [jnp.asarray(np.array(col, np.int32)) for col in zip(*steps)]


def _fox_fwd(q, k, v, ccol, crow, blk):
    T, D = q.shape
    H = D // HEAD_DIM
    nb = T // blk
    steps = _triangle(nb, H // FOX_HEADS)

    def body(qi_ref, hg_ref, kj_ref, q_ref, k_ref, v_ref, ccol_ref, crow_ref, o_ref, lse_ref, m_s, l_s, acc_s):
        step_id = pl.program_id(0)
        qi, hg, kj = qi_ref[step_id], hg_ref[step_id], kj_ref[step_id]

        @pl.when(kj == 0)
        def _():
            m_s[...] = jnp.full_like(m_s, NEG)
            l_s[...] = jnp.zeros_like(l_s)
            acc_s[...] = jnp.zeros_like(acc_s)

        @pl.when((kj == 0) & (hg == 0))
        def _():
            lse_ref[...] = jnp.zeros_like(lse_ref)

        def step(diagonal):
            heads = range(FOX_HEADS)
            sls = [slice(hh * HEAD_DIM, (hh + 1) * HEAD_DIM) for hh in heads]
            m_old = [m_s[hh] for hh in heads]
            l_old = [l_s[hh] for hh in heads]
            acc_old = [acc_s[:, sl] for sl in sls]
            s = []
            for hh in heads:
                h = hg * FOX_HEADS + hh
                bias = _lane_pick(ccol_ref[...], h) - crow_ref[pl.ds(h, 1), :]
                sc = lax.dot_general(q_ref[:, sls[hh]], k_ref[:, sls[hh]], _NT, preferred_element_type=F32) * SCALE + bias
                s.append(jnp.where(_causal(blk, blk), sc, NEG) if diagonal else sc)
            m_new = [jnp.maximum(m_old[hh], jnp.max(s[hh], axis=1, keepdims=True)) for hh in heads]
            p = [jnp.exp(s[hh] - m_new[hh]) for hh in heads]
            alpha = [jnp.exp(m_old[hh] - m_new[hh]) for hh in heads]
            pv = [jnp.dot(p[hh].astype(BF16), v_ref[:, sls[hh]], preferred_element_type=F32) for hh in heads]
            for hh in heads:
                l_s[hh] = alpha[hh] * l_old[hh] + jnp.sum(p[hh], axis=1, keepdims=True)
                acc_s[:, sls[hh]] = alpha[hh] * acc_old[hh] + pv[hh]
                m_s[hh] = m_new[hh]

        @pl.when(kj < qi)
        def _():
            step(False)

        @pl.when(kj == qi)
        def _():
            step(True)
            for hh in range(FOX_HEADS):
                sl = slice(hh * HEAD_DIM, (hh + 1) * HEAD_DIM)
                o_ref[:, sl] = (acc_s[:, sl] / l_s[hh]).astype(BF16)
                lse_ref[...] = _lane_put(lse_ref[...], hg * FOX_HEADS + hh, m_s[hh] + jnp.log(l_s[hh]))

    wide = FOX_HEADS * HEAD_DIM
    qsp = pl.BlockSpec((blk, wide), lambda s, qi, hg, kj: (qi[s], hg[s]))
    ksp = pl.BlockSpec((blk, wide), lambda s, qi, hg, kj: (kj[s], hg[s]))
    stat = pl.BlockSpec((blk, LANES), lambda s, qi, hg, kj: (qi[s], 0))
    return pl.pallas_call(
        body, name="fox_fwd",
        grid_spec=pltpu.PrefetchScalarGridSpec(
            num_scalar_prefetch=3, grid=(steps[0].shape[0],),
            in_specs=[qsp, ksp, ksp, stat, pl.BlockSpec((H, blk), lambda s, qi, hg, kj: (0, kj[s]))],
            out_specs=[qsp, stat],
            scratch_shapes=[pltpu.VMEM((FOX_HEADS, blk, 1), F32), pltpu.VMEM((FOX_HEADS, blk, 1), F32),
                            pltpu.VMEM((blk, wide), F32)]),
        out_shape=[jax.ShapeDtypeStruct((T, D), BF16), jax.ShapeDtypeStruct((T, LANES), F32)],
        compiler_params=_params("arbitrary"),
    )(*steps, q, k, v, ccol, crow)


def _head_stat(name, a, b, behind=()):
    T, D = a.shape
    H = D // HEAD_DIM
    bt = _tile(T, 256)

    def body(a_ref, b_ref, *rest):
        o_ref = rest[-1]
        out = jnp.zeros((bt, LANES), F32)
        for h in range(H):
            sl = slice(h * HEAD_DIM, (h + 1) * HEAD_DIM)
            prod = a_ref[:, sl].astype(F32) * b_ref[:, sl].astype(F32)
            out = _lane_put(out, h, jnp.sum(prod, axis=1, keepdims=True))
        o_ref[...] = out

    row = pl.BlockSpec((bt, D), lambda i: (i, 0))
    return pl.pallas_call(
        body, name=name, grid=(T // bt,), in_specs=[row, row] + [ANY] * len(behind),
        out_specs=pl.BlockSpec((bt, LANES), lambda i: (i, 0)),
        out_shape=jax.ShapeDtypeStruct((T, LANES), F32), compiler_params=_params("parallel"),
    )(a, b, *behind)


def _fox_bwd_dq(q, k, v, do, ccol, crow, lse, delta, blk):
    T, D = q.shape
    H = D // HEAD_DIM
    nb = T // blk

    steps = _triangle(nb, H // FOX_HEADS)

    def body(qi_ref, hg_ref, kj_ref, q_ref, k_ref, v_ref, do_ref, ccol_ref, crow_ref, lse_ref, delta_ref,
             dq_ref, dc_ref, acc_s, dc_s):
        step_id = pl.program_id(0)
        qi, hg, kj = qi_ref[step_id], hg_ref[step_id], kj_ref[step_id]

        @pl.when(kj == 0)
        def _():
            acc_s[...] = jnp.zeros_like(acc_s)
            dc_s[...] = jnp.zeros_like(dc_s)

        @pl.when((kj == 0) & (hg == 0))
        def _():
            dc_ref[...] = jnp.zeros_like(dc_ref)

        def step(diagonal):
            heads = range(FOX_HEADS)
            sls = [slice(hh * HEAD_DIM, (hh + 1) * HEAD_DIM) for hh in heads]
            p, dp = [], []
            for hh in heads:
                h = hg * FOX_HEADS + hh
                bias = _lane_pick(ccol_ref[...] - lse_ref[...], h) - crow_ref[pl.ds(h, 1), :]
                sc = lax.dot_general(q_ref[:, sls[hh]], k_ref[:, sls[hh]], _NT, preferred_element_type=F32) * SCALE + bias
                if diagonal:
                    sc = jnp.where(_causal(blk, blk), sc, NEG)
                p.append(jnp.exp(sc))
                dp.append(lax.dot_general(do_ref[:, sls[hh]], v_ref[:, sls[hh]], _NT, preferred_element_type=F32)
                          - _lane_pick(delta_ref[...], h))
            ds = [p[hh] * dp[hh] for hh in heads]
            for hh in heads:
                acc_s[:, sls[hh]] += jnp.dot(ds[hh].astype(BF16), k_ref[:, sls[hh]], preferred_element_type=F32)
                dc_s[hh] += jnp.sum(ds[hh], axis=1, keepdims=True)

        @pl.when(kj < qi)
        def _():
            step(False)

        @pl.when(kj == qi)
        def _():
            step(True)
            dq_ref[...] = acc_s[...] * SCALE
            for hh in range(FOX_HEADS):
                dc_ref[...] = _lane_put(dc_ref[...], hg * FOX_HEADS + hh, dc_s[hh])

    wide = FOX_HEADS * HEAD_DIM
    qsp = pl.BlockSpec((blk, wide), lambda s, qi, hg, kj: (qi[s], hg[s]))
    ksp = pl.BlockSpec((blk, wide), lambda s, qi, hg, kj: (kj[s], hg[s]))
    stat = pl.BlockSpec((blk, LANES), lambda s, qi, hg, kj: (qi[s], 0))
    return pl.pallas_call(
        body, name="fox_bwd_dq",
        grid_spec=pltpu.PrefetchScalarGridSpec(
            num_scalar_prefetch=3, grid=(steps[0].shape[0],),
            in_specs=[qsp, ksp, ksp, qsp, stat, pl.BlockSpec((H, blk), lambda s, qi, hg, kj: (0, kj[s])), stat, stat],
            out_specs=[qsp, stat],
            scratch_shapes=[pltpu.VMEM((blk, wide), F32), pltpu.VMEM((FOX_HEADS, blk, 1), F32)]),
        out_shape=[jax.ShapeDtypeStruct((T, D), F32), jax.ShapeDtypeStruct((T, LANES), F32)],
        compiler_params=_params("arbitrary"),
    )(*steps, q, k, v, do, ccol, crow, lse, delta)


def _fox_bwd_dkv(q, k, v, do, ccol, crow, lse_row, delta_row, blk, behind=()):
    T, D = q.shape
    H = D // HEAD_DIM
    nb = T // blk

    steps = _triangle(nb, H // FOX_HEADS, keys_outermost=True)

    def body(qi_ref, hg_ref, kj_ref, q_ref, k_ref, v_ref, do_ref, ccol_ref, crow_ref, lse_ref, delta_ref, *rest):
        dk_ref, dv_ref, dc_ref, dk_s, dv_s, dc_s = rest[-6:]
        step_id = pl.program_id(0)
        qi, hg, kj = qi_ref[step_id], hg_ref[step_id], kj_ref[step_id]

        @pl.when(qi == kj)
        def _():
            dk_s[...] = jnp.zeros_like(dk_s)
            dv_s[...] = jnp.zeros_like(dv_s)
            dc_s[...] = jnp.zeros_like(dc_s)

        @pl.when((qi == kj) & (hg == 0))
        def _():
            dc_ref[...] = jnp.zeros_like(dc_ref)

        def step(diagonal):
            heads = range(FOX_HEADS)
            sls = [slice(hh * HEAD_DIM, (hh + 1) * HEAD_DIM) for hh in heads]
            pt, dpt = [], []
            for hh in heads:
                h = hg * FOX_HEADS + hh
                bias = (crow_ref[pl.ds(h, 1), :] - lse_ref[pl.ds(h, 1), :]) - _lane_pick(ccol_ref[...], h)
                st = lax.dot_general(k_ref[:, sls[hh]], q_ref[:, sls[hh]], _NT, preferred_element_type=F32) * SCALE + bias
                if diagonal:
                    st = jnp.where(_causal(blk, blk, transposed=True), st, NEG)
                pt.append(jnp.exp(st))
                dpt.append(lax.dot_general(v_ref[:, sls[hh]], do_ref[:, sls[hh]], _NT, preferred_element_type=F32)
                           - delta_ref[pl.ds(h, 1), :])
            dst = [pt[hh] * dpt[hh] for hh in heads]
            for hh in heads:
                dv_s[:, sls[hh]] += jnp.dot(pt[hh].astype(BF16), do_ref[:, sls[hh]], preferred_element_type=F32)
                dk_s[:, sls[hh]] += jnp.dot(dst[hh].astype(BF16), q_ref[:, sls[hh]], preferred_element_type=F32)
                dc_s[hh] -= jnp.sum(dst[hh], axis=1, keepdims=True)

        @pl.when(qi > kj)
        def _():
            step(False)

        @pl.when(qi == kj)
        def _():
            step(True)

        @pl.when(qi == nb - 1)
        def _():
            dk_ref[...] = dk_s[...] * SCALE
            dv_ref[...] = dv_s[...]
            for hh in range(FOX_HEADS):
                dc_ref[...] = _lane_put(dc_ref[...], hg * FOX_HEADS + hh, dc_s[hh])

    wide = FOX_HEADS * HEAD_DIM
    ksp = pl.BlockSpec((blk, wide), lambda s, qi, hg, kj: (kj[s], hg[s]))
    qsp = pl.BlockSpec((blk, wide), lambda s, qi, hg, kj: (qi[s], hg[s]))
    kstat = pl.BlockSpec((blk, LANES), lambda s, qi, hg, kj: (kj[s], 0))
    qrow = pl.BlockSpec((H, blk), lambda s, qi, hg, kj: (0, qi[s]))
    return pl.pallas_call(
        body, name="fox_bwd_dkv",
        grid_spec=pltpu.PrefetchScalarGridSpec(
            num_scalar_prefetch=3, grid=(steps[0].shape[0],),
            in_specs=[qsp, ksp, ksp, qsp, kstat, qrow, qrow, qrow] + [ANY] * len(behind), out_specs=[ksp, ksp, kstat],
            scratch_shapes=[pltpu.VMEM((blk, wide), F32), pltpu.VMEM((blk, wide), F32),
                            pltpu.VMEM((FOX_HEADS, blk, 1), F32)]),
        out_shape=[jax.ShapeDtypeStruct((T, D), F32), jax.ShapeDtypeStruct((T, D), F32),
                   jax.ShapeDtypeStruct((T, LANES), F32)],
        compiler_params=_params("arbitrary"),
    )(*steps, q, k, v, do, ccol, crow, lse_row, delta_row, *behind)


def _rel_index():
    j = np.arange(ROLL_W)
    u = np.where(j < 2 * WIN, j, j - ROLL_W)
    return np.clip(WIN - u, -(CHUNK - 1), REL_CLIP) + (CHUNK - 1)


def _band_bias(g_row, bias_s):
    base = jnp.broadcast_to(g_row, (CHUNK, ROLL_W))
    rowb = lax.broadcasted_iota(jnp.int32, (CHUNK, ROLL_W), 0)
    for bit in range(CHUNK.bit_length() - 1):
        base = jnp.where((rowb >> bit) & 1 == 1, pltpu.roll(base, 1 << bit, 1), base)
    kc = lax.broadcasted_iota(jnp.int32, (CHUNK, 2 * WIN), 1) // CHUNK
    for a in range(N_PREV_CHUNKS):
        part = base if a == 0 else pltpu.roll(base, CHUNK * a, 1)
        valid = (kc >= a) & (kc <= a + N_PREV_CHUNKS)
        bias_s[a * CHUNK:(a + 1) * CHUNK, :] = jnp.where(valid, part[:, :2 * WIN], NEG)


PARTS = 4
PART = WIN // PARTS
SPAN = WIN + PART


def _part(r):
    return slice(r * PART, (r + 1) * PART), slice(r * PART, r * PART + SPAN)


def _chunk_probs(q, kw, bias_s, qi, r):
    rows, keys = _part(r)
    s = lax.dot_general(q[rows], kw[keys], _NT, preferred_element_type=F32) * SCALE + bias_s[rows, keys]
    key = r * PART + lax.broadcasted_iota(jnp.int32, (PART, SPAN), 1)
    s = jnp.where((qi > 0) | (key >= WIN), s, NEG)
    p = jnp.exp(s - jnp.max(s, axis=1, keepdims=True))
    return p * (1.0 / jnp.sum(p, axis=1, keepdims=True))


def _chunk_fwd(q, k, v, g_rows):
    T, D = q.shape
    H = D // HEAD_DIM
    nq = T // WIN

    def body(q_ref, kp_ref, kc_ref, vp_ref, vc_ref, g_ref, o_ref, bias_s):
        qi = pl.program_id(1)

        @pl.when(qi == 0)
        def _():
            _band_bias(g_ref[...], bias_s)

        qv = q_ref[...]
        kw = jnp.concatenate([kp_ref[...], kc_ref[...]], axis=0)
        vw = jnp.concatenate([vp_ref[...], vc_ref[...]], axis=0)
        p = [_chunk_probs(qv, kw, bias_s, qi, r) for r in range(PARTS)]
        for r in range(PARTS):
            rows, keys = _part(r)
            o_ref[rows, :] = jnp.dot(p[r].astype(BF16), vw[keys], preferred_element_type=F32).astype(BF16)

    cur = pl.BlockSpec((WIN, HEAD_DIM), lambda h, qi: (qi, h))
    prev = pl.BlockSpec((WIN, HEAD_DIM), lambda h, qi: (jnp.maximum(qi - 1, 0), h))
    return pl.pallas_call(
        body, name="chunk_fwd", grid=(H, nq),
        in_specs=[cur, prev, cur, prev, cur, pl.BlockSpec((None, 1, ROLL_W), lambda h, qi: (h, 0, 0))],
        out_specs=cur, out_shape=jax.ShapeDtypeStruct((T, D), BF16),
        scratch_shapes=[pltpu.VMEM((WIN, 2 * WIN), F32)],
        compiler_params=_params("arbitrary", "arbitrary"),
    )(q, k, k, v, v, g_rows)


def _chunk_bwd(q, k, v, g_rows, do, o):
    T, D = q.shape
    H = D // HEAD_DIM
    nq = T // WIN

    def body(q_ref, kp_ref, kc_ref, vp_ref, vc_ref, g_ref, do_ref, o_ref,
             dq_ref, dk_ref, dv_ref, ds_ref, bias_s, ck_s, cv_s, dkw_s, dvw_s):
        qi = pl.program_id(1)

        @pl.when(qi == 0)
        def _():
            _band_bias(g_ref[...], bias_s)
            ds_ref[...] = jnp.zeros_like(ds_ref)
            ck_s[...] = jnp.zeros_like(ck_s)
            cv_s[...] = jnp.zeros_like(cv_s)

        @pl.when(qi < nq)
        def _():
            qv, dov = q_ref[...], do_ref[...]
            kw = jnp.concatenate([kp_ref[...], kc_ref[...]], axis=0)
            vw = jnp.concatenate([vp_ref[...], vc_ref[...]], axis=0)
            delta = jnp.sum(dov.astype(F32) * o_ref[...].astype(F32), axis=1, keepdims=True)
            p = [_chunk_probs(qv, kw, bias_s, qi, r) for r in range(PARTS)]
            ds = []
            for r in range(PARTS):
                rows, keys = _part(r)
                dp = lax.dot_general(dov[rows], vw[keys], _NT, preferred_element_type=F32) - delta[rows]
                ds.append(p[r] * dp)
            dkw_s[...] = jnp.zeros_like(dkw_s)
            dvw_s[...] = jnp.zeros_like(dvw_s)
            for r in range(PARTS):
                rows, keys = _part(r)
                ds16 = ds[r].astype(BF16)
                dq_ref[rows, :] = jnp.dot(ds16, kw[keys], preferred_element_type=F32) * SCALE
                ds_ref[rows, keys] += ds[r]
                dkw_s[keys, :] += lax.dot_general(ds16, qv[rows], _TN, preferred_element_type=F32) * SCALE
                dvw_s[keys, :] += lax.dot_general(p[r].astype(BF16), dov[rows], _TN, preferred_element_type=F32)
            dk_ref[...] = ck_s[...] + dkw_s[:WIN, :]
            dv_ref[...] = cv_s[...] + dvw_s[:WIN, :]
            ck_s[...] = dkw_s[WIN:, :]
            cv_s[...] = dvw_s[WIN:, :]

        @pl.when(qi == nq)
        def _():
            dk_ref[...] = ck_s[...]
            dv_ref[...] = cv_s[...]

    cur = pl.BlockSpec((WIN, HEAD_DIM), lambda h, qi: (jnp.minimum(qi, nq - 1), h))
    prev = pl.BlockSpec((WIN, HEAD_DIM), lambda h, qi: (jnp.clip(qi - 1, 0, nq - 2), h))
    late = pl.BlockSpec((WIN, HEAD_DIM), lambda h, qi: (jnp.maximum(qi - 1, 0), h))
    return pl.pallas_call(
        body, name="chunk_bwd", grid=(H, nq + 1),
        in_specs=[cur, prev, cur, prev, cur, pl.BlockSpec((None, 1, ROLL_W), lambda h, qi: (h, 0, 0)), cur, cur],
        out_specs=[cur, late, late, pl.BlockSpec((None, WIN, 2 * WIN), lambda h, qi: (h, 0, 0))],
        out_shape=[jax.ShapeDtypeStruct((T, D), F32)] * 3 + [jax.ShapeDtypeStruct((H, WIN, 2 * WIN), F32)],
        scratch_shapes=[pltpu.VMEM((WIN, 2 * WIN), F32), pltpu.VMEM((WIN, HEAD_DIM), F32),
                        pltpu.VMEM((WIN, HEAD_DIM), F32), pltpu.VMEM((2 * WIN, HEAD_DIM), F32),
                        pltpu.VMEM((2 * WIN, HEAD_DIM), F32)],
        compiler_params=_params("arbitrary", "arbitrary"),
    )(q, k, k, v, v, g_rows, do, o)


def _rel_rows(rel, onehot_t):
    H = rel.shape[0]

    def body(rel_ref, oh_ref, out_ref):
        out_ref[...] = jnp.dot(rel_ref[...], oh_ref[...], precision=HIGHEST, preferred_element_type=F32)

    return pl.pallas_call(body, name="rel_rows", out_shape=jax.ShapeDtypeStruct((H, ROLL_W), F32))(rel, onehot_t)


def _rel_grad(ds_sum, onehot):
    H = ds_sum.shape[0]

    def body(ds_ref, oh_ref, out_ref):
        y = jnp.zeros((CHUNK, ROLL_W), F32)
        pad = jnp.zeros((CHUNK, ROLL_W - 2 * WIN), F32)
        for a in range(N_PREV_CHUNKS):
            part = jnp.concatenate([ds_ref[a * CHUNK:(a + 1) * CHUNK, :], pad], axis=1)
            y = y + (part if a == 0 else pltpu.roll(part, ROLL_W - CHUNK * a, 1))
        rowb = lax.broadcasted_iota(jnp.int32, (CHUNK, ROLL_W), 0)
        for bit in range(CHUNK.bit_length() - 1):
            y = jnp.where((rowb >> bit) & 1 == 1, pltpu.roll(y, ROLL_W - (1 << bit), 1), y)
        diag = jnp.broadcast_to(jnp.sum(y, axis=0, keepdims=True), (8, ROLL_W))
        out_ref[...] = jnp.dot(diag, oh_ref[...], precision=HIGHEST, preferred_element_type=F32)

    return pl.pallas_call(
        body, name="rel_grad", grid=(H,),
        in_specs=[pl.BlockSpec((None, WIN, 2 * WIN), lambda h: (h, 0, 0)),
                  pl.BlockSpec((ROLL_W, N_REL_PAD), lambda h: (0, 0))],
        out_specs=pl.BlockSpec((None, 8, N_REL_PAD), lambda h: (h, 0, 0)),
        out_shape=jax.ShapeDtypeStruct((H, 8, N_REL_PAD), F32), compiler_params=_params("parallel"),
    )(ds_sum, onehot)


def _loss_head(y, target):
    T, D = y.shape
    bt = _tile(T, 256)

    def body(y_ref, t_ref, dy_ref, dy16_ref, loss_ref):
        @pl.when(pl.program_id(0) == 0)
        def _():
            loss_ref[...] = jnp.zeros_like(loss_ref)

        err = y_ref[...] - t_ref[...]
        dy = err * (1.0 / D)
        dy_ref[...] = dy
        dy16_ref[...] = dy.astype(BF16)
        loss_ref[...] += 0.5 * jnp.sum(jnp.mean(err * err, axis=-1, keepdims=True))

    row = pl.BlockSpec((bt, D), lambda i: (i, 0))
    one = pl.BlockSpec((8, LANES), lambda i: (0, 0))
    return pl.pallas_call(
        body, name="loss_head", grid=(T // bt,), in_specs=[row, row], out_specs=[row, row, one],
        out_shape=[jax.ShapeDtypeStruct((T, D), F32), jax.ShapeDtypeStruct((T, D), BF16),
                   jax.ShapeDtypeStruct((8, LANES), F32)],
        compiler_params=_params("arbitrary"),
    )(y, target)


def _adam_step(g, w_ref, m_ref, v_ref, g_ref, d_ref, nm_ref, nv_ref):
    nm = ADAM_B1 * m_ref[...] + (1.0 - ADAM_B1) * g
    nv = ADAM_B2 * v_ref[...] + (1.0 - ADAM_B2) * (g * g)
    m_hat = nm / (1.0 - ADAM_B1 ** ADAM_STEP)
    v_hat = nv / (1.0 - ADAM_B2 ** ADAM_STEP)
    g_ref[...] = g
    nm_ref[...] = nm
    nv_ref[...] = nv
    d_ref[...] = -ADAM_LR * (m_hat / (jnp.sqrt(v_hat) + ADAM_EPS) + ADAM_WD * w_ref[...])


def _adamw(name, w, m, v, g):
    def body(w_ref, m_ref, v_ref, gin_ref, g_ref, d_ref, nm_ref, nv_ref):
        _adam_step(gin_ref[...], w_ref, m_ref, v_ref, g_ref, d_ref, nm_ref, nv_ref)

    return pl.pallas_call(body, name=name, out_shape=[jax.ShapeDtypeStruct(w.shape, F32)] * 4)(w, m, v, g)


def _adamw_reduced(name, w, m, v, own, land, chip, row0=0, into=None):
    rows_all, C = w.shape
    _, R, _ = land.shape

    def body(chip_ref, w_ref, m_ref, v_ref, own_ref, land_ref, *rest):
        g = jnp.where(chip_ref[0] == 0, own_ref[...], land_ref[0]).astype(F32)
        for q in range(1, N_CHIP):
            g = g + jnp.where(chip_ref[0] == q, own_ref[...], land_ref[q]).astype(F32)
        _adam_step(g, w_ref, m_ref, v_ref, *rest[-4:])

    if R % 16 == 0:
        br = _tile(R, max(16, (128 * 1024) // C))
        first = row0 // br
        grid = (R // br,)
        wsp = pl.BlockSpec((br, C), lambda i, chip_ref: (first + i, 0))
        osp = pl.BlockSpec((None, br, C), lambda i, chip_ref: (chip_ref[0], i, 0))
        lsp = pl.BlockSpec((N_CHIP, br, C), lambda i, chip_ref: (0, i, 0))
    else:
        bc = _tile(C, 512)
        grid = (C // bc,)
        wsp = pl.BlockSpec((R, bc), lambda i, chip_ref: (0, i))
        osp = pl.BlockSpec((None, R, bc), lambda i, chip_ref: (chip_ref[0], 0, i))
        lsp = pl.BlockSpec((N_CHIP, R, bc), lambda i, chip_ref: (0, 0, i))
    extra = [] if into is None else list(into)
    return pl.pallas_call(
        body, name=name,
        grid_spec=pltpu.PrefetchScalarGridSpec(
            num_scalar_prefetch=1, grid=grid, in_specs=[wsp, wsp, wsp, osp, lsp] + [ANY] * len(extra),
            out_specs=[wsp] * 4),
        out_shape=[jax.ShapeDtypeStruct((rows_all, C), F32)] * 4,
        input_output_aliases={6 + j: j for j in range(len(extra))}, compiler_params=_params("parallel"),
    )(chip, w, m, v, own, land, *extra)


def _sum_parts(name, parts, out_dtype):
    P, R, C = parts.shape
    br = _tile(R, max(8, (256 * 1024) // C)) if R % 8 == 0 else R

    def body(p_ref, o_ref):
        g = p_ref[0].astype(F32)
        for i in range(1, P):
            g = g + p_ref[i].astype(F32)
        o_ref[...] = g.astype(out_dtype)

    return pl.pallas_call(
        body, name=name, grid=(R // br,),
        in_specs=[pl.BlockSpec((P, br, C), lambda i: (0, i, 0))], out_specs=pl.BlockSpec((br, C), lambda i: (i, 0)),
        out_shape=jax.ShapeDtypeStruct((R, C), out_dtype), compiler_params=_params("parallel"),
    )(parts)


def _pair_sum(name, g, recv, core):
    _, R, C = g.shape

    def body(core_ref, g_ref, r_ref, o_ref):
        o_ref[...] = (g_ref[...] + r_ref[...]).astype(BF16)

    if R % 16 == 0:
        br = _tile(R, max(16, (256 * 1024) // C))
        blk, n = (None, br, C), R // br
        mine = lambda q, i, core_ref: (2 * q + core_ref[0], i, 0)
        same = lambda q, i, core_ref: (q, i, 0)
    else:
        bc = _tile(C, 512)
        blk, n = (None, R, bc), C // bc
        mine = lambda q, i, core_ref: (2 * q + core_ref[0], 0, i)
        same = lambda q, i, core_ref: (q, 0, i)
    return pl.pallas_call(
        body, name=name,
        grid_spec=pltpu.PrefetchScalarGridSpec(
            num_scalar_prefetch=1, grid=(N_CHIP, n),
            in_specs=[pl.BlockSpec(blk, mine), pl.BlockSpec(blk, same)], out_specs=pl.BlockSpec(blk, same)),
        out_shape=jax.ShapeDtypeStruct((N_CHIP, R, C), BF16), compiler_params=_params("parallel", "parallel"),
    )(core, g, recv)


def _chips_of(x, y):
    return [(1 - x, y), (x, 1 - y), (1 - x, 1 - y)]


def _position():
    x, y, c = lax.axis_index("x"), lax.axis_index("y"), lax.axis_index("c")
    return x, y, c, _chips_of(x, y)


def _all_gather(name, xs):
    n = len(xs)

    def body(*refs):
        x_refs, out_refs = refs[:n], refs[n:2 * n]
        send_sems, recv_sems, local_sems = refs[2 * n:]
        x, y, c, chips = _position()
        me, sibling = (x, y, c), (x, y, 1 - c)

        def copy(a, k, block, to, src=None):
            px, py, pc = block
            slot = out_refs[a].at[4 * px + 2 * py + pc]
            return pltpu.make_async_remote_copy(
                src_ref=slot if src is None else src, dst_ref=slot,
                send_sem=send_sems.at[a, k], recv_sem=recv_sems.at[a, k], device_id=to, device_id_type=MESH)

        mine = [pltpu.make_async_copy(x_refs[a], out_refs[a].at[4 * x + 2 * y + c], local_sems.at[a]) for a in range(n)]
        for cp in mine:
            cp.start()
        first = []
        for a in range(n):
            first.append(copy(a, 0, me, sibling, src=x_refs[a]))
            first += [copy(a, 1 + j, me, (*chip, c), src=x_refs[a]) for j, chip in enumerate(chips)]
        for cp in first:
            cp.start()
        passed = []
        for j, chip in enumerate(chips):
            for a in range(n):
                copy(a, 1 + j, (*chip, c), me).wait_recv()
                fwd = copy(a, 4 + j, (*chip, c), sibling)
                fwd.start()
                passed.append(fwd)
        for a in range(n):
            copy(a, 0, sibling, me).wait_recv()
            for j, chip in enumerate(chips):
                copy(a, 4 + j, (*chip, 1 - c), me).wait_recv()
        for cp in first + passed:
            cp.wait_send()
        for cp in mine:
            cp.wait()

    return pl.pallas_call(
        body, name=name, in_specs=[ANY] * n, out_specs=[ANY] * n,
        out_shape=[jax.ShapeDtypeStruct((N_DEV,) + x.shape, x.dtype) for x in xs],
        scratch_shapes=[pltpu.SemaphoreType.DMA((n, 7)), pltpu.SemaphoreType.DMA((n, 7)),
                        pltpu.SemaphoreType.DMA((n,))],
    )(*xs)


HBM = pl.BlockSpec(memory_space=pltpu.HBM)
SEM = pl.BlockSpec(memory_space=pltpu.SEMAPHORE)
EFFECT = pltpu.SideEffectType.DATAFLOW_SIDE_EFFECTING


def _own_block_out(x, y, c):
    mine = 4 * x + 2 * y + c
    return [(mine, (x, y, 1 - c))] + [(mine, (cx, cy, c)) for cx, cy in _chips_of(x, y)]


def _own_blocks_in(x, y, c):
    return [(4 * x + 2 * y + 1 - c, (x, y, 1 - c))] + [(4 * cx + 2 * cy + c, (cx, cy, c)) for cx, cy in _chips_of(x, y)]


def _passed_on_out(x, y, c):
    return [(4 * cx + 2 * cy + c, (x, y, 1 - c)) for cx, cy in _chips_of(x, y)]


def _passed_on_in(x, y, c):
    return [(4 * cx + 2 * cy + 1 - c, (x, y, 1 - c)) for cx, cy in _chips_of(x, y)]


def _gather_start(name, lands, plan):
    n = len(lands)
    n_copies = len(plan(0, 0, 0))

    def body(*refs):
        land = refs[:n]
        send_sems, recv_sems = refs[n:2 * n], refs[2 * n:3 * n]
        token = refs[4 * n]
        x, y, c, _ = _position()
        for a in range(n):
            for k, (block, peer) in enumerate(plan(x, y, c)):
                pltpu.make_async_remote_copy(
                    src_ref=land[a].at[block], dst_ref=land[a].at[block], send_sem=send_sems[a].at[k],
                    recv_sem=recv_sems[a].at[k], device_id=peer, device_id_type=MESH).start()
        token[...] = jnp.zeros_like(token)

    res = pl.pallas_call(
        body, name=name, in_specs=[HBM] * n,
        out_specs=[SEM] * (2 * n) + [HBM] * n + [pl.BlockSpec(memory_space=pltpu.VMEM)],
        out_shape=[pltpu.SemaphoreType.DMA((n_copies,))] * (2 * n) + [pltpu.HBM(l.shape, l.dtype) for l in lands]
        + [jax.ShapeDtypeStruct((8, LANES), F32)],
        input_output_aliases={a: 2 * n + a for a in range(n)},
        compiler_params=pltpu.CompilerParams(has_side_effects=EFFECT),
    )(*[pltpu.with_memory_space_constraint(l, pltpu.HBM) for l in lands])
    return res[:n], res[n:2 * n], res[2 * n:3 * n], res[3 * n]


def _gather_wait(name, lands, send_sems, recv_sems, plan, after):
    n = len(lands)
    tail = [] if after is None else [after]

    def body(*refs):
        land = refs[:n]
        send, recv = refs[n:2 * n], refs[2 * n:3 * n]
        x, y, c, _ = _position()
        for a in range(n):
            for k, (block, peer) in enumerate(plan(x, y, c)):
                cp = pltpu.make_async_remote_copy(
                    src_ref=land[a].at[block], dst_ref=land[a].at[block], send_sem=send[a].at[k],
                    recv_sem=recv[a].at[k], device_id=peer, device_id_type=MESH)
                cp.wait_send()
                cp.wait_recv()

    return pl.pallas_call(
        body, name=name, in_specs=[HBM] * n + [SEM] * (2 * n) + [ANY] * len(tail), out_specs=[HBM] * n,
        out_shape=[pltpu.HBM(l.shape, l.dtype) for l in lands], input_output_aliases={a: a for a in range(n)},
        compiler_params=pltpu.CompilerParams(has_side_effects=EFFECT),
    )(*lands, *send_sems, *recv_sems, *tail)


def _split_start(name, arrays, lands, plan):
    n = len(arrays)
    n_copies = len(plan(0, 0, 0))

    def body(*refs):
        src, land = refs[:n], refs[n:2 * n]
        send_sems, recv_sems = refs[2 * n:3 * n], refs[3 * n:4 * n]
        token = refs[6 * n]
        x, y, c, _ = _position()
        for a in range(n):
            for k, (src_block, land_block, peer) in enumerate(plan(x, y, c)):
                pltpu.make_async_remote_copy(
                    src_ref=src[a].at[src_block], dst_ref=land[a].at[land_block], send_sem=send_sems[a].at[k],
                    recv_sem=recv_sems[a].at[k], device_id=peer, device_id_type=MESH).start()
        token[...] = jnp.zeros_like(token)

    both = list(arrays) + list(lands)
    res = pl.pallas_call(
        body, name=name, in_specs=[HBM] * (2 * n),
        out_specs=[SEM] * (2 * n) + [HBM] * (2 * n) + [pl.BlockSpec(memory_space=pltpu.VMEM)],
        out_shape=[pltpu.SemaphoreType.DMA((n_copies,))] * (2 * n) + [pltpu.HBM(b.shape, b.dtype) for b in both]
        + [jax.ShapeDtypeStruct((8, LANES), F32)],
        input_output_aliases={i: 2 * n + i for i in range(2 * n)},
        compiler_params=pltpu.CompilerParams(has_side_effects=EFFECT),
    )(*[pltpu.with_memory_space_constraint(b, pltpu.HBM) for b in both])
    return res[:n], res[n:2 * n], res[2 * n:3 * n], res[3 * n:4 * n], res[4 * n]


def _split_wait(name, arrays, lands, send_sems, recv_sems, plan, after):
    n = len(arrays)
    tail = [] if after is None else [after]

    def body(*refs):
        src, land = refs[:n], refs[n:2 * n]
        send, recv = refs[2 * n:3 * n], refs[3 * n:4 * n]
        x, y, c, _ = _position()
        for a in range(n):
            for k, (src_block, land_block, peer) in enumerate(plan(x, y, c)):
                cp = pltpu.make_async_remote_copy(
                    src_ref=src[a].at[src_block], dst_ref=land[a].at[land_block], send_sem=send[a].at[k],
                    recv_sem=recv[a].at[k], device_id=peer, device_id_type=MESH)
                cp.wait_send()
                cp.wait_recv()

    both = list(arrays) + list(lands)
    res = pl.pallas_call(
        body, name=name, in_specs=[HBM] * (2 * n) + [SEM] * (2 * n) + [ANY] * len(tail), out_specs=[HBM] * (2 * n),
        out_shape=[pltpu.HBM(b.shape, b.dtype) for b in both], input_output_aliases={i: i for i in range(2 * n)},
        compiler_params=pltpu.CompilerParams(has_side_effects=EFFECT),
    )(*both, *send_sems, *recv_sems, *tail)
    return res[:n], res[n:]


def _to_other_core(x, y, c):
    return [(2 * q + (1 - c), q, (x, y, 1 - c)) for q in range(N_CHIP)]


def _to_other_chips(x, y, c):
    return [(2 * cx + cy, 2 * x + y, (cx, cy, c)) for cx, cy in _chips_of(x, y)]


def _from_other_chips(x, y, c):
    return [(2 * x + y, 2 * cx + cy, (cx, cy, c)) for cx, cy in _chips_of(x, y)]


def _mlp_fwd(tag, h, g, w1g, w2):
    T, D = h.shape
    F = w2.shape[0]
    (n,) = _rmsnorm_fwd(f"mlp{tag}_norm", h, [g])
    a, hid = _mm(f"mlp{tag}_up", "nn", n, w1g, T, F, D, b_groups=N_DEV, out_dtypes=(BF16, BF16),
                 epilogue=_ep_relu_square)
    out = _mm(f"mlp{tag}_down", "nn", hid, w2, T, D, F, extras=(h,), epilogue=_ep_add)
    return out, (n, a, hid)


def _mlp_bwd(tag, h, g, w1g, w2, saved, dy, dy16, behind=()):
    T, D = h.shape
    F = w2.shape[0]
    n, a, hid = saved
    dw2 = _mm(f"mlp{tag}_dw2", "tn", hid, dy16, F, D, T, behind=behind)
    dpre = _mm(f"mlp{tag}_dpre", "nt", dy16, w2, T, F, D, out_dtypes=(BF16,), extras=(a,), epilogue=_ep_times_2a)
    dw1 = _mm(f"mlp{tag}_dw1", "tn", n, dpre, D, F, T, out_groups=N_DEV)
    dn = _mm(f"mlp{tag}_dn", "nt", dpre, w1g, T, D, F, b_groups=N_DEV)
    dh, dh16, dg = _rmsnorm_bwd(f"mlp{tag}_norm_bwd", h, g, dn, dy)
    return dh, dh16, dg, dw1, dw2


def _local_step(x, target, W, fox_blk, arrive=lambda group, after: {}, sync=lambda tag, after, G: ()):
    T, D = x.shape
    H = D // HEAD_DIM
    G = {}
    W = dict(W)

    (ua,) = _rmsnorm_fwd("a_norm", x, [W["a_norm_g"]])
    qkv = _mm("a_qkv", "nt", ua, W["a_w_in_t"], T, 3 * D, D)
    fz = _mm("a_fz", "nt", ua, W["a_w_f_t"], T, LANES, D)
    qa, ka, va = _head_post("a_heads", qkv, D, [(0, W["a_q_g"]), (1, W["a_k_g"]), (2, None)])
    ccol = _gates_fwd(fz, W["a_b_f"], H)
    crow = ccol[:, :H].T
    oa, lse = _fox_fwd(qa, ka, va, ccol, crow, fox_blk)
    W.update(arrive("a_out", oa))
    h1 = _mm("a_out", "nn", oa, W["a_w_out"], T, D, D, extras=(x,), epilogue=_ep_add)
    W.update(arrive("middle", h1))
    h2, mlp0 = _mlp_fwd("0", h1, W["mlp_norm_g0"], W["mlp_w1_0"], W["mlp_w2_0"])
    W.update(arrive("kv_q", h2))
    nkv, ub = _rmsnorm_fwd("kv_b_norm", h2, [W["kv_norm_g"], W["b_norm_g"]])
    kv = _mm("kv_proj", "nn", nkv, W["kv_w"], T, 2 * D, D, b_groups=N_DEV)
    ks, vs = _head_post("kv_heads", kv, D, [(0, W["kv_k_g"]), (1, None)])
    qb_pre = _mm("b_q", "nn", ub, W["b_w_q"], T, D, D)
    (qb,) = _head_post("b_heads", qb_pre, D, [(0, W["b_q_g"])])
    onehot = (jnp.asarray(_rel_index())[:, None] == jnp.arange(N_REL_PAD)[None, :]).astype(F32)
    rel_pad = jnp.pad(W["b_rel"], ((0, 0), (0, N_REL_PAD - N_REL)))
    g_rows = _rel_rows(rel_pad, onehot.T).reshape(H, 1, ROLL_W)
    ob = _chunk_fwd(qb, ks, vs, g_rows)
    W.update(arrive("late", ob))
    h3 = _mm("b_out", "nn", ob, W["b_w_out"], T, D, D, extras=(h2,), epilogue=_ep_add)
    h4, mlp1 = _mlp_fwd("1", h3, W["mlp_norm_g1"], W["mlp_w1_1"], W["mlp_w2_1"])
    dy, dy16, loss = _loss_head(h4, target)

    dh3, dh3_16, G["mlp_norm_g1"], G["mlp_w1_1"], G["mlp_w2_1"] = _mlp_bwd(
        "1", h3, W["mlp_norm_g1"], W["mlp_w1_1"], W["mlp_w2_1"], mlp1, dy, dy16)
    dob = _mm("b_out_dx", "nt", dh3_16, W["b_w_out"], T, D, D, out_dtypes=(BF16,), behind=sync("mlp1", dh3, G))
    G["b_w_out"] = _mm("b_out_dw", "tn", ob, dh3_16, D, D, T, behind=sync("b_out_dx", dob, G))
    dqb, dks, dvs, ds_sum = _chunk_bwd(qb, ks, vs, g_rows, dob, ob)
    G["b_rel"] = _rel_grad(ds_sum, onehot)[:, 0, :]
    dqb_pre, (G["b_q_g"],) = _head_post_bwd("b_heads_bwd", qb_pre, D, [(0, W["b_q_g"], dqb)])
    G["b_w_q"] = _mm("b_q_dw", "tn", ub, dqb_pre, D, D, T)
    dub = _mm("b_q_dx", "nt", dqb_pre, W["b_w_q"], T, D, D)
    dkv, (G["kv_k_g"],) = _head_post_bwd("kv_heads_bwd", kv, D, [(0, W["kv_k_g"], dks), (1, None, dvs)])
    G["kv_w"] = _mm("kv_dw", "tn", nkv, dkv, D, 2 * D, T, out_groups=N_DEV)
    dnkv = _mm("kv_dx", "nt", dkv, W["kv_w"], T, D, 2 * D, b_groups=N_DEV, behind=sync("kv_dw", G["kv_w"], G))
    dh2, _, G["b_norm_g"] = _rmsnorm_bwd("b_norm_bwd", h2, W["b_norm_g"], dub, dh3)
    dh2, dh2_16, G["kv_norm_g"] = _rmsnorm_bwd("kv_norm_bwd", h2, W["kv_norm_g"], dnkv, dh2)
    dh1, dh1_16, G["mlp_norm_g0"], G["mlp_w1_0"], G["mlp_w2_0"] = _mlp_bwd(
        "0", h1, W["mlp_norm_g0"], W["mlp_w1_0"], W["mlp_w2_0"], mlp0, dh2, dh2_16, behind=sync("dh2", dh2, G))
    G["a_w_out"] = _mm("a_out_dw", "tn", oa, dh1_16, D, D, T, behind=sync("mlp0", dh1, G))
    doa = _mm("a_out_dx", "nt", dh1_16, W["a_w_out"], T, D, D, out_dtypes=(BF16,))
    delta = _head_stat("fox_delta", doa, oa)
    dqa, dcq = _fox_bwd_dq(qa, ka, va, doa, ccol, crow, lse, delta, fox_blk)
    dka, dva, dck = _fox_bwd_dkv(qa, ka, va, doa, ccol, crow, lse[:, :H].T, delta[:, :H].T, fox_blk,
                                 behind=sync("fox_bwd_dq", dqa, G))
    dfz, G["a_b_f"] = _gates_bwd(dcq, dck, fz, W["a_b_f"], H)
    dqkv, (G["a_q_g"], G["a_k_g"]) = _head_post_bwd(
        "a_heads_bwd", qkv, D, [(0, W["a_q_g"], dqa), (1, W["a_k_g"], dka), (2, None, dva)])
    G["a_w_qkv_t"] = _mm("a_qkv_dw", "tn", dqkv, ua, 3 * D, D, T)
    G["a_w_f_t"] = _mm("a_fz_dw", "tn", dfz, ua, LANES, D, T)
    dua = _mm("a_qkv_dx", "nn", dqkv, W["a_w_in_t"], T, D, 3 * D, behind=sync("a_in_dw", G["a_w_f_t"], G))
    dua = _mm("a_fz_dx", "nn", dfz, W["a_w_f_t"], T, D, LANES, extras=(dua,), epilogue=_ep_add,
              behind=sync("a_qkv_dx", dua, G))
    dx, _, G["a_norm_g"] = _rmsnorm_bwd("a_norm_bwd", x, W["a_norm_g"], dua, dh1)
    return loss, dx, G


def _pad_lanes(a):
    return jnp.pad(a, ((0, 0), (0, LANES - a.shape[1])))


def kernel(x, a_norm_g, a_w_in, a_b_f, a_q_g, a_k_g, a_w_out, mlp_norm_g, mlp_w1, mlp_w2, kv_norm_g, kv_w, kv_k_g, b_norm_g, b_w_q, b_q_g, b_rel, b_w_out, loss_target, m_a_norm_g, m_a_w_in, m_a_b_f, m_a_q_g, m_a_k_g, m_a_w_out, m_mlp_norm_g, m_mlp_w1, m_mlp_w2, m_kv_norm_g, m_kv_w, m_kv_k_g, m_b_norm_g, m_b_w_q, m_b_q_g, m_b_rel, m_b_w_out, v_a_norm_g, v_a_w_in, v_a_b_f, v_a_q_g, v_a_k_g, v_a_w_out, v_mlp_norm_g, v_mlp_w1, v_mlp_w2, v_kv_norm_g, v_kv_w, v_kv_k_g, v_b_norm_g, v_b_w_q, v_b_q_g, v_b_rel, v_b_w_out):
    T, D = x.shape[1], x.shape[2]
    H = D // HEAD_DIM
    F = mlp_w2.shape[1] * N_DEV
    me = 4 * lax.axis_index("x") + 2 * lax.axis_index("y") + lax.axis_index("c")
    core = lax.axis_index("c").astype(jnp.int32).reshape(1)
    n_in = a_w_in.shape[2]
    n_rel = b_rel.shape[2]

    small_in = jnp.concatenate([a_norm_g.reshape(1, -1), b_rel.reshape(1, -1)], axis=1)
    local = [jnp.swapaxes(a_w_in[0], 0, 1), a_w_out[0], mlp_w1[0], mlp_w2[0], kv_w, b_w_q[0], b_w_out[0],
             mlp_w1[1], mlp_w2[1]]
    def landing(w):
        return lax.dynamic_update_slice(lax.empty((N_DEV,) + w.shape, w.dtype), w[None], (me,) + (0,) * w.ndim)

    send1, recv1, lands, started = _gather_start(
        "gather_start_first", [landing(small_in), landing(local[0].astype(BF16))], _own_block_out)
    more = _gather_start("gather_start_rest", [landing((w + started[0, 0]).astype(BF16)) for w in local[1:]],
                         _own_block_out)
    send1, recv1, lands, sems2 = send1 + more[0], recv1 + more[1], list(lands + more[2]), {}

    def pass_on(tag, idx, after):
        got = _gather_wait(f"gather_wait_{tag}", [lands[i] for i in idx], [send1[i] for i in idx],
                           [recv1[i] for i in idx], _own_blocks_in, after)
        send, recv, got, _ = _gather_start(f"gather_pass_{tag}", got, _passed_on_out)
        for k, i in enumerate(idx):
            lands[i], sems2[i] = got[k], (send[k], recv[k])

    def landed(tag, idx, after):
        got = _gather_wait(f"gather_landed_{tag}", [lands[i] for i in idx], [sems2[i][0] for i in idx],
                           [sems2[i][1] for i in idx], _passed_on_in, after)
        for k, i in enumerate(idx):
            lands[i] = got[k]
        return got

    def arrive(group, after):
        if group == "first":
            pass_on("first", [0, 1], after)
            g_small, w_in_t = landed("first", [0, 1], None)
            w_in_t = w_in_t.reshape(N_DEV * n_in, D)
            rel = jnp.transpose(g_small[:, 0, D // N_DEV:].reshape(N_DEV, H, n_rel), (1, 0, 2))
            return {"a_norm_g": g_small[:, 0, :D // N_DEV].reshape(1, D), "b_rel": rel.reshape(H, N_DEV * n_rel),
                    "a_w_in_t": w_in_t, "a_w_f_t": jnp.pad(w_in_t[3 * D:], ((0, LANES - H), (0, 0)))}
        if group == "a_out":
            pass_on("early", [2, 3, 4], after)
            return {"a_w_out": landed("a_out", [2], None)[0].reshape(D, D)}
        if group == "middle":
            got = landed("middle", [3, 4], after)
            return {"mlp_w1_0": got[0], "mlp_w2_0": got[1].reshape(F, D)}
        if group == "kv_q":
            pass_on("late", [5, 6, 7, 8, 9], after)
            got = landed("kv_q", [5, 6], None)
            return {"kv_w": got[0], "b_w_q": got[1].reshape(D, D)}
        got = landed("late", [7, 8, 9], after)
        return {"b_w_out": got[0].reshape(D, D), "mlp_w1_1": got[1], "mlp_w2_1": got[2].reshape(F, D)}

    W = {
        "a_b_f": _pad_lanes(a_b_f), "a_q_g": a_q_g, "a_k_g": a_k_g,
        "mlp_norm_g0": mlp_norm_g[0:1], "mlp_norm_g1": mlp_norm_g[1:2],
        "kv_norm_g": kv_norm_g.reshape(1, D), "kv_k_g": kv_k_g.reshape(1, HEAD_DIM),
        "b_norm_g": b_norm_g, "b_q_g": b_q_g, **arrive("first", more[3]),
    }

    chip = (2 * lax.axis_index("x") + lax.axis_index("y")).astype(jnp.int32).reshape(1)
    flat = lambda a: a.reshape(-1, a.shape[-1])
    state = {"a_w_in": [jnp.swapaxes(a[0], 0, 1) for a in (a_w_in, m_a_w_in, v_a_w_in)]}
    for nm, w, m, v in [("a_w_out", a_w_out, m_a_w_out, v_a_w_out), ("kv_w", kv_w, m_kv_w, v_kv_w),
                        ("b_w_q", b_w_q, m_b_w_q, v_b_w_q), ("b_w_out", b_w_out, m_b_w_out, v_b_w_out),
                        ("mlp_w1", mlp_w1, m_mlp_w1, v_mlp_w1), ("mlp_w2", mlp_w2, m_mlp_w2, v_mlp_w2)]:
        state[nm] = [flat(w), flat(m), flat(v)]
    upd, flying = {}, {}

    def chunks(nm, G):
        if nm == "a_w_in":
            return jnp.concatenate([G["a_w_qkv_t"], G["a_w_f_t"][:H]], axis=0).reshape(N_DEV, n_in, D)
        g = G[nm]
        return g if g.ndim == 3 else g.reshape(N_DEV, g.shape[0] // N_DEV, g.shape[1])

    def to_core(tag, names, G):
        gs = [chunks(nm, G) for nm in names]
        lands = [lax.empty((N_CHIP,) + g.shape[1:], g.dtype) for g in gs]
        send, recv, gs, lands, token = _split_start(f"reduce_cores_start_{tag}", gs, lands, _to_other_core)
        flying[tag] = (names, send, recv, gs, lands)
        return (token,)

    def to_chips(tag, after):
        names, send, recv, gs, lands = flying[tag]
        gs, lands = _split_wait(f"reduce_cores_wait_{tag}", gs, lands, send, recv, _to_other_core, after)
        ps = [_pair_sum(f"pair_sum_{nm}", g, r, core) for nm, g, r in zip(names, gs, lands)]
        lands = [lax.empty(p.shape, p.dtype) for p in ps]
        send, recv, ps, lands, token = _split_start(f"reduce_chips_start_{tag}", ps, lands, _to_other_chips)
        flying[tag] = (names, send, recv, ps, lands)
        return (token,)

    def finish(tag, after):
        names, send, recv, ps, lands = flying.pop(tag)
        ps, lands = _split_wait(f"reduce_chips_wait_{tag}", ps, lands, send, recv, _from_other_chips, after)
        for nm, own, land in zip(names, ps, lands):
            if nm.startswith("mlp_w"):
                base, layer = nm[:-2], int(nm[-1])
                w, m, v = state[base]
                upd[base] = _adamw_reduced(f"adamw_{nm}", w, m, v, own, land, chip, row0=layer * (w.shape[0] // 2),
                                           into=upd.get(base))
            else:
                upd[nm] = _adamw_reduced(f"adamw_{nm}", *state[nm], own, land, chip)

    def sync(tag, after, G):
        if tag == "mlp1":
            return to_core("r1", ["mlp_w2_1", "mlp_w1_1"], G)
        if tag == "b_out_dx":
            return to_chips("r1", after)
        if tag == "kv_dw":
            return to_core("r2", ["b_w_out", "b_w_q", "kv_w"], G)
        if tag == "dh2":
            return to_chips("r2", after)
        if tag == "mlp0":
            return to_core("r3", ["mlp_w2_0", "mlp_w1_0"], G)
        if tag == "fox_bwd_dq":
            finish("r1", after)
            finish("r2", after)
            return to_chips("r3", after)
        if tag == "a_in_dw":
            return to_core("r4", ["a_w_out", "a_w_in"], G)
        if tag == "a_qkv_dx":
            return to_chips("r4", after)
        return ()

    loss_tile, dx, G = _local_step(x[0], loss_target[0], W, min(T, 512), arrive, sync)
    finish("r3", dx)

    def rows8(vecs):
        r = lax.broadcasted_iota(jnp.int32, (8, LANES), 0)
        tile = jnp.zeros((8, LANES), F32)
        for i, vec in enumerate(vecs):
            tile = jnp.where(r == i, vec, tile)
        return tile

    rows = lambda a: a.reshape(-1, LANES)
    packed = [rows(G["a_norm_g"]), rows(G["mlp_norm_g0"]), rows(G["mlp_norm_g1"]), rows(G["kv_norm_g"]),
              rows(G["b_norm_g"]), rows(G["b_rel"]),
              rows8([G["a_b_f"], G["a_q_g"], G["a_k_g"], G["kv_k_g"], G["b_q_g"], loss_tile[0:1]])]
    sizes = [p.shape[0] for p in packed]
    (all_small,) = _all_gather("gather_small_grads", [jnp.concatenate(packed, axis=0)])
    total = _sum_parts("sum_small_grads", all_small, F32)
    offs = np.concatenate([[0], np.cumsum(sizes)])
    s = [total[offs[i]:offs[i + 1]] for i in range(len(sizes))]
    loss = s[6][5, 0]
    g_a_norm = lax.dynamic_slice(s[0].reshape(1, D), (0, me * (D // N_DEV)), (1, D // N_DEV))
    g_rel = lax.dynamic_slice(s[5].reshape(H, N_REL_PAD), (0, me * n_rel), (H, n_rel))
    small = {
        "a_norm_g": (a_norm_g, m_a_norm_g, v_a_norm_g, g_a_norm),
        "a_b_f": (a_b_f, m_a_b_f, v_a_b_f, s[6][0:1, :H]),
        "a_q_g": (a_q_g, m_a_q_g, v_a_q_g, s[6][1:2]),
        "a_k_g": (a_k_g, m_a_k_g, v_a_k_g, s[6][2:3]),
        "mlp_norm_g": (mlp_norm_g, m_mlp_norm_g, v_mlp_norm_g,
                       jnp.concatenate([s[1].reshape(1, D), s[2].reshape(1, D)], axis=0)),
        "kv_norm_g": (kv_norm_g.reshape(1, D), m_kv_norm_g.reshape(1, D), v_kv_norm_g.reshape(1, D), s[3].reshape(1, D)),
        "kv_k_g": (kv_k_g.reshape(1, HEAD_DIM), m_kv_k_g.reshape(1, HEAD_DIM), v_kv_k_g.reshape(1, HEAD_DIM),
                   s[6][3:4]),
        "b_norm_g": (b_norm_g, m_b_norm_g, v_b_norm_g, s[4].reshape(1, D)),
        "b_q_g": (b_q_g, m_b_q_g, v_b_q_g, s[6][4:5]),
        "b_rel": (b_rel[0], m_b_rel[0], v_b_rel[0], g_rel),
    }
    for nm, (w, m, v, g) in small.items():
        upd[nm] = _adamw(f"adamw_{nm}", w, m, v, g)
    upd["kv_norm_g"] = [a.reshape(D) for a in upd["kv_norm_g"]]
    upd["kv_k_g"] = [a.reshape(HEAD_DIM) for a in upd["kv_k_g"]]
    upd["b_rel"] = [a[None] for a in upd["b_rel"]]
    finish("r4", total)
    upd["a_w_in"] = [jnp.swapaxes(o, 0, 1)[None] for o in upd["a_w_in"]]
    for nm, w in [("a_w_out", a_w_out), ("kv_w", kv_w), ("b_w_q", b_w_q), ("b_w_out", b_w_out), ("mlp_w1", mlp_w1),
                  ("mlp_w2", mlp_w2)]:
        upd[nm] = [o.reshape(w.shape) for o in upd[nm]]

    order = ["a_norm_g", "a_w_in", "a_b_f", "a_q_g", "a_k_g", "a_w_out", "mlp_norm_g", "mlp_w1", "mlp_w2",
             "kv_norm_g", "kv_w", "kv_k_g", "b_norm_g", "b_w_q", "b_q_g", "b_rel", "b_w_out"]
    return (loss, dx[None], *[upd[n][0] for n in order], *[upd[n][1] for n in order],
            *[upd[n][2] for n in order], *[upd[n][3] for n in order])
```

```python
import functools

import numpy as np
import jax
import jax.numpy as jnp
from jax import lax
from jax.experimental import pallas as pl
from jax.experimental.pallas import tpu as pltpu

F32 = jnp.float32
BF16 = jnp.bfloat16
HIGHEST = lax.Precision.HIGHEST
MESH = pl.DeviceIdType.MESH
ANY = pl.BlockSpec(memory_space=pl.ANY)

N_DEV = 8
N_CHIP = 4
LANES = 128
HEAD_DIM = 128
CHUNK = 64
N_PREV_CHUNKS = 8
REL_CLIP = 256
N_REL = REL_CLIP + CHUNK
N_REL_PAD = 384
WIN = N_PREV_CHUNKS * CHUNK
ROLL_W = 4 * WIN
EPS = 1e-6
NEG = -1e30
SCALE = HEAD_DIM ** -0.5
ADAM_LR, ADAM_B1, ADAM_B2, ADAM_EPS, ADAM_WD, ADAM_STEP = 0.001, 0.9, 0.999, 1e-08, 0.01, 10
V7X_VMEM_BYTES = 64 * 1024 * 1024
VMEM_LIMIT = V7X_VMEM_BYTES * 3 // 4

_NN = (((1,), (0,)), ((), ()))
_NT = (((1,), (1,)), ((), ()))
_TN = (((0,), (0,)), ((), ()))


def _params(*sem):
    return pltpu.CompilerParams(dimension_semantics=sem or None, vmem_limit_bytes=VMEM_LIMIT)


def _tile(n, pref):
    t = n
    while t > pref and t % 2 == 0:
        t //= 2
    return t


def _ep_store(acc, ex, outs):
    outs[0][...] = acc.astype(outs[0].dtype)


def _ep_add(acc, ex, outs):
    outs[0][...] = (acc + ex[0][...]).astype(outs[0].dtype)


def _ep_relu_square(acc, ex, outs):
    a = jnp.maximum(acc, 0.0)
    outs[0][...] = a.astype(outs[0].dtype)
    outs[1][...] = (a * a).astype(outs[1].dtype)


def _ep_times_2a(acc, ex, outs):
    outs[0][...] = (acc * (2.0 * ex[0][...].astype(F32))).astype(outs[0].dtype)


def _mm(name, kind, a, b, M, N, K, *, b_groups=0, out_groups=0, out_dtypes=(F32,), extras=(),
        epilogue=_ep_store, tm=1024, tn=1024, tk=2048, behind=()):
    tm, tn, tk = _tile(M, tm), _tile(N, tn), _tile(K, tk)
    if b_groups:
        cg = b.shape[-1]
        if kind == "nn":
            tn = min(tn, cg)
        else:
            tk = min(tk, cg)
    if out_groups:
        tn = min(tn, N // out_groups)
    grid = (M // tm, N // tn, K // tk)
    nk = grid[2]
    if kind == "nn":
        dims = _NN
        a_spec = pl.BlockSpec((tm, tk), lambda i, j, k: (i, k))
        if b_groups:
            npg = cg // tn
            b_spec = pl.BlockSpec((None, tk, tn), lambda i, j, k: (j // npg, k, j % npg))
        else:
            b_spec = pl.BlockSpec((tk, tn), lambda i, j, k: (k, j))
    elif kind == "nt":
        dims = _NT
        a_spec = pl.BlockSpec((tm, tk), lambda i, j, k: (i, k))
        if b_groups:
            npg = cg // tk
            b_spec = pl.BlockSpec((None, tn, tk), lambda i, j, k: (k // npg, j, k % npg))
        else:
            b_spec = pl.BlockSpec((tn, tk), lambda i, j, k: (j, k))
    else:
        dims = _TN
        a_spec = pl.BlockSpec((tk, tm), lambda i, j, k: (k, i))
        b_spec = pl.BlockSpec((tk, tn), lambda i, j, k: (k, j))
    tile_spec = pl.BlockSpec((tm, tn), lambda i, j, k: (i, j))
    if out_groups:
        ng = (N // out_groups) // tn
        out_spec = pl.BlockSpec((None, tm, tn), lambda i, j, k: (j // ng, i, j % ng))
        out_shape = [jax.ShapeDtypeStruct((out_groups, M, N // out_groups), d) for d in out_dtypes]
    else:
        out_spec = tile_spec
        out_shape = [jax.ShapeDtypeStruct((M, N), d) for d in out_dtypes]
    n_ex, n_out = len(extras), len(out_dtypes)

    def body(a_ref, b_ref, *rest):
        ex, outs, acc = rest[:n_ex], rest[-1 - n_out:-1], rest[-1]
        k = pl.program_id(2)

        @pl.when(k == 0)
        def _():
            acc[...] = jnp.zeros_like(acc)

        acc[...] += lax.dot_general(a_ref[...].astype(BF16), b_ref[...].astype(BF16), dims,
                                    preferred_element_type=F32)

        @pl.when(k == nk - 1)
        def _():
            epilogue(acc[...], ex, outs)

    res = pl.pallas_call(
        body, name=name, grid=grid,
        in_specs=[a_spec, b_spec] + [tile_spec] * n_ex + [ANY] * len(behind),
        out_specs=[out_spec] * n_out, out_shape=out_shape,
        scratch_shapes=[pltpu.VMEM((tm, tn), F32)],
        compiler_params=_params("parallel", "parallel", "arbitrary"),
    )(a, b, *extras, *behind)
    return res[0] if n_out == 1 else res


def _rmsnorm_fwd(name, x, gains):
    T, D = x.shape
    bt = _tile(T, 256)
    n = len(gains)

    def body(x_ref, *rest):
        xv = x_ref[...]
        y = xv * lax.rsqrt(jnp.mean(xv * xv, axis=-1, keepdims=True) + EPS)
        for g_ref, o_ref in zip(rest[:n], rest[n:]):
            o_ref[...] = (y * g_ref[...]).astype(BF16)

    row = pl.BlockSpec((bt, D), lambda i: (i, 0))
    gsp = pl.BlockSpec((1, D), lambda i: (0, 0))
    return pl.pallas_call(
        body, name=name, grid=(T // bt,), in_specs=[row] + [gsp] * n, out_specs=[row] * n,
        out_shape=[jax.ShapeDtypeStruct((T, D), BF16)] * n, compiler_params=_params("parallel"),
    )(x, *gains)


def _rmsnorm_bwd(name, x, g, dn, res):
    T, D = x.shape
    bt = _tile(T, 256)

    def body(x_ref, g_ref, dn_ref, res_ref, dx_ref, dx16_ref, dg_ref):
        @pl.when(pl.program_id(0) == 0)
        def _():
            dg_ref[...] = jnp.zeros_like(dg_ref)

        xv, dnv = x_ref[...], dn_ref[...]
        r = lax.rsqrt(jnp.mean(xv * xv, axis=-1, keepdims=True) + EPS)
        xhat = xv * r
        dyg = dnv * g_ref[...]
        dx = res_ref[...] + r * (dyg - xhat * jnp.mean(dyg * xhat, axis=-1, keepdims=True))
        dx_ref[...] = dx
        dx16_ref[...] = dx.astype(BF16)
        dg_ref[...] += jnp.sum(dnv * xhat, axis=0, keepdims=True)

    row = pl.BlockSpec((bt, D), lambda i: (i, 0))
    gsp = pl.BlockSpec((1, D), lambda i: (0, 0))
    return pl.pallas_call(
        body, name=name, grid=(T // bt,), in_specs=[row, gsp, row, row], out_specs=[row, row, gsp],
        out_shape=[jax.ShapeDtypeStruct((T, D), F32), jax.ShapeDtypeStruct((T, D), BF16),
                   jax.ShapeDtypeStruct((1, D), F32)],
        compiler_params=_params("arbitrary"),
    )(x, g, dn, res)


def _head_post(name, x, D, items):
    T = x.shape[0]
    H = D // HEAD_DIM
    bt = _tile(T, 256)
    gains = [g for _, g in items if g is not None]
    n_it, n_g = len(items), len(gains)

    def body(*refs):
        x_refs, g_refs, o_refs = refs[:n_it], refs[n_it:n_it + n_g], refs[n_it + n_g:]
        gi = 0
        for it, (_, g) in enumerate(items):
            if g is None:
                o_refs[it][...] = x_refs[it][...].astype(BF16)
                continue
            gv = g_refs[gi][...]
            gi += 1
            for h in range(H):
                sl = slice(h * HEAD_DIM, (h + 1) * HEAD_DIM)
                xs = x_refs[it][:, sl]
                r = lax.rsqrt(jnp.mean(xs * xs, axis=-1, keepdims=True) + EPS)
                o_refs[it][:, sl] = (xs * r * gv).astype(BF16)

    in_specs = [pl.BlockSpec((bt, D), functools.partial(lambda cb, i: (i, cb), cb)) for cb, _ in items]
    in_specs += [pl.BlockSpec((1, HEAD_DIM), lambda i: (0, 0))] * n_g
    row = pl.BlockSpec((bt, D), lambda i: (i, 0))
    res = pl.pallas_call(
        body, name=name, grid=(T // bt,), in_specs=in_specs, out_specs=[row] * n_it,
        out_shape=[jax.ShapeDtypeStruct((T, D), BF16)] * n_it, compiler_params=_params("parallel"),
    )(*([x] * n_it), *gains)
    return res


def _head_post_bwd(name, x, D, items):
    T = x.shape[0]
    H = D // HEAD_DIM
    bt = _tile(T, 128)
    gains = [g for _, g, _ in items if g is not None]
    n_it, n_g = len(items), len(gains)

    def body(*refs):
        x_refs, dy_refs = refs[:n_it], refs[n_it:2 * n_it]
        g_refs = refs[2 * n_it:2 * n_it + n_g]
        dx_ref = refs[2 * n_it + n_g]
        dg_refs = refs[2 * n_it + n_g + 1:]

        @pl.when(pl.program_id(0) == 0)
        def _():
            for r in dg_refs:
                r[...] = jnp.zeros_like(r)

        gi = 0
        for it, (_, g, _) in enumerate(items):
            if g is None:
                dx_ref[:, it * D:(it + 1) * D] = dy_refs[it][...].astype(BF16)
                continue
            gv = g_refs[gi][...]
            dg = jnp.zeros((1, HEAD_DIM), F32)
            for h in range(H):
                sl = slice(h * HEAD_DIM, (h + 1) * HEAD_DIM)
                xs, dy = x_refs[it][:, sl], dy_refs[it][:, sl]
                r = lax.rsqrt(jnp.mean(xs * xs, axis=-1, keepdims=True) + EPS)
                xhat = xs * r
                dyg = dy * gv
                dx = r * (dyg - xhat * jnp.mean(dyg * xhat, axis=-1, keepdims=True))
                dx_ref[:, it * D + h * HEAD_DIM:it * D + (h + 1) * HEAD_DIM] = dx.astype(BF16)
                dg = dg + jnp.sum(dy * xhat, axis=0, keepdims=True)
            dg_refs[gi][...] += dg
            gi += 1

    in_specs = [pl.BlockSpec((bt, D), functools.partial(lambda cb, i: (i, cb), cb)) for cb, _, _ in items]
    in_specs += [pl.BlockSpec((bt, D), lambda i: (i, 0))] * n_it
    gsp = pl.BlockSpec((1, HEAD_DIM), lambda i: (0, 0))
    in_specs += [gsp] * n_g
    res = pl.pallas_call(
        body, name=name, grid=(T // bt,), in_specs=in_specs,
        out_specs=[pl.BlockSpec((bt, n_it * D), lambda i: (i, 0))] + [gsp] * n_g,
        out_shape=[jax.ShapeDtypeStruct((T, n_it * D), BF16)] + [jax.ShapeDtypeStruct((1, HEAD_DIM), F32)] * n_g,
        compiler_params=_params("arbitrary"),
    )(*([x] * n_it), *[dy for _, _, dy in items], *gains)
    return res[0], list(res[1:])


def _gates_fwd(fz, bf, H):
    T = fz.shape[0]
    bt = _tile(T, 512)

    def body(fz_ref, bf_ref, c_ref, carry):
        @pl.when(pl.program_id(0) == 0)
        def _():
            carry[...] = jnp.zeros_like(carry)

        z = fz_ref[...] + bf_ref[...]
        logf = jnp.minimum(z, 0.0) - jnp.log(1.0 + jnp.exp(-jnp.abs(z)))
        lane = lax.broadcasted_iota(jnp.int32, (bt, LANES), 1)
        logf = jnp.where(lane < H, logf, 0.0)
        tri = (lax.broadcasted_iota(jnp.int32, (bt, bt), 0) >= lax.broadcasted_iota(jnp.int32, (bt, bt), 1))
        c = jnp.dot(tri.astype(F32), logf, precision=HIGHEST, preferred_element_type=F32) + carry[...]
        c_ref[...] = c
        carry[...] = c[bt - 1:bt, :]

    row = pl.BlockSpec((bt, LANES), lambda i: (i, 0))
    return pl.pallas_call(
        body, name="fox_gates_fwd", grid=(T // bt,),
        in_specs=[row, pl.BlockSpec((1, LANES), lambda i: (0, 0))], out_specs=row,
        out_shape=jax.ShapeDtypeStruct((T, LANES), F32), scratch_shapes=[pltpu.VMEM((1, LANES), F32)],
        compiler_params=_params("arbitrary"),
    )(fz, bf)


def _gates_bwd(dcq, dck, fz, bf, H):
    T = fz.shape[0]
    bt = _tile(T, 512)
    nb = T // bt

    def body(dcq_ref, dck_ref, fz_ref, bf_ref, dfz_ref, dbf_ref, carry):
        @pl.when(pl.program_id(0) == 0)
        def _():
            carry[...] = jnp.zeros_like(carry)
            dbf_ref[...] = jnp.zeros_like(dbf_ref)

        tri = (lax.broadcasted_iota(jnp.int32, (bt, bt), 0) <= lax.broadcasted_iota(jnp.int32, (bt, bt), 1))
        dc = dcq_ref[...] + dck_ref[...]
        dlogf = jnp.dot(tri.astype(F32), dc, precision=HIGHEST, preferred_element_type=F32) + carry[...]
        carry[...] = dlogf[0:1, :]
        z = fz_ref[...] + bf_ref[...]
        lane = lax.broadcasted_iota(jnp.int32, (bt, LANES), 1)
        dfz = jnp.where(lane < H, dlogf / (1.0 + jnp.exp(z)), 0.0)
        dfz_ref[...] = dfz
        dbf_ref[...] += jnp.sum(dfz, axis=0, keepdims=True)

    row = pl.BlockSpec((bt, LANES), lambda i: (nb - 1 - i, 0))
    one = pl.BlockSpec((1, LANES), lambda i: (0, 0))
    return pl.pallas_call(
        body, name="fox_gates_bwd", grid=(nb,), in_specs=[row, row, row, one], out_specs=[row, one],
        out_shape=[jax.ShapeDtypeStruct((T, LANES), F32), jax.ShapeDtypeStruct((1, LANES), F32)],
        scratch_shapes=[pltpu.VMEM((1, LANES), F32)], compiler_params=_params("arbitrary"),
    )(dcq, dck, fz, bf)


def _lane_pick(ref_value, h):
    lane = lax.broadcasted_iota(jnp.int32, (1, LANES), 1)
    return jnp.sum(jnp.where(lane == h, ref_value, 0.0), axis=1, keepdims=True)


def _lane_put(old, h, col):
    lane = lax.broadcasted_iota(jnp.int32, (1, LANES), 1)
    return jnp.where(lane == h, col, old)


FOX_HEADS = 8


def _causal(rows, cols, row0=0, transposed=False):
    row = row0 + lax.broadcasted_iota(jnp.int32, (rows, cols), 0)
    col = lax.broadcasted_iota(jnp.int32, (rows, cols), 1)
    return col >= row if transposed else row >= col


def _triangle(nb, groups, keys_outermost=False):
    if keys_outermost:
        steps = [(qi, g, kj) for kj in range(nb) for g in range(groups) for qi in range(kj, nb)]
    else:
        steps = [(qi, g, kj) for qi in range(nb) for g in range(groups) for kj in range(qi + 1)]
    return [jnp.asarray(np.array(col, np.int32)) for col in zip(*steps)]


def _fox_fwd(q, k, v, ccol, crow, blk):
    T, D = q.shape
    H = D // HEAD_DIM
    nb = T // blk
    steps = _triangle(nb, H // FOX_HEADS)

    def body(qi_ref, hg_ref, kj_ref, q_ref, k_ref, v_ref, ccol_ref, crow_ref, o_ref, lse_ref, m_s, l_s, acc_s):
        step_id = pl.program_id(0)
        qi, hg, kj = qi_ref[step_id], hg_ref[step_id], kj_ref[step_id]

        @pl.when(kj == 0)
        def _():
            m_s[...] = jnp.full_like(m_s, NEG)
            l_s[...] = jnp.zeros_like(l_s)
            acc_s[...] = jnp.zeros_like(acc_s)

        @pl.when((kj == 0) & (hg == 0))
        def _():
            lse_ref[...] = jnp.zeros_like(lse_ref)

        def step(diagonal):
            heads = range(FOX_HEADS)
            sls = [slice(hh * HEAD_DIM, (hh + 1) * HEAD_DIM) for hh in heads]
            m_old = [m_s[hh] for hh in heads]
            l_old = [l_s[hh] for hh in heads]
            acc_old = [acc_s[:, sl] for sl in sls]
            s = []
            for hh in heads:
                h = hg * FOX_HEADS + hh
                bias = _lane_pick(ccol_ref[...], h) - crow_ref[pl.ds(h, 1), :]
                sc = lax.dot_general(q_ref[:, sls[hh]], k_ref[:, sls[hh]], _NT, preferred_element_type=F32) * SCALE + bias
                s.append(jnp.where(_causal(blk, blk), sc, NEG) if diagonal else sc)
            m_new = [jnp.maximum(m_old[hh], jnp.max(s[hh], axis=1, keepdims=True)) for hh in heads]
            p = [jnp.exp(s[hh] - m_new[hh]) for hh in heads]
            alpha = [jnp.exp(m_old[hh] - m_new[hh]) for hh in heads]
            pv = [jnp.dot(p[hh].astype(BF16), v_ref[:, sls[hh]], preferred_element_type=F32) for hh in heads]
            for hh in heads:
                l_s[hh] = alpha[hh] * l_old[hh] + jnp.sum(p[hh], axis=1, keepdims=True)
                acc_s[:, sls[hh]] = alpha[hh] * acc_old[hh] + pv[hh]
                m_s[hh] = m_new[hh]

        @pl.when(kj < qi)
        def _():
            step(False)

        @pl.when(kj == qi)
        def _():
            step(True)
            for hh in range(FOX_HEADS):
                sl = slice(hh * HEAD_DIM, (hh + 1) * HEAD_DIM)
                o_ref[:, sl] = (acc_s[:, sl] / l_s[hh]).astype(BF16)
                lse_ref[...] = _lane_put(lse_ref[...], hg * FOX_HEADS + hh, m_s[hh] + jnp.log(l_s[hh]))

    wide = FOX_HEADS * HEAD_DIM
    qsp = pl.BlockSpec((blk, wide), lambda s, qi, hg, kj: (qi[s], hg[s]))
    ksp = pl.BlockSpec((blk, wide), lambda s, qi, hg, kj: (kj[s], hg[s]))
    stat = pl.BlockSpec((blk, LANES), lambda s, qi, hg, kj: (qi[s], 0))
    return pl.pallas_call(
        body, name="fox_fwd",
        grid_spec=pltpu.PrefetchScalarGridSpec(
            num_scalar_prefetch=3, grid=(steps[0].shape[0],),
            in_specs=[qsp, ksp, ksp, stat, pl.BlockSpec((H, blk), lambda s, qi, hg, kj: (0, kj[s]))],
            out_specs=[qsp, stat],
            scratch_shapes=[pltpu.VMEM((FOX_HEADS, blk, 1), F32), pltpu.VMEM((FOX_HEADS, blk, 1), F32),
                            pltpu.VMEM((blk, wide), F32)]),
        out_shape=[jax.ShapeDtypeStruct((T, D), BF16), jax.ShapeDtypeStruct((T, LANES), F32)],
        compiler_params=_params("arbitrary"),
    )(*steps, q, k, v, ccol, crow)


def _head_stat(name, a, b, behind=()):
    T, D = a.shape
    H = D // HEAD_DIM
    bt = _tile(T, 256)

    def body(a_ref, b_ref, *rest):
        o_ref = rest[-1]
        out = jnp.zeros((bt, LANES), F32)
        for h in range(H):
            sl = slice(h * HEAD_DIM, (h + 1) * HEAD_DIM)
            prod = a_ref[:, sl].astype(F32) * b_ref[:, sl].astype(F32)
            out = _lane_put(out, h, jnp.sum(prod, axis=1, keepdims=True))
        o_ref[...] = out

    row = pl.BlockSpec((bt, D), lambda i: (i, 0))
    return pl.pallas_call(
        body, name=name, grid=(T // bt,), in_specs=[row, row] + [ANY] * len(behind),
        out_specs=pl.BlockSpec((bt, LANES), lambda i: (i, 0)),
        out_shape=jax.ShapeDtypeStruct((T, LANES), F32), compiler_params=_params("parallel"),
    )(a, b, *behind)


def _fox_bwd_dq(q, k, v, do, ccol, crow, lse, delta, blk):
    T, D = q.shape
    H = D // HEAD_DIM
    nb = T // blk

    steps = _triangle(nb, H // FOX_HEADS)

    def body(qi_ref, hg_ref, kj_ref, q_ref, k_ref, v_ref, do_ref, ccol_ref, crow_ref, lse_ref, delta_ref,
             dq_ref, dc_ref, acc_s, dc_s):
        step_id = pl.program_id(0)
        qi, hg, kj = qi_ref[step_id], hg_ref[step_id], kj_ref[step_id]

        @pl.when(kj == 0)
        def _():
            acc_s[...] = jnp.zeros_like(acc_s)
            dc_s[...] = jnp.zeros_like(dc_s)

        @pl.when((kj == 0) & (hg == 0))
        def _():
            dc_ref[...] = jnp.zeros_like(dc_ref)

        def step(diagonal):
            heads = range(FOX_HEADS)
            sls = [slice(hh * HEAD_DIM, (hh + 1) * HEAD_DIM) for hh in heads]
            p, dp = [], []
            for hh in heads:
                h = hg * FOX_HEADS + hh
                bias = _lane_pick(ccol_ref[...] - lse_ref[...], h) - crow_ref[pl.ds(h, 1), :]
                sc = lax.dot_general(q_ref[:, sls[hh]], k_ref[:, sls[hh]], _NT, preferred_element_type=F32) * SCALE + bias
                if diagonal:
                    sc = jnp.where(_causal(blk, blk), sc, NEG)
                p.append(jnp.exp(sc))
                dp.append(lax.dot_general(do_ref[:, sls[hh]], v_ref[:, sls[hh]], _NT, preferred_element_type=F32)
                          - _lane_pick(delta_ref[...], h))
            ds = [p[hh] * dp[hh] for hh in heads]
            for hh in heads:
                acc_s[:, sls[hh]] += jnp.dot(ds[hh].astype(BF16), k_ref[:, sls[hh]], preferred_element_type=F32)
                dc_s[hh] += jnp.sum(ds[hh], axis=1, keepdims=True)

        @pl.when(kj < qi)
        def _():
            step(False)

        @pl.when(kj == qi)
        def _():
            step(True)
            dq_ref[...] = acc_s[...] * SCALE
            for hh in range(FOX_HEADS):
                dc_ref[...] = _lane_put(dc_ref[...], hg * FOX_HEADS + hh, dc_s[hh])

    wide = FOX_HEADS * HEAD_DIM
    qsp = pl.BlockSpec((blk, wide), lambda s, qi, hg, kj: (qi[s], hg[s]))
    ksp = pl.BlockSpec((blk, wide), lambda s, qi, hg, kj: (kj[s], hg[s]))
    stat = pl.BlockSpec((blk, LANES), lambda s, qi, hg, kj: (qi[s], 0))
    return pl.pallas_call(
        body, name="fox_bwd_dq",
        grid_spec=pltpu.PrefetchScalarGridSpec(
            num_scalar_prefetch=3, grid=(steps[0].shape[0],),
            in_specs=[qsp, ksp, ksp, qsp, stat, pl.BlockSpec((H, blk), lambda s, qi, hg, kj: (0, kj[s])), stat, stat],
            out_specs=[qsp, stat],
            scratch_shapes=[pltpu.VMEM((blk, wide), F32), pltpu.VMEM((FOX_HEADS, blk, 1), F32)]),
        out_shape=[jax.ShapeDtypeStruct((T, D), F32), jax.ShapeDtypeStruct((T, LANES), F32)],
        compiler_params=_params("arbitrary"),
    )(*steps, q, k, v, do, ccol, crow, lse, delta)


def _fox_bwd_dkv(q, k, v, do, ccol, crow, lse_row, delta_row, blk, behind=()):
    T, D = q.shape
    H = D // HEAD_DIM
    nb = T // blk

    steps = _triangle(nb, H // FOX_HEADS, keys_outermost=True)

    def body(qi_ref, hg_ref, kj_ref, q_ref, k_ref, v_ref, do_ref, ccol_ref, crow_ref, lse_ref, delta_ref, *rest):
        dk_ref, dv_ref, dc_ref, dk_s, dv_s, dc_s = rest[-6:]
        step_id = pl.program_id(0)
        qi, hg, kj = qi_ref[step_id], hg_ref[step_id], kj_ref[step_id]

        @pl.when(qi == kj)
        def _():
            dk_s[...] = jnp.zeros_like(dk_s)
            dv_s[...] = jnp.zeros_like(dv_s)
            dc_s[...] = jnp.zeros_like(dc_s)

        @pl.when((qi == kj) & (hg == 0))
        def _():
            dc_ref[...] = jnp.zeros_like(dc_ref)

        def step(diagonal):
            heads = range(FOX_HEADS)
            sls = [slice(hh * HEAD_DIM, (hh + 1) * HEAD_DIM) for hh in heads]
            pt, dpt = [], []
            for hh in heads:
                h = hg * FOX_HEADS + hh
                bias = (crow_ref[pl.ds(h, 1), :] - lse_ref[pl.ds(h, 1), :]) - _lane_pick(ccol_ref[...], h)
                st = lax.dot_general(k_ref[:, sls[hh]], q_ref[:, sls[hh]], _NT, preferred_element_type=F32) * SCALE + bias
                if diagonal:
                    st = jnp.where(_causal(blk, blk, transposed=True), st, NEG)
                pt.append(jnp.exp(st))
                dpt.append(lax.dot_general(v_ref[:, sls[hh]], do_ref[:, sls[hh]], _NT, preferred_element_type=F32)
                           - delta_ref[pl.ds(h, 1), :])
            dst = [pt[hh] * dpt[hh] for hh in heads]
            for hh in heads:
                dv_s[:, sls[hh]] += jnp.dot(pt[hh].astype(BF16), do_ref[:, sls[hh]], preferred_element_type=F32)
                dk_s[:, sls[hh]] += jnp.dot(dst[hh].astype(BF16), q_ref[:, sls[hh]], preferred_element_type=F32)
                dc_s[hh] -= jnp.sum(dst[hh], axis=1, keepdims=True)

        @pl.when(qi > kj)
        def _():
            step(False)

        @pl.when(qi == kj)
        def _():
            step(True)

        @pl.when(qi == nb - 1)
        def _():
            dk_ref[...] = dk_s[...] * SCALE
            dv_ref[...] = dv_s[...]
            for hh in range(FOX_HEADS):
                dc_ref[...] = _lane_put(dc_ref[...], hg * FOX_HEADS + hh, dc_s[hh])

    wide = FOX_HEADS * HEAD_DIM
    ksp = pl.BlockSpec((blk, wide), lambda s, qi, hg, kj: (kj[s], hg[s]))
    qsp = pl.BlockSpec((blk, wide), lambda s, qi, hg, kj: (qi[s], hg[s]))
    kstat = pl.BlockSpec((blk, LANES), lambda s, qi, hg, kj: (kj[s], 0))
    qrow = pl.BlockSpec((H, blk), lambda s, qi, hg, kj: (0, qi[s]))
    return pl.pallas_call(
        body, name="fox_bwd_dkv",
        grid_spec=pltpu.PrefetchScalarGridSpec(
            num_scalar_prefetch=3, grid=(steps[0].shape[0],),
            in_specs=[qsp, ksp, ksp, qsp, kstat, qrow, qrow, qrow] + [ANY] * len(behind), out_specs=[ksp, ksp, kstat],
            scratch_shapes=[pltpu.VMEM((blk, wide), F32), pltpu.VMEM((blk, wide), F32),
                            pltpu.VMEM((FOX_HEADS, blk, 1), F32)]),
        out_shape=[jax.ShapeDtypeStruct((T, D), F32), jax.ShapeDtypeStruct((T, D), F32),
                   jax.ShapeDtypeStruct((T, LANES), F32)],
        compiler_params=_params("arbitrary"),
    )(*steps, q, k, v, do, ccol, crow, lse_row, delta_row, *behind)


def _rel_index():
    j = np.arange(ROLL_W)
    u = np.where(j < 2 * WIN, j, j - ROLL_W)
    return np.clip(WIN - u, -(CHUNK - 1), REL_CLIP) + (CHUNK - 1)


def _band_bias(g_row, bias_s):
    base = jnp.broadcast_to(g_row, (CHUNK, ROLL_W))
    rowb = lax.broadcasted_iota(jnp.int32, (CHUNK, ROLL_W), 0)
    for bit in range(CHUNK.bit_length() - 1):
        base = jnp.where((rowb >> bit) & 1 == 1, pltpu.roll(base, 1 << bit, 1), base)
    kc = lax.broadcasted_iota(jnp.int32, (CHUNK, 2 * WIN), 1) // CHUNK
    for a in range(N_PREV_CHUNKS):
        part = base if a == 0 else pltpu.roll(base, CHUNK * a, 1)
        valid = (kc >= a) & (kc <= a + N_PREV_CHUNKS)
        bias_s[a * CHUNK:(a + 1) * CHUNK, :] = jnp.where(valid, part[:, :2 * WIN], NEG)


HALF = WIN // 2
SPAN = WIN + HALF


def _half(r):
    return slice(r * HALF, (r + 1) * HALF), slice(r * HALF, r * HALF + SPAN)


def _chunk_probs(q, kw, bias_s, qi, r):
    rows, keys = _half(r)
    s = lax.dot_general(q[rows], kw[keys], _NT, preferred_element_type=F32) * SCALE + bias_s[rows, keys]
    key = r * HALF + lax.broadcasted_iota(jnp.int32, (HALF, SPAN), 1)
    s = jnp.where((qi > 0) | (key >= WIN), s, NEG)
    p = jnp.exp(s - jnp.max(s, axis=1, keepdims=True))
    return p * (1.0 / jnp.sum(p, axis=1, keepdims=True))


def _chunk_fwd(q, k, v, g_rows):
    T, D = q.shape
    H = D // HEAD_DIM
    nq = T // WIN

    def body(q_ref, kp_ref, kc_ref, vp_ref, vc_ref, g_ref, o_ref, bias_s):
        qi = pl.program_id(1)

        @pl.when(qi == 0)
        def _():
            _band_bias(g_ref[...], bias_s)

        qv = q_ref[...]
        kw = jnp.concatenate([kp_ref[...], kc_ref[...]], axis=0)
        vw = jnp.concatenate([vp_ref[...], vc_ref[...]], axis=0)
        p = [_chunk_probs(qv, kw, bias_s, qi, r) for r in range(2)]
        for r in range(2):
            rows, keys = _half(r)
            o_ref[rows, :] = jnp.dot(p[r].astype(BF16), vw[keys], preferred_element_type=F32).astype(BF16)

    cur = pl.BlockSpec((WIN, HEAD_DIM), lambda h, qi: (qi, h))
    prev = pl.BlockSpec((WIN, HEAD_DIM), lambda h, qi: (jnp.maximum(qi - 1, 0), h))
    return pl.pallas_call(
        body, name="chunk_fwd", grid=(H, nq),
        in_specs=[cur, prev, cur, prev, cur, pl.BlockSpec((None, 1, ROLL_W), lambda h, qi: (h, 0, 0))],
        out_specs=cur, out_shape=jax.ShapeDtypeStruct((T, D), BF16),
        scratch_shapes=[pltpu.VMEM((WIN, 2 * WIN), F32)],
        compiler_params=_params("arbitrary", "arbitrary"),
    )(q, k, k, v, v, g_rows)


def _chunk_bwd(q, k, v, g_rows, do, o):
    T, D = q.shape
    H = D // HEAD_DIM
    nq = T // WIN

    def body(q_ref, kp_ref, kc_ref, vp_ref, vc_ref, g_ref, do_ref, o_ref,
             dq_ref, dk_ref, dv_ref, ds_ref, bias_s, ck_s, cv_s):
        qi = pl.program_id(1)

        @pl.when(qi == 0)
        def _():
            _band_bias(g_ref[...], bias_s)
            ds_ref[...] = jnp.zeros_like(ds_ref)
            ck_s[...] = jnp.zeros_like(ck_s)
            cv_s[...] = jnp.zeros_like(cv_s)

        @pl.when(qi < nq)
        def _():
            qv, dov = q_ref[...], do_ref[...]
            kw = jnp.concatenate([kp_ref[...], kc_ref[...]], axis=0)
            vw = jnp.concatenate([vp_ref[...], vc_ref[...]], axis=0)
            delta = jnp.sum(dov.astype(F32) * o_ref[...].astype(F32), axis=1, keepdims=True)
            p = [_chunk_probs(qv, kw, bias_s, qi, r) for r in range(2)]
            ds, dk, dv = [], [], []
            for r in range(2):
                rows, keys = _half(r)
                dp = lax.dot_general(dov[rows], vw[keys], _NT, preferred_element_type=F32) - delta[rows]
                ds.append(p[r] * dp)
            for r in range(2):
                rows, keys = _half(r)
                ds16 = ds[r].astype(BF16)
                dq_ref[rows, :] = jnp.dot(ds16, kw[keys], preferred_element_type=F32) * SCALE
                ds_ref[rows, keys] += ds[r]
                dk.append(lax.dot_general(ds16, qv[rows], _TN, preferred_element_type=F32) * SCALE)
                dv.append(lax.dot_general(p[r].astype(BF16), dov[rows], _TN, preferred_element_type=F32))

            def split(parts):
                lo, hi = parts
                prev_blk = jnp.concatenate([lo[:HALF], lo[HALF:WIN] + hi[:HALF]], axis=0)
                own_blk = jnp.concatenate([lo[WIN:] + hi[HALF:WIN], hi[WIN:]], axis=0)
                return prev_blk, own_blk

            dk_prev, dk_own = split(dk)
            dv_prev, dv_own = split(dv)
            dk_ref[...] = ck_s[...] + dk_prev
            dv_ref[...] = cv_s[...] + dv_prev
            ck_s[...] = dk_own
            cv_s[...] = dv_own

        @pl.when(qi == nq)
        def _():
            dk_ref[...] = ck_s[...]
            dv_ref[...] = cv_s[...]

    cur = pl.BlockSpec((WIN, HEAD_DIM), lambda h, qi: (jnp.minimum(qi, nq - 1), h))
    prev = pl.BlockSpec((WIN, HEAD_DIM), lambda h, qi: (jnp.clip(qi - 1, 0, nq - 2), h))
    late = pl.BlockSpec((WIN, HEAD_DIM), lambda h, qi: (jnp.maximum(qi - 1, 0), h))
    return pl.pallas_call(
        body, name="chunk_bwd", grid=(H, nq + 1),
        in_specs=[cur, prev, cur, prev, cur, pl.BlockSpec((None, 1, ROLL_W), lambda h, qi: (h, 0, 0)), cur, cur],
        out_specs=[cur, late, late, pl.BlockSpec((None, WIN, 2 * WIN), lambda h, qi: (h, 0, 0))],
        out_shape=[jax.ShapeDtypeStruct((T, D), F32)] * 3 + [jax.ShapeDtypeStruct((H, WIN, 2 * WIN), F32)],
        scratch_shapes=[pltpu.VMEM((WIN, 2 * WIN), F32), pltpu.VMEM((WIN, HEAD_DIM), F32),
                        pltpu.VMEM((WIN, HEAD_DIM), F32)],
        compiler_params=_params("arbitrary", "arbitrary"),
    )(q, k, k, v, v, g_rows, do, o)


def _rel_rows(rel, onehot_t):
    H = rel.shape[0]

    def body(rel_ref, oh_ref, out_ref):
        out_ref[...] = jnp.dot(rel_ref[...], oh_ref[...], precision=HIGHEST, preferred_element_type=F32)

    return pl.pallas_call(body, name="rel_rows", out_shape=jax.ShapeDtypeStruct((H, ROLL_W), F32))(rel, onehot_t)


def _rel_grad(ds_sum, onehot):
    H = ds_sum.shape[0]

    def body(ds_ref, oh_ref, out_ref):
        y = jnp.zeros((CHUNK, ROLL_W), F32)
        pad = jnp.zeros((CHUNK, ROLL_W - 2 * WIN), F32)
        for a in range(N_PREV_CHUNKS):
            part = jnp.concatenate([ds_ref[a * CHUNK:(a + 1) * CHUNK, :], pad], axis=1)
            y = y + (part if a == 0 else pltpu.roll(part, ROLL_W - CHUNK * a, 1))
        rowb = lax.broadcasted_iota(jnp.int32, (CHUNK, ROLL_W), 0)
        for bit in range(CHUNK.bit_length() - 1):
            y = jnp.where((rowb >> bit) & 1 == 1, pltpu.roll(y, ROLL_W - (1 << bit), 1), y)
        diag = jnp.broadcast_to(jnp.sum(y, axis=0, keepdims=True), (8, ROLL_W))
        out_ref[...] = jnp.dot(diag, oh_ref[...], precision=HIGHEST, preferred_element_type=F32)

    return pl.pallas_call(
        body, name="rel_grad", grid=(H,),
        in_specs=[pl.BlockSpec((None, WIN, 2 * WIN), lambda h: (h, 0, 0)),
                  pl.BlockSpec((ROLL_W, N_REL_PAD), lambda h: (0, 0))],
        out_specs=pl.BlockSpec((None, 8, N_REL_PAD), lambda h: (h, 0, 0)),
        out_shape=jax.ShapeDtypeStruct((H, 8, N_REL_PAD), F32), compiler_params=_params("parallel"),
    )(ds_sum, onehot)


def _loss_head(y, target):
    T, D = y.shape
    bt = _tile(T, 256)

    def body(y_ref, t_ref, dy_ref, dy16_ref, loss_ref):
        @pl.when(pl.program_id(0) == 0)
        def _():
            loss_ref[...] = jnp.zeros_like(loss_ref)

        err = y_ref[...] - t_ref[...]
        dy = err * (1.0 / D)
        dy_ref[...] = dy
        dy16_ref[...] = dy.astype(BF16)
        loss_ref[...] += 0.5 * jnp.sum(jnp.mean(err * err, axis=-1, keepdims=True))

    row = pl.BlockSpec((bt, D), lambda i: (i, 0))
    one = pl.BlockSpec((8, LANES), lambda i: (0, 0))
    return pl.pallas_call(
        body, name="loss_head", grid=(T // bt,), in_specs=[row, row], out_specs=[row, row, one],
        out_shape=[jax.ShapeDtypeStruct((T, D), F32), jax.ShapeDtypeStruct((T, D), BF16),
                   jax.ShapeDtypeStruct((8, LANES), F32)],
        compiler_params=_params("arbitrary"),
    )(y, target)


def _adam_step(g, w_ref, m_ref, v_ref, g_ref, d_ref, nm_ref, nv_ref):
    nm = ADAM_B1 * m_ref[...] + (1.0 - ADAM_B1) * g
    nv = ADAM_B2 * v_ref[...] + (1.0 - ADAM_B2) * (g * g)
    m_hat = nm / (1.0 - ADAM_B1 ** ADAM_STEP)
    v_hat = nv / (1.0 - ADAM_B2 ** ADAM_STEP)
    g_ref[...] = g
    nm_ref[...] = nm
    nv_ref[...] = nv
    d_ref[...] = -ADAM_LR * (m_hat / (jnp.sqrt(v_hat) + ADAM_EPS) + ADAM_WD * w_ref[...])


def _adamw(name, w, m, v, g):
    def body(w_ref, m_ref, v_ref, gin_ref, g_ref, d_ref, nm_ref, nv_ref):
        _adam_step(gin_ref[...], w_ref, m_ref, v_ref, g_ref, d_ref, nm_ref, nv_ref)

    return pl.pallas_call(body, name=name, out_shape=[jax.ShapeDtypeStruct(w.shape, F32)] * 4)(w, m, v, g)


def _adamw_reduced(name, w, m, v, own, land, chip, row0=0, into=None):
    rows_all, C = w.shape
    _, R, _ = land.shape

    def body(chip_ref, w_ref, m_ref, v_ref, own_ref, land_ref, *rest):
        g = jnp.where(chip_ref[0] == 0, own_ref[...], land_ref[0]).astype(F32)
        for q in range(1, N_CHIP):
            g = g + jnp.where(chip_ref[0] == q, own_ref[...], land_ref[q]).astype(F32)
        _adam_step(g, w_ref, m_ref, v_ref, *rest[-4:])

    if R % 16 == 0:
        br = _tile(R, max(16, (128 * 1024) // C))
        first = row0 // br
        grid = (R // br,)
        wsp = pl.BlockSpec((br, C), lambda i, chip_ref: (first + i, 0))
        osp = pl.BlockSpec((None, br, C), lambda i, chip_ref: (chip_ref[0], i, 0))
        lsp = pl.BlockSpec((N_CHIP, br, C), lambda i, chip_ref: (0, i, 0))
    else:
        bc = _tile(C, 512)
        grid = (C // bc,)
        wsp = pl.BlockSpec((R, bc), lambda i, chip_ref: (0, i))
        osp = pl.BlockSpec((None, R, bc), lambda i, chip_ref: (chip_ref[0], 0, i))
        lsp = pl.BlockSpec((N_CHIP, R, bc), lambda i, chip_ref: (0, 0, i))
    extra = [] if into is None else list(into)
    return pl.pallas_call(
        body, name=name,
        grid_spec=pltpu.PrefetchScalarGridSpec(
            num_scalar_prefetch=1, grid=grid, in_specs=[wsp, wsp, wsp, osp, lsp] + [ANY] * len(extra),
            out_specs=[wsp] * 4),
        out_shape=[jax.ShapeDtypeStruct((rows_all, C), F32)] * 4,
        input_output_aliases={6 + j: j for j in range(len(extra))}, compiler_params=_params("parallel"),
    )(chip, w, m, v, own, land, *extra)


def _sum_parts(name, parts, out_dtype):
    P, R, C = parts.shape
    br = _tile(R, max(8, (256 * 1024) // C)) if R % 8 == 0 else R

    def body(p_ref, o_ref):
        g = p_ref[0].astype(F32)
        for i in range(1, P):
            g = g + p_ref[i].astype(F32)
        o_ref[...] = g.astype(out_dtype)

    return pl.pallas_call(
        body, name=name, grid=(R // br,),
        in_specs=[pl.BlockSpec((P, br, C), lambda i: (0, i, 0))], out_specs=pl.BlockSpec((br, C), lambda i: (i, 0)),
        out_shape=jax.ShapeDtypeStruct((R, C), out_dtype), compiler_params=_params("parallel"),
    )(parts)


def _pair_sum(name, g, recv, core):
    _, R, C = g.shape

    def body(core_ref, g_ref, r_ref, o_ref):
        o_ref[...] = (g_ref[...] + r_ref[...]).astype(BF16)

    if R % 16 == 0:
        br = _tile(R, max(16, (256 * 1024) // C))
        blk, n = (None, br, C), R // br
        mine = lambda q, i, core_ref: (2 * q + core_ref[0], i, 0)
        same = lambda q, i, core_ref: (q, i, 0)
    else:
        bc = _tile(C, 512)
        blk, n = (None, R, bc), C // bc
        mine = lambda q, i, core_ref: (2 * q + core_ref[0], 0, i)
        same = lambda q, i, core_ref: (q, 0, i)
    return pl.pallas_call(
        body, name=name,
        grid_spec=pltpu.PrefetchScalarGridSpec(
            num_scalar_prefetch=1, grid=(N_CHIP, n),
            in_specs=[pl.BlockSpec(blk, mine), pl.BlockSpec(blk, same)], out_specs=pl.BlockSpec(blk, same)),
        out_shape=jax.ShapeDtypeStruct((N_CHIP, R, C), BF16), compiler_params=_params("parallel", "parallel"),
    )(core, g, recv)


def _chips_of(x, y):
    return [(1 - x, y), (x, 1 - y), (1 - x, 1 - y)]


def _position():
    x, y, c = lax.axis_index("x"), lax.axis_index("y"), lax.axis_index("c")
    return x, y, c, _chips_of(x, y)


def _all_gather(name, xs):
    n = len(xs)

    def body(*refs):
        x_refs, out_refs = refs[:n], refs[n:2 * n]
        send_sems, recv_sems, local_sems = refs[2 * n:]
        x, y, c, chips = _position()
        me, sibling = (x, y, c), (x, y, 1 - c)

        def copy(a, k, block, to, src=None):
            px, py, pc = block
            slot = out_refs[a].at[4 * px + 2 * py + pc]
            return pltpu.make_async_remote_copy(
                src_ref=slot if src is None else src, dst_ref=slot,
                send_sem=send_sems.at[a, k], recv_sem=recv_sems.at[a, k], device_id=to, device_id_type=MESH)

        mine = [pltpu.make_async_copy(x_refs[a], out_refs[a].at[4 * x + 2 * y + c], local_sems.at[a]) for a in range(n)]
        for cp in mine:
            cp.start()
        first = []
        for a in range(n):
            first.append(copy(a, 0, me, sibling, src=x_refs[a]))
            first += [copy(a, 1 + j, me, (*chip, c), src=x_refs[a]) for j, chip in enumerate(chips)]
        for cp in first:
            cp.start()
        passed = []
        for j, chip in enumerate(chips):
            for a in range(n):
                copy(a, 1 + j, (*chip, c), me).wait_recv()
                fwd = copy(a, 4 + j, (*chip, c), sibling)
                fwd.start()
                passed.append(fwd)
        for a in range(n):
            copy(a, 0, sibling, me).wait_recv()
            for j, chip in enumerate(chips):
                copy(a, 4 + j, (*chip, 1 - c), me).wait_recv()
        for cp in first + passed:
            cp.wait_send()
        for cp in mine:
            cp.wait()

    return pl.pallas_call(
        body, name=name, in_specs=[ANY] * n, out_specs=[ANY] * n,
        out_shape=[jax.ShapeDtypeStruct((N_DEV,) + x.shape, x.dtype) for x in xs],
        scratch_shapes=[pltpu.SemaphoreType.DMA((n, 7)), pltpu.SemaphoreType.DMA((n, 7)),
                        pltpu.SemaphoreType.DMA((n,))],
    )(*xs)


HBM = pl.BlockSpec(memory_space=pltpu.HBM)
SEM = pl.BlockSpec(memory_space=pltpu.SEMAPHORE)
EFFECT = pltpu.SideEffectType.DATAFLOW_SIDE_EFFECTING


def _own_block_out(x, y, c):
    mine = 4 * x + 2 * y + c
    return [(mine, (x, y, 1 - c))] + [(mine, (cx, cy, c)) for cx, cy in _chips_of(x, y)]


def _own_blocks_in(x, y, c):
    return [(4 * x + 2 * y + 1 - c, (x, y, 1 - c))] + [(4 * cx + 2 * cy + c, (cx, cy, c)) for cx, cy in _chips_of(x, y)]


def _passed_on_out(x, y, c):
    return [(4 * cx + 2 * cy + c, (x, y, 1 - c)) for cx, cy in _chips_of(x, y)]


def _passed_on_in(x, y, c):
    return [(4 * cx + 2 * cy + 1 - c, (x, y, 1 - c)) for cx, cy in _chips_of(x, y)]


def _gather_start(name, lands, plan):
    n = len(lands)
    n_copies = len(plan(0, 0, 0))

    def body(*refs):
        land = refs[:n]
        send_sems, recv_sems = refs[n:2 * n], refs[2 * n:3 * n]
        token = refs[4 * n]
        x, y, c, _ = _position()
        for a in range(n):
            for k, (block, peer) in enumerate(plan(x, y, c)):
                pltpu.make_async_remote_copy(
                    src_ref=land[a].at[block], dst_ref=land[a].at[block], send_sem=send_sems[a].at[k],
                    recv_sem=recv_sems[a].at[k], device_id=peer, device_id_type=MESH).start()
        token[...] = jnp.zeros_like(token)

    res = pl.pallas_call(
        body, name=name, in_specs=[HBM] * n,
        out_specs=[SEM] * (2 * n) + [HBM] * n + [pl.BlockSpec(memory_space=pltpu.VMEM)],
        out_shape=[pltpu.SemaphoreType.DMA((n_copies,))] * (2 * n) + [pltpu.HBM(l.shape, l.dtype) for l in lands]
        + [jax.ShapeDtypeStruct((8, LANES), F32)],
        input_output_aliases={a: 2 * n + a for a in range(n)},
        compiler_params=pltpu.CompilerParams(has_side_effects=EFFECT),
    )(*[pltpu.with_memory_space_constraint(l, pltpu.HBM) for l in lands])
    return res[:n], res[n:2 * n], res[2 * n:3 * n], res[3 * n]


def _gather_wait(name, lands, send_sems, recv_sems, plan, after):
    n = len(lands)
    tail = [] if after is None else [after]

    def body(*refs):
        land = refs[:n]
        send, recv = refs[n:2 * n], refs[2 * n:3 * n]
        x, y, c, _ = _position()
        for a in range(n):
            for k, (block, peer) in enumerate(plan(x, y, c)):
                cp = pltpu.make_async_remote_copy(
                    src_ref=land[a].at[block], dst_ref=land[a].at[block], send_sem=send[a].at[k],
                    recv_sem=recv[a].at[k], device_id=peer, device_id_type=MESH)
                cp.wait_send()
                cp.wait_recv()

    return pl.pallas_call(
        body, name=name, in_specs=[HBM] * n + [SEM] * (2 * n) + [ANY] * len(tail), out_specs=[HBM] * n,
        out_shape=[pltpu.HBM(l.shape, l.dtype) for l in lands], input_output_aliases={a: a for a in range(n)},
        compiler_params=pltpu.CompilerParams(has_side_effects=EFFECT),
    )(*lands, *send_sems, *recv_sems, *tail)


def _split_start(name, arrays, lands, plan):
    n = len(arrays)
    n_copies = len(plan(0, 0, 0))

    def body(*refs):
        src, land = refs[:n], refs[n:2 * n]
        send_sems, recv_sems = refs[2 * n:3 * n], refs[3 * n:4 * n]
        token = refs[6 * n]
        x, y, c, _ = _position()
        for a in range(n):
            for k, (src_block, land_block, peer) in enumerate(plan(x, y, c)):
                pltpu.make_async_remote_copy(
                    src_ref=src[a].at[src_block], dst_ref=land[a].at[land_block], send_sem=send_sems[a].at[k],
                    recv_sem=recv_sems[a].at[k], device_id=peer, device_id_type=MESH).start()
        token[...] = jnp.zeros_like(token)

    both = list(arrays) + list(lands)
    res = pl.pallas_call(
        body, name=name, in_specs=[HBM] * (2 * n),
        out_specs=[SEM] * (2 * n) + [HBM] * (2 * n) + [pl.BlockSpec(memory_space=pltpu.VMEM)],
        out_shape=[pltpu.SemaphoreType.DMA((n_copies,))] * (2 * n) + [pltpu.HBM(b.shape, b.dtype) for b in both]
        + [jax.ShapeDtypeStruct((8, LANES), F32)],
        input_output_aliases={i: 2 * n + i for i in range(2 * n)},
        compiler_params=pltpu.CompilerParams(has_side_effects=EFFECT),
    )(*[pltpu.with_memory_space_constraint(b, pltpu.HBM) for b in both])
    return res[:n], res[n:2 * n], res[2 * n:3 * n], res[3 * n:4 * n], res[4 * n]


def _split_wait(name, arrays, lands, send_sems, recv_sems, plan, after):
    n = len(arrays)
    tail = [] if after is None else [after]

    def body(*refs):
        src, land = refs[:n], refs[n:2 * n]
        send, recv = refs[2 * n:3 * n], refs[3 * n:4 * n]
        x, y, c, _ = _position()
        for a in range(n):
            for k, (src_block, land_block, peer) in enumerate(plan(x, y, c)):
                cp = pltpu.make_async_remote_copy(
                    src_ref=src[a].at[src_block], dst_ref=land[a].at[land_block], send_sem=send[a].at[k],
                    recv_sem=recv[a].at[k], device_id=peer, device_id_type=MESH)
                cp.wait_send()
                cp.wait_recv()

    both = list(arrays) + list(lands)
    res = pl.pallas_call(
        body, name=name, in_specs=[HBM] * (2 * n) + [SEM] * (2 * n) + [ANY] * len(tail), out_specs=[HBM] * (2 * n),
        out_shape=[pltpu.HBM(b.shape, b.dtype) for b in both], input_output_aliases={i: i for i in range(2 * n)},
        compiler_params=pltpu.CompilerParams(has_side_effects=EFFECT),
    )(*both, *send_sems, *recv_sems, *tail)
    return res[:n], res[n:]


def _to_other_core(x, y, c):
    return [(2 * q + (1 - c), q, (x, y, 1 - c)) for q in range(N_CHIP)]


def _to_other_chips(x, y, c):
    return [(2 * cx + cy, 2 * x + y, (cx, cy, c)) for cx, cy in _chips_of(x, y)]


def _from_other_chips(x, y, c):
    return [(2 * x + y, 2 * cx + cy, (cx, cy, c)) for cx, cy in _chips_of(x, y)]


def _mlp_fwd(tag, h, g, w1g, w2):
    T, D = h.shape
    F = w2.shape[0]
    (n,) = _rmsnorm_fwd(f"mlp{tag}_norm", h, [g])
    a, hid = _mm(f"mlp{tag}_up", "nn", n, w1g, T, F, D, b_groups=N_DEV, out_dtypes=(BF16, BF16),
                 epilogue=_ep_relu_square)
    out = _mm(f"mlp{tag}_down", "nn", hid, w2, T, D, F, extras=(h,), epilogue=_ep_add)
    return out, (n, a, hid)


def _mlp_bwd(tag, h, g, w1g, w2, saved, dy, dy16, behind=()):
    T, D = h.shape
    F = w2.shape[0]
    n, a, hid = saved
    dw2 = _mm(f"mlp{tag}_dw2", "tn", hid, dy16, F, D, T, behind=behind)
    dpre = _mm(f"mlp{tag}_dpre", "nt", dy16, w2, T, F, D, out_dtypes=(BF16,), extras=(a,), epilogue=_ep_times_2a)
    dw1 = _mm(f"mlp{tag}_dw1", "tn", n, dpre, D, F, T, out_groups=N_DEV)
    dn = _mm(f"mlp{tag}_dn", "nt", dpre, w1g, T, D, F, b_groups=N_DEV)
    dh, dh16, dg = _rmsnorm_bwd(f"mlp{tag}_norm_bwd", h, g, dn, dy)
    return dh, dh16, dg, dw1, dw2


def _local_step(x, target, W, fox_blk, arrive=lambda group, after: {}, sync=lambda tag, after, G: ()):
    T, D = x.shape
    H = D // HEAD_DIM
    G = {}
    W = dict(W)

    (ua,) = _rmsnorm_fwd("a_norm", x, [W["a_norm_g"]])
    qkv = _mm("a_qkv", "nt", ua, W["a_w_in_t"], T, 3 * D, D)
    fz = _mm("a_fz", "nt", ua, W["a_w_f_t"], T, LANES, D)
    qa, ka, va = _head_post("a_heads", qkv, D, [(0, W["a_q_g"]), (1, W["a_k_g"]), (2, None)])
    ccol = _gates_fwd(fz, W["a_b_f"], H)
    crow = ccol[:, :H].T
    oa, lse = _fox_fwd(qa, ka, va, ccol, crow, fox_blk)
    W.update(arrive("a_out", oa))
    h1 = _mm("a_out", "nn", oa, W["a_w_out"], T, D, D, extras=(x,), epilogue=_ep_add)
    W.update(arrive("middle", h1))
    h2, mlp0 = _mlp_fwd("0", h1, W["mlp_norm_g0"], W["mlp_w1_0"], W["mlp_w2_0"])
    W.update(arrive("kv_q", h2))
    nkv, ub = _rmsnorm_fwd("kv_b_norm", h2, [W["kv_norm_g"], W["b_norm_g"]])
    kv = _mm("kv_proj", "nn", nkv, W["kv_w"], T, 2 * D, D, b_groups=N_DEV)
    ks, vs = _head_post("kv_heads", kv, D, [(0, W["kv_k_g"]), (1, None)])
    qb_pre = _mm("b_q", "nn", ub, W["b_w_q"], T, D, D)
    (qb,) = _head_post("b_heads", qb_pre, D, [(0, W["b_q_g"])])
    onehot = (jnp.asarray(_rel_index())[:, None] == jnp.arange(N_REL_PAD)[None, :]).astype(F32)
    rel_pad = jnp.pad(W["b_rel"], ((0, 0), (0, N_REL_PAD - N_REL)))
    g_rows = _rel_rows(rel_pad, onehot.T).reshape(H, 1, ROLL_W)
    ob = _chunk_fwd(qb, ks, vs, g_rows)
    W.update(arrive("late", ob))
    h3 = _mm("b_out", "nn", ob, W["b_w_out"], T, D, D, extras=(h2,), epilogue=_ep_add)
    h4, mlp1 = _mlp_fwd("1", h3, W["mlp_norm_g1"], W["mlp_w1_1"], W["mlp_w2_1"])
    dy, dy16, loss = _loss_head(h4, target)

    dh3, dh3_16, G["mlp_norm_g1"], G["mlp_w1_1"], G["mlp_w2_1"] = _mlp_bwd(
        "1", h3, W["mlp_norm_g1"], W["mlp_w1_1"], W["mlp_w2_1"], mlp1, dy, dy16)
    dob = _mm("b_out_dx", "nt", dh3_16, W["b_w_out"], T, D, D, out_dtypes=(BF16,), behind=sync("mlp1", dh3, G))
    G["b_w_out"] = _mm("b_out_dw", "tn", ob, dh3_16, D, D, T, behind=sync("b_out_dx", dob, G))
    dqb, dks, dvs, ds_sum = _chunk_bwd(qb, ks, vs, g_rows, dob, ob)
    G["b_rel"] = _rel_grad(ds_sum, onehot)[:, 0, :]
    dqb_pre, (G["b_q_g"],) = _head_post_bwd("b_heads_bwd", qb_pre, D, [(0, W["b_q_g"], dqb)])
    G["b_w_q"] = _mm("b_q_dw", "tn", ub, dqb_pre, D, D, T)
    dub = _mm("b_q_dx", "nt", dqb_pre, W["b_w_q"], T, D, D)
    dkv, (G["kv_k_g"],) = _head_post_bwd("kv_heads_bwd", kv, D, [(0, W["kv_k_g"], dks), (1, None, dvs)])
    G["kv_w"] = _mm("kv_dw", "tn", nkv, dkv, D, 2 * D, T, out_groups=N_DEV)
    dnkv = _mm("kv_dx", "nt", dkv, W["kv_w"], T, D, 2 * D, b_groups=N_DEV, behind=sync("kv_dw", G["kv_w"], G))
    dh2, _, G["b_norm_g"] = _rmsnorm_bwd("b_norm_bwd", h2, W["b_norm_g"], dub, dh3)
    dh2, dh2_16, G["kv_norm_g"] = _rmsnorm_bwd("kv_norm_bwd", h2, W["kv_norm_g"], dnkv, dh2)
    dh1, dh1_16, G["mlp_norm_g0"], G["mlp_w1_0"], G["mlp_w2_0"] = _mlp_bwd(
        "0", h1, W["mlp_norm_g0"], W["mlp_w1_0"], W["mlp_w2_0"], mlp0, dh2, dh2_16, behind=sync("dh2", dh2, G))
    G["a_w_out"] = _mm("a_out_dw", "tn", oa, dh1_16, D, D, T, behind=sync("mlp0", dh1, G))
    doa = _mm("a_out_dx", "nt", dh1_16, W["a_w_out"], T, D, D, out_dtypes=(BF16,))
    delta = _head_stat("fox_delta", doa, oa)
    dqa, dcq = _fox_bwd_dq(qa, ka, va, doa, ccol, crow, lse, delta, fox_blk)
    dka, dva, dck = _fox_bwd_dkv(qa, ka, va, doa, ccol, crow, lse[:, :H].T, delta[:, :H].T, fox_blk,
                                 behind=sync("fox_bwd_dq", dqa, G))
    dfz, G["a_b_f"] = _gates_bwd(dcq, dck, fz, W["a_b_f"], H)
    dqkv, (G["a_q_g"], G["a_k_g"]) = _head_post_bwd(
        "a_heads_bwd", qkv, D, [(0, W["a_q_g"], dqa), (1, W["a_k_g"], dka), (2, None, dva)])
    G["a_w_qkv_t"] = _mm("a_qkv_dw", "tn", dqkv, ua, 3 * D, D, T)
    G["a_w_f_t"] = _mm("a_fz_dw", "tn", dfz, ua, LANES, D, T)
    dua = _mm("a_qkv_dx", "nn", dqkv, W["a_w_in_t"], T, D, 3 * D, behind=sync("a_in_dw", G["a_w_f_t"], G))
    dua = _mm("a_fz_dx", "nn", dfz, W["a_w_f_t"], T, D, LANES, extras=(dua,), epilogue=_ep_add,
              behind=sync("a_qkv_dx", dua, G))
    dx, _, G["a_norm_g"] = _rmsnorm_bwd("a_norm_bwd", x, W["a_norm_g"], dua, dh1)
    return loss, dx, G


def _pad_lanes(a):
    return jnp.pad(a, ((0, 0), (0, LANES - a.shape[1])))


def kernel(x, a_norm_g, a_w_in, a_b_f, a_q_g, a_k_g, a_w_out, mlp_norm_g, mlp_w1, mlp_w2, kv_norm_g, kv_w, kv_k_g, b_norm_g, b_w_q, b_q_g, b_rel, b_w_out, loss_target, m_a_norm_g, m_a_w_in, m_a_b_f, m_a_q_g, m_a_k_g, m_a_w_out, m_mlp_norm_g, m_mlp_w1, m_mlp_w2, m_kv_norm_g, m_kv_w, m_kv_k_g, m_b_norm_g, m_b_w_q, m_b_q_g, m_b_rel, m_b_w_out, v_a_norm_g, v_a_w_in, v_a_b_f, v_a_q_g, v_a_k_g, v_a_w_out, v_mlp_norm_g, v_mlp_w1, v_mlp_w2, v_kv_norm_g, v_kv_w, v_kv_k_g, v_b_norm_g, v_b_w_q, v_b_q_g, v_b_rel, v_b_w_out):
    T, D = x.shape[1], x.shape[2]
    H = D // HEAD_DIM
    F = mlp_w2.shape[1] * N_DEV
    me = 4 * lax.axis_index("x") + 2 * lax.axis_index("y") + lax.axis_index("c")
    core = lax.axis_index("c").astype(jnp.int32).reshape(1)
    n_in = a_w_in.shape[2]
    n_rel = b_rel.shape[2]

    small_in = jnp.concatenate([a_norm_g.reshape(1, -1), b_rel.reshape(1, -1)], axis=1)
    local = [jnp.swapaxes(a_w_in[0], 0, 1), a_w_out[0], mlp_w1[0], mlp_w2[0], kv_w, b_w_q[0], b_w_out[0],
             mlp_w1[1], mlp_w2[1]]
    def landing(w):
        return lax.dynamic_update_slice(lax.empty((N_DEV,) + w.shape, w.dtype), w[None], (me,) + (0,) * w.ndim)

    send1, recv1, lands, started = _gather_start(
        "gather_start_first", [landing(small_in), landing(local[0].astype(BF16))], _own_block_out)
    more = _gather_start("gather_start_rest", [landing((w + started[0, 0]).astype(BF16)) for w in local[1:]],
                         _own_block_out)
    send1, recv1, lands, sems2 = send1 + more[0], recv1 + more[1], list(lands + more[2]), {}

    def pass_on(tag, idx, after):
        got = _gather_wait(f"gather_wait_{tag}", [lands[i] for i in idx], [send1[i] for i in idx],
                           [recv1[i] for i in idx], _own_blocks_in, after)
        send, recv, got, _ = _gather_start(f"gather_pass_{tag}", got, _passed_on_out)
        for k, i in enumerate(idx):
            lands[i], sems2[i] = got[k], (send[k], recv[k])

    def landed(tag, idx, after):
        got = _gather_wait(f"gather_landed_{tag}", [lands[i] for i in idx], [sems2[i][0] for i in idx],
                           [sems2[i][1] for i in idx], _passed_on_in, after)
        for k, i in enumerate(idx):
            lands[i] = got[k]
        return got

    def arrive(group, after):
        if group == "first":
            pass_on("first", [0, 1], after)
            g_small, w_in_t = landed("first", [0, 1], None)
            w_in_t = w_in_t.reshape(N_DEV * n_in, D)
            rel = jnp.transpose(g_small[:, 0, D // N_DEV:].reshape(N_DEV, H, n_rel), (1, 0, 2))
            return {"a_norm_g": g_small[:, 0, :D // N_DEV].reshape(1, D), "b_rel": rel.reshape(H, N_DEV * n_rel),
                    "a_w_in_t": w_in_t, "a_w_f_t": jnp.pad(w_in_t[3 * D:], ((0, LANES - H), (0, 0)))}
        if group == "a_out":
            pass_on("early", [2, 3, 4], after)
            return {"a_w_out": landed("a_out", [2], None)[0].reshape(D, D)}
        if group == "middle":
            got = landed("middle", [3, 4], after)
            return {"mlp_w1_0": got[0], "mlp_w2_0": got[1].reshape(F, D)}
        if group == "kv_q":
            pass_on("late", [5, 6, 7, 8, 9], after)
            got = landed("kv_q", [5, 6], None)
            return {"kv_w": got[0], "b_w_q": got[1].reshape(D, D)}
        got = landed("late", [7, 8, 9], after)
        return {"b_w_out": got[0].reshape(D, D), "mlp_w1_1": got[1], "mlp_w2_1": got[2].reshape(F, D)}

    W = {
        "a_b_f": _pad_lanes(a_b_f), "a_q_g": a_q_g, "a_k_g": a_k_g,
        "mlp_norm_g0": mlp_norm_g[0:1], "mlp_norm_g1": mlp_norm_g[1:2],
        "kv_norm_g": kv_norm_g.reshape(1, D), "kv_k_g": kv_k_g.reshape(1, HEAD_DIM),
        "b_norm_g": b_norm_g, "b_q_g": b_q_g, **arrive("first", more[3]),
    }

    chip = (2 * lax.axis_index("x") + lax.axis_index("y")).astype(jnp.int32).reshape(1)
    flat = lambda a: a.reshape(-1, a.shape[-1])
    state = {"a_w_in": [jnp.swapaxes(a[0], 0, 1) for a in (a_w_in, m_a_w_in, v_a_w_in)]}
    for nm, w, m, v in [("a_w_out", a_w_out, m_a_w_out, v_a_w_out), ("kv_w", kv_w, m_kv_w, v_kv_w),
                        ("b_w_q", b_w_q, m_b_w_q, v_b_w_q), ("b_w_out", b_w_out, m_b_w_out, v_b_w_out),
                        ("mlp_w1", mlp_w1, m_mlp_w1, v_mlp_w1), ("mlp_w2", mlp_w2, m_mlp_w2, v_mlp_w2)]:
        state[nm] = [flat(w), flat(m), flat(v)]
    upd, flying = {}, {}

    def chunks(nm, G):
        if nm == "a_w_in":
            return jnp.concatenate([G["a_w_qkv_t"], G["a_w_f_t"][:H]], axis=0).reshape(N_DEV, n_in, D)
        g = G[nm]
        return g if g.ndim == 3 else g.reshape(N_DEV, g.shape[0] // N_DEV, g.shape[1])

    def to_core(tag, names, G):
        gs = [chunks(nm, G) for nm in names]
        lands = [lax.empty((N_CHIP,) + g.shape[1:], g.dtype) for g in gs]
        send, recv, gs, lands, token = _split_start(f"reduce_cores_start_{tag}", gs, lands, _to_other_core)
        flying[tag] = (names, send, recv, gs, lands)
        return (token,)

    def to_chips(tag, after):
        names, send, recv, gs, lands = flying[tag]
        gs, lands = _split_wait(f"reduce_cores_wait_{tag}", gs, lands, send, recv, _to_other_core, after)
        ps = [_pair_sum(f"pair_sum_{nm}", g, r, core) for nm, g, r in zip(names, gs, lands)]
        lands = [lax.empty(p.shape, p.dtype) for p in ps]
        send, recv, ps, lands, token = _split_start(f"reduce_chips_start_{tag}", ps, lands, _to_other_chips)
        flying[tag] = (names, send, recv, ps, lands)
        return (token,)

    def finish(tag, after):
        names, send, recv, ps, lands = flying.pop(tag)
        ps, lands = _split_wait(f"reduce_chips_wait_{tag}", ps, lands, send, recv, _from_other_chips, after)
        for nm, own, land in zip(names, ps, lands):
            if nm.startswith("mlp_w"):
                base, layer = nm[:-2], int(nm[-1])
                w, m, v = state[base]
                upd[base] = _adamw_reduced(f"adamw_{nm}", w, m, v, own, land, chip, row0=layer * (w.shape[0] // 2),
                                           into=upd.get(base))
            else:
                upd[nm] = _adamw_reduced(f"adamw_{nm}", *state[nm], own, land, chip)

    def sync(tag, after, G):
        if tag == "mlp1":
            return to_core("r1", ["mlp_w2_1", "mlp_w1_1"], G)
        if tag == "b_out_dx":
            return to_chips("r1", after)
        if tag == "kv_dw":
            return to_core("r2", ["b_w_out", "b_w_q", "kv_w"], G)
        if tag == "dh2":
            return to_chips("r2", after)
        if tag == "mlp0":
            return to_core("r3", ["mlp_w2_0", "mlp_w1_0"], G)
        if tag == "fox_bwd_dq":
            finish("r1", after)
            finish("r2", after)
            return to_chips("r3", after)
        if tag == "a_in_dw":
            return to_core("r4", ["a_w_out", "a_w_in"], G)
        if tag == "a_qkv_dx":
            return to_chips("r4", after)
        return ()

    loss_tile, dx, G = _local_step(x[0], loss_target[0], W, min(T, 512), arrive, sync)
    finish("r3", dx)

    def rows8(vecs):
        r = lax.broadcasted_iota(jnp.int32, (8, LANES), 0)
        tile = jnp.zeros((8, LANES), F32)
        for i, vec in enumerate(vecs):
            tile = jnp.where(r == i, vec, tile)
        return tile

    rows = lambda a: a.reshape(-1, LANES)
    packed = [rows(G["a_norm_g"]), rows(G["mlp_norm_g0"]), rows(G["mlp_norm_g1"]), rows(G["kv_norm_g"]),
              rows(G["b_norm_g"]), rows(G["b_rel"]),
              rows8([G["a_b_f"], G["a_q_g"], G["a_k_g"], G["kv_k_g"], G["b_q_g"], loss_tile[0:1]])]
    sizes = [p.shape[0] for p in packed]
    (all_small,) = _all_gather("gather_small_grads", [jnp.concatenate(packed, axis=0)])
    total = _sum_parts("sum_small_grads", all_small, F32)
    offs = np.concatenate([[0], np.cumsum(sizes)])
    s = [total[offs[i]:offs[i + 1]] for i in range(len(sizes))]
    loss = s[6][5, 0]
    g_a_norm = lax.dynamic_slice(s[0].reshape(1, D), (0, me * (D // N_DEV)), (1, D // N_DEV))
    g_rel = lax.dynamic_slice(s[5].reshape(H, N_REL_PAD), (0, me * n_rel), (H, n_rel))
    small = {
        "a_norm_g": (a_norm_g, m_a_norm_g, v_a_norm_g, g_a_norm),
        "a_b_f": (a_b_f, m_a_b_f, v_a_b_f, s[6][0:1, :H]),
        "a_q_g": (a_q_g, m_a_q_g, v_a_q_g, s[6][1:2]),
        "a_k_g": (a_k_g, m_a_k_g, v_a_k_g, s[6][2:3]),
        "mlp_norm_g": (mlp_norm_g, m_mlp_norm_g, v_mlp_norm_g,
                       jnp.concatenate([s[1].reshape(1, D), s[2].reshape(1, D)], axis=0)),
        "kv_norm_g": (kv_norm_g.reshape(1, D), m_kv_norm_g.reshape(1, D), v_kv_norm_g.reshape(1, D), s[3].reshape(1, D)),
        "kv_k_g": (kv_k_g.reshape(1, HEAD_DIM), m_kv_k_g.reshape(1, HEAD_DIM), v_kv_k_g.reshape(1, HEAD_DIM),
                   s[6][3:4]),
        "b_norm_g": (b_norm_g, m_b_norm_g, v_b_norm_g, s[4].reshape(1, D)),
        "b_q_g": (b_q_g, m_b_q_g, v_b_q_g, s[6][4:5]),
        "b_rel": (b_rel[0], m_b_rel[0], v_b_rel[0], g_rel),
    }
    for nm, (w, m, v, g) in small.items():
        upd[nm] = _adamw(f"adamw_{nm}", w, m, v, g)
    upd["kv_norm_g"] = [a.reshape(D) for a in upd["kv_norm_g"]]
    upd["kv_k_g"] = [a.reshape(HEAD_DIM) for a in upd["kv_k_g"]]
    upd["b_rel"] = [a[None] for a in upd["b_rel"]]
    finish("r4", total)
    upd["a_w_in"] = [jnp.swapaxes(o, 0, 1)[None] for o in upd["a_w_in"]]
    for nm, w in [("a_w_out", a_w_out), ("kv_w", kv_w), ("b_w_q", b_w_q), ("b_w_out", b_w_out), ("mlp_w1", mlp_w1),
                  ("mlp_w2", mlp_w2)]:
        upd[nm] = [o.reshape(w.shape) for o in upd[nm]]

    order = ["a_norm_g", "a_w_in", "a_b_f", "a_q_g", "a_k_g", "a_w_out", "mlp_norm_g", "mlp_w1", "mlp_w2",
             "kv_norm_g", "kv_w", "kv_k_g", "b_norm_g", "b_w_q", "b_q_g", "b_rel", "b_w_out"]
    return (loss, dx[None], *[upd[n][0] for n in order], *[upd[n][1] for n in order],
            *[upd[n][2] for n in order], *[upd[n][3] for n in order])
```
